```python
import jax, jax.numpy as jnp
from jax import lax
import numpy as np

D_MODEL = 1024
BATCH = 8
SEQ = 2048
DEPTH = 2
DEC_BATCH = 128
DEC_SEQ = 1
PAST_LEN = 2048
PAGE_SIZE = 128

N_HEADS = 8
HEAD_DIM = D_MODEL // N_HEADS
QKV_W = N_HEADS * HEAD_DIM
IDX_HEADS = 4
IDX_DIM = 64
TOPK_MAX = 256
IDX_Q_BLOCK = 64
ROPE_THETA = 10000.0
IDX_SCALE = (IDX_DIM * IDX_HEADS) ** -0.5
ATTN_IN_W = 3 * QKV_W + IDX_HEADS * IDX_DIM + IDX_DIM + IDX_HEADS
ATTN_SPLITS = (QKV_W, 2 * QKV_W, 3 * QKV_W, 3 * QKV_W + IDX_HEADS * IDX_DIM,
               3 * QKV_W + IDX_HEADS * IDX_DIM + IDX_DIM)
LRU_WIDTH = D_MODEL
LRU_BLOCKS = 8
LRU_BLOCK_W = LRU_WIDTH // LRU_BLOCKS
RG_C = 8.0
REC_CONV_W = 4
D_FF = 2816
FFN_CONV_W = 3
RMS_EPS = 1e-6
N_ATTN_LAYERS = (DEPTH + 1) // 2
N_REC_LAYERS = DEPTH // 2

kernel_name = "dsa_rglru_convffn_hybrid_step"


def rmsnorm(x, g):
    xf = x.astype(jnp.float32)
    y = xf * lax.rsqrt(jnp.mean(xf * xf, axis=-1, keepdims=True) + RMS_EPS) * g.astype(jnp.float32)
    return y.astype(x.dtype)


def rope(x, pos):
    d = x.shape[-1]
    half = d // 2
    inv = ROPE_THETA ** (-jnp.arange(half, dtype=jnp.float32) * 2.0 / d)
    ang = pos.astype(jnp.float32)[:, None] * inv[None, :]
    cos = jnp.cos(ang)[None, :, None, :]
    sin = jnp.sin(ang)[None, :, None, :]
    xf = x.astype(jnp.float32)
    x1, x2 = xf[..., :half], xf[..., half:]
    return jnp.concatenate([x1 * cos - x2 * sin, x2 * cos + x1 * sin], axis=-1).astype(x.dtype)


def causal_dwconv(x, buf, w, b):
    width = w.shape[0]
    t = x.shape[1]
    xp = jnp.concatenate([buf.astype(x.dtype), x], axis=1)
    y = b + sum(xp[:, k:k + t] * w[k] for k in range(width))
    return y, xp[:, -(width - 1):]


def attn_project(a, pos, w_in):
    bsz, t, _ = a.shape
    q, k, v, qi, ki, wi = jnp.split(a @ w_in, ATTN_SPLITS, axis=-1)
    q = rope(q.reshape(bsz, t, N_HEADS, HEAD_DIM), pos)
    k = rope(k.reshape(bsz, t, N_HEADS, HEAD_DIM), pos)
    v = v.reshape(bsz, t, N_HEADS, HEAD_DIM)
    qi = rope(qi.reshape(bsz, t, IDX_HEADS, IDX_DIM), pos)
    ki = rope(ki[:, :, None, :], pos)[:, :, 0]
    return q, k, v, qi, ki, wi


def indexer_topk(qi, wi, ki, q_pos, n_keys, topk):
    s = jnp.einsum('bqhd,bsd->bqhs', qi, ki, preferred_element_type=jnp.float32)
    score = jnp.einsum('bqhs,bqh->bqs', jax.nn.relu(s), wi.astype(jnp.float32)) * IDX_SCALE
    causal = jnp.arange(n_keys)[None, :] <= q_pos[:, None]
    score = jnp.where(causal[None], score, -jnp.inf)
    _, idx = lax.top_k(score, topk)
    valid = idx <= q_pos[None, :, None]
    return idx, valid


def sparse_attend(q, k_sel, v_sel, valid):
    logits = jnp.einsum('bqhd,bqkhd->bqhk', q, k_sel, preferred_element_type=jnp.float32) * (HEAD_DIM ** -0.5)
    logits = jnp.where(valid[:, :, None, :], logits, -jnp.inf)
    p = jax.nn.softmax(logits, axis=-1)
    return jnp.einsum('bqhk,bqkhd->bqhd', p.astype(v_sel.dtype), v_sel)


def dsa_prompt(q, k, v, qi, ki, wi):
    bsz, s = q.shape[:2]
    topk = min(TOPK_MAX, s // 4)
    nb = s // IDX_Q_BLOCK
    gather = jax.vmap(lambda src, ib: src[ib])

    def blk(args):
        qb, qib, wib, start = args
        q_pos = start + jnp.arange(IDX_Q_BLOCK)
        idx, valid = indexer_topk(qib, wib, ki, q_pos, s, topk)
        return sparse_attend(qb, gather(k, idx), gather(v, idx), valid)

    to_blocks = lambda t: jnp.moveaxis(t.reshape(bsz, nb, IDX_Q_BLOCK, *t.shape[2:]), 1, 0)
    out = lax.map(blk, (to_blocks(q), to_blocks(qi), to_blocks(wi), jnp.arange(nb) * IDX_Q_BLOCK))
    return jnp.moveaxis(out, 0, 1).reshape(bsz, s, N_HEADS, HEAD_DIM)


def dsa_sample(q, k_new, v_new, qi, ki_new, wi, pool_k, pool_v, pool_ki, layer, page_table):
    db, t = q.shape[:2]
    past = page_table.shape[1] * PAGE_SIZE
    n_keys = past + t
    topk = min(TOPK_MAX, n_keys // 4)
    ki_past = pool_ki[layer, page_table].reshape(db, past, IDX_DIM)
    ki_all = jnp.concatenate([ki_past.astype(ki_new.dtype), ki_new], axis=1)
    q_pos = past + jnp.arange(t)
    idx, valid = indexer_topk(qi, wi, ki_all, q_pos, n_keys, topk)
    in_past = idx < past
    pidx = jnp.minimum(idx, past - 1)
    phys = jnp.take_along_axis(page_table, (pidx // PAGE_SIZE).reshape(db, -1), axis=1).reshape(idx.shape)
    off = pidx % PAGE_SIZE
    nidx = jnp.clip(idx - past, 0, t - 1)
    gather = jax.vmap(lambda src, ib: src[ib])

    def pick(pool, new):
        old = pool[layer, phys, off].astype(new.dtype)
        return jnp.where(in_past[..., None, None], old, gather(new, nidx))

    return sparse_attend(q, pick(pool_k, k_new), pick(pool_v, v_new), valid)


def block_diag(x, w):
    xb = x.reshape(*x.shape[:-1], LRU_BLOCKS, LRU_BLOCK_W)
    return jnp.einsum('btnc,ncd->btnd', xb, w).reshape(x.shape)


def rglru_mixer(a, conv_buf, h0, w_in, w_conv, b_conv, w_ga, b_ga, w_gx, b_gx, lam, w_out):
    gate, xb = jnp.split(a @ w_in, 2, axis=-1)
    xc, new_buf = causal_dwconv(xb, conv_buf, w_conv, b_conv)
    r = jax.nn.sigmoid((block_diag(xc, w_ga) + b_ga).astype(jnp.float32))
    ig = jax.nn.sigmoid((block_diag(xc, w_gx) + b_gx).astype(jnp.float32))
    log_a = -RG_C * r * jax.nn.softplus(-lam.astype(jnp.float32))
    a_t = jnp.exp(log_a)
    b_t = jnp.sqrt(-jnp.expm1(2.0 * log_a)) * ig * xc.astype(jnp.float32)

    def step(h, ab):
        h = ab[0] * h + ab[1]
        return h, h

    h_last, hs = lax.scan(step, h0.astype(jnp.float32), (jnp.swapaxes(a_t, 0, 1), jnp.swapaxes(b_t, 0, 1)))
    hs = jnp.swapaxes(hs, 0, 1).astype(a.dtype)
    y = (jax.nn.gelu(gate) * hs) @ w_out
    return y, new_buf, h_last.astype(a.dtype)


def conv_ffn(a, buf, w_up, w_conv, b_conv, w_down):
    gate, val = jnp.split(a @ w_up, 2, axis=-1)
    gate_c, new_buf = causal_dwconv(gate, buf, w_conv, b_conv)
    return (jax.nn.gelu(gate_c) * val) @ w_down, new_buf


def setup_inputs(seed: int = 0) -> dict:
    key = jax.random.key(seed)
    ks = jax.random.split(key, 32)
    n_pages = PAST_LEN // PAGE_SIZE
    n_used = DEC_BATCH * n_pages
    n_phys = n_used + max(1, n_used // 4)
    f32 = jnp.float32
    nrm = lambda k, shape, scale: jax.random.normal(k, shape, f32) * scale
    lru_base = jax.random.uniform(ks[21], (N_REC_LAYERS, LRU_WIDTH), f32, 0.9, 0.999)
    return {
        "x_prompt": nrm(ks[0], (BATCH, SEQ, D_MODEL), 1.0),
        "x_sample": nrm(ks[1], (DEC_BATCH, DEC_SEQ, D_MODEL), 1.0),
        "cache_k": nrm(ks[2], (N_ATTN_LAYERS, n_phys, PAGE_SIZE, N_HEADS, HEAD_DIM), 1.0),
        "cache_v": nrm(ks[3], (N_ATTN_LAYERS, n_phys, PAGE_SIZE, N_HEADS, HEAD_DIM), 1.0),
        "cache_kidx": nrm(ks[4], (N_ATTN_LAYERS, n_phys, PAGE_SIZE, IDX_DIM), 1.0),
        "state_lru_h": nrm(ks[5], (N_REC_LAYERS, DEC_BATCH, LRU_WIDTH), 0.5),
        "state_lru_conv": nrm(ks[6], (N_REC_LAYERS, DEC_BATCH, REC_CONV_W - 1, LRU_WIDTH), 1.0),
        "state_ffn_conv": nrm(ks[7], (DEPTH, DEC_BATCH, FFN_CONV_W - 1, D_FF), 1.0),
        "page_table": jax.random.permutation(ks[8], n_phys)[:n_used].reshape(DEC_BATCH, n_pages).astype(jnp.int32),
        "norm_mix": 1.0 + nrm(ks[9], (DEPTH, D_MODEL), 0.02),
        "norm_ffn": 1.0 + nrm(ks[10], (DEPTH, D_MODEL), 0.02),
        "norm_final": 1.0 + nrm(ks[11], (D_MODEL,), 0.02),
        "w_attn_in": nrm(ks[12], (N_ATTN_LAYERS, D_MODEL, ATTN_IN_W), D_MODEL ** -0.5),
        "w_attn_out": nrm(ks[13], (N_ATTN_LAYERS, QKV_W, D_MODEL), QKV_W ** -0.5),
        "w_rec_in": nrm(ks[14], (N_REC_LAYERS, D_MODEL, 2 * LRU_WIDTH), D_MODEL ** -0.5),
        "w_rec_conv": nrm(ks[15], (N_REC_LAYERS, REC_CONV_W, LRU_WIDTH), REC_CONV_W ** -0.5),
        "b_rec_conv": nrm(ks[16], (N_REC_LAYERS, LRU_WIDTH), 0.01),
        "w_gate_a": nrm(ks[17], (N_REC_LAYERS, LRU_BLOCKS, LRU_BLOCK_W, LRU_BLOCK_W), LRU_BLOCK_W ** -0.5),
        "b_gate_a": nrm(ks[18], (N_REC_LAYERS, LRU_WIDTH), 0.01),
        "w_gate_x": nrm(ks[19], (N_REC_LAYERS, LRU_BLOCKS, LRU_BLOCK_W, LRU_BLOCK_W), LRU_BLOCK_W ** -0.5),
        "b_gate_x": nrm(ks[20], (N_REC_LAYERS, LRU_WIDTH), 0.01),
        "lru_lambda": jnp.log(lru_base) - jnp.log1p(-lru_base),
        "w_rec_out": nrm(ks[22], (N_REC_LAYERS, LRU_WIDTH, D_MODEL), LRU_WIDTH ** -0.5),
        "w_ffn_up": nrm(ks[23], (DEPTH, D_MODEL, 2 * D_FF), D_MODEL ** -0.5),
        "w_ffn_conv": nrm(ks[24], (DEPTH, FFN_CONV_W, D_FF), FFN_CONV_W ** -0.5),
        "b_ffn_conv": nrm(ks[25], (DEPTH, D_FF), 0.01),
        "w_ffn_down": nrm(ks[26], (DEPTH, D_FF, D_MODEL), D_FF ** -0.5),
    }


def reference(x_prompt, x_sample, cache_k, cache_v, cache_kidx, state_lru_h, state_lru_conv, state_ffn_conv,
              page_table, norm_mix, norm_ffn, norm_final, w_attn_in, w_attn_out, w_rec_in, w_rec_conv,
              b_rec_conv, w_gate_a, b_gate_a, w_gate_x, b_gate_x, lru_lambda, w_rec_out, w_ffn_up,
              w_ffn_conv, b_ffn_conv, w_ffn_down):
    bsz, s, _ = x_prompt.shape
    db, t, _ = x_sample.shape
    past = page_table.shape[1] * PAGE_SIZE
    pos_p = jnp.arange(s)
    pos_s = past + jnp.arange(t)
    hp, hs = x_prompt, x_sample
    kp, vp, kip, ksm, vsm, kism = [], [], [], [], [], []
    lhp, lcp, lhs, lcs = [], [], [], []
    fcp, fcs = [], []
    for i in range(DEPTH):
        j = i // 2
        ap = rmsnorm(hp, norm_mix[i])
        asm = rmsnorm(hs, norm_mix[i])
        if i % 2 == 0:
            q, k, v, qi, ki, wi = attn_project(ap, pos_p, w_attn_in[j])
            o = dsa_prompt(q, k, v, qi, ki, wi)
            hp = hp + o.reshape(bsz, s, QKV_W) @ w_attn_out[j]
            kp.append(k); vp.append(v); kip.append(ki)
            q, k, v, qi, ki, wi = attn_project(asm, pos_s, w_attn_in[j])
            o = dsa_sample(q, k, v, qi, ki, wi, cache_k, cache_v, cache_kidx, j, page_table)
            hs = hs + o.reshape(db, t, QKV_W) @ w_attn_out[j]
            ksm.append(k); vsm.append(v); kism.append(ki)
        else:
            rec = (w_rec_in[j], w_rec_conv[j], b_rec_conv[j], w_gate_a[j], b_gate_a[j],
                   w_gate_x[j], b_gate_x[j], lru_lambda[j], w_rec_out[j])
            o, cb, hl = rglru_mixer(ap, jnp.zeros((bsz, REC_CONV_W - 1, LRU_WIDTH), ap.dtype),
                                    jnp.zeros((bsz, LRU_WIDTH), ap.dtype), *rec)
            hp = hp + o
            lhp.append(hl); lcp.append(cb)
            o, cb, hl = rglru_mixer(asm, state_lru_conv[j], state_lru_h[j], *rec)
            hs = hs + o
            lhs.append(hl); lcs.append(cb)
        ffn = (w_ffn_up[i], w_ffn_conv[i], b_ffn_conv[i], w_ffn_down[i])
        o, fb = conv_ffn(rmsnorm(hp, norm_ffn[i]), jnp.zeros((bsz, FFN_CONV_W - 1, D_FF), hp.dtype), *ffn)
        hp = hp + o
        fcp.append(fb)
        o, fb = conv_ffn(rmsnorm(hs, norm_ffn[i]), state_ffn_conv[i], *ffn)
        hs = hs + o
        fcs.append(fb)
    y_prompt = rmsnorm(hp, norm_final)
    y_sample = rmsnorm(hs, norm_final)
    new_k_prompt = jnp.stack(kp)
    new_v_prompt = jnp.stack(vp)
    new_kidx_prompt = jnp.stack(kip)
    new_k_sample = jnp.stack(ksm)
    new_v_sample = jnp.stack(vsm)
    new_kidx_sample = jnp.stack(kism)
    lru_h_prompt = jnp.stack(lhp)
    lru_conv_prompt = jnp.stack(lcp)
    lru_h_sample = jnp.stack(lhs)
    lru_conv_sample = jnp.stack(lcs)
    ffn_conv_prompt = jnp.stack(fcp)
    ffn_conv_sample = jnp.stack(fcs)
    return (y_prompt, y_sample, new_k_prompt, new_v_prompt, new_kidx_prompt, new_k_sample, new_v_sample,
            new_kidx_sample, lru_h_prompt, lru_conv_prompt, lru_h_sample, lru_conv_sample,
            ffn_conv_prompt, ffn_conv_sample)
```

```python
import functools

import jax
import jax.numpy as jnp
from jax import lax
from jax.experimental import pallas as pl
from jax.experimental.pallas import tpu as pltpu

F32 = jnp.float32
BF16 = jnp.bfloat16
I32 = jnp.int32

D_MODEL = 1024
N_HEADS = 8
HEAD_DIM = 128
IDX_HEADS = 4
IDX_DIM = 64
TOPK = 256
PAGE = 128
ROPE_THETA = 10000.0
IDX_SCALE = (IDX_DIM * IDX_HEADS) ** -0.5
Q_SCALE = HEAD_DIM ** -0.5
RG_C = 8.0
D_FF = 2816
RMS_EPS = 1e-6

LANES = 128
SUBLANES = 8
MXU_N = 256
IDX_W = (IDX_HEADS + 1) * LANES
INT_MIN = -(2 ** 31)
NEG = -1e30
VMEM_LIMIT = 52 * 1024 * 1024

ROW_BLOCK = 512
TQ = 256
TK = 256
FF_CHUNK = 256
N_FF_CHUNKS = D_FF // FF_CHUNK
NT_DIMS = (((1,), (1,)), ((), ()))


def _cparams(*sem):
    return pltpu.CompilerParams(dimension_semantics=sem if sem else None, vmem_limit_bytes=VMEM_LIMIT)


def _const_spec(shape):
    nd = len(shape)
    return pl.BlockSpec(shape, lambda *_: (0,) * nd, pipeline_mode=pl.Buffered(1))


def _rms(x, g):
    return x * lax.rsqrt(jnp.mean(x * x, axis=-1, keepdims=True) + RMS_EPS) * g


def _shift_rows(x, s, prev_rows):
    r = pltpu.roll(x, s, 0)
    top = r[:SUBLANES]
    row = lax.broadcasted_iota(I32, top.shape, 0)
    for k in range(s):
        top = jnp.where(row == k, prev_rows[k], top)
    return jnp.concatenate([top, r[SUBLANES:]], axis=0)


def _sortable_key(score):
    bits = pltpu.bitcast(score + 0.0, I32)
    return jnp.where(bits < 0, bits ^ jnp.int32(0x7FFFFFFF), bits)


def _attn_in_kernel(x_ref, g_ref, wqk_ref, wv_ref, widx_ref, cos_ref, sin_ref, icos_ref, isa_ref, isb_ref,
                    q_ref, kf_ref, kb_ref, vf_ref, vb_ref, qi_ref, kiw_ref, a_scr):
    a_scr[...] = _rms(x_ref[...], g_ref[...]).astype(BF16)
    cos = cos_ref[...]
    sin = sin_ref[...]
    for c in range(2 * D_MODEL // MXU_N):
        r = jnp.dot(a_scr[...], wqk_ref[:, c * MXU_N:(c + 1) * MXU_N], preferred_element_type=F32)
        for hh in range(MXU_N // HEAD_DIM):
            xh = r[:, hh * HEAD_DIM:(hh + 1) * HEAD_DIM]
            y = xh * cos + pltpu.roll(xh, HEAD_DIM // 2, 1) * sin
            col = (c * MXU_N) % D_MODEL + hh * HEAD_DIM
            if c < D_MODEL // MXU_N:
                q_ref[:, col:col + HEAD_DIM] = (y * Q_SCALE).astype(BF16)
            else:
                kf_ref[:, col:col + HEAD_DIM] = y
                kb_ref[:, col:col + HEAD_DIM] = y.astype(BF16)
    for c in range(D_MODEL // MXU_N):
        r = jnp.dot(a_scr[...], wv_ref[:, c * MXU_N:(c + 1) * MXU_N], preferred_element_type=F32)
        vf_ref[:, c * MXU_N:(c + 1) * MXU_N] = r
        vb_ref[:, c * MXU_N:(c + 1) * MXU_N] = r.astype(BF16)
    ri = jnp.dot(a_scr[...], widx_ref[...], preferred_element_type=F32)
    yi = (ri * icos_ref[...] + pltpu.roll(ri, IDX_W - IDX_DIM // 2, 1) * isa_ref[...]
          + pltpu.roll(ri, IDX_DIM // 2, 1) * isb_ref[...])
    qi_ref[...] = yi[:, :IDX_HEADS * LANES].astype(BF16)
    kiw_ref[...] = yi[:, IDX_HEADS * LANES:]


def _attn_in(x, g, wqk, wv, widx, tabs, rows, n_pos_blocks):
    n = x.shape[0]
    cos, sin, icos, isa, isb = tabs
    row_spec = lambda w: pl.BlockSpec((rows, w), lambda i: (i, 0))
    tab_spec = lambda w: pl.BlockSpec((rows, w), lambda i: (i % n_pos_blocks, 0))
    return pl.pallas_call(
        _attn_in_kernel,
        grid=(n // rows,),
        in_specs=[row_spec(D_MODEL), _const_spec((1, D_MODEL)), _const_spec(wqk.shape), _const_spec(wv.shape),
                  _const_spec(widx.shape), tab_spec(HEAD_DIM), tab_spec(HEAD_DIM), tab_spec(IDX_W),
                  tab_spec(IDX_W), tab_spec(IDX_W)],
        out_specs=[row_spec(D_MODEL)] * 5 + [row_spec(IDX_HEADS * LANES), row_spec(LANES)],
        out_shape=[jax.ShapeDtypeStruct((n, D_MODEL), BF16), jax.ShapeDtypeStruct((n, D_MODEL), F32),
                   jax.ShapeDtypeStruct((n, D_MODEL), BF16), jax.ShapeDtypeStruct((n, D_MODEL), F32),
                   jax.ShapeDtypeStruct((n, D_MODEL), BF16), jax.ShapeDtypeStruct((n, IDX_HEADS * LANES), BF16),
                   jax.ShapeDtypeStruct((n, LANES), F32)],
        scratch_shapes=[pltpu.VMEM((rows, D_MODEL), BF16)],
        compiler_params=_cparams("arbitrary"),
        name="attn_in",
    )(x, g, wqk, wv, widx, cos, sin, icos, isa, isb)


def _kth_largest_key(count_ge):
    kf = float(TOPK)
    t = jnp.where(count_ge(0) >= kf, jnp.int32(0), jnp.int32(INT_MIN))

    def bit_body(it, t):
        cand = t + (jnp.int32(1) << (30 - it))
        return jnp.where(count_ge(cand) >= kf, cand, t)

    return lax.fori_loop(0, 31, bit_body, t)


def _tie_cutoff(count_tie_le, need, n_bits):
    def bit_body(it, c):
        cand = c + (jnp.int32(1) << (n_bits - 1 - it))
        return jnp.where(count_tie_le(cand - 1) < need, cand, c)

    return lax.fori_loop(0, n_bits, bit_body, jnp.zeros_like(need, dtype=I32))


def _dsa_prompt_kernel(q_ref, kb_ref, vb_ref, qi_ref, kiwk_ref, kiwq_ref, x_ref, wo_ref, out_ref,
                       key_scr, eq_scr, acc_scr, m_scr, l_scr, kib_scr, o_scr):
    i = pl.program_id(1)
    nk = i + 1

    @pl.when(i == 0)
    def _():
        kib_scr[...] = kiwk_ref[...].astype(BF16)

    wi = kiwq_ref[:, IDX_DIM:IDX_DIM + IDX_HEADS]
    qpos = i * TQ + lax.broadcasted_iota(I32, (TQ, 1), 0)

    def chunk(kc):
        return pl.ds(pl.multiple_of(kc * TK, TK), TK)

    def spos(kc):
        return kc * TK + lax.broadcasted_iota(I32, (1, TK), 1)

    def score_body(kc, carry):
        kic = kib_scr[chunk(kc), :]
        sc = jnp.zeros((TQ, TK), F32)
        for h in range(IDX_HEADS):
            s = lax.dot_general(qi_ref[:, h * LANES:(h + 1) * LANES], kic, NT_DIMS, preferred_element_type=F32)
            sc = sc + wi[:, h:h + 1] * jnp.maximum(s, 0.0)
        key = _sortable_key(sc * IDX_SCALE)
        key_scr[:, chunk(kc)] = jnp.where(spos(kc) <= qpos, key, jnp.int32(INT_MIN))
        return carry

    lax.fori_loop(0, nk, score_body, 0)

    def count_ge(cand):
        def body(kc, acc):
            ind = jnp.where(key_scr[:, chunk(kc)] >= cand, 1.0, 0.0)
            return acc + ind[:, :LANES] + ind[:, LANES:]
        acc = lax.fori_loop(0, nk, body, jnp.zeros((TQ, LANES), F32))
        return jnp.sum(acc, axis=-1, keepdims=True)

    thr = _kth_largest_key(count_ge)
    need = float(TOPK) - count_ge(thr + 1)

    def eq_body(kc, carry):
        eq_scr[:, chunk(kc)] = jnp.where(key_scr[:, chunk(kc)] == thr, 1.0, 0.0)
        return carry

    lax.fori_loop(0, nk, eq_body, 0)

    def count_tie_le(cm):
        def body(kc, acc):
            ind = jnp.where(spos(kc) <= cm, eq_scr[:, chunk(kc)], 0.0)
            return acc + ind[:, :LANES] + ind[:, LANES:]
        acc = lax.fori_loop(0, nk, body, jnp.zeros((TQ, LANES), F32))
        return jnp.sum(acc, axis=-1, keepdims=True)

    cut = _tie_cutoff(count_tie_le, need, 11)
    cut = jnp.where(thr == INT_MIN, jnp.int32(-1), cut)

    m_scr[...] = jnp.full(m_scr.shape, NEG, F32)
    l_scr[...] = jnp.zeros(l_scr.shape, F32)
    acc_scr[...] = jnp.zeros(acc_scr.shape, F32)

    def att_body(kc, carry):
        sel = (key_scr[:, chunk(kc)] > thr) | ((eq_scr[:, chunk(kc)] > 0.0) & (spos(kc) <= cut))
        bias = jnp.where(sel, 0.0, NEG)
        for h in range(N_HEADS):
            hs = slice(h * HEAD_DIM, (h + 1) * HEAD_DIM)
            s = lax.dot_general(q_ref[:, hs], kb_ref[chunk(kc), hs], NT_DIMS, preferred_element_type=F32) + bias
            m_prev = m_scr[h][:, :1]
            l_prev = l_scr[h][:, :1]
            m_new = jnp.maximum(m_prev, jnp.max(s, axis=-1, keepdims=True))
            alpha = jnp.exp(m_prev - m_new)
            p = jnp.exp(s - m_new)
            l_scr[h] = jnp.broadcast_to(alpha * l_prev + jnp.sum(p, axis=-1, keepdims=True), (TQ, LANES))
            m_scr[h] = jnp.broadcast_to(m_new, (TQ, LANES))
            pv = jnp.dot(p.astype(BF16), vb_ref[chunk(kc), hs], preferred_element_type=F32)
            acc_scr[:, hs] = alpha * acc_scr[:, hs] + pv
        return carry

    lax.fori_loop(0, nk, att_body, 0)

    for h in range(N_HEADS):
        hs = slice(h * HEAD_DIM, (h + 1) * HEAD_DIM)
        o_scr[:, hs] = (acc_scr[:, hs] / l_scr[h][:, :1]).astype(BF16)
    out_ref[...] = x_ref[...] + jnp.dot(o_scr[...], wo_ref[...], preferred_element_type=F32)


def _dsa_prompt(q, kb, vb, qi, kiw, x, wo, bsz, seq):
    n = x.shape[0]
    nq = seq // TQ
    qrow = lambda w: pl.BlockSpec((TQ, w), lambda b, i: (b * nq + i, 0))
    brow = lambda w: pl.BlockSpec((seq, w), lambda b, i: (b, 0))
    return pl.pallas_call(
        _dsa_prompt_kernel,
        grid=(bsz, nq),
        in_specs=[qrow(D_MODEL), brow(D_MODEL), brow(D_MODEL), qrow(IDX_HEADS * LANES), brow(LANES), qrow(LANES),
                  qrow(D_MODEL), _const_spec(wo.shape)],
        out_specs=qrow(D_MODEL),
        out_shape=jax.ShapeDtypeStruct((n, D_MODEL), F32),
        scratch_shapes=[pltpu.VMEM((TQ, seq), I32), pltpu.VMEM((TQ, seq), F32), pltpu.VMEM((TQ, D_MODEL), F32),
                        pltpu.VMEM((N_HEADS, TQ, LANES), F32), pltpu.VMEM((N_HEADS, TQ, LANES), F32),
                        pltpu.VMEM((seq, LANES), BF16), pltpu.VMEM((TQ, D_MODEL), BF16)],
        compiler_params=_cparams("arbitrary", "arbitrary"),
        name="dsa_prompt",
    )(q, kb, vb, qi, kiw, kiw, x, wo)


def _glu_hidden(a_scr, wup_ref, wc_ref, bc_ref, h_scr, gate_taps):
    for c in range(N_FF_CHUNKS):
        cs = slice(c * FF_CHUNK, (c + 1) * FF_CHUNK)
        gate = jnp.dot(a_scr[...], wup_ref[:, cs], preferred_element_type=F32)
        val = jnp.dot(a_scr[...], wup_ref[:, D_FF + c * FF_CHUNK:D_FF + (c + 1) * FF_CHUNK],
                      preferred_element_type=F32)
        g2, g1 = gate_taps(c, gate)
        gc = bc_ref[:, cs] + wc_ref[0:1, cs] * g2 + wc_ref[1:2, cs] * g1 + wc_ref[2:3, cs] * gate
        h_scr[:, cs] = (jax.nn.gelu(gc) * val).astype(BF16)


def _ffn_prompt_kernel(x_ref, g_ref, wup_ref, wc_ref, bc_ref, wdn_ref, gf_ref, out_ref, tail_ref,
                       a_scr, h_scr, carry_scr, *, final_norm):
    j = pl.program_id(1)
    rows = x_ref.shape[0]
    a_scr[...] = _rms(x_ref[...], g_ref[...]).astype(BF16)

    @pl.when(j == 0)
    def _():
        carry_scr[...] = jnp.zeros(carry_scr.shape, F32)

    def gate_taps(c, gate):
        cs = slice(c * FF_CHUNK, (c + 1) * FF_CHUNK)
        p0 = carry_scr[SUBLANES - 2:SUBLANES - 1, cs]
        p1 = carry_scr[SUBLANES - 1:SUBLANES, cs]
        g1 = _shift_rows(gate, 1, [p1])
        g2 = _shift_rows(gate, 2, [p0, p1])
        carry_scr[:, cs] = gate[rows - SUBLANES:, :]
        tail_ref[0, :, cs] = gate[rows - SUBLANES:, :]
        return g2, g1

    _glu_hidden(a_scr, wup_ref, wc_ref, bc_ref, h_scr, gate_taps)
    y = x_ref[...] + jnp.dot(h_scr[...], wdn_ref[...], preferred_element_type=F32)
    out_ref[...] = _rms(y, gf_ref[...]) if final_norm else y


def _ffn_prompt(x, g, wup, wc, bc, wdn, gf, bsz, seq, final_norm):
    n = x.shape[0]
    nb = seq // ROW_BLOCK
    row = pl.BlockSpec((ROW_BLOCK, D_MODEL), lambda b, j: (b * nb + j, 0))
    return pl.pallas_call(
        functools.partial(_ffn_prompt_kernel, final_norm=final_norm),
        grid=(bsz, nb),
        in_specs=[row, _const_spec((1, D_MODEL)), _const_spec(wup.shape), _const_spec(wc.shape),
                  _const_spec(bc.shape), _const_spec(wdn.shape), _const_spec((1, D_MODEL))],
        out_specs=[row, pl.BlockSpec((1, SUBLANES, D_FF), lambda b, j: (b, 0, 0))],
        out_shape=[jax.ShapeDtypeStruct((n, D_MODEL), F32), jax.ShapeDtypeStruct((bsz, SUBLANES, D_FF), F32)],
        scratch_shapes=[pltpu.VMEM((ROW_BLOCK, D_MODEL), BF16), pltpu.VMEM((ROW_BLOCK, D_FF), BF16),
                        pltpu.VMEM((SUBLANES, D_FF), F32)],
        compiler_params=_cparams("arbitrary", "arbitrary"),
        name="ffn_prompt_final" if final_norm else "ffn_prompt",
    )(x, g, wup, wc, bc, wdn, gf)


def _expm1(x):
    u = jnp.exp(x)
    safe = (u - 1.0) * x / jnp.log(u)
    return jnp.where(u == 1.0, x, jnp.where(u == 0.0, -1.0, safe))


def _lru_coeffs(xc, gg, bga, bgx, lam):
    r = jax.nn.sigmoid(gg[:, :LANES] + bga)
    ig = jax.nn.sigmoid(gg[:, LANES:] + bgx)
    log_a = -RG_C * r * jax.nn.softplus(-lam)
    return jnp.exp(log_a), jnp.sqrt(-_expm1(2.0 * log_a)) * ig * xc


def _rec_prompt_kernel(x_ref, g_ref, win_ref, wc_ref, bc_ref, wg_ref, bga_ref, bgx_ref, lam_ref, wo_ref,
                       out_ref, hlast_ref, ctail_ref, a_scr, at_scr, bt_scr, u_scr, hcar_scr, ccar_scr):
    j = pl.program_id(1)
    rows = x_ref.shape[0]
    a_scr[...] = _rms(x_ref[...], g_ref[...]).astype(BF16)

    @pl.when(j == 0)
    def _():
        ccar_scr[...] = jnp.zeros(ccar_scr.shape, F32)
        hcar_scr[...] = jnp.zeros(hcar_scr.shape, F32)

    nblk = MXU_N // LANES
    for c in range(D_MODEL // MXU_N):
        cs = slice(c * MXU_N, (c + 1) * MXU_N)
        xb = jnp.dot(a_scr[...], win_ref[:, D_MODEL + c * MXU_N:D_MODEL + (c + 1) * MXU_N],
                     preferred_element_type=F32)
        prev = [ccar_scr[SUBLANES - 3 + k:SUBLANES - 2 + k, cs] for k in range(3)]
        xc = (bc_ref[:, cs] + wc_ref[0:1, cs] * _shift_rows(xb, 3, prev) + wc_ref[1:2, cs] * _shift_rows(xb, 2, prev[1:])
              + wc_ref[2:3, cs] * _shift_rows(xb, 1, prev[2:]) + wc_ref[3:4, cs] * xb)
        ccar_scr[:, cs] = xb[rows - SUBLANES:, :]
        ctail_ref[0, :, cs] = xb[rows - SUBLANES:, :]
        for k in range(nblk):
            n = c * nblk + k
            ls = slice(n * LANES, (n + 1) * LANES)
            xcn = xc[:, k * LANES:(k + 1) * LANES]
            gg = jnp.dot(xcn.astype(BF16), wg_ref[n], preferred_element_type=F32)
            a_t, b_t = _lru_coeffs(xcn, gg, bga_ref[:, ls], bgx_ref[:, ls], lam_ref[:, ls])
            at_scr[:, ls] = a_t
            bt_scr[:, ls] = b_t

    def step(t, h):
        h = at_scr[pl.ds(t, 1), :] * h + bt_scr[pl.ds(t, 1), :]
        bt_scr[pl.ds(t, 1), :] = h
        return h

    h_last = lax.fori_loop(0, rows, step, hcar_scr[0:1, :], unroll=8)
    hcar_scr[...] = jnp.broadcast_to(h_last, hcar_scr.shape)
    hlast_ref[0] = jnp.broadcast_to(h_last, hcar_scr.shape)
    for c in range(D_MODEL // MXU_N):
        cs = slice(c * MXU_N, (c + 1) * MXU_N)
        gate = jnp.dot(a_scr[...], win_ref[:, cs], preferred_element_type=F32)
        u_scr[:, cs] = (jax.nn.gelu(gate) * bt_scr[:, cs]).astype(BF16)
    out_ref[...] = x_ref[...] + jnp.dot(u_scr[...], wo_ref[...], preferred_element_type=F32)


def _rec_prompt(x, g, win, wc, bc, wg, bga, bgx, lam, wo, bsz, seq):
    n = x.shape[0]
    nb = seq // ROW_BLOCK
    row = pl.BlockSpec((ROW_BLOCK, D_MODEL), lambda b, j: (b * nb + j, 0))
    tail = pl.BlockSpec((1, SUBLANES, D_MODEL), lambda b, j: (b, 0, 0))
    vec = _const_spec((1, D_MODEL))
    return pl.pallas_call(
        _rec_prompt_kernel,
        grid=(bsz, nb),
        in_specs=[row, vec, _const_spec(win.shape), _const_spec(wc.shape), vec, _const_spec(wg.shape), vec, vec,
                  vec, _const_spec(wo.shape)],
        out_specs=[row, tail, tail],
        out_shape=[jax.ShapeDtypeStruct((n, D_MODEL), F32), jax.ShapeDtypeStruct((bsz, SUBLANES, D_MODEL), F32),
                   jax.ShapeDtypeStruct((bsz, SUBLANES, D_MODEL), F32)],
        scratch_shapes=[pltpu.VMEM((ROW_BLOCK, D_MODEL), BF16), pltpu.VMEM((ROW_BLOCK, D_MODEL), F32),
                        pltpu.VMEM((ROW_BLOCK, D_MODEL), F32), pltpu.VMEM((ROW_BLOCK, D_MODEL), BF16),
                        pltpu.VMEM((SUBLANES, D_MODEL), F32), pltpu.VMEM((SUBLANES, D_MODEL), F32)],
        compiler_params=_cparams("arbitrary", "arbitrary"),
        name="rec_prompt",
    )(x, g, win, wc, bc, wg, bga, bgx, lam, wo)


def _idx_sample_kernel(pt_ref, qi_ref, wi_ref, kin_ref, *refs, n_pages):
    pages, out_ref = refs[:n_pages], refs[n_pages]
    qi = qi_ref[...]
    wi = wi_ref[...]
    for p in range(n_pages):
        s = lax.dot_general(qi[:, :IDX_DIM], pages[p][...].astype(BF16), NT_DIMS, preferred_element_type=F32)
        out_ref[:, p * PAGE:(p + 1) * PAGE] = (
            jnp.sum(wi * jnp.maximum(s, 0.0), axis=0, keepdims=True) * IDX_SCALE)
    s_new = jnp.sum(qi.astype(F32) * kin_ref[...].astype(BF16).astype(F32), axis=-1, keepdims=True)
    sc_new = jnp.sum(wi * jnp.maximum(s_new, 0.0), axis=0, keepdims=True) * IDX_SCALE
    lane = lax.broadcasted_iota(I32, (1, LANES), 1)
    out_ref[:, n_pages * PAGE:] = jnp.where(lane == 0, sc_new, -jnp.inf)


def _idx_sample(pt, qi3, wi3, kin3, pool_ki, n_pages):
    db = qi3.shape[0]
    per_seq = lambda shape: pl.BlockSpec((None,) + shape, lambda d, pt: (d, 0, 0))
    page_spec = lambda p: pl.BlockSpec((None, PAGE, IDX_DIM), lambda d, pt: (pt[d * n_pages + p], 0, 0))
    width = n_pages * PAGE + LANES
    return pl.pallas_call(
        functools.partial(_idx_sample_kernel, n_pages=n_pages),
        grid_spec=pltpu.PrefetchScalarGridSpec(
            num_scalar_prefetch=1, grid=(db,),
            in_specs=[per_seq((IDX_HEADS, LANES)), per_seq((IDX_HEADS, 1)), per_seq((1, LANES))]
            + [page_spec(p) for p in range(n_pages)],
            out_specs=per_seq((1, width))),
        out_shape=jax.ShapeDtypeStruct((db, 1, width), F32),
        compiler_params=_cparams("arbitrary"),
        name="idx_sample",
    )(pt, qi3, wi3, kin3, *([pool_ki] * n_pages))


def _select_sample_kernel(sc_ref, bias_ref, key_scr, *, n_keys):
    lane = lax.broadcasted_iota(I32, (1, sc_ref.shape[1]), 1)
    key_scr[...] = jnp.where(lane < n_keys, _sortable_key(sc_ref[...]), jnp.int32(INT_MIN))

    def count_ge(cand):
        return jnp.sum(jnp.where(key_scr[...] >= cand, 1.0, 0.0), axis=-1, keepdims=True)

    thr = _kth_largest_key(count_ge)
    need = float(TOPK) - count_ge(thr + 1)

    def count_tie_le(cm):
        return jnp.sum(jnp.where((key_scr[...] == thr) & (lane <= cm), 1.0, 0.0), axis=-1, keepdims=True)

    cut = _tie_cutoff(count_tie_le, need, 12)
    cut = jnp.where(thr == INT_MIN, jnp.int32(-1), cut)
    key = key_scr[...]
    sel = (key > thr) | ((key == thr) & (lane <= cut))
    bias_ref[...] = jnp.where(sel, 0.0, NEG)


def _select_sample(scores, n_keys):
    return pl.pallas_call(
        functools.partial(_select_sample_kernel, n_keys=n_keys),
        out_shape=jax.ShapeDtypeStruct(scores.shape, F32),
        scratch_shapes=[pltpu.VMEM(scores.shape, I32)],
        compiler_params=_cparams(),
        name="select_sample",
    )(scores)


def _attn_sample_kernel(pt_ref, q_ref, kn_ref, vn_ref, bias_ref, *refs, n_pages):
    kpages, vpages = refs[:n_pages], refs[n_pages:2 * n_pages]
    o_ref, kall, vall = refs[2 * n_pages:]
    past = n_pages * PAGE
    for p in range(n_pages):
        kall[p * PAGE:(p + 1) * PAGE, :] = kpages[p][...].astype(BF16)
        vall[p * PAGE:(p + 1) * PAGE, :] = vpages[p][...].astype(BF16)
    row = lax.broadcasted_iota(I32, (LANES, D_MODEL), 0)
    kall[past:, :] = jnp.where(row == 0, kn_ref[...], 0.0).astype(BF16)
    vall[past:, :] = jnp.where(row == 0, vn_ref[...], 0.0).astype(BF16)
    head_of_lane = lax.broadcasted_iota(I32, (N_HEADS, D_MODEL), 1) >> 7
    own = head_of_lane == lax.broadcasted_iota(I32, (N_HEADS, D_MODEL), 0)
    qbd = jnp.where(own, q_ref[...].astype(F32), 0.0).astype(BF16)
    s = lax.dot_general(qbd, kall[...], NT_DIMS, preferred_element_type=F32) + bias_ref[...]
    m = jnp.max(s, axis=-1, keepdims=True)
    p = jnp.exp(s - m)
    l = jnp.sum(p, axis=-1, keepdims=True)
    o_full = jnp.dot(p.astype(BF16), vall[...], preferred_element_type=F32) / l
    o_ref[...] = jnp.sum(jnp.where(own, o_full, 0.0), axis=0, keepdims=True)


def _attn_sample(pt, q3, kn3, vn3, bias3, pool_k, pool_v, n_pages):
    db = q3.shape[0]
    per_seq = lambda w: pl.BlockSpec((None, 1, w), lambda d, pt: (d, 0, 0))
    page_spec = lambda p: pl.BlockSpec((None, PAGE, D_MODEL), lambda d, pt: (pt[d * n_pages + p], 0, 0))
    width = n_pages * PAGE + LANES
    return pl.pallas_call(
        functools.partial(_attn_sample_kernel, n_pages=n_pages),
        grid_spec=pltpu.PrefetchScalarGridSpec(
            num_scalar_prefetch=1, grid=(db,),
            in_specs=[per_seq(D_MODEL), per_seq(D_MODEL), per_seq(D_MODEL), per_seq(width)]
            + [page_spec(p) for p in range(n_pages)] * 2,
            out_specs=per_seq(D_MODEL),
            scratch_shapes=[pltpu.VMEM((width, D_MODEL), BF16), pltpu.VMEM((width, D_MODEL), BF16)]),
        out_shape=jax.ShapeDtypeStruct((db, 1, D_MODEL), F32),
        compiler_params=_cparams("arbitrary"),
        name="attn_sample",
    )(pt, q3, kn3, vn3, bias3, *([pool_k] * n_pages), *([pool_v] * n_pages))


def _ffn_sample_body(x, g_ref, wup_ref, wc_ref, bc_ref, wdn_ref, st0_ref, st1_ref, gate_ref, a_scr, h_scr):
    a_scr[...] = _rms(x, g_ref[...]).astype(BF16)

    def gate_taps(c, gate):
        cs = slice(c * FF_CHUNK, (c + 1) * FF_CHUNK)
        gate_ref[:, cs] = gate
        return st0_ref[:, cs], st1_ref[:, cs]

    _glu_hidden(a_scr, wup_ref, wc_ref, bc_ref, h_scr, gate_taps)
    return x + jnp.dot(h_scr[...], wdn_ref[...], preferred_element_type=F32)


def _layer0_sample_kernel(x_ref, o_ref, wo_ref, g_ref, wup_ref, wc_ref, bc_ref, wdn_ref, st0_ref, st1_ref,
                          out_ref, gate_ref, a_scr, h_scr):
    x = x_ref[...] + jnp.dot(o_ref[...].astype(BF16), wo_ref[...], preferred_element_type=F32)
    out_ref[...] = _ffn_sample_body(x, g_ref, wup_ref, wc_ref, bc_ref, wdn_ref, st0_ref, st1_ref, gate_ref,
                                    a_scr, h_scr)


def _layer1_sample_kernel(x_ref, gm_ref, win_ref, wcr_ref, bcr_ref, wg_ref, bga_ref, bgx_ref, lam_ref, wor_ref,
                          cs0_ref, cs1_ref, cs2_ref, h0_ref, g_ref, wup_ref, wc_ref, bc_ref, wdn_ref, st0_ref,
                          st1_ref, gf_ref, out_ref, hnew_ref, xb_ref, gate_ref, a_scr, h_scr, u_scr):
    x = x_ref[...]
    a_scr[...] = _rms(x, gm_ref[...]).astype(BF16)
    nblk = MXU_N // LANES
    for c in range(D_MODEL // MXU_N):
        cs = slice(c * MXU_N, (c + 1) * MXU_N)
        xb = jnp.dot(a_scr[...], win_ref[:, D_MODEL + c * MXU_N:D_MODEL + (c + 1) * MXU_N],
                     preferred_element_type=F32)
        gate = jnp.dot(a_scr[...], win_ref[:, cs], preferred_element_type=F32)
        xb_ref[:, cs] = xb
        xc = (bcr_ref[:, cs] + wcr_ref[0:1, cs] * cs0_ref[:, cs] + wcr_ref[1:2, cs] * cs1_ref[:, cs]
              + wcr_ref[2:3, cs] * cs2_ref[:, cs] + wcr_ref[3:4, cs] * xb)
        for k in range(nblk):
            n = c * nblk + k
            ls = slice(n * LANES, (n + 1) * LANES)
            xcn = xc[:, k * LANES:(k + 1) * LANES]
            gg = jnp.dot(xcn.astype(BF16), wg_ref[n], preferred_element_type=F32)
            a_t, b_t = _lru_coeffs(xcn, gg, bga_ref[:, ls], bgx_ref[:, ls], lam_ref[:, ls])
            h = a_t * h0_ref[:, ls] + b_t
            hnew_ref[:, ls] = h
            u_scr[:, ls] = (jax.nn.gelu(gate[:, k * LANES:(k + 1) * LANES]) * h).astype(BF16)
    x = x + jnp.dot(u_scr[...], wor_ref[...], preferred_element_type=F32)
    y = _ffn_sample_body(x, g_ref, wup_ref, wc_ref, bc_ref, wdn_ref, st0_ref, st1_ref, gate_ref, a_scr, h_scr)
    out_ref[...] = _rms(y, gf_ref[...])


def _sample_call(kernel, name, args, out_widths, extra_scratch=()):
    db = args[0].shape[0]
    return pl.pallas_call(
        kernel,
        out_shape=[jax.ShapeDtypeStruct((db, w), F32) for w in out_widths],
        scratch_shapes=[pltpu.VMEM((db, D_MODEL), BF16), pltpu.VMEM((db, D_FF), BF16), *extra_scratch],
        compiler_params=_cparams(),
        name=name,
    )(*args)


def _rope_tables(pos):
    posf = pos.astype(F32)[:, None]

    def cs(d):
        half = d // 2
        inv = ROPE_THETA ** (-jnp.arange(half, dtype=F32) * 2.0 / d)
        ang = posf * inv[None, :]
        return jnp.cos(ang), jnp.sin(ang)

    c, s = cs(HEAD_DIM)
    cos = jnp.concatenate([c, c], axis=-1)
    sin = jnp.concatenate([-s, s], axis=-1)
    c, s = cs(IDX_DIM)
    one, zero = jnp.ones_like(c), jnp.zeros_like(c)
    tile = lambda parts: jnp.concatenate(parts * (IDX_HEADS + 1), axis=-1)
    icos = tile([c, c, one, one])
    isa = tile([-s, zero, zero, zero])
    isb = tile([zero, s, zero, zero])
    return cos, sin, icos, isa, isb


def _split_attn_in(w):
    qkv = N_HEADS * HEAD_DIM
    wqk, wv, wi = w[:, :2 * qkv], w[:, 2 * qkv:3 * qkv], w[:, 3 * qkv:]
    pad = lambda a: jnp.pad(a, ((0, 0), (0, LANES - a.shape[1])))
    groups = [pad(wi[:, h * IDX_DIM:(h + 1) * IDX_DIM]) for h in range(IDX_HEADS)]
    groups.append(pad(wi[:, IDX_HEADS * IDX_DIM:]))
    return wqk.astype(BF16), wv.astype(BF16), jnp.concatenate(groups, axis=-1).astype(BF16)


def kernel(x_prompt, x_sample, cache_k, cache_v, cache_kidx, state_lru_h, state_lru_conv, state_ffn_conv,
           page_table, norm_mix, norm_ffn, norm_final, w_attn_in, w_attn_out, w_rec_in, w_rec_conv, b_rec_conv,
           w_gate_a, b_gate_a, w_gate_x, b_gate_x, lru_lambda, w_rec_out, w_ffn_up, w_ffn_conv, b_ffn_conv,
           w_ffn_down):
    bsz, seq, d = x_prompt.shape
    db = x_sample.shape[0]
    n_pages = page_table.shape[1]
    past = n_pages * PAGE
    assert d == D_MODEL and x_sample.shape[1] == 1 and seq % ROW_BLOCK == 0 and seq % TQ == 0
    assert min(TOPK, seq // 4) == TOPK and min(TOPK, (past + 1) // 4) == TOPK

    vec = lambda a: a.reshape(1, -1)
    wqk, wv, widx = _split_attn_in(w_attn_in[0])
    wo_attn = w_attn_out[0].astype(BF16)
    w_rin = w_rec_in[0].astype(BF16)
    w_gates = jnp.concatenate([w_gate_a[0], w_gate_x[0]], axis=-1).astype(BF16)
    wo_rec = w_rec_out[0].astype(BF16)
    wup = [w_ffn_up[i].astype(BF16) for i in range(2)]
    wdn = [w_ffn_down[i].astype(BF16) for i in range(2)]
    rec_vecs = (w_rec_conv[0], vec(b_rec_conv[0]), w_gates, vec(b_gate_a[0]), vec(b_gate_x[0]),
                vec(lru_lambda[0]), wo_rec)
    ffn_vecs = [(vec(norm_ffn[i]), wup[i], w_ffn_conv[i], vec(b_ffn_conv[i]), wdn[i]) for i in range(2)]
    gfin = vec(norm_final)

    xp = x_prompt.reshape(bsz * seq, d)
    q, kf, kb, vf, vb, qi, kiw = _attn_in(xp, vec(norm_mix[0]), wqk, wv, widx, _rope_tables(jnp.arange(seq)),
                                          ROW_BLOCK, seq // ROW_BLOCK)
    hp = _dsa_prompt(q, kb, vb, qi, kiw, xp, wo_attn, bsz, seq)
    hp, ftail0 = _ffn_prompt(hp, *ffn_vecs[0], gfin, bsz, seq, False)
    hp, hlast, ctail = _rec_prompt(hp, vec(norm_mix[1]), w_rin, *rec_vecs, bsz, seq)
    yp, ftail1 = _ffn_prompt(hp, *ffn_vecs[1], gfin, bsz, seq, True)

    xs = x_sample.reshape(db, d)
    tabs = tuple(jnp.broadcast_to(t, (db, t.shape[1])) for t in _rope_tables(jnp.full((1,), past)))
    qs, kfs, kbs, vfs, vbs, qis, kiws = _attn_in(xs, vec(norm_mix[0]), wqk, wv, widx, tabs, db, 1)
    pt = page_table.reshape(-1)
    scores = _idx_sample(pt, qis.reshape(db, IDX_HEADS, LANES),
                         kiws[:, IDX_DIM:IDX_DIM + IDX_HEADS].reshape(db, IDX_HEADS, 1),
                         kiws.reshape(db, 1, LANES), cache_kidx[0], n_pages)
    bias = _select_sample(scores.reshape(db, -1), past + 1)
    o_s = _attn_sample(pt, qs.reshape(db, 1, d), kfs.reshape(db, 1, d), vfs.reshape(db, 1, d),
                       bias.reshape(db, 1, -1), cache_k[0].reshape(-1, PAGE, d), cache_v[0].reshape(-1, PAGE, d),
                       n_pages)
    st = state_ffn_conv
    hs, gate0 = _sample_call(_layer0_sample_kernel, "layer0_sample",
                             (xs, o_s.reshape(db, d), wo_attn, *ffn_vecs[0], st[0, :, 0], st[0, :, 1]),
                             (D_MODEL, D_FF))
    cst = state_lru_conv[0]
    ys, hnew, xbs, gate1 = _sample_call(
        _layer1_sample_kernel, "layer1_sample",
        (hs, vec(norm_mix[1]), w_rin, *rec_vecs, cst[:, 0], cst[:, 1], cst[:, 2], state_lru_h[0], *ffn_vecs[1],
         st[1, :, 0], st[1, :, 1], gfin),
        (D_MODEL, D_MODEL, D_MODEL, D_FF), extra_scratch=(pltpu.VMEM((db, D_MODEL), BF16),))

    heads = (N_HEADS, HEAD_DIM)
    return (
        yp.reshape(bsz, seq, d),
        ys.reshape(db, 1, d),
        kf.reshape(1, bsz, seq, *heads),
        vf.reshape(1, bsz, seq, *heads),
        kiw[:, :IDX_DIM].reshape(1, bsz, seq, IDX_DIM),
        kfs.reshape(1, db, 1, *heads),
        vfs.reshape(1, db, 1, *heads),
        kiws[:, :IDX_DIM].reshape(1, db, 1, IDX_DIM),
        hlast[None, :, 0, :],
        ctail[None, :, SUBLANES - 3:, :],
        hnew[None],
        jnp.stack([cst[:, 1], cst[:, 2], xbs], axis=1)[None],
        jnp.stack([ftail0[:, SUBLANES - 2:], ftail1[:, SUBLANES - 2:]]),
        jnp.stack([jnp.stack([st[0, :, 1], gate0], axis=1), jnp.stack([st[1, :, 1], gate1], axis=1)]),
    )
```

```python
import functools

import jax
import jax.numpy as jnp
from jax import lax
from jax.experimental import pallas as pl
from jax.experimental.pallas import tpu as pltpu

F32 = jnp.float32
BF16 = jnp.bfloat16
I32 = jnp.int32

D_MODEL = 1024
N_HEADS = 8
HEAD_DIM = 128
IDX_HEADS = 4
IDX_DIM = 64
TOPK = 256
PAGE = 128
ROPE_THETA = 10000.0
IDX_SCALE = (IDX_DIM * IDX_HEADS) ** -0.5
Q_SCALE = HEAD_DIM ** -0.5
RG_C = 8.0
D_FF = 2816
RMS_EPS = 1e-6

LANES = 128
SUBLANES = 8
MXU_N = 256
IDX_W = (IDX_HEADS + 1) * LANES
INT_MIN = -(2 ** 31)
NEG = -1e30
VMEM_LIMIT = 52 * 1024 * 1024

ROW_BLOCK = 512
TQ = 256
TK = 256
FF_CHUNK = 256
N_FF_CHUNKS = D_FF // FF_CHUNK
NT_DIMS = (((1,), (1,)), ((), ()))


def _cparams(*sem):
    return pltpu.CompilerParams(dimension_semantics=sem if sem else None, vmem_limit_bytes=VMEM_LIMIT)


def _const_spec(shape):
    nd = len(shape)
    return pl.BlockSpec(shape, lambda *_: (0,) * nd, pipeline_mode=pl.Buffered(1))


def _rms(x, g):
    return x * lax.rsqrt(jnp.mean(x * x, axis=-1, keepdims=True) + RMS_EPS) * g


def _shift_rows(x, s, prev_rows):
    r = pltpu.roll(x, s, 0)
    top = r[:SUBLANES]
    row = lax.broadcasted_iota(I32, top.shape, 0)
    for k in range(s):
        top = jnp.where(row == k, prev_rows[k], top)
    return jnp.concatenate([top, r[SUBLANES:]], axis=0)


def _sortable_key(score):
    bits = pltpu.bitcast(score + 0.0, I32)
    return jnp.where(bits < 0, bits ^ jnp.int32(0x7FFFFFFF), bits)


def _attn_in_kernel(x_ref, g_ref, wqk_ref, wv_ref, widx_ref, cos_ref, sin_ref, icos_ref, isa_ref, isb_ref,
                    q_ref, kf_ref, kb_ref, vf_ref, vt_ref, qi_ref, kiw_ref, a_scr):
    a_scr[...] = _rms(x_ref[...], g_ref[...]).astype(BF16)
    cos = cos_ref[...]
    sin = sin_ref[...]
    for c in range(2 * D_MODEL // MXU_N):
        r = jnp.dot(a_scr[...], wqk_ref[:, c * MXU_N:(c + 1) * MXU_N], preferred_element_type=F32)
        for hh in range(MXU_N // HEAD_DIM):
            xh = r[:, hh * HEAD_DIM:(hh + 1) * HEAD_DIM]
            y = xh * cos + pltpu.roll(xh, HEAD_DIM // 2, 1) * sin
            col = (c * MXU_N) % D_MODEL + hh * HEAD_DIM
            if c < D_MODEL // MXU_N:
                q_ref[:, col:col + HEAD_DIM] = (y * Q_SCALE).astype(BF16)
            else:
                kf_ref[:, col:col + HEAD_DIM] = y
                kb_ref[:, col:col + HEAD_DIM] = y.astype(BF16)
    for c in range(D_MODEL // MXU_N):
        r = jnp.dot(a_scr[...], wv_ref[:, c * MXU_N:(c + 1) * MXU_N], preferred_element_type=F32)
        vf_ref[:, c * MXU_N:(c + 1) * MXU_N] = r
        vt_ref[c * MXU_N:(c + 1) * MXU_N, :] = r.T.astype(BF16)
    ri = jnp.dot(a_scr[...], widx_ref[...], preferred_element_type=F32)
    yi = (ri * icos_ref[...] + pltpu.roll(ri, IDX_W - IDX_DIM // 2, 1) * isa_ref[...]
          + pltpu.roll(ri, IDX_DIM // 2, 1) * isb_ref[...])
    qi_ref[...] = yi[:, :IDX_HEADS * LANES].astype(BF16)
    kiw_ref[...] = yi[:, IDX_HEADS * LANES:]


def _attn_in(x, g, wqk, wv, widx, tabs, rows, n_pos_blocks):
    n = x.shape[0]
    cos, sin, icos, isa, isb = tabs
    row_spec = lambda w: pl.BlockSpec((rows, w), lambda i: (i, 0))
    tab_spec = lambda w: pl.BlockSpec((rows, w), lambda i: (i % n_pos_blocks, 0))
    vt_spec = pl.BlockSpec((None, D_MODEL, rows), lambda i: (i // n_pos_blocks, 0, i % n_pos_blocks))
    return pl.pallas_call(
        _attn_in_kernel,
        grid=(n // rows,),
        in_specs=[row_spec(D_MODEL), _const_spec((1, D_MODEL)), _const_spec(wqk.shape), _const_spec(wv.shape),
                  _const_spec(widx.shape), tab_spec(HEAD_DIM), tab_spec(HEAD_DIM), tab_spec(IDX_W),
                  tab_spec(IDX_W), tab_spec(IDX_W)],
        out_specs=[row_spec(D_MODEL)] * 4 + [vt_spec, row_spec(IDX_HEADS * LANES), row_spec(LANES)],
        out_shape=[jax.ShapeDtypeStruct((n, D_MODEL), BF16), jax.ShapeDtypeStruct((n, D_MODEL), F32),
                   jax.ShapeDtypeStruct((n, D_MODEL), BF16), jax.ShapeDtypeStruct((n, D_MODEL), F32),
                   jax.ShapeDtypeStruct((n // (rows * n_pos_blocks), D_MODEL, rows * n_pos_blocks), BF16),
                   jax.ShapeDtypeStruct((n, IDX_HEADS * LANES), BF16),
                   jax.ShapeDtypeStruct((n, LANES), F32)],
        scratch_shapes=[pltpu.VMEM((rows, D_MODEL), BF16)],
        compiler_params=_cparams("arbitrary"),
        name="attn_in",
    )(x, g, wqk, wv, widx, cos, sin, icos, isa, isb)


def _kth_largest_key(count_ge, n_total):
    kf = float(TOPK)
    c0 = count_ge(0)
    t = jnp.where(c0 >= kf, jnp.int32(0), jnp.int32(INT_MIN))
    ct = jnp.where(c0 >= kf, c0, n_total)

    def bit_body(it, carry):
        t, ct = carry
        cand = t + (jnp.int32(1) << (30 - it))
        cnt = count_ge(cand)
        return jnp.where(cnt >= kf, cand, t), jnp.where(cnt >= kf, cnt, ct)

    return lax.fori_loop(0, 31, bit_body, (t, ct))


def _tie_cutoff(count_tie_le, need, n_bits):
    def bit_body(it, c):
        cand = c + (jnp.int32(1) << (n_bits - 1 - it))
        return jnp.where(count_tie_le(cand - 1) < need, cand, c)

    return lax.fori_loop(0, n_bits, bit_body, jnp.zeros_like(need, dtype=I32))


def _dsa_prompt_kernel(q_ref, kb_ref, vt_ref, qi_ref, kiwk_ref, wit_ref, x_ref, wo_ref, out_ref,
                       key_scr, bias_scr, kib_scr, o_scr, acc_scr, s_scr):
    i = pl.program_id(1)
    nk = i + 1

    @pl.when(i == 0)
    def _():
        kib_scr[...] = kiwk_ref[...].astype(BF16)

    qpos = i * TQ + lax.broadcasted_iota(I32, (1, TQ), 1)

    def rows(kc):
        return pl.ds(pl.multiple_of(kc * TK, TK), TK)

    def spos(kc):
        return kc * TK + lax.broadcasted_iota(I32, (TK, 1), 0)

    def colsum(a):
        return jnp.sum(a.reshape(TK // SUBLANES, SUBLANES, TQ), axis=0)

    wis = wit_ref[...] * IDX_SCALE

    def score_body(kc, carry):
        kic = kib_scr[rows(kc), :]
        sc = jnp.zeros((TK, TQ), F32)
        for h in range(IDX_HEADS):
            s = lax.dot_general(kic, qi_ref[:, h * LANES:(h + 1) * LANES], NT_DIMS, preferred_element_type=F32)
            sc = sc + wis[h:h + 1, :] * jnp.maximum(s, 0.0)
        key = _sortable_key(sc)
        key_scr[rows(kc), :] = jnp.where(spos(kc) <= qpos, key, jnp.int32(INT_MIN))
        return carry

    lax.fori_loop(0, nk, score_body, 0)

    def count_ge(cand):
        def body(kc, acc):
            return acc + colsum(jnp.where(key_scr[rows(kc), :] >= cand, 1.0, 0.0))
        acc = lax.fori_loop(0, nk, body, jnp.zeros((SUBLANES, TQ), F32))
        return jnp.sum(acc, axis=0, keepdims=True)

    thr, cnt_thr = _kth_largest_key(count_ge, (nk * TK).astype(F32))
    short = thr == INT_MIN
    has_ties = jnp.max(jnp.where(short, 0.0, cnt_thr)) > float(TOPK)

    @pl.when(jnp.logical_not(has_ties))
    def _():
        thr_vis = jnp.maximum(thr, jnp.int32(INT_MIN + 1))

        def bias_body(kc, carry):
            bias_scr[rows(kc), :] = jnp.where(key_scr[rows(kc), :] >= thr_vis, 0.0, NEG)
            return carry

        lax.fori_loop(0, nk, bias_body, 0)

    @pl.when(has_ties)
    def _():
        need = float(TOPK) - count_ge(thr + 1)

        def eq_body(kc, carry):
            bias_scr[rows(kc), :] = jnp.where(key_scr[rows(kc), :] == thr, 1.0, 0.0)
            return carry

        lax.fori_loop(0, nk, eq_body, 0)

        def count_tie_le(cm):
            def body(kc, acc):
                return acc + colsum(jnp.where(spos(kc) <= cm, bias_scr[rows(kc), :], 0.0))
            acc = lax.fori_loop(0, nk, body, jnp.zeros((SUBLANES, TQ), F32))
            return jnp.sum(acc, axis=0, keepdims=True)

        cut = jnp.where(short, jnp.int32(-1), _tie_cutoff(count_tie_le, need, 11))

        def bias_body(kc, carry):
            sel = (key_scr[rows(kc), :] > thr) | ((bias_scr[rows(kc), :] > 0.0) & (spos(kc) <= cut))
            bias_scr[rows(kc), :] = jnp.where(sel, 0.0, NEG)
            return carry

        lax.fori_loop(0, nk, bias_body, 0)

    acc_scr[...] = jnp.zeros(acc_scr.shape, F32)

    def att_body(kc, carry):
        ms, ls = carry
        for h in range(N_HEADS):
            hs = slice(h * HEAD_DIM, (h + 1) * HEAD_DIM)
            s_scr[h] = lax.dot_general(kb_ref[rows(kc), hs], q_ref[:, hs], NT_DIMS, preferred_element_type=F32)
        bias = bias_scr[rows(kc), :]
        new_ms, new_ls = [], []
        for h in range(N_HEADS):
            hs = slice(h * HEAD_DIM, (h + 1) * HEAD_DIM)
            s = s_scr[h] + bias
            m_new = jnp.maximum(ms[h], jnp.max(s, axis=0, keepdims=True))
            alpha = jnp.exp(ms[h] - m_new)
            p = jnp.exp(s - m_new)
            new_ms.append(m_new)
            new_ls.append(alpha * ls[h] + jnp.sum(p, axis=0, keepdims=True))
            pv = jnp.dot(vt_ref[hs, rows(kc)], p.astype(BF16), preferred_element_type=F32)
            acc_scr[hs, :] = alpha * acc_scr[hs, :] + pv
        return tuple(new_ms), tuple(new_ls)

    init = ((jnp.full((1, TQ), NEG, F32),) * N_HEADS, (jnp.zeros((1, TQ), F32),) * N_HEADS)
    _, ls = lax.fori_loop(0, nk, att_body, init)
    for h in range(N_HEADS):
        hs = slice(h * HEAD_DIM, (h + 1) * HEAD_DIM)
        o_scr[:, hs] = (acc_scr[hs, :] / ls[h]).T.astype(BF16)
    out_ref[...] = x_ref[...] + jnp.dot(o_scr[...], wo_ref[...], preferred_element_type=F32)


def _dsa_prompt(q, kb, vt, qi, kiw, wit, x, wo, bsz, seq):
    n = x.shape[0]
    nq = seq // TQ
    qrow = lambda w: pl.BlockSpec((TQ, w), lambda b, i: (b * nq + i, 0))
    brow = lambda w: pl.BlockSpec((seq, w), lambda b, i: (b, 0))
    return pl.pallas_call(
        _dsa_prompt_kernel,
        grid=(bsz, nq),
        in_specs=[qrow(D_MODEL), brow(D_MODEL), pl.BlockSpec((None, D_MODEL, seq), lambda b, i: (b, 0, 0)),
                  qrow(IDX_HEADS * LANES), brow(LANES), pl.BlockSpec((IDX_HEADS, TQ), lambda b, i: (0, b * nq + i)),
                  qrow(D_MODEL), _const_spec(wo.shape)],
        out_specs=qrow(D_MODEL),
        out_shape=jax.ShapeDtypeStruct((n, D_MODEL), F32),
        scratch_shapes=[pltpu.VMEM((seq, TQ), I32), pltpu.VMEM((seq, TQ), F32), pltpu.VMEM((seq, LANES), BF16),
                        pltpu.VMEM((TQ, D_MODEL), BF16), pltpu.VMEM((D_MODEL, TQ), F32),
                        pltpu.VMEM((N_HEADS, TK, TQ), F32)],
        compiler_params=_cparams("arbitrary", "arbitrary"),
        name="dsa_prompt",
    )(q, kb, vt, qi, kiw, wit, x, wo)


def _glu_hidden(a_scr, wup_ref, wc_ref, bc_ref, h_scr, gate_taps):
    for c in range(N_FF_CHUNKS):
        cs = slice(c * FF_CHUNK, (c + 1) * FF_CHUNK)
        gate = jnp.dot(a_scr[...], wup_ref[:, cs], preferred_element_type=F32)
        val = jnp.dot(a_scr[...], wup_ref[:, D_FF + c * FF_CHUNK:D_FF + (c + 1) * FF_CHUNK],
                      preferred_element_type=F32)
        g2, g1 = gate_taps(c, gate)
        gc = bc_ref[:, cs] + wc_ref[0:1, cs] * g2 + wc_ref[1:2, cs] * g1 + wc_ref[2:3, cs] * gate
        h_scr[:, cs] = (jax.nn.gelu(gc) * val).astype(BF16)


def _ffn_prompt_kernel(x_ref, g_ref, wup_ref, wc_ref, bc_ref, wdn_ref, gf_ref, out_ref, tail_ref,
                       a_scr, h_scr, carry_scr, *, final_norm):
    j = pl.program_id(1)
    rows = x_ref.shape[0]
    a_scr[...] = _rms(x_ref[...], g_ref[...]).astype(BF16)

    @pl.when(j == 0)
    def _():
        carry_scr[...] = jnp.zeros(carry_scr.shape, F32)

    def gate_taps(c, gate):
        cs = slice(c * FF_CHUNK, (c + 1) * FF_CHUNK)
        p0 = carry_scr[SUBLANES - 2:SUBLANES - 1, cs]
        p1 = carry_scr[SUBLANES - 1:SUBLANES, cs]
        g1 = _shift_rows(gate, 1, [p1])
        g2 = _shift_rows(gate, 2, [p0, p1])
        carry_scr[:, cs] = gate[rows - SUBLANES:, :]
        tail_ref[0, :, cs] = gate[rows - SUBLANES:, :]
        return g2, g1

    _glu_hidden(a_scr, wup_ref, wc_ref, bc_ref, h_scr, gate_taps)
    y = x_ref[...] + jnp.dot(h_scr[...], wdn_ref[...], preferred_element_type=F32)
    out_ref[...] = _rms(y, gf_ref[...]) if final_norm else y


def _ffn_prompt(x, g, wup, wc, bc, wdn, gf, bsz, seq, final_norm):
    n = x.shape[0]
    nb = seq // ROW_BLOCK
    row = pl.BlockSpec((ROW_BLOCK, D_MODEL), lambda b, j: (b * nb + j, 0))
    return pl.pallas_call(
        functools.partial(_ffn_prompt_kernel, final_norm=final_norm),
        grid=(bsz, nb),
        in_specs=[row, _const_spec((1, D_MODEL)), _const_spec(wup.shape), _const_spec(wc.shape),
                  _const_spec(bc.shape), _const_spec(wdn.shape), _const_spec((1, D_MODEL))],
        out_specs=[row, pl.BlockSpec((1, SUBLANES, D_FF), lambda b, j: (b, 0, 0))],
        out_shape=[jax.ShapeDtypeStruct((n, D_MODEL), F32), jax.ShapeDtypeStruct((bsz, SUBLANES, D_FF), F32)],
        scratch_shapes=[pltpu.VMEM((ROW_BLOCK, D_MODEL), BF16), pltpu.VMEM((ROW_BLOCK, D_FF), BF16),
                        pltpu.VMEM((SUBLANES, D_FF), F32)],
        compiler_params=_cparams("arbitrary", "arbitrary"),
        name="ffn_prompt_final" if final_norm else "ffn_prompt",
    )(x, g, wup, wc, bc, wdn, gf)


def _expm1(x):
    u = jnp.exp(x)
    safe = (u - 1.0) * x / jnp.log(u)
    return jnp.where(u == 1.0, x, jnp.where(u == 0.0, -1.0, safe))


def _lru_coeffs(xc, gg, bga, bgx, lam):
    r = jax.nn.sigmoid(gg[:, :LANES] + bga)
    ig = jax.nn.sigmoid(gg[:, LANES:] + bgx)
    log_a = -RG_C * r * jax.nn.softplus(-lam)
    return jnp.exp(log_a), jnp.sqrt(-_expm1(2.0 * log_a)) * ig * xc


def _rec_prompt_kernel(x_ref, g_ref, win_ref, wc_ref, bc_ref, wg_ref, bga_ref, bgx_ref, lam_ref, wo_ref,
                       out_ref, hlast_ref, ctail_ref, a_scr, at_scr, bt_scr, u_scr, hcar_scr, ccar_scr):
    j = pl.program_id(1)
    rows = x_ref.shape[0]
    a_scr[...] = _rms(x_ref[...], g_ref[...]).astype(BF16)

    @pl.when(j == 0)
    def _():
        ccar_scr[...] = jnp.zeros(ccar_scr.shape, F32)
        hcar_scr[...] = jnp.zeros(hcar_scr.shape, F32)

    nblk = MXU_N // LANES
    for c in range(D_MODEL // MXU_N):
        cs = slice(c * MXU_N, (c + 1) * MXU_N)
        xb = jnp.dot(a_scr[...], win_ref[:, D_MODEL + c * MXU_N:D_MODEL + (c + 1) * MXU_N],
                     preferred_element_type=F32)
        prev = [ccar_scr[SUBLANES - 3 + k:SUBLANES - 2 + k, cs] for k in range(3)]
        xc = (bc_ref[:, cs] + wc_ref[0:1, cs] * _shift_rows(xb, 3, prev) + wc_ref[1:2, cs] * _shift_rows(xb, 2, prev[1:])
              + wc_ref[2:3, cs] * _shift_rows(xb, 1, prev[2:]) + wc_ref[3:4, cs] * xb)
        ccar_scr[:, cs] = xb[rows - SUBLANES:, :]
        ctail_ref[0, :, cs] = xb[rows - SUBLANES:, :]
        for k in range(nblk):
            n = c * nblk + k
            ls = slice(n * LANES, (n + 1) * LANES)
            xcn = xc[:, k * LANES:(k + 1) * LANES]
            gg = jnp.dot(xcn.astype(BF16), wg_ref[n], preferred_element_type=F32)
            a_t, b_t = _lru_coeffs(xcn, gg, bga_ref[:, ls], bgx_ref[:, ls], lam_ref[:, ls])
            at_scr[:, ls] = a_t
            bt_scr[:, ls] = b_t

    def step(t, h):
        h = at_scr[pl.ds(t, 1), :] * h + bt_scr[pl.ds(t, 1), :]
        bt_scr[pl.ds(t, 1), :] = h
        return h

    h_last = lax.fori_loop(0, rows, step, hcar_scr[0:1, :], unroll=8)
    hcar_scr[...] = jnp.broadcast_to(h_last, hcar_scr.shape)
    hlast_ref[0] = jnp.broadcast_to(h_last, hcar_scr.shape)
    for c in range(D_MODEL // MXU_N):
        cs = slice(c * MXU_N, (c + 1) * MXU_N)
        gate = jnp.dot(a_scr[...], win_ref[:, cs], preferred_element_type=F32)
        u_scr[:, cs] = (jax.nn.gelu(gate) * bt_scr[:, cs]).astype(BF16)
    out_ref[...] = x_ref[...] + jnp.dot(u_scr[...], wo_ref[...], preferred_element_type=F32)


def _rec_prompt(x, g, win, wc, bc, wg, bga, bgx, lam, wo, bsz, seq):
    n = x.shape[0]
    nb = seq // ROW_BLOCK
    row = pl.BlockSpec((ROW_BLOCK, D_MODEL), lambda b, j: (b * nb + j, 0))
    tail = pl.BlockSpec((1, SUBLANES, D_MODEL), lambda b, j: (b, 0, 0))
    vec = _const_spec((1, D_MODEL))
    return pl.pallas_call(
        _rec_prompt_kernel,
        grid=(bsz, nb),
        in_specs=[row, vec, _const_spec(win.shape), _const_spec(wc.shape), vec, _const_spec(wg.shape), vec, vec,
                  vec, _const_spec(wo.shape)],
        out_specs=[row, tail, tail],
        out_shape=[jax.ShapeDtypeStruct((n, D_MODEL), F32), jax.ShapeDtypeStruct((bsz, SUBLANES, D_MODEL), F32),
                   jax.ShapeDtypeStruct((bsz, SUBLANES, D_MODEL), F32)],
        scratch_shapes=[pltpu.VMEM((ROW_BLOCK, D_MODEL), BF16), pltpu.VMEM((ROW_BLOCK, D_MODEL), F32),
                        pltpu.VMEM((ROW_BLOCK, D_MODEL), F32), pltpu.VMEM((ROW_BLOCK, D_MODEL), BF16),
                        pltpu.VMEM((SUBLANES, D_MODEL), F32), pltpu.VMEM((SUBLANES, D_MODEL), F32)],
        compiler_params=_cparams("arbitrary", "arbitrary"),
        name="rec_prompt",
    )(x, g, win, wc, bc, wg, bga, bgx, lam, wo)


def _idx_sample_kernel(pt_ref, qi_ref, wi_ref, kin_ref, *refs, n_pages):
    pages, out_ref = refs[:n_pages], refs[n_pages]
    qi = qi_ref[...]
    wi = wi_ref[...]
    for p in range(n_pages):
        s = lax.dot_general(qi[:, :IDX_DIM], pages[p][...].astype(BF16), NT_DIMS, preferred_element_type=F32)
        out_ref[:, p * PAGE:(p + 1) * PAGE] = (
            jnp.sum(wi * jnp.maximum(s, 0.0), axis=0, keepdims=True) * IDX_SCALE)
    s_new = jnp.sum(qi.astype(F32) * kin_ref[...].astype(BF16).astype(F32), axis=-1, keepdims=True)
    sc_new = jnp.sum(wi * jnp.maximum(s_new, 0.0), axis=0, keepdims=True) * IDX_SCALE
    lane = lax.broadcasted_iota(I32, (1, LANES), 1)
    out_ref[:, n_pages * PAGE:] = jnp.where(lane == 0, sc_new, -jnp.inf)


def _idx_sample(pt, qi3, wi3, kin3, pool_ki, n_pages):
    db = qi3.shape[0]
    per_seq = lambda shape: pl.BlockSpec((None,) + shape, lambda d, pt: (d, 0, 0))
    page_spec = lambda p: pl.BlockSpec((None, PAGE, IDX_DIM), lambda d, pt: (pt[d * n_pages + p], 0, 0))
    width = n_pages * PAGE + LANES
    return pl.pallas_call(
        functools.partial(_idx_sample_kernel, n_pages=n_pages),
        grid_spec=pltpu.PrefetchScalarGridSpec(
            num_scalar_prefetch=1, grid=(db,),
            in_specs=[per_seq((IDX_HEADS, LANES)), per_seq((IDX_HEADS, 1)), per_seq((1, LANES))]
            + [page_spec(p) for p in range(n_pages)],
            out_specs=per_seq((1, width))),
        out_shape=jax.ShapeDtypeStruct((db, 1, width), F32),
        compiler_params=_cparams("arbitrary"),
        name="idx_sample",
    )(pt, qi3, wi3, kin3, *([pool_ki] * n_pages))


def _select_sample_kernel(sc_ref, bias_ref, key_scr, *, n_keys):
    lane = lax.broadcasted_iota(I32, (1, sc_ref.shape[1]), 1)
    key_scr[...] = jnp.where(lane < n_keys, _sortable_key(sc_ref[...]), jnp.int32(INT_MIN))

    def count_ge(cand):
        return jnp.sum(jnp.where(key_scr[...] >= cand, 1.0, 0.0), axis=-1, keepdims=True)

    thr, _ = _kth_largest_key(count_ge, float(sc_ref.shape[1]))
    need = float(TOPK) - count_ge(thr + 1)

    def count_tie_le(cm):
        return jnp.sum(jnp.where((key_scr[...] == thr) & (lane <= cm), 1.0, 0.0), axis=-1, keepdims=True)

    cut = _tie_cutoff(count_tie_le, need, 12)
    cut = jnp.where(thr == INT_MIN, jnp.int32(-1), cut)
    key = key_scr[...]
    sel = (key > thr) | ((key == thr) & (lane <= cut))
    bias_ref[...] = jnp.where(sel, 0.0, NEG)


def _select_sample(scores, n_keys):
    return pl.pallas_call(
        functools.partial(_select_sample_kernel, n_keys=n_keys),
        out_shape=jax.ShapeDtypeStruct(scores.shape, F32),
        scratch_shapes=[pltpu.VMEM(scores.shape, I32)],
        compiler_params=_cparams(),
        name="select_sample",
    )(scores)


def _attn_sample_kernel(pt_ref, q_ref, kn_ref, vn_ref, bias_ref, *refs, n_pages):
    kpages, vpages = refs[:n_pages], refs[n_pages:2 * n_pages]
    o_ref, kall, vall = refs[2 * n_pages:]
    page_rows = PAGE * N_HEADS
    past_rows = n_pages * page_rows
    for p in range(n_pages):
        kall[p * page_rows:(p + 1) * page_rows, :] = kpages[p][...].astype(BF16)
        vall[p * page_rows:(p + 1) * page_rows, :] = vpages[p][...].astype(BF16)
    zeros = jnp.zeros((LANES - N_HEADS, HEAD_DIM), F32)
    kall[past_rows:, :] = jnp.concatenate([kn_ref[...], zeros], axis=0).astype(BF16)
    vall[past_rows:, :] = jnp.concatenate([vn_ref[...], zeros], axis=0).astype(BF16)
    s = lax.dot_general(q_ref[...], kall[...], NT_DIMS, preferred_element_type=F32)
    own = (lax.broadcasted_iota(I32, s.shape, 1) & (N_HEADS - 1)) == lax.broadcasted_iota(I32, s.shape, 0)
    s = jnp.where(own, s + bias_ref[...], NEG)
    m = jnp.max(s, axis=-1, keepdims=True)
    p = jnp.exp(s - m)
    l = jnp.sum(p, axis=-1, keepdims=True)
    o_ref[...] = jnp.dot(p.astype(BF16), vall[...], preferred_element_type=F32) / l


def _attn_sample(pt, q3, kn3, vn3, bias3, pool_k, pool_v, n_pages):
    db = q3.shape[0]
    page_rows = PAGE * N_HEADS
    width = n_pages * page_rows + LANES
    per_seq = lambda r, w: pl.BlockSpec((None, r, w), lambda d, pt: (d, 0, 0))
    page_spec = lambda p: pl.BlockSpec((None, page_rows, HEAD_DIM), lambda d, pt: (pt[d * n_pages + p], 0, 0))
    return pl.pallas_call(
        functools.partial(_attn_sample_kernel, n_pages=n_pages),
        grid_spec=pltpu.PrefetchScalarGridSpec(
            num_scalar_prefetch=1, grid=(db,),
            in_specs=[per_seq(N_HEADS, HEAD_DIM)] * 3 + [per_seq(1, width)]
            + [page_spec(p) for p in range(n_pages)] * 2,
            out_specs=per_seq(N_HEADS, HEAD_DIM),
            scratch_shapes=[pltpu.VMEM((width, HEAD_DIM), BF16), pltpu.VMEM((width, HEAD_DIM), BF16)]),
        out_shape=jax.ShapeDtypeStruct((db, N_HEADS, HEAD_DIM), F32),
        compiler_params=_cparams("arbitrary"),
        name="attn_sample",
    )(pt, q3, kn3, vn3, bias3, *([pool_k] * n_pages), *([pool_v] * n_pages))


def _ffn_sample_body(x, g_ref, wup_ref, wc_ref, bc_ref, wdn_ref, st0_ref, st1_ref, gate_ref, a_scr, h_scr):
    a_scr[...] = _rms(x, g_ref[...]).astype(BF16)

    def gate_taps(c, gate):
        cs = slice(c * FF_CHUNK, (c + 1) * FF_CHUNK)
        gate_ref[:, cs] = gate
        return st0_ref[:, cs], st1_ref[:, cs]

    _glu_hidden(a_scr, wup_ref, wc_ref, bc_ref, h_scr, gate_taps)
    return x + jnp.dot(h_scr[...], wdn_ref[...], preferred_element_type=F32)


def _layer0_sample_kernel(x_ref, o_ref, wo_ref, g_ref, wup_ref, wc_ref, bc_ref, wdn_ref, st0_ref, st1_ref,
                          out_ref, gate_ref, a_scr, h_scr):
    x = x_ref[...] + jnp.dot(o_ref[...].astype(BF16), wo_ref[...], preferred_element_type=F32)
    out_ref[...] = _ffn_sample_body(x, g_ref, wup_ref, wc_ref, bc_ref, wdn_ref, st0_ref, st1_ref, gate_ref,
                                    a_scr, h_scr)


def _layer1_sample_kernel(x_ref, gm_ref, win_ref, wcr_ref, bcr_ref, wg_ref, bga_ref, bgx_ref, lam_ref, wor_ref,
                          cs0_ref, cs1_ref, cs2_ref, h0_ref, g_ref, wup_ref, wc_ref, bc_ref, wdn_ref, st0_ref,
                          st1_ref, gf_ref, out_ref, hnew_ref, xb_ref, gate_ref, a_scr, h_scr, u_scr):
    x = x_ref[...]
    a_scr[...] = _rms(x, gm_ref[...]).astype(BF16)
    nblk = MXU_N // LANES
    for c in range(D_MODEL // MXU_N):
        cs = slice(c * MXU_N, (c + 1) * MXU_N)
        xb = jnp.dot(a_scr[...], win_ref[:, D_MODEL + c * MXU_N:D_MODEL + (c + 1) * MXU_N],
                     preferred_element_type=F32)
        gate = jnp.dot(a_scr[...], win_ref[:, cs], preferred_element_type=F32)
        xb_ref[:, cs] = xb
        xc = (bcr_ref[:, cs] + wcr_ref[0:1, cs] * cs0_ref[:, cs] + wcr_ref[1:2, cs] * cs1_ref[:, cs]
              + wcr_ref[2:3, cs] * cs2_ref[:, cs] + wcr_ref[3:4, cs] * xb)
        for k in range(nblk):
            n = c * nblk + k
            ls = slice(n * LANES, (n + 1) * LANES)
            xcn = xc[:, k * LANES:(k + 1) * LANES]
            gg = jnp.dot(xcn.astype(BF16), wg_ref[n], preferred_element_type=F32)
            a_t, b_t = _lru_coeffs(xcn, gg, bga_ref[:, ls], bgx_ref[:, ls], lam_ref[:, ls])
            h = a_t * h0_ref[:, ls] + b_t
            hnew_ref[:, ls] = h
            u_scr[:, ls] = (jax.nn.gelu(gate[:, k * LANES:(k + 1) * LANES]) * h).astype(BF16)
    x = x + jnp.dot(u_scr[...], wor_ref[...], preferred_element_type=F32)
    y = _ffn_sample_body(x, g_ref, wup_ref, wc_ref, bc_ref, wdn_ref, st0_ref, st1_ref, gate_ref, a_scr, h_scr)
    out_ref[...] = _rms(y, gf_ref[...])


def _sample_call(kernel, name, args, out_widths, extra_scratch=()):
    db = args[0].shape[0]
    return pl.pallas_call(
        kernel,
        out_shape=[jax.ShapeDtypeStruct((db, w), F32) for w in out_widths],
        scratch_shapes=[pltpu.VMEM((db, D_MODEL), BF16), pltpu.VMEM((db, D_FF), BF16), *extra_scratch],
        compiler_params=_cparams(),
        name=name,
    )(*args)


def _rope_tables(pos):
    posf = pos.astype(F32)[:, None]

    def cs(d):
        half = d // 2
        inv = ROPE_THETA ** (-jnp.arange(half, dtype=F32) * 2.0 / d)
        ang = posf * inv[None, :]
        return jnp.cos(ang), jnp.sin(ang)

    c, s = cs(HEAD_DIM)
    cos = jnp.concatenate([c, c], axis=-1)
    sin = jnp.concatenate([-s, s], axis=-1)
    c, s = cs(IDX_DIM)
    one, zero = jnp.ones_like(c), jnp.zeros_like(c)
    tile = lambda parts: jnp.concatenate(parts * (IDX_HEADS + 1), axis=-1)
    icos = tile([c, c, one, one])
    isa = tile([-s, zero, zero, zero])
    isb = tile([zero, s, zero, zero])
    return cos, sin, icos, isa, isb


def _split_attn_in(w):
    qkv = N_HEADS * HEAD_DIM
    wqk, wv, wi = w[:, :2 * qkv], w[:, 2 * qkv:3 * qkv], w[:, 3 * qkv:]
    pad = lambda a: jnp.pad(a, ((0, 0), (0, LANES - a.shape[1])))
    groups = [pad(wi[:, h * IDX_DIM:(h + 1) * IDX_DIM]) for h in range(IDX_HEADS)]
    groups.append(pad(wi[:, IDX_HEADS * IDX_DIM:]))
    return wqk.astype(BF16), wv.astype(BF16), jnp.concatenate(groups, axis=-1).astype(BF16)


def kernel(x_prompt, x_sample, cache_k, cache_v, cache_kidx, state_lru_h, state_lru_conv, state_ffn_conv,
           page_table, norm_mix, norm_ffn, norm_final, w_attn_in, w_attn_out, w_rec_in, w_rec_conv, b_rec_conv,
           w_gate_a, b_gate_a, w_gate_x, b_gate_x, lru_lambda, w_rec_out, w_ffn_up, w_ffn_conv, b_ffn_conv,
           w_ffn_down):
    bsz, seq, d = x_prompt.shape
    db = x_sample.shape[0]
    n_pages = page_table.shape[1]
    past = n_pages * PAGE
    assert d == D_MODEL and x_sample.shape[1] == 1 and seq % ROW_BLOCK == 0 and seq % TQ == 0
    assert min(TOPK, seq // 4) == TOPK and min(TOPK, (past + 1) // 4) == TOPK

    vec = lambda a: a.reshape(1, -1)
    wqk, wv, widx = _split_attn_in(w_attn_in[0])
    wo_attn = w_attn_out[0].astype(BF16)
    w_rin = w_rec_in[0].astype(BF16)
    w_gates = jnp.concatenate([w_gate_a[0], w_gate_x[0]], axis=-1).astype(BF16)
    wo_rec = w_rec_out[0].astype(BF16)
    wup = [w_ffn_up[i].astype(BF16) for i in range(2)]
    wdn = [w_ffn_down[i].astype(BF16) for i in range(2)]
    rec_vecs = (w_rec_conv[0], vec(b_rec_conv[0]), w_gates, vec(b_gate_a[0]), vec(b_gate_x[0]),
                vec(lru_lambda[0]), wo_rec)
    ffn_vecs = [(vec(norm_ffn[i]), wup[i], w_ffn_conv[i], vec(b_ffn_conv[i]), wdn[i]) for i in range(2)]
    gfin = vec(norm_final)

    xp = x_prompt.reshape(bsz * seq, d)
    q, kf, kb, vf, vt, qi, kiw = _attn_in(xp, vec(norm_mix[0]), wqk, wv, widx, _rope_tables(jnp.arange(seq)),
                                          ROW_BLOCK, seq // ROW_BLOCK)
    wit = kiw[:, IDX_DIM:IDX_DIM + IDX_HEADS].T
    hp = _dsa_prompt(q, kb, vt, qi, kiw, wit, xp, wo_attn, bsz, seq)
    hp, ftail0 = _ffn_prompt(hp, *ffn_vecs[0], gfin, bsz, seq, False)
    hp, hlast, ctail = _rec_prompt(hp, vec(norm_mix[1]), w_rin, *rec_vecs, bsz, seq)
    yp, ftail1 = _ffn_prompt(hp, *ffn_vecs[1], gfin, bsz, seq, True)

    xs = x_sample.reshape(db, d)
    tabs = tuple(jnp.broadcast_to(t, (db, t.shape[1])) for t in _rope_tables(jnp.full((1,), past)))
    qs, kfs, _, vfs, _, qis, kiws = _attn_in(xs, vec(norm_mix[0]), wqk, wv, widx, tabs, db, 1)
    pt = page_table.reshape(-1)
    scores = _idx_sample(pt, qis.reshape(db, IDX_HEADS, LANES),
                         kiws[:, IDX_DIM:IDX_DIM + IDX_HEADS].reshape(db, IDX_HEADS, 1),
                         kiws.reshape(db, 1, LANES), cache_kidx[0], n_pages)
    bias = _select_sample(scores.reshape(db, -1), past + 1)
    bias_rows = jnp.pad(jnp.repeat(bias[:, :past + 1], N_HEADS, axis=1), ((0, 0), (0, LANES - N_HEADS)),
                        constant_values=NEG)
    heads = (N_HEADS, HEAD_DIM)
    flat_rows = lambda pool: pool[0].reshape(-1, PAGE * N_HEADS, HEAD_DIM)
    o_s = _attn_sample(pt, qs.reshape(db, *heads), kfs.reshape(db, *heads), vfs.reshape(db, *heads),
                       bias_rows.reshape(db, 1, -1), flat_rows(cache_k), flat_rows(cache_v), n_pages)
    st = state_ffn_conv
    hs, gate0 = _sample_call(_layer0_sample_kernel, "layer0_sample",
                             (xs, o_s.reshape(db, d), wo_attn, *ffn_vecs[0], st[0, :, 0], st[0, :, 1]),
                             (D_MODEL, D_FF))
    cst = state_lru_conv[0]
    ys, hnew, xbs, gate1 = _sample_call(
        _layer1_sample_kernel, "layer1_sample",
        (hs, vec(norm_mix[1]), w_rin, *rec_vecs, cst[:, 0], cst[:, 1], cst[:, 2], state_lru_h[0], *ffn_vecs[1],
         st[1, :, 0], st[1, :, 1], gfin),
        (D_MODEL, D_MODEL, D_MODEL, D_FF), extra_scratch=(pltpu.VMEM((db, D_MODEL), BF16),))

    return (
        yp.reshape(bsz, seq, d),
        ys.reshape(db, 1, d),
        kf.reshape(1, bsz, seq, *heads),
        vf.reshape(1, bsz, seq, *heads),
        kiw[:, :IDX_DIM].reshape(1, bsz, seq, IDX_DIM),
        kfs.reshape(1, db, 1, *heads),
        vfs.reshape(1, db, 1, *heads),
        kiws[:, :IDX_DIM].reshape(1, db, 1, IDX_DIM),
        hlast[None, :, 0, :],
        ctail[None, :, SUBLANES - 3:, :],
        hnew[None],
        jnp.stack([cst[:, 1], cst[:, 2], xbs], axis=1)[None],
        jnp.stack([ftail0[:, SUBLANES - 2:], ftail1[:, SUBLANES - 2:]]),
        jnp.stack([jnp.stack([st[0, :, 1], gate0], axis=1), jnp.stack([st[1, :, 1], gate1], axis=1)]),
    )
```

```python
import functools

import jax
import jax.numpy as jnp
from jax import lax
from jax.experimental import pallas as pl
from jax.experimental.pallas import tpu as pltpu

F32 = jnp.float32
BF16 = jnp.bfloat16
I32 = jnp.int32

D_MODEL = 1024
N_HEADS = 8
HEAD_DIM = 128
IDX_HEADS = 4
IDX_DIM = 64
TOPK = 256
PAGE = 128
ROPE_THETA = 10000.0
IDX_SCALE = (IDX_DIM * IDX_HEADS) ** -0.5
Q_SCALE = HEAD_DIM ** -0.5
RG_C = 8.0
D_FF = 2816
RMS_EPS = 1e-6

LANES = 128
SUBLANES = 8
MXU_N = 256
IDX_W = (IDX_HEADS + 1) * LANES
INT_MIN = -(2 ** 31)
NEG = -1e30
VMEM_LIMIT = 52 * 1024 * 1024

ROW_BLOCK = 512
TQ = 256
TK = 256
COUNT_ROWS = 32
IDX_SEQS = 4
FF_CHUNK = 256
N_FF_CHUNKS = D_FF // FF_CHUNK
NT_DIMS = (((1,), (1,)), ((), ()))


def _cparams(*sem):
    return pltpu.CompilerParams(dimension_semantics=sem if sem else None, vmem_limit_bytes=VMEM_LIMIT)


def _const_spec(shape):
    nd = len(shape)
    return pl.BlockSpec(shape, lambda *_: (0,) * nd, pipeline_mode=pl.Buffered(1))


def _rms(x, g):
    return x * lax.rsqrt(jnp.mean(x * x, axis=-1, keepdims=True) + RMS_EPS) * g


def _shift_rows(x, s, prev_rows):
    r = pltpu.roll(x, s, 0)
    top = r[:SUBLANES]
    row = lax.broadcasted_iota(I32, top.shape, 0)
    for k in range(s):
        top = jnp.where(row == k, prev_rows[k], top)
    return jnp.concatenate([top, r[SUBLANES:]], axis=0)


def _sortable_key(score):
    bits = pltpu.bitcast(score + 0.0, I32)
    return jnp.where(bits < 0, bits ^ jnp.int32(0x7FFFFFFF), bits)


def _attn_in_kernel(x_ref, g_ref, wqk_ref, wv_ref, widx_ref, cos_ref, sin_ref, icos_ref, isa_ref, isb_ref,
                    q_ref, kf_ref, kb_ref, vf_ref, vt_ref, qi_ref, kiw_ref, a_scr):
    a_scr[...] = _rms(x_ref[...], g_ref[...]).astype(BF16)
    cos = cos_ref[...]
    sin = sin_ref[...]
    for c in range(2 * D_MODEL // MXU_N):
        r = jnp.dot(a_scr[...], wqk_ref[:, c * MXU_N:(c + 1) * MXU_N], preferred_element_type=F32)
        for hh in range(MXU_N // HEAD_DIM):
            xh = r[:, hh * HEAD_DIM:(hh + 1) * HEAD_DIM]
            y = xh * cos + pltpu.roll(xh, HEAD_DIM // 2, 1) * sin
            col = (c * MXU_N) % D_MODEL + hh * HEAD_DIM
            if c < D_MODEL // MXU_N:
                q_ref[:, col:col + HEAD_DIM] = (y * Q_SCALE).astype(BF16)
            else:
                kf_ref[:, col:col + HEAD_DIM] = y
                kb_ref[:, col:col + HEAD_DIM] = y.astype(BF16)
    for c in range(D_MODEL // MXU_N):
        r = jnp.dot(a_scr[...], wv_ref[:, c * MXU_N:(c + 1) * MXU_N], preferred_element_type=F32)
        vf_ref[:, c * MXU_N:(c + 1) * MXU_N] = r
        vt_ref[c * MXU_N:(c + 1) * MXU_N, :] = r.T.astype(BF16)
    ri = jnp.dot(a_scr[...], widx_ref[...], preferred_element_type=F32)
    yi = (ri * icos_ref[...] + pltpu.roll(ri, IDX_W - IDX_DIM // 2, 1) * isa_ref[...]
          + pltpu.roll(ri, IDX_DIM // 2, 1) * isb_ref[...])
    qi_ref[...] = yi[:, :IDX_HEADS * LANES].astype(BF16)
    kiw_ref[...] = yi[:, IDX_HEADS * LANES:]


def _attn_in(x, g, wqk, wv, widx, tabs, rows, n_pos_blocks):
    n = x.shape[0]
    cos, sin, icos, isa, isb = tabs
    row_spec = lambda w: pl.BlockSpec((rows, w), lambda i: (i, 0))
    tab_spec = lambda w: pl.BlockSpec((rows, w), lambda i: (i % n_pos_blocks, 0))
    vt_spec = pl.BlockSpec((None, D_MODEL, rows), lambda i: (i // n_pos_blocks, 0, i % n_pos_blocks))
    return pl.pallas_call(
        _attn_in_kernel,
        grid=(n // rows,),
        in_specs=[row_spec(D_MODEL), _const_spec((1, D_MODEL)), _const_spec(wqk.shape), _const_spec(wv.shape),
                  _const_spec(widx.shape), tab_spec(HEAD_DIM), tab_spec(HEAD_DIM), tab_spec(IDX_W),
                  tab_spec(IDX_W), tab_spec(IDX_W)],
        out_specs=[row_spec(D_MODEL)] * 4 + [vt_spec, row_spec(IDX_HEADS * LANES), row_spec(LANES)],
        out_shape=[jax.ShapeDtypeStruct((n, D_MODEL), BF16), jax.ShapeDtypeStruct((n, D_MODEL), F32),
                   jax.ShapeDtypeStruct((n, D_MODEL), BF16), jax.ShapeDtypeStruct((n, D_MODEL), F32),
                   jax.ShapeDtypeStruct((n // (rows * n_pos_blocks), D_MODEL, rows * n_pos_blocks), BF16),
                   jax.ShapeDtypeStruct((n, IDX_HEADS * LANES), BF16),
                   jax.ShapeDtypeStruct((n, LANES), F32)],
        scratch_shapes=[pltpu.VMEM((rows, D_MODEL), BF16)],
        compiler_params=_cparams("arbitrary"),
        name="attn_in",
    )(x, g, wqk, wv, widx, cos, sin, icos, isa, isb)


def _kth_largest_key(count_ge, n_total):
    kf = float(TOPK)
    c0 = count_ge(0)
    t = jnp.where(c0 >= kf, jnp.int32(0), jnp.int32(INT_MIN))
    ct = jnp.where(c0 >= kf, c0, n_total)

    def bit_body(it, carry):
        t, ct = carry
        cand = t + (jnp.int32(1) << (30 - it))
        cnt = count_ge(cand)
        return jnp.where(cnt >= kf, cand, t), jnp.where(cnt >= kf, cnt, ct)

    return lax.fori_loop(0, 31, bit_body, (t, ct))


def _tie_cutoff(count_tie_le, need, n_bits):
    def bit_body(it, c):
        cand = c + (jnp.int32(1) << (n_bits - 1 - it))
        return jnp.where(count_tie_le(cand - 1) < need, cand, c)

    return lax.fori_loop(0, n_bits, bit_body, jnp.zeros_like(need, dtype=I32))


def _dsa_prompt_kernel(q_ref, kb_ref, vt_ref, qi_ref, kiwk_ref, wit_ref, x_ref, wo_ref, out_ref,
                       key_scr, bias_scr, kib_scr, o_scr, acc_scr, s_scr):
    i = pl.program_id(1)
    nk = i + 1

    @pl.when(i == 0)
    def _():
        kib_scr[...] = kiwk_ref[...].astype(BF16)

    qpos = i * TQ + lax.broadcasted_iota(I32, (1, TQ), 1)

    def rows(kc):
        return pl.ds(pl.multiple_of(kc * TK, TK), TK)

    def spos(kc):
        return kc * TK + lax.broadcasted_iota(I32, (TK, 1), 0)

    def colsum(a):
        return jnp.sum(a.reshape(TK // COUNT_ROWS, COUNT_ROWS, TQ), axis=0)

    wis = wit_ref[...] * IDX_SCALE

    def score_body(kc, carry):
        kic = kib_scr[rows(kc), :]
        sc = jnp.zeros((TK, TQ), F32)
        for h in range(IDX_HEADS):
            s = lax.dot_general(kic, qi_ref[:, h * LANES:(h + 1) * LANES], NT_DIMS, preferred_element_type=F32)
            sc = sc + wis[h:h + 1, :] * jnp.maximum(s, 0.0)
        key = _sortable_key(sc)
        key_scr[rows(kc), :] = jnp.where(spos(kc) <= qpos, key, jnp.int32(INT_MIN))
        return carry

    lax.fori_loop(0, nk, score_body, 0)

    def count_ge(cand):
        def body(kc, acc):
            return acc + colsum(jnp.where(key_scr[rows(kc), :] >= cand, 1.0, 0.0))
        acc = lax.fori_loop(0, nk, body, jnp.zeros((COUNT_ROWS, TQ), F32))
        return jnp.sum(acc, axis=0, keepdims=True)

    thr, cnt_thr = _kth_largest_key(count_ge, (nk * TK).astype(F32))
    short = thr == INT_MIN
    has_ties = jnp.max(jnp.where(short, 0.0, cnt_thr)) > float(TOPK)

    @pl.when(jnp.logical_not(has_ties))
    def _():
        thr_vis = jnp.maximum(thr, jnp.int32(INT_MIN + 1))

        def bias_body(kc, carry):
            bias_scr[rows(kc), :] = jnp.where(key_scr[rows(kc), :] >= thr_vis, 0.0, NEG)
            return carry

        lax.fori_loop(0, nk, bias_body, 0)

    @pl.when(has_ties)
    def _():
        need = float(TOPK) - count_ge(thr + 1)

        def eq_body(kc, carry):
            bias_scr[rows(kc), :] = jnp.where(key_scr[rows(kc), :] == thr, 1.0, 0.0)
            return carry

        lax.fori_loop(0, nk, eq_body, 0)

        def count_tie_le(cm):
            def body(kc, acc):
                return acc + colsum(jnp.where(spos(kc) <= cm, bias_scr[rows(kc), :], 0.0))
            acc = lax.fori_loop(0, nk, body, jnp.zeros((COUNT_ROWS, TQ), F32))
            return jnp.sum(acc, axis=0, keepdims=True)

        cut = jnp.where(short, jnp.int32(-1), _tie_cutoff(count_tie_le, need, 11))

        def bias_body(kc, carry):
            sel = (key_scr[rows(kc), :] > thr) | ((bias_scr[rows(kc), :] > 0.0) & (spos(kc) <= cut))
            bias_scr[rows(kc), :] = jnp.where(sel, 0.0, NEG)
            return carry

        lax.fori_loop(0, nk, bias_body, 0)

    acc_scr[...] = jnp.zeros(acc_scr.shape, F32)

    def att_body(kc, carry):
        ms, ls = carry
        for h in range(N_HEADS):
            hs = slice(h * HEAD_DIM, (h + 1) * HEAD_DIM)
            s_scr[h] = lax.dot_general(kb_ref[rows(kc), hs], q_ref[:, hs], NT_DIMS, preferred_element_type=F32)
        bias = bias_scr[rows(kc), :]
        new_ms, new_ls = [], []
        for h in range(N_HEADS):
            hs = slice(h * HEAD_DIM, (h + 1) * HEAD_DIM)
            s = s_scr[h] + bias
            m_new = jnp.maximum(ms[h], jnp.max(s, axis=0, keepdims=True))
            alpha = jnp.exp(ms[h] - m_new)
            p = jnp.exp(s - m_new)
            new_ms.append(m_new)
            new_ls.append(alpha * ls[h] + jnp.sum(p, axis=0, keepdims=True))
            pv = jnp.dot(vt_ref[hs, rows(kc)], p.astype(BF16), preferred_element_type=F32)
            acc_scr[hs, :] = alpha * acc_scr[hs, :] + pv
        return tuple(new_ms), tuple(new_ls)

    init = ((jnp.full((1, TQ), NEG, F32),) * N_HEADS, (jnp.zeros((1, TQ), F32),) * N_HEADS)
    _, ls = lax.fori_loop(0, nk, att_body, init)
    for h in range(N_HEADS):
        hs = slice(h * HEAD_DIM, (h + 1) * HEAD_DIM)
        o_scr[:, hs] = (acc_scr[hs, :] / ls[h]).T.astype(BF16)
    out_ref[...] = x_ref[...] + jnp.dot(o_scr[...], wo_ref[...], preferred_element_type=F32)


def _dsa_prompt(q, kb, vt, qi, kiw, wit, x, wo, bsz, seq):
    n = x.shape[0]
    nq = seq // TQ
    qrow = lambda w: pl.BlockSpec((TQ, w), lambda b, i: (b * nq + i, 0))
    brow = lambda w: pl.BlockSpec((seq, w), lambda b, i: (b, 0))
    return pl.pallas_call(
        _dsa_prompt_kernel,
        grid=(bsz, nq),
        in_specs=[qrow(D_MODEL), brow(D_MODEL), pl.BlockSpec((None, D_MODEL, seq), lambda b, i: (b, 0, 0)),
                  qrow(IDX_HEADS * LANES), brow(LANES), pl.BlockSpec((IDX_HEADS, TQ), lambda b, i: (0, b * nq + i)),
                  qrow(D_MODEL), _const_spec(wo.shape)],
        out_specs=qrow(D_MODEL),
        out_shape=jax.ShapeDtypeStruct((n, D_MODEL), F32),
        scratch_shapes=[pltpu.VMEM((seq, TQ), I32), pltpu.VMEM((seq, TQ), F32), pltpu.VMEM((seq, LANES), BF16),
                        pltpu.VMEM((TQ, D_MODEL), BF16), pltpu.VMEM((D_MODEL, TQ), F32),
                        pltpu.VMEM((N_HEADS, TK, TQ), F32)],
        compiler_params=_cparams("arbitrary", "arbitrary"),
        name="dsa_prompt",
    )(q, kb, vt, qi, kiw, wit, x, wo)


def _glu_hidden(a_scr, wup_ref, wc_ref, bc_ref, h_scr, gate_taps):
    for c in range(N_FF_CHUNKS):
        cs = slice(c * FF_CHUNK, (c + 1) * FF_CHUNK)
        gate = jnp.dot(a_scr[...], wup_ref[:, cs], preferred_element_type=F32)
        val = jnp.dot(a_scr[...], wup_ref[:, D_FF + c * FF_CHUNK:D_FF + (c + 1) * FF_CHUNK],
                      preferred_element_type=F32)
        g2, g1 = gate_taps(c, gate)
        gc = bc_ref[:, cs] + wc_ref[0:1, cs] * g2 + wc_ref[1:2, cs] * g1 + wc_ref[2:3, cs] * gate
        h_scr[:, cs] = (jax.nn.gelu(gc) * val).astype(BF16)


def _ffn_prompt_kernel(x_ref, g_ref, wup_ref, wc_ref, bc_ref, wdn_ref, gf_ref, out_ref, tail_ref,
                       a_scr, h_scr, carry_scr, *, final_norm):
    j = pl.program_id(1)
    rows = x_ref.shape[0]
    a_scr[...] = _rms(x_ref[...], g_ref[...]).astype(BF16)

    @pl.when(j == 0)
    def _():
        carry_scr[...] = jnp.zeros(carry_scr.shape, F32)

    def gate_taps(c, gate):
        cs = slice(c * FF_CHUNK, (c + 1) * FF_CHUNK)
        p0 = carry_scr[SUBLANES - 2:SUBLANES - 1, cs]
        p1 = carry_scr[SUBLANES - 1:SUBLANES, cs]
        g1 = _shift_rows(gate, 1, [p1])
        g2 = _shift_rows(gate, 2, [p0, p1])
        carry_scr[:, cs] = gate[rows - SUBLANES:, :]
        tail_ref[0, :, cs] = gate[rows - SUBLANES:, :]
        return g2, g1

    _glu_hidden(a_scr, wup_ref, wc_ref, bc_ref, h_scr, gate_taps)
    y = x_ref[...] + jnp.dot(h_scr[...], wdn_ref[...], preferred_element_type=F32)
    out_ref[...] = _rms(y, gf_ref[...]) if final_norm else y


def _ffn_prompt(x, g, wup, wc, bc, wdn, gf, bsz, seq, final_norm):
    n = x.shape[0]
    nb = seq // ROW_BLOCK
    row = pl.BlockSpec((ROW_BLOCK, D_MODEL), lambda b, j: (b * nb + j, 0))
    (wup, lu), (wdn, ld) = wup, wdn
    return pl.pallas_call(
        functools.partial(_ffn_prompt_kernel, final_norm=final_norm),
        grid=(bsz, nb),
        in_specs=[row, _const_spec((1, D_MODEL)), _layer_spec(wup, lu), _const_spec(wc.shape),
                  _const_spec(bc.shape), _layer_spec(wdn, ld), _const_spec((1, D_MODEL))],
        out_specs=[row, pl.BlockSpec((1, SUBLANES, D_FF), lambda b, j: (b, 0, 0))],
        out_shape=[jax.ShapeDtypeStruct((n, D_MODEL), F32), jax.ShapeDtypeStruct((bsz, SUBLANES, D_FF), F32)],
        scratch_shapes=[pltpu.VMEM((ROW_BLOCK, D_MODEL), BF16), pltpu.VMEM((ROW_BLOCK, D_FF), BF16),
                        pltpu.VMEM((SUBLANES, D_FF), F32)],
        compiler_params=_cparams("arbitrary", "arbitrary"),
        name="ffn_prompt_final" if final_norm else "ffn_prompt",
    )(x, g, wup, wc, bc, wdn, gf)


def _sigmoid(x):
    return 0.5 * jnp.tanh(0.5 * x) + 0.5


def _lru_coeffs(xc, gg, bga, bgx, neg_c_softplus):
    r = _sigmoid(gg[:, :LANES] + bga)
    ig = _sigmoid(gg[:, LANES:] + bgx)
    log_a = r * neg_c_softplus
    a = jnp.exp(log_a)
    return a, jnp.sqrt(-jnp.tanh(log_a) * (a * a + 1.0)) * ig * xc


def _rec_prompt_kernel(x_ref, g_ref, win_ref, wc_ref, bc_ref, wg_ref, bga_ref, bgx_ref, lam_ref, wo_ref,
                       out_ref, hlast_ref, ctail_ref, a_scr, at_scr, bt_scr, u_scr, hcar_scr, ccar_scr):
    j = pl.program_id(1)
    rows = x_ref.shape[0]
    a_scr[...] = _rms(x_ref[...], g_ref[...]).astype(BF16)

    @pl.when(j == 0)
    def _():
        ccar_scr[...] = jnp.zeros(ccar_scr.shape, F32)
        hcar_scr[...] = jnp.zeros(hcar_scr.shape, F32)

    nblk = MXU_N // LANES
    ncs = -RG_C * jax.nn.softplus(-lam_ref[...])
    for c in range(D_MODEL // MXU_N):
        cs = slice(c * MXU_N, (c + 1) * MXU_N)
        xb = jnp.dot(a_scr[...], win_ref[:, D_MODEL + c * MXU_N:D_MODEL + (c + 1) * MXU_N],
                     preferred_element_type=F32)
        prev = [ccar_scr[SUBLANES - 3 + k:SUBLANES - 2 + k, cs] for k in range(3)]
        xc = (bc_ref[:, cs] + wc_ref[0:1, cs] * _shift_rows(xb, 3, prev) + wc_ref[1:2, cs] * _shift_rows(xb, 2, prev[1:])
              + wc_ref[2:3, cs] * _shift_rows(xb, 1, prev[2:]) + wc_ref[3:4, cs] * xb)
        ccar_scr[:, cs] = xb[rows - SUBLANES:, :]
        ctail_ref[0, :, cs] = xb[rows - SUBLANES:, :]
        for k in range(nblk):
            n = c * nblk + k
            ls = slice(n * LANES, (n + 1) * LANES)
            xcn = xc[:, k * LANES:(k + 1) * LANES]
            gg = jnp.dot(xcn.astype(BF16), wg_ref[n], preferred_element_type=F32)
            a_t, b_t = _lru_coeffs(xcn, gg, bga_ref[:, ls], bgx_ref[:, ls], ncs[:, ls])
            at_scr[:, ls] = a_t
            bt_scr[:, ls] = b_t

    def step(t, h):
        h = at_scr[pl.ds(t, 1), :] * h + bt_scr[pl.ds(t, 1), :]
        bt_scr[pl.ds(t, 1), :] = h
        return h

    h_last = lax.fori_loop(0, rows, step, hcar_scr[0:1, :], unroll=8)
    hcar_scr[...] = jnp.broadcast_to(h_last, hcar_scr.shape)
    hlast_ref[0] = jnp.broadcast_to(h_last, hcar_scr.shape)
    for c in range(D_MODEL // MXU_N):
        cs = slice(c * MXU_N, (c + 1) * MXU_N)
        gate = jnp.dot(a_scr[...], win_ref[:, cs], preferred_element_type=F32)
        u_scr[:, cs] = (jax.nn.gelu(gate) * bt_scr[:, cs]).astype(BF16)
    out_ref[...] = x_ref[...] + jnp.dot(u_scr[...], wo_ref[...], preferred_element_type=F32)


def _rec_prompt(x, g, win, wc, bc, wg, bga, bgx, lam, wo, bsz, seq):
    n = x.shape[0]
    nb = seq // ROW_BLOCK
    row = pl.BlockSpec((ROW_BLOCK, D_MODEL), lambda b, j: (b * nb + j, 0))
    tail = pl.BlockSpec((1, SUBLANES, D_MODEL), lambda b, j: (b, 0, 0))
    vec = _const_spec((1, D_MODEL))
    return pl.pallas_call(
        _rec_prompt_kernel,
        grid=(bsz, nb),
        in_specs=[row, vec, _const_spec(win.shape), _const_spec(wc.shape), vec, _const_spec(wg.shape), vec, vec,
                  vec, _const_spec(wo.shape)],
        out_specs=[row, tail, tail],
        out_shape=[jax.ShapeDtypeStruct((n, D_MODEL), F32), jax.ShapeDtypeStruct((bsz, SUBLANES, D_MODEL), F32),
                   jax.ShapeDtypeStruct((bsz, SUBLANES, D_MODEL), F32)],
        scratch_shapes=[pltpu.VMEM((ROW_BLOCK, D_MODEL), BF16), pltpu.VMEM((ROW_BLOCK, D_MODEL), F32),
                        pltpu.VMEM((ROW_BLOCK, D_MODEL), F32), pltpu.VMEM((ROW_BLOCK, D_MODEL), BF16),
                        pltpu.VMEM((SUBLANES, D_MODEL), F32), pltpu.VMEM((SUBLANES, D_MODEL), F32)],
        compiler_params=_cparams("arbitrary", "arbitrary"),
        name="rec_prompt",
    )(x, g, win, wc, bc, wg, bga, bgx, lam, wo)


def _idx_sample_kernel(pt_ref, qi_ref, wi_ref, kin_ref, *refs, n_pages):
    pages, out_ref = refs[:-1], refs[-1]
    lane = lax.broadcasted_iota(I32, (1, LANES), 1)
    for g in range(IDX_SEQS):
        qi = qi_ref[g]
        wi = wi_ref[g]
        for p in range(n_pages):
            page_t = pages[g * n_pages + p][...].astype(BF16)
            s = jnp.dot(qi[:, :IDX_DIM], page_t, preferred_element_type=F32)
            out_ref[g, :, p * PAGE:(p + 1) * PAGE] = (
                jnp.sum(wi * jnp.maximum(s, 0.0), axis=0, keepdims=True) * IDX_SCALE)
        s_new = jnp.sum(qi.astype(F32) * kin_ref[g].astype(BF16).astype(F32), axis=-1, keepdims=True)
        sc_new = jnp.sum(wi * jnp.maximum(s_new, 0.0), axis=0, keepdims=True) * IDX_SCALE
        out_ref[g, :, n_pages * PAGE:] = jnp.where(lane == 0, sc_new, -jnp.inf)


def _idx_sample(pt, qi3, wi3, kin3, pool_ki_t, n_pages):
    db = qi3.shape[0]
    assert db % IDX_SEQS == 0
    per_step = lambda shape: pl.BlockSpec((IDX_SEQS,) + shape, lambda d, pt: (d, 0, 0))
    page_spec = lambda g, p: pl.BlockSpec((None, IDX_DIM, PAGE),
                                          lambda d, pt: (pt[(d * IDX_SEQS + g) * n_pages + p], 0, 0))
    width = n_pages * PAGE + LANES
    return pl.pallas_call(
        functools.partial(_idx_sample_kernel, n_pages=n_pages),
        grid_spec=pltpu.PrefetchScalarGridSpec(
            num_scalar_prefetch=1, grid=(db // IDX_SEQS,),
            in_specs=[per_step((IDX_HEADS, LANES)), per_step((IDX_HEADS, 1)), per_step((1, LANES))]
            + [page_spec(g, p) for g in range(IDX_SEQS) for p in range(n_pages)],
            out_specs=per_step((1, width))),
        out_shape=jax.ShapeDtypeStruct((db, 1, width), F32),
        compiler_params=_cparams("arbitrary"),
        name="idx_sample",
    )(pt, qi3, wi3, kin3, *([pool_ki_t] * (IDX_SEQS * n_pages)))


def _select_sample_kernel(sc_ref, bias_ref, key_scr, *, n_keys):
    lane = lax.broadcasted_iota(I32, (1, sc_ref.shape[1]), 1)
    key_scr[...] = jnp.where(lane < n_keys, _sortable_key(sc_ref[...]), jnp.int32(INT_MIN))

    def count_ge(cand):
        return jnp.sum(jnp.where(key_scr[...] >= cand, 1.0, 0.0), axis=-1, keepdims=True)

    thr, _ = _kth_largest_key(count_ge, float(sc_ref.shape[1]))
    need = float(TOPK) - count_ge(thr + 1)

    def count_tie_le(cm):
        return jnp.sum(jnp.where((key_scr[...] == thr) & (lane <= cm), 1.0, 0.0), axis=-1, keepdims=True)

    cut = _tie_cutoff(count_tie_le, need, 12)
    cut = jnp.where(thr == INT_MIN, jnp.int32(-1), cut)
    key = key_scr[...]
    sel = (key > thr) | ((key == thr) & (lane <= cut))
    bias_ref[...] = jnp.where(sel, 0.0, NEG)


def _select_sample(scores, n_keys):
    return pl.pallas_call(
        functools.partial(_select_sample_kernel, n_keys=n_keys),
        out_shape=jax.ShapeDtypeStruct(scores.shape, F32),
        scratch_shapes=[pltpu.VMEM(scores.shape, I32)],
        compiler_params=_cparams(),
        name="select_sample",
    )(scores)


def _attn_sample_kernel(pt_ref, q_ref, kn_ref, vn_ref, bias_ref, *refs, n_pages):
    kpages, vpages = refs[:n_pages], refs[n_pages:2 * n_pages]
    o_ref, kall, vall = refs[2 * n_pages:]
    page_rows = PAGE * N_HEADS
    past_rows = n_pages * page_rows
    for p in range(n_pages):
        kall[p * page_rows:(p + 1) * page_rows, :] = kpages[p][...].astype(BF16)
        vall[p * page_rows:(p + 1) * page_rows, :] = vpages[p][...].astype(BF16)
    zeros = jnp.zeros((LANES - N_HEADS, HEAD_DIM), F32)
    kall[past_rows:, :] = jnp.concatenate([kn_ref[...], zeros], axis=0).astype(BF16)
    vall[past_rows:, :] = jnp.concatenate([vn_ref[...], zeros], axis=0).astype(BF16)
    s = lax.dot_general(q_ref[...], kall[...], NT_DIMS, preferred_element_type=F32)
    own = (lax.broadcasted_iota(I32, s.shape, 1) & (N_HEADS - 1)) == lax.broadcasted_iota(I32, s.shape, 0)
    s = jnp.where(own, s + bias_ref[...], NEG)
    m = jnp.max(s, axis=-1, keepdims=True)
    p = jnp.exp(s - m)
    l = jnp.sum(p, axis=-1, keepdims=True)
    o_ref[...] = jnp.dot(p.astype(BF16), vall[...], preferred_element_type=F32) / l


def _attn_sample(pt, q3, kn3, vn3, bias3, pool_k, pool_v, n_pages):
    db = q3.shape[0]
    page_rows = PAGE * N_HEADS
    width = n_pages * page_rows + LANES
    per_seq = lambda r, w: pl.BlockSpec((None, r, w), lambda d, pt: (d, 0, 0))
    page_spec = lambda p: pl.BlockSpec((None, page_rows, HEAD_DIM), lambda d, pt: (pt[d * n_pages + p], 0, 0))
    return pl.pallas_call(
        functools.partial(_attn_sample_kernel, n_pages=n_pages),
        grid_spec=pltpu.PrefetchScalarGridSpec(
            num_scalar_prefetch=1, grid=(db,),
            in_specs=[per_seq(N_HEADS, HEAD_DIM)] * 3 + [per_seq(1, width)]
            + [page_spec(p) for p in range(n_pages)] * 2,
            out_specs=per_seq(N_HEADS, HEAD_DIM),
            scratch_shapes=[pltpu.VMEM((width, HEAD_DIM), BF16), pltpu.VMEM((width, HEAD_DIM), BF16)]),
        out_shape=jax.ShapeDtypeStruct((db, N_HEADS, HEAD_DIM), F32),
        compiler_params=_cparams("arbitrary"),
        name="attn_sample",
    )(pt, q3, kn3, vn3, bias3, *([pool_k] * n_pages), *([pool_v] * n_pages))


def _ffn_sample_body(x, g_ref, wup_ref, wc_ref, bc_ref, wdn_ref, st0_ref, st1_ref, gate_ref, a_scr, h_scr):
    a_scr[...] = _rms(x, g_ref[...]).astype(BF16)

    def gate_taps(c, gate):
        cs = slice(c * FF_CHUNK, (c + 1) * FF_CHUNK)
        gate_ref[:, cs] = gate
        return st0_ref[:, cs], st1_ref[:, cs]

    _glu_hidden(a_scr, wup_ref, wc_ref, bc_ref, h_scr, gate_taps)
    return x + jnp.dot(h_scr[...], wdn_ref[...], preferred_element_type=F32)


def _layer0_sample_kernel(x_ref, o_ref, wo_ref, g_ref, wup_ref, wc_ref, bc_ref, wdn_ref, st0_ref, st1_ref,
                          out_ref, gate_ref, a_scr, h_scr):
    x = x_ref[...] + jnp.dot(o_ref[...].astype(BF16), wo_ref[...], preferred_element_type=F32)
    out_ref[...] = _ffn_sample_body(x, g_ref, wup_ref, wc_ref, bc_ref, wdn_ref, st0_ref, st1_ref, gate_ref,
                                    a_scr, h_scr)


def _layer1_sample_kernel(x_ref, gm_ref, win_ref, wcr_ref, bcr_ref, wg_ref, bga_ref, bgx_ref, lam_ref, wor_ref,
                          cs0_ref, cs1_ref, cs2_ref, h0_ref, g_ref, wup_ref, wc_ref, bc_ref, wdn_ref, st0_ref,
                          st1_ref, gf_ref, out_ref, hnew_ref, xb_ref, gate_ref, a_scr, h_scr, u_scr):
    x = x_ref[...]
    a_scr[...] = _rms(x, gm_ref[...]).astype(BF16)
    nblk = MXU_N // LANES
    ncs = -RG_C * jax.nn.softplus(-lam_ref[...])
    for c in range(D_MODEL // MXU_N):
        cs = slice(c * MXU_N, (c + 1) * MXU_N)
        xb = jnp.dot(a_scr[...], win_ref[:, D_MODEL + c * MXU_N:D_MODEL + (c + 1) * MXU_N],
                     preferred_element_type=F32)
        gate = jnp.dot(a_scr[...], win_ref[:, cs], preferred_element_type=F32)
        xb_ref[:, cs] = xb
        xc = (bcr_ref[:, cs] + wcr_ref[0:1, cs] * cs0_ref[:, cs] + wcr_ref[1:2, cs] * cs1_ref[:, cs]
              + wcr_ref[2:3, cs] * cs2_ref[:, cs] + wcr_ref[3:4, cs] * xb)
        for k in range(nblk):
            n = c * nblk + k
            ls = slice(n * LANES, (n + 1) * LANES)
            xcn = xc[:, k * LANES:(k + 1) * LANES]
            gg = jnp.dot(xcn.astype(BF16), wg_ref[n], preferred_element_type=F32)
            a_t, b_t = _lru_coeffs(xcn, gg, bga_ref[:, ls], bgx_ref[:, ls], ncs[:, ls])
            h = a_t * h0_ref[:, ls] + b_t
            hnew_ref[:, ls] = h
            u_scr[:, ls] = (jax.nn.gelu(gate[:, k * LANES:(k + 1) * LANES]) * h).astype(BF16)
    x = x + jnp.dot(u_scr[...], wor_ref[...], preferred_element_type=F32)
    y = _ffn_sample_body(x, g_ref, wup_ref, wc_ref, bc_ref, wdn_ref, st0_ref, st1_ref, gate_ref, a_scr, h_scr)
    out_ref[...] = _rms(y, gf_ref[...])


def _layer_spec(stacked, layer):
    nd = stacked.ndim - 1
    return pl.BlockSpec((None,) + stacked.shape[1:], lambda *_: (layer,) + (0,) * nd, pipeline_mode=pl.Buffered(1))


def _sample_call(kernel, name, args, out_widths, extra_scratch=()):
    db = args[0].shape[0]
    specs = [_layer_spec(*a) if isinstance(a, tuple) else _const_spec(a.shape) for a in args]
    arrays = [a[0] if isinstance(a, tuple) else a for a in args]
    return pl.pallas_call(
        kernel,
        grid=(1,),
        in_specs=specs,
        out_specs=[pl.BlockSpec((db, w), lambda i: (0, 0)) for w in out_widths],
        out_shape=[jax.ShapeDtypeStruct((db, w), F32) for w in out_widths],
        scratch_shapes=[pltpu.VMEM((db, D_MODEL), BF16), pltpu.VMEM((db, D_FF), BF16), *extra_scratch],
        compiler_params=_cparams("arbitrary"),
        name=name,
    )(*arrays)


def _rope_tables(pos):
    posf = pos.astype(F32)[:, None]

    def cs(d):
        half = d // 2
        inv = ROPE_THETA ** (-jnp.arange(half, dtype=F32) * 2.0 / d)
        ang = posf * inv[None, :]
        return jnp.cos(ang), jnp.sin(ang)

    c, s = cs(HEAD_DIM)
    cos = jnp.concatenate([c, c], axis=-1)
    sin = jnp.concatenate([-s, s], axis=-1)
    c, s = cs(IDX_DIM)
    one, zero = jnp.ones_like(c), jnp.zeros_like(c)
    tile = lambda parts: jnp.concatenate(parts * (IDX_HEADS + 1), axis=-1)
    icos = tile([c, c, one, one])
    isa = tile([-s, zero, zero, zero])
    isb = tile([zero, s, zero, zero])
    return cos, sin, icos, isa, isb


def _split_attn_in(w):
    qkv = N_HEADS * HEAD_DIM
    wqk, wv, wi = w[:, :2 * qkv], w[:, 2 * qkv:3 * qkv], w[:, 3 * qkv:]
    pad = lambda a: jnp.pad(a, ((0, 0), (0, LANES - a.shape[1])))
    groups = [pad(wi[:, h * IDX_DIM:(h + 1) * IDX_DIM]) for h in range(IDX_HEADS)]
    groups.append(pad(wi[:, IDX_HEADS * IDX_DIM:]))
    return wqk.astype(BF16), wv.astype(BF16), jnp.concatenate(groups, axis=-1).astype(BF16)


def kernel(x_prompt, x_sample, cache_k, cache_v, cache_kidx, state_lru_h, state_lru_conv, state_ffn_conv,
           page_table, norm_mix, norm_ffn, norm_final, w_attn_in, w_attn_out, w_rec_in, w_rec_conv, b_rec_conv,
           w_gate_a, b_gate_a, w_gate_x, b_gate_x, lru_lambda, w_rec_out, w_ffn_up, w_ffn_conv, b_ffn_conv,
           w_ffn_down):
    bsz, seq, d = x_prompt.shape
    db = x_sample.shape[0]
    n_pages = page_table.shape[1]
    past = n_pages * PAGE
    assert d == D_MODEL and x_sample.shape[1] == 1 and seq % ROW_BLOCK == 0 and seq % TQ == 0
    assert min(TOPK, seq // 4) == TOPK and min(TOPK, (past + 1) // 4) == TOPK

    vec = lambda a: a.reshape(1, -1)
    wqk, wv, widx = _split_attn_in(w_attn_in[0])
    wo_attn = w_attn_out[0].astype(BF16)
    w_rin = w_rec_in[0].astype(BF16)
    w_gates = jnp.concatenate([w_gate_a[0], w_gate_x[0]], axis=-1).astype(BF16)
    wo_rec = w_rec_out[0].astype(BF16)
    wup_all, wdn_all = w_ffn_up.astype(BF16), w_ffn_down.astype(BF16)
    wup = [(wup_all, i) for i in range(2)]
    wdn = [(wdn_all, i) for i in range(2)]
    rec_vecs = (w_rec_conv[0], vec(b_rec_conv[0]), w_gates, vec(b_gate_a[0]), vec(b_gate_x[0]),
                vec(lru_lambda[0]), wo_rec)
    ffn_vecs = [(vec(norm_ffn[i]), wup[i], w_ffn_conv[i], vec(b_ffn_conv[i]), wdn[i]) for i in range(2)]
    gfin = vec(norm_final)

    xp = x_prompt.reshape(bsz * seq, d)
    q, kf, kb, vf, vt, qi, kiw = _attn_in(xp, vec(norm_mix[0]), wqk, wv, widx, _rope_tables(jnp.arange(seq)),
                                          ROW_BLOCK, seq // ROW_BLOCK)
    wit = kiw[:, IDX_DIM:IDX_DIM + IDX_HEADS].T
    hp = _dsa_prompt(q, kb, vt, qi, kiw, wit, xp, wo_attn, bsz, seq)
    hp, ftail0 = _ffn_prompt(hp, *ffn_vecs[0], gfin, bsz, seq, False)
    hp, hlast, ctail = _rec_prompt(hp, vec(norm_mix[1]), w_rin, *rec_vecs, bsz, seq)
    yp, ftail1 = _ffn_prompt(hp, *ffn_vecs[1], gfin, bsz, seq, True)

    xs = x_sample.reshape(db, d)
    tabs = tuple(jnp.broadcast_to(t, (db, t.shape[1])) for t in _rope_tables(jnp.full((1,), past)))
    qs, kfs, _, vfs, _, qis, kiws = _attn_in(xs, vec(norm_mix[0]), wqk, wv, widx, tabs, db, 1)
    pt = page_table.reshape(-1)
    scores = _idx_sample(pt, qis.reshape(db, IDX_HEADS, LANES),
                         kiws[:, IDX_DIM:IDX_DIM + IDX_HEADS].reshape(db, IDX_HEADS, 1),
                         kiws.reshape(db, 1, LANES), jnp.swapaxes(cache_kidx[0], 1, 2), n_pages)
    bias = _select_sample(scores.reshape(db, -1), past + 1)
    bias_rows = jnp.pad(jnp.repeat(bias[:, :past + 1], N_HEADS, axis=1), ((0, 0), (0, LANES - N_HEADS)),
                        constant_values=NEG)
    heads = (N_HEADS, HEAD_DIM)
    flat_rows = lambda pool: pool[0].reshape(-1, PAGE * N_HEADS, HEAD_DIM)
    o_s = _attn_sample(pt, qs.reshape(db, *heads), kfs.reshape(db, *heads), vfs.reshape(db, *heads),
                       bias_rows.reshape(db, 1, -1), flat_rows(cache_k), flat_rows(cache_v), n_pages)
    st = state_ffn_conv
    hs, gate0 = _sample_call(_layer0_sample_kernel, "layer0_sample",
                             (xs, o_s.reshape(db, d), wo_attn, *ffn_vecs[0], st[0, :, 0], st[0, :, 1]),
                             (D_MODEL, D_FF))
    cst = state_lru_conv[0]
    ys, hnew, xbs, gate1 = _sample_call(
        _layer1_sample_kernel, "layer1_sample",
        (hs, vec(norm_mix[1]), w_rin, *rec_vecs, cst[:, 0], cst[:, 1], cst[:, 2], state_lru_h[0], *ffn_vecs[1],
         st[1, :, 0], st[1, :, 1], gfin),
        (D_MODEL, D_MODEL, D_MODEL, D_FF), extra_scratch=(pltpu.VMEM((db, D_MODEL), BF16),))

    return (
        yp.reshape(bsz, seq, d),
        ys.reshape(db, 1, d),
        kf.reshape(1, bsz, seq, *heads),
        vf.reshape(1, bsz, seq, *heads),
        kiw[:, :IDX_DIM].reshape(1, bsz, seq, IDX_DIM),
        kfs.reshape(1, db, 1, *heads),
        vfs.reshape(1, db, 1, *heads),
        kiws[:, :IDX_DIM].reshape(1, db, 1, IDX_DIM),
        hlast[None, :, 0, :],
        ctail[None, :, SUBLANES - 3:, :],
        hnew[None],
        jnp.stack([cst[:, 1], cst[:, 2], xbs], axis=1)[None],
        jnp.stack([ftail0[:, SUBLANES - 2:], ftail1[:, SUBLANES - 2:]]),
        jnp.stack([jnp.stack([st[0, :, 1], gate0], axis=1), jnp.stack([st[1, :, 1], gate1], axis=1)]),
    )
```

```python
import functools

import jax
import jax.numpy as jnp
from jax import lax
from jax.experimental import pallas as pl
from jax.experimental.pallas import tpu as pltpu
from jax.experimental.pallas import tpu_sc as plsc

F32 = jnp.float32
BF16 = jnp.bfloat16
I32 = jnp.int32

D_MODEL = 1024
N_HEADS = 8
HEAD_DIM = 128
IDX_HEADS = 4
IDX_DIM = 64
TOPK = 256
PAGE = 128
ROPE_THETA = 10000.0
IDX_SCALE = (IDX_DIM * IDX_HEADS) ** -0.5
Q_SCALE = HEAD_DIM ** -0.5
RG_C = 8.0
D_FF = 2816
RMS_EPS = 1e-6

LANES = 128
SUBLANES = 8
MXU_N = 256
IDX_W = (IDX_HEADS + 1) * LANES
INT_MIN = -(2 ** 31)
NEG = -1e30
VMEM_LIMIT = 52 * 1024 * 1024

ROW_BLOCK = 512
TQ = 256
TK = 256
COUNT_ROWS = 32
IDX_SEQS = 4
SC_GATHER_ROWS = 32
FF_CHUNK = 256
N_FF_CHUNKS = D_FF // FF_CHUNK
NT_DIMS = (((1,), (1,)), ((), ()))


def _cparams(*sem):
    return pltpu.CompilerParams(dimension_semantics=sem if sem else None, vmem_limit_bytes=VMEM_LIMIT)


def _const_spec(shape):
    nd = len(shape)
    return pl.BlockSpec(shape, lambda *_: (0,) * nd, pipeline_mode=pl.Buffered(1))


def _rms(x, g):
    return x * lax.rsqrt(jnp.mean(x * x, axis=-1, keepdims=True) + RMS_EPS) * g


def _shift_rows(x, s, prev_rows):
    r = pltpu.roll(x, s, 0)
    top = r[:SUBLANES]
    row = lax.broadcasted_iota(I32, top.shape, 0)
    for k in range(s):
        top = jnp.where(row == k, prev_rows[k], top)
    return jnp.concatenate([top, r[SUBLANES:]], axis=0)


def _sortable_key(score):
    bits = pltpu.bitcast(score + 0.0, I32)
    return jnp.where(bits < 0, bits ^ jnp.int32(0x7FFFFFFF), bits)


def _attn_in_kernel(x_ref, g_ref, wqk_ref, wv_ref, widx_ref, cos_ref, sin_ref, icos_ref, isa_ref, isb_ref,
                    q_ref, kf_ref, kb_ref, vf_ref, vt_ref, qi_ref, kiw_ref, a_scr):
    a_scr[...] = _rms(x_ref[...], g_ref[...]).astype(BF16)
    cos = cos_ref[...]
    sin = sin_ref[...]
    for c in range(2 * D_MODEL // MXU_N):
        r = jnp.dot(a_scr[...], wqk_ref[:, c * MXU_N:(c + 1) * MXU_N], preferred_element_type=F32)
        for hh in range(MXU_N // HEAD_DIM):
            xh = r[:, hh * HEAD_DIM:(hh + 1) * HEAD_DIM]
            y = xh * cos + pltpu.roll(xh, HEAD_DIM // 2, 1) * sin
            col = (c * MXU_N) % D_MODEL + hh * HEAD_DIM
            if c < D_MODEL // MXU_N:
                q_ref[:, col:col + HEAD_DIM] = (y * Q_SCALE).astype(BF16)
            else:
                kf_ref[:, col:col + HEAD_DIM] = y
                kb_ref[:, col:col + HEAD_DIM] = y.astype(BF16)
    for c in range(D_MODEL // MXU_N):
        r = jnp.dot(a_scr[...], wv_ref[:, c * MXU_N:(c + 1) * MXU_N], preferred_element_type=F32)
        vf_ref[:, c * MXU_N:(c + 1) * MXU_N] = r
        vt_ref[c * MXU_N:(c + 1) * MXU_N, :] = r.T.astype(BF16)
    ri = jnp.dot(a_scr[...], widx_ref[...], preferred_element_type=F32)
    yi = (ri * icos_ref[...] + pltpu.roll(ri, IDX_W - IDX_DIM // 2, 1) * isa_ref[...]
          + pltpu.roll(ri, IDX_DIM // 2, 1) * isb_ref[...])
    qi_ref[...] = yi[:, :IDX_HEADS * LANES].astype(BF16)
    kiw_ref[...] = yi[:, IDX_HEADS * LANES:]


def _attn_in(x, g, wqk, wv, widx, tabs, rows, n_pos_blocks):
    n = x.shape[0]
    cos, sin, icos, isa, isb = tabs
    row_spec = lambda w: pl.BlockSpec((rows, w), lambda i: (i, 0))
    tab_spec = lambda w: pl.BlockSpec((rows, w), lambda i: (i % n_pos_blocks, 0))
    vt_spec = pl.BlockSpec((None, D_MODEL, rows), lambda i: (i // n_pos_blocks, 0, i % n_pos_blocks))
    return pl.pallas_call(
        _attn_in_kernel,
        grid=(n // rows,),
        in_specs=[row_spec(D_MODEL), _const_spec((1, D_MODEL)), _const_spec(wqk.shape), _const_spec(wv.shape),
                  _const_spec(widx.shape), tab_spec(HEAD_DIM), tab_spec(HEAD_DIM), tab_spec(IDX_W),
                  tab_spec(IDX_W), tab_spec(IDX_W)],
        out_specs=[row_spec(D_MODEL)] * 4 + [vt_spec, row_spec(IDX_HEADS * LANES), row_spec(LANES)],
        out_shape=[jax.ShapeDtypeStruct((n, D_MODEL), BF16), jax.ShapeDtypeStruct((n, D_MODEL), F32),
                   jax.ShapeDtypeStruct((n, D_MODEL), BF16), jax.ShapeDtypeStruct((n, D_MODEL), F32),
                   jax.ShapeDtypeStruct((n // (rows * n_pos_blocks), D_MODEL, rows * n_pos_blocks), BF16),
                   jax.ShapeDtypeStruct((n, IDX_HEADS * LANES), BF16),
                   jax.ShapeDtypeStruct((n, LANES), F32)],
        scratch_shapes=[pltpu.VMEM((rows, D_MODEL), BF16)],
        compiler_params=_cparams("arbitrary"),
        name="attn_in",
    )(x, g, wqk, wv, widx, cos, sin, icos, isa, isb)


def _kth_largest_key(count_ge, n_total):
    kf = float(TOPK)
    c0 = count_ge(0)
    t = jnp.where(c0 >= kf, jnp.int32(0), jnp.int32(INT_MIN))
    ct = jnp.where(c0 >= kf, c0, n_total)

    def bit_body(it, carry):
        t, ct = carry
        cand = t + (jnp.int32(1) << (30 - it))
        cnt = count_ge(cand)
        return jnp.where(cnt >= kf, cand, t), jnp.where(cnt >= kf, cnt, ct)

    return lax.fori_loop(0, 31, bit_body, (t, ct))


def _tie_cutoff(count_tie_le, need, n_bits):
    def bit_body(it, c):
        cand = c + (jnp.int32(1) << (n_bits - 1 - it))
        return jnp.where(count_tie_le(cand - 1) < need, cand, c)

    return lax.fori_loop(0, n_bits, bit_body, jnp.zeros_like(need, dtype=I32))


def _dsa_prompt_kernel(q_ref, kb_ref, vt_ref, qi_ref, kiwk_ref, wit_ref, x_ref, wo_ref, out_ref,
                       key_scr, bias_scr, kib_scr, o_scr, acc_scr, s_scr):
    i = pl.program_id(1)
    nk = i + 1

    @pl.when(i == 0)
    def _():
        kib_scr[...] = kiwk_ref[...].astype(BF16)

    qpos = i * TQ + lax.broadcasted_iota(I32, (1, TQ), 1)

    def rows(kc):
        return pl.ds(pl.multiple_of(kc * TK, TK), TK)

    def spos(kc):
        return kc * TK + lax.broadcasted_iota(I32, (TK, 1), 0)

    def colsum(a):
        return jnp.sum(a.reshape(TK // COUNT_ROWS, COUNT_ROWS, TQ), axis=0)

    wis = wit_ref[...] * IDX_SCALE

    def score_body(kc, carry):
        kic = kib_scr[rows(kc), :]
        sc = jnp.zeros((TK, TQ), F32)
        for h in range(IDX_HEADS):
            s = lax.dot_general(kic, qi_ref[:, h * LANES:(h + 1) * LANES], NT_DIMS, preferred_element_type=F32)
            sc = sc + wis[h:h + 1, :] * jnp.maximum(s, 0.0)
        key = _sortable_key(sc)
        key_scr[rows(kc), :] = jnp.where(spos(kc) <= qpos, key, jnp.int32(INT_MIN))
        return carry

    lax.fori_loop(0, nk, score_body, 0)

    def count_ge(cand):
        def body(kc, acc):
            return acc + colsum(jnp.where(key_scr[rows(kc), :] >= cand, 1.0, 0.0))
        acc = lax.fori_loop(0, nk, body, jnp.zeros((COUNT_ROWS, TQ), F32))
        return jnp.sum(acc, axis=0, keepdims=True)

    thr, cnt_thr = _kth_largest_key(count_ge, (nk * TK).astype(F32))
    short = thr == INT_MIN
    has_ties = jnp.max(jnp.where(short, 0.0, cnt_thr)) > float(TOPK)

    @pl.when(jnp.logical_not(has_ties))
    def _():
        thr_vis = jnp.maximum(thr, jnp.int32(INT_MIN + 1))

        def bias_body(kc, carry):
            bias_scr[rows(kc), :] = jnp.where(key_scr[rows(kc), :] >= thr_vis, 0.0, NEG)
            return carry

        lax.fori_loop(0, nk, bias_body, 0)

    @pl.when(has_ties)
    def _():
        need = float(TOPK) - count_ge(thr + 1)

        def eq_body(kc, carry):
            bias_scr[rows(kc), :] = jnp.where(key_scr[rows(kc), :] == thr, 1.0, 0.0)
            return carry

        lax.fori_loop(0, nk, eq_body, 0)

        def count_tie_le(cm):
            def body(kc, acc):
                return acc + colsum(jnp.where(spos(kc) <= cm, bias_scr[rows(kc), :], 0.0))
            acc = lax.fori_loop(0, nk, body, jnp.zeros((COUNT_ROWS, TQ), F32))
            return jnp.sum(acc, axis=0, keepdims=True)

        cut = jnp.where(short, jnp.int32(-1), _tie_cutoff(count_tie_le, need, 11))

        def bias_body(kc, carry):
            sel = (key_scr[rows(kc), :] > thr) | ((bias_scr[rows(kc), :] > 0.0) & (spos(kc) <= cut))
            bias_scr[rows(kc), :] = jnp.where(sel, 0.0, NEG)
            return carry

        lax.fori_loop(0, nk, bias_body, 0)

    acc_scr[...] = jnp.zeros(acc_scr.shape, F32)

    def att_body(kc, carry):
        ms, ls = carry
        for h in range(N_HEADS):
            hs = slice(h * HEAD_DIM, (h + 1) * HEAD_DIM)
            s_scr[h] = lax.dot_general(kb_ref[rows(kc), hs], q_ref[:, hs], NT_DIMS, preferred_element_type=F32)
        bias = bias_scr[rows(kc), :]
        new_ms, new_ls = [], []
        for h in range(N_HEADS):
            hs = slice(h * HEAD_DIM, (h + 1) * HEAD_DIM)
            s = s_scr[h] + bias
            m_new = jnp.maximum(ms[h], jnp.max(s, axis=0, keepdims=True))
            alpha = jnp.exp(ms[h] - m_new)
            p = jnp.exp(s - m_new)
            new_ms.append(m_new)
            new_ls.append(alpha * ls[h] + jnp.sum(p, axis=0, keepdims=True))
            pv = jnp.dot(vt_ref[hs, rows(kc)], p.astype(BF16), preferred_element_type=F32)
            acc_scr[hs, :] = alpha * acc_scr[hs, :] + pv
        return tuple(new_ms), tuple(new_ls)

    init = ((jnp.full((1, TQ), NEG, F32),) * N_HEADS, (jnp.zeros((1, TQ), F32),) * N_HEADS)
    _, ls = lax.fori_loop(0, nk, att_body, init)
    for h in range(N_HEADS):
        hs = slice(h * HEAD_DIM, (h + 1) * HEAD_DIM)
        o_scr[:, hs] = (acc_scr[hs, :] / ls[h]).T.astype(BF16)
    out_ref[...] = x_ref[...] + jnp.dot(o_scr[...], wo_ref[...], preferred_element_type=F32)


def _dsa_prompt(q, kb, vt, qi, kiw, wit, x, wo, bsz, seq):
    n = x.shape[0]
    nq = seq // TQ
    qrow = lambda w: pl.BlockSpec((TQ, w), lambda b, i: (b * nq + i, 0))
    brow = lambda w: pl.BlockSpec((seq, w), lambda b, i: (b, 0))
    return pl.pallas_call(
        _dsa_prompt_kernel,
        grid=(bsz, nq),
        in_specs=[qrow(D_MODEL), brow(D_MODEL), pl.BlockSpec((None, D_MODEL, seq), lambda b, i: (b, 0, 0)),
                  qrow(IDX_HEADS * LANES), brow(LANES), pl.BlockSpec((IDX_HEADS, TQ), lambda b, i: (0, b * nq + i)),
                  qrow(D_MODEL), _const_spec(wo.shape)],
        out_specs=qrow(D_MODEL),
        out_shape=jax.ShapeDtypeStruct((n, D_MODEL), F32),
        scratch_shapes=[pltpu.VMEM((seq, TQ), I32), pltpu.VMEM((seq, TQ), F32), pltpu.VMEM((seq, LANES), BF16),
                        pltpu.VMEM((TQ, D_MODEL), BF16), pltpu.VMEM((D_MODEL, TQ), F32),
                        pltpu.VMEM((N_HEADS, TK, TQ), F32)],
        compiler_params=_cparams("arbitrary", "arbitrary"),
        name="dsa_prompt",
    )(q, kb, vt, qi, kiw, wit, x, wo)


def _glu_hidden(a_scr, wup_ref, wc_ref, bc_ref, h_scr, gate_taps):
    for c in range(N_FF_CHUNKS):
        cs = slice(c * FF_CHUNK, (c + 1) * FF_CHUNK)
        gate = jnp.dot(a_scr[...], wup_ref[:, cs], preferred_element_type=F32)
        val = jnp.dot(a_scr[...], wup_ref[:, D_FF + c * FF_CHUNK:D_FF + (c + 1) * FF_CHUNK],
                      preferred_element_type=F32)
        g2, g1 = gate_taps(c, gate)
        gc = bc_ref[:, cs] + wc_ref[0:1, cs] * g2 + wc_ref[1:2, cs] * g1 + wc_ref[2:3, cs] * gate
        h_scr[:, cs] = (jax.nn.gelu(gc) * val).astype(BF16)


def _ffn_prompt_kernel(x_ref, g_ref, wup_ref, wc_ref, bc_ref, wdn_ref, gf_ref, out_ref, tail_ref,
                       a_scr, h_scr, carry_scr, *, final_norm):
    j = pl.program_id(1)
    rows = x_ref.shape[0]
    a_scr[...] = _rms(x_ref[...], g_ref[...]).astype(BF16)

    @pl.when(j == 0)
    def _():
        carry_scr[...] = jnp.zeros(carry_scr.shape, F32)

    def gate_taps(c, gate):
        cs = slice(c * FF_CHUNK, (c + 1) * FF_CHUNK)
        p0 = carry_scr[SUBLANES - 2:SUBLANES - 1, cs]
        p1 = carry_scr[SUBLANES - 1:SUBLANES, cs]
        g1 = _shift_rows(gate, 1, [p1])
        g2 = _shift_rows(gate, 2, [p0, p1])
        carry_scr[:, cs] = gate[rows - SUBLANES:, :]
        tail_ref[0, :, cs] = gate[rows - SUBLANES:, :]
        return g2, g1

    _glu_hidden(a_scr, wup_ref, wc_ref, bc_ref, h_scr, gate_taps)
    y = x_ref[...] + jnp.dot(h_scr[...], wdn_ref[...], preferred_element_type=F32)
    out_ref[...] = _rms(y, gf_ref[...]) if final_norm else y


def _ffn_prompt(x, g, wup, wc, bc, wdn, gf, bsz, seq, final_norm):
    n = x.shape[0]
    nb = seq // ROW_BLOCK
    row = pl.BlockSpec((ROW_BLOCK, D_MODEL), lambda b, j: (b * nb + j, 0))
    (wup, lu), (wdn, ld) = wup, wdn
    return pl.pallas_call(
        functools.partial(_ffn_prompt_kernel, final_norm=final_norm),
        grid=(bsz, nb),
        in_specs=[row, _const_spec((1, D_MODEL)), _layer_spec(wup, lu), _const_spec(wc.shape),
                  _const_spec(bc.shape), _layer_spec(wdn, ld), _const_spec((1, D_MODEL))],
        out_specs=[row, pl.BlockSpec((1, SUBLANES, D_FF), lambda b, j: (b, 0, 0))],
        out_shape=[jax.ShapeDtypeStruct((n, D_MODEL), F32), jax.ShapeDtypeStruct((bsz, SUBLANES, D_FF), F32)],
        scratch_shapes=[pltpu.VMEM((ROW_BLOCK, D_MODEL), BF16), pltpu.VMEM((ROW_BLOCK, D_FF), BF16),
                        pltpu.VMEM((SUBLANES, D_FF), F32)],
        compiler_params=_cparams("arbitrary", "arbitrary"),
        name="ffn_prompt_final" if final_norm else "ffn_prompt",
    )(x, g, wup, wc, bc, wdn, gf)


def _sigmoid(x):
    return 0.5 * jnp.tanh(0.5 * x) + 0.5


def _lru_coeffs(xc, gg, bga, bgx, neg_c_softplus):
    r = _sigmoid(gg[:, :LANES] + bga)
    ig = _sigmoid(gg[:, LANES:] + bgx)
    log_a = r * neg_c_softplus
    a = jnp.exp(log_a)
    return a, jnp.sqrt(-jnp.tanh(log_a) * (a * a + 1.0)) * ig * xc


def _rec_prompt_kernel(x_ref, g_ref, win_ref, wc_ref, bc_ref, wg_ref, bga_ref, bgx_ref, lam_ref, wo_ref,
                       out_ref, hlast_ref, ctail_ref, a_scr, at_scr, bt_scr, u_scr, hcar_scr, ccar_scr):
    j = pl.program_id(1)
    rows = x_ref.shape[0]
    a_scr[...] = _rms(x_ref[...], g_ref[...]).astype(BF16)

    @pl.when(j == 0)
    def _():
        ccar_scr[...] = jnp.zeros(ccar_scr.shape, F32)
        hcar_scr[...] = jnp.zeros(hcar_scr.shape, F32)

    nblk = MXU_N // LANES
    ncs = -RG_C * jax.nn.softplus(-lam_ref[...])
    for c in range(D_MODEL // MXU_N):
        cs = slice(c * MXU_N, (c + 1) * MXU_N)
        xb = jnp.dot(a_scr[...], win_ref[:, D_MODEL + c * MXU_N:D_MODEL + (c + 1) * MXU_N],
                     preferred_element_type=F32)
        prev = [ccar_scr[SUBLANES - 3 + k:SUBLANES - 2 + k, cs] for k in range(3)]
        xc = (bc_ref[:, cs] + wc_ref[0:1, cs] * _shift_rows(xb, 3, prev) + wc_ref[1:2, cs] * _shift_rows(xb, 2, prev[1:])
              + wc_ref[2:3, cs] * _shift_rows(xb, 1, prev[2:]) + wc_ref[3:4, cs] * xb)
        ccar_scr[:, cs] = xb[rows - SUBLANES:, :]
        ctail_ref[0, :, cs] = xb[rows - SUBLANES:, :]
        for k in range(nblk):
            n = c * nblk + k
            ls = slice(n * LANES, (n + 1) * LANES)
            xcn = xc[:, k * LANES:(k + 1) * LANES]
            gg = jnp.dot(xcn.astype(BF16), wg_ref[n], preferred_element_type=F32)
            a_t, b_t = _lru_coeffs(xcn, gg, bga_ref[:, ls], bgx_ref[:, ls], ncs[:, ls])
            at_scr[:, ls] = a_t
            bt_scr[:, ls] = b_t

    def step(t, h):
        h = at_scr[pl.ds(t, 1), :] * h + bt_scr[pl.ds(t, 1), :]
        bt_scr[pl.ds(t, 1), :] = h
        return h

    h_last = lax.fori_loop(0, rows, step, hcar_scr[0:1, :], unroll=8)
    hcar_scr[...] = jnp.broadcast_to(h_last, hcar_scr.shape)
    hlast_ref[0] = jnp.broadcast_to(h_last, hcar_scr.shape)
    for c in range(D_MODEL // MXU_N):
        cs = slice(c * MXU_N, (c + 1) * MXU_N)
        gate = jnp.dot(a_scr[...], win_ref[:, cs], preferred_element_type=F32)
        u_scr[:, cs] = (jax.nn.gelu(gate) * bt_scr[:, cs]).astype(BF16)
    out_ref[...] = x_ref[...] + jnp.dot(u_scr[...], wo_ref[...], preferred_element_type=F32)


def _rec_prompt(x, g, win, wc, bc, wg, bga, bgx, lam, wo, bsz, seq):
    n = x.shape[0]
    nb = seq // ROW_BLOCK
    row = pl.BlockSpec((ROW_BLOCK, D_MODEL), lambda b, j: (b * nb + j, 0))
    tail = pl.BlockSpec((1, SUBLANES, D_MODEL), lambda b, j: (b, 0, 0))
    vec = _const_spec((1, D_MODEL))
    return pl.pallas_call(
        _rec_prompt_kernel,
        grid=(bsz, nb),
        in_specs=[row, vec, _const_spec(win.shape), _const_spec(wc.shape), vec, _const_spec(wg.shape), vec, vec,
                  vec, _const_spec(wo.shape)],
        out_specs=[row, tail, tail],
        out_shape=[jax.ShapeDtypeStruct((n, D_MODEL), F32), jax.ShapeDtypeStruct((bsz, SUBLANES, D_MODEL), F32),
                   jax.ShapeDtypeStruct((bsz, SUBLANES, D_MODEL), F32)],
        scratch_shapes=[pltpu.VMEM((ROW_BLOCK, D_MODEL), BF16), pltpu.VMEM((ROW_BLOCK, D_MODEL), F32),
                        pltpu.VMEM((ROW_BLOCK, D_MODEL), F32), pltpu.VMEM((ROW_BLOCK, D_MODEL), BF16),
                        pltpu.VMEM((SUBLANES, D_MODEL), F32), pltpu.VMEM((SUBLANES, D_MODEL), F32)],
        compiler_params=_cparams("arbitrary", "arbitrary"),
        name="rec_prompt",
    )(x, g, win, wc, bc, wg, bga, bgx, lam, wo)


def _idx_sample_kernel(pt_ref, qi_ref, wi_ref, kin_ref, *refs, n_pages):
    pages, out_ref = refs[:-1], refs[-1]
    lane = lax.broadcasted_iota(I32, (1, LANES), 1)
    for g in range(IDX_SEQS):
        qi = qi_ref[g]
        wi = wi_ref[g]
        for p in range(n_pages):
            page_t = pages[g * n_pages + p][...].astype(BF16)
            s = jnp.dot(qi[:, :IDX_DIM], page_t, preferred_element_type=F32)
            out_ref[g, :, p * PAGE:(p + 1) * PAGE] = (
                jnp.sum(wi * jnp.maximum(s, 0.0), axis=0, keepdims=True) * IDX_SCALE)
        s_new = jnp.sum(qi.astype(F32) * kin_ref[g].astype(BF16).astype(F32), axis=-1, keepdims=True)
        sc_new = jnp.sum(wi * jnp.maximum(s_new, 0.0), axis=0, keepdims=True) * IDX_SCALE
        out_ref[g, :, n_pages * PAGE:] = jnp.where(lane == 0, sc_new, -jnp.inf)


def _idx_sample(pt, qi3, wi3, kin3, pool_ki_t, n_pages):
    db = qi3.shape[0]
    assert db % IDX_SEQS == 0
    per_step = lambda shape: pl.BlockSpec((IDX_SEQS,) + shape, lambda d, pt: (d, 0, 0))
    page_spec = lambda g, p: pl.BlockSpec((None, IDX_DIM, PAGE),
                                          lambda d, pt: (pt[(d * IDX_SEQS + g) * n_pages + p], 0, 0))
    width = n_pages * PAGE + LANES
    return pl.pallas_call(
        functools.partial(_idx_sample_kernel, n_pages=n_pages),
        grid_spec=pltpu.PrefetchScalarGridSpec(
            num_scalar_prefetch=1, grid=(db // IDX_SEQS,),
            in_specs=[per_step((IDX_HEADS, LANES)), per_step((IDX_HEADS, 1)), per_step((1, LANES))]
            + [page_spec(g, p) for g in range(IDX_SEQS) for p in range(n_pages)],
            out_specs=per_step((1, width))),
        out_shape=jax.ShapeDtypeStruct((db, 1, width), F32),
        compiler_params=_cparams("arbitrary"),
        name="idx_sample",
    )(pt, qi3, wi3, kin3, *([pool_ki_t] * (IDX_SEQS * n_pages)))


def _select_sample_kernel(sc_ref, rows_ref, ids_ref, bias_ref, key_scr, tri_scr, rank_scr, ids_scr, *, n_keys):
    lane = lax.broadcasted_iota(I32, (1, sc_ref.shape[1]), 1)
    key_scr[...] = jnp.where(lane < n_keys, _sortable_key(sc_ref[...]), jnp.int32(INT_MIN))

    def count_ge(cand):
        return jnp.sum(jnp.where(key_scr[...] >= cand, 1.0, 0.0), axis=-1, keepdims=True)

    thr, _ = _kth_largest_key(count_ge, float(sc_ref.shape[1]))
    need = float(TOPK) - count_ge(thr + 1)

    def count_tie_le(cm):
        return jnp.sum(jnp.where((key_scr[...] == thr) & (lane <= cm), 1.0, 0.0), axis=-1, keepdims=True)

    cut = _tie_cutoff(count_tie_le, need, 12)
    cut = jnp.where(thr == INT_MIN, jnp.int32(-1), cut)
    key = key_scr[...]
    sel = (key > thr) | ((key == thr) & (lane <= cut))

    past = n_keys - 1
    n_seq = sc_ref.shape[0]
    taken = jnp.where(sel, 1.0, 0.0)[:, :past]
    for r in range(past // TK):
        upper = (lax.broadcasted_iota(I32, (TK, past), 0) + r * TK) <= lax.broadcasted_iota(I32, (TK, past), 1)
        tri_scr[r * TK:(r + 1) * TK, :] = jnp.where(upper, 1.0, 0.0).astype(BF16)
    rank = jnp.dot(taken.astype(BF16), tri_scr[...], preferred_element_type=F32)
    rank_scr[...] = rank * taken
    n_taken = rank[:, past - 1:past]
    slot = (lax.broadcasted_iota(I32, (TOPK, 1), 0) + 1).astype(F32)
    seq_lane = lax.broadcasted_iota(I32, (1, n_seq), 1)
    ids_scr[...] = jnp.zeros(ids_scr.shape, F32)

    def seq_body(d, carry):
        hit = rank_scr[pl.ds(d, 1), :] == slot
        row = jnp.sum(jnp.where(hit, rows_ref[pl.ds(d, 1), :], 0.0), axis=-1, keepdims=True)
        ids_scr[...] += jnp.where(seq_lane == d, row, 0.0)
        return carry

    lax.fori_loop(0, n_seq, seq_body, 0)
    ids_ref[...] = ids_scr[...].T.astype(I32)
    blane = lax.broadcasted_iota(I32, bias_ref.shape, 1)
    new_taken = jnp.sum(jnp.where(sel & (lane == past), 1.0, 0.0), axis=-1, keepdims=True)
    live = (blane.astype(F32) < n_taken) | ((blane == TOPK) & (new_taken > 0.0))
    bias_ref[...] = jnp.where(live, 0.0, NEG)


def _select_sample(scores, cache_rows, n_keys):
    n_seq = scores.shape[0]
    past = n_keys - 1
    return pl.pallas_call(
        functools.partial(_select_sample_kernel, n_keys=n_keys),
        out_shape=[jax.ShapeDtypeStruct((n_seq, TOPK), I32), jax.ShapeDtypeStruct((n_seq, TOPK + LANES), F32)],
        scratch_shapes=[pltpu.VMEM(scores.shape, I32), pltpu.VMEM((past, past), BF16),
                        pltpu.VMEM((n_seq, past), F32), pltpu.VMEM((TOPK, n_seq), F32)],
        compiler_params=_cparams(),
        name="select_sample",
    )(scores, cache_rows)


def _gather_kv_rows(pool_k, pool_v, ids):
    n = ids.shape[0]
    mesh = plsc.VectorSubcoreMesh(core_axis_name="core", subcore_axis_name="subcore")
    n_workers = mesh.num_cores * mesh.num_subcores
    per_worker = n // n_workers
    assert n % (n_workers * SC_GATHER_ROWS) == 0
    row = pool_k.shape[1:]
    out = jax.ShapeDtypeStruct((n,) + row, pool_k.dtype)

    @functools.partial(
        pl.kernel, mesh=mesh, out_type=[out, out],
        scratch_types=[pltpu.VMEM((SC_GATHER_ROWS,), I32), pltpu.VMEM((SC_GATHER_ROWS,) + row, pool_k.dtype),
                       pltpu.VMEM((SC_GATHER_ROWS,) + row, pool_v.dtype), pltpu.SemaphoreType.DMA,
                       pltpu.SemaphoreType.DMA],
        compiler_params=pltpu.CompilerParams(use_tc_tiling_on_sc=True),
        name="gather_kv_rows",
    )
    def gather(k_hbm, v_hbm, ids_hbm, ko_hbm, vo_hbm, ids_v, k_v, v_v, ksem, vsem):
        worker = lax.axis_index("subcore") * mesh.num_cores + lax.axis_index("core")

        @pl.loop(0, per_worker // SC_GATHER_ROWS)
        def _(j):
            off = pl.multiple_of(worker * per_worker + j * SC_GATHER_ROWS, SC_GATHER_ROWS)
            pltpu.sync_copy(ids_hbm.at[pl.ds(off, SC_GATHER_ROWS)], ids_v)
            kcopy = pltpu.async_copy(k_hbm.at[ids_v], k_v, ksem)
            vcopy = pltpu.async_copy(v_hbm.at[ids_v], v_v, vsem)
            kcopy.wait()
            pltpu.sync_copy(k_v, ko_hbm.at[pl.ds(off, SC_GATHER_ROWS)])
            vcopy.wait()
            pltpu.sync_copy(v_v, vo_hbm.at[pl.ds(off, SC_GATHER_ROWS)])

    return gather(pool_k, pool_v, ids)


def _attn_sample_kernel(pt_ref, q_ref, kn_ref, vn_ref, bias_ref, *refs, n_pages):
    kpages, vpages = refs[:n_pages], refs[n_pages:2 * n_pages]
    o_ref, kall, vall = refs[2 * n_pages:]
    page_rows = PAGE * N_HEADS
    past_rows = n_pages * page_rows
    for p in range(n_pages):
        kall[p * page_rows:(p + 1) * page_rows, :] = kpages[p][...].astype(BF16)
        vall[p * page_rows:(p + 1) * page_rows, :] = vpages[p][...].astype(BF16)
    zeros = jnp.zeros((LANES - N_HEADS, HEAD_DIM), F32)
    kall[past_rows:, :] = jnp.concatenate([kn_ref[...], zeros], axis=0).astype(BF16)
    vall[past_rows:, :] = jnp.concatenate([vn_ref[...], zeros], axis=0).astype(BF16)
    s = lax.dot_general(q_ref[...], kall[...], NT_DIMS, preferred_element_type=F32)
    own = (lax.broadcasted_iota(I32, s.shape, 1) & (N_HEADS - 1)) == lax.broadcasted_iota(I32, s.shape, 0)
    s = jnp.where(own, s + bias_ref[...], NEG)
    m = jnp.max(s, axis=-1, keepdims=True)
    p = jnp.exp(s - m)
    l = jnp.sum(p, axis=-1, keepdims=True)
    o_ref[...] = jnp.dot(p.astype(BF16), vall[...], preferred_element_type=F32) / l


def _attn_sample(pt, q3, kn3, vn3, bias3, pool_k, pool_v, n_pages):
    db = q3.shape[0]
    page_rows = PAGE * N_HEADS
    width = n_pages * page_rows + LANES
    per_seq = lambda r, w: pl.BlockSpec((None, r, w), lambda d, pt: (d, 0, 0))
    page_spec = lambda p: pl.BlockSpec((None, page_rows, HEAD_DIM), lambda d, pt: (pt[d * n_pages + p], 0, 0))
    return pl.pallas_call(
        functools.partial(_attn_sample_kernel, n_pages=n_pages),
        grid_spec=pltpu.PrefetchScalarGridSpec(
            num_scalar_prefetch=1, grid=(db,),
            in_specs=[per_seq(N_HEADS, HEAD_DIM)] * 3 + [per_seq(1, width)]
            + [page_spec(p) for p in range(n_pages)] * 2,
            out_specs=per_seq(N_HEADS, HEAD_DIM),
            scratch_shapes=[pltpu.VMEM((width, HEAD_DIM), BF16), pltpu.VMEM((width, HEAD_DIM), BF16)]),
        out_shape=jax.ShapeDtypeStruct((db, N_HEADS, HEAD_DIM), F32),
        compiler_params=_cparams("arbitrary"),
        name="attn_sample",
    )(pt, q3, kn3, vn3, bias3, *([pool_k] * n_pages), *([pool_v] * n_pages))


def _ffn_sample_body(x, g_ref, wup_ref, wc_ref, bc_ref, wdn_ref, st0_ref, st1_ref, gate_ref, a_scr, h_scr):
    a_scr[...] = _rms(x, g_ref[...]).astype(BF16)

    def gate_taps(c, gate):
        cs = slice(c * FF_CHUNK, (c + 1) * FF_CHUNK)
        gate_ref[:, cs] = gate
        return st0_ref[:, cs], st1_ref[:, cs]

    _glu_hidden(a_scr, wup_ref, wc_ref, bc_ref, h_scr, gate_taps)
    return x + jnp.dot(h_scr[...], wdn_ref[...], preferred_element_type=F32)


def _layer0_sample_kernel(x_ref, o_ref, wo_ref, g_ref, wup_ref, wc_ref, bc_ref, wdn_ref, st0_ref, st1_ref,
                          out_ref, gate_ref, a_scr, h_scr):
    x = x_ref[...] + jnp.dot(o_ref[...].astype(BF16), wo_ref[...], preferred_element_type=F32)
    out_ref[...] = _ffn_sample_body(x, g_ref, wup_ref, wc_ref, bc_ref, wdn_ref, st0_ref, st1_ref, gate_ref,
                                    a_scr, h_scr)


def _layer1_sample_kernel(x_ref, gm_ref, win_ref, wcr_ref, bcr_ref, wg_ref, bga_ref, bgx_ref, lam_ref, wor_ref,
                          cs0_ref, cs1_ref, cs2_ref, h0_ref, g_ref, wup_ref, wc_ref, bc_ref, wdn_ref, st0_ref,
                          st1_ref, gf_ref, out_ref, hnew_ref, xb_ref, gate_ref, a_scr, h_scr, u_scr):
    x = x_ref[...]
    a_scr[...] = _rms(x, gm_ref[...]).astype(BF16)
    nblk = MXU_N // LANES
    ncs = -RG_C * jax.nn.softplus(-lam_ref[...])
    for c in range(D_MODEL // MXU_N):
        cs = slice(c * MXU_N, (c + 1) * MXU_N)
        xb = jnp.dot(a_scr[...], win_ref[:, D_MODEL + c * MXU_N:D_MODEL + (c + 1) * MXU_N],
                     preferred_element_type=F32)
        gate = jnp.dot(a_scr[...], win_ref[:, cs], preferred_element_type=F32)
        xb_ref[:, cs] = xb
        xc = (bcr_ref[:, cs] + wcr_ref[0:1, cs] * cs0_ref[:, cs] + wcr_ref[1:2, cs] * cs1_ref[:, cs]
              + wcr_ref[2:3, cs] * cs2_ref[:, cs] + wcr_ref[3:4, cs] * xb)
        for k in range(nblk):
            n = c * nblk + k
            ls = slice(n * LANES, (n + 1) * LANES)
            xcn = xc[:, k * LANES:(k + 1) * LANES]
            gg = jnp.dot(xcn.astype(BF16), wg_ref[n], preferred_element_type=F32)
            a_t, b_t = _lru_coeffs(xcn, gg, bga_ref[:, ls], bgx_ref[:, ls], ncs[:, ls])
            h = a_t * h0_ref[:, ls] + b_t
            hnew_ref[:, ls] = h
            u_scr[:, ls] = (jax.nn.gelu(gate[:, k * LANES:(k + 1) * LANES]) * h).astype(BF16)
    x = x + jnp.dot(u_scr[...], wor_ref[...], preferred_element_type=F32)
    y = _ffn_sample_body(x, g_ref, wup_ref, wc_ref, bc_ref, wdn_ref, st0_ref, st1_ref, gate_ref, a_scr, h_scr)
    out_ref[...] = _rms(y, gf_ref[...])


def _layer_spec(stacked, layer):
    nd = stacked.ndim - 1
    return pl.BlockSpec((None,) + stacked.shape[1:], lambda *_: (layer,) + (0,) * nd, pipeline_mode=pl.Buffered(1))


def _sample_call(kernel, name, args, out_widths, extra_scratch=()):
    db = args[0].shape[0]
    specs = [_layer_spec(*a) if isinstance(a, tuple) else _const_spec(a.shape) for a in args]
    arrays = [a[0] if isinstance(a, tuple) else a for a in args]
    return pl.pallas_call(
        kernel,
        grid=(1,),
        in_specs=specs,
        out_specs=[pl.BlockSpec((db, w), lambda i: (0, 0)) for w in out_widths],
        out_shape=[jax.ShapeDtypeStruct((db, w), F32) for w in out_widths],
        scratch_shapes=[pltpu.VMEM((db, D_MODEL), BF16), pltpu.VMEM((db, D_FF), BF16), *extra_scratch],
        compiler_params=_cparams("arbitrary"),
        name=name,
    )(*arrays)


def _rope_tables(pos):
    posf = pos.astype(F32)[:, None]

    def cs(d):
        half = d // 2
        inv = ROPE_THETA ** (-jnp.arange(half, dtype=F32) * 2.0 / d)
        ang = posf * inv[None, :]
        return jnp.cos(ang), jnp.sin(ang)

    c, s = cs(HEAD_DIM)
    cos = jnp.concatenate([c, c], axis=-1)
    sin = jnp.concatenate([-s, s], axis=-1)
    c, s = cs(IDX_DIM)
    one, zero = jnp.ones_like(c), jnp.zeros_like(c)
    tile = lambda parts: jnp.concatenate(parts * (IDX_HEADS + 1), axis=-1)
    icos = tile([c, c, one, one])
    isa = tile([-s, zero, zero, zero])
    isb = tile([zero, s, zero, zero])
    return cos, sin, icos, isa, isb


def _split_attn_in(w):
    qkv = N_HEADS * HEAD_DIM
    wqk, wv, wi = w[:, :2 * qkv], w[:, 2 * qkv:3 * qkv], w[:, 3 * qkv:]
    pad = lambda a: jnp.pad(a, ((0, 0), (0, LANES - a.shape[1])))
    groups = [pad(wi[:, h * IDX_DIM:(h + 1) * IDX_DIM]) for h in range(IDX_HEADS)]
    groups.append(pad(wi[:, IDX_HEADS * IDX_DIM:]))
    return wqk.astype(BF16), wv.astype(BF16), jnp.concatenate(groups, axis=-1).astype(BF16)


def kernel(x_prompt, x_sample, cache_k, cache_v, cache_kidx, state_lru_h, state_lru_conv, state_ffn_conv,
           page_table, norm_mix, norm_ffn, norm_final, w_attn_in, w_attn_out, w_rec_in, w_rec_conv, b_rec_conv,
           w_gate_a, b_gate_a, w_gate_x, b_gate_x, lru_lambda, w_rec_out, w_ffn_up, w_ffn_conv, b_ffn_conv,
           w_ffn_down):
    bsz, seq, d = x_prompt.shape
    db = x_sample.shape[0]
    n_pages = page_table.shape[1]
    past = n_pages * PAGE
    assert d == D_MODEL and x_sample.shape[1] == 1 and seq % ROW_BLOCK == 0 and seq % TQ == 0
    assert min(TOPK, seq // 4) == TOPK and min(TOPK, (past + 1) // 4) == TOPK

    vec = lambda a: a.reshape(1, -1)
    wqk, wv, widx = _split_attn_in(w_attn_in[0])
    wo_attn = w_attn_out[0].astype(BF16)
    w_rin = w_rec_in[0].astype(BF16)
    w_gates = jnp.concatenate([w_gate_a[0], w_gate_x[0]], axis=-1).astype(BF16)
    wo_rec = w_rec_out[0].astype(BF16)
    wup_all, wdn_all = w_ffn_up.astype(BF16), w_ffn_down.astype(BF16)
    wup = [(wup_all, i) for i in range(2)]
    wdn = [(wdn_all, i) for i in range(2)]
    rec_vecs = (w_rec_conv[0], vec(b_rec_conv[0]), w_gates, vec(b_gate_a[0]), vec(b_gate_x[0]),
                vec(lru_lambda[0]), wo_rec)
    ffn_vecs = [(vec(norm_ffn[i]), wup[i], w_ffn_conv[i], vec(b_ffn_conv[i]), wdn[i]) for i in range(2)]
    gfin = vec(norm_final)

    xp = x_prompt.reshape(bsz * seq, d)
    q, kf, kb, vf, vt, qi, kiw = _attn_in(xp, vec(norm_mix[0]), wqk, wv, widx, _rope_tables(jnp.arange(seq)),
                                          ROW_BLOCK, seq // ROW_BLOCK)
    wit = kiw[:, IDX_DIM:IDX_DIM + IDX_HEADS].T
    hp = _dsa_prompt(q, kb, vt, qi, kiw, wit, xp, wo_attn, bsz, seq)
    hp, ftail0 = _ffn_prompt(hp, *ffn_vecs[0], gfin, bsz, seq, False)
    hp, hlast, ctail = _rec_prompt(hp, vec(norm_mix[1]), w_rin, *rec_vecs, bsz, seq)
    yp, ftail1 = _ffn_prompt(hp, *ffn_vecs[1], gfin, bsz, seq, True)

    xs = x_sample.reshape(db, d)
    tabs = tuple(jnp.broadcast_to(t, (db, t.shape[1])) for t in _rope_tables(jnp.full((1,), past)))
    qs, kfs, _, vfs, _, qis, kiws = _attn_in(xs, vec(norm_mix[0]), wqk, wv, widx, tabs, db, 1)
    pt = page_table.reshape(-1)
    scores = _idx_sample(pt, qis.reshape(db, IDX_HEADS, LANES),
                         kiws[:, IDX_DIM:IDX_DIM + IDX_HEADS].reshape(db, IDX_HEADS, 1),
                         kiws.reshape(db, 1, LANES), jnp.swapaxes(cache_kidx[0], 1, 2), n_pages)
    cache_rows = (page_table[:, :, None] * PAGE + jnp.arange(PAGE, dtype=I32)).reshape(db, past).astype(F32)
    ids, bias = _select_sample(scores.reshape(db, -1), cache_rows, past + 1)
    heads = (N_HEADS, HEAD_DIM)
    key_rows = lambda pool: pool[0].reshape(-1, *heads)
    kg, vg = _gather_kv_rows(key_rows(cache_k), key_rows(cache_v), ids.reshape(-1))
    bias_rows = jnp.pad(jnp.repeat(bias[:, :TOPK + 1], N_HEADS, axis=1), ((0, 0), (0, LANES - N_HEADS)),
                        constant_values=NEG)
    g_pages = TOPK // PAGE
    as_pages = lambda g: g.reshape(db * g_pages, PAGE * N_HEADS, HEAD_DIM)
    o_s = _attn_sample(jnp.arange(db * g_pages, dtype=I32), qs.reshape(db, *heads), kfs.reshape(db, *heads),
                       vfs.reshape(db, *heads), bias_rows.reshape(db, 1, -1), as_pages(kg), as_pages(vg), g_pages)
    st = state_ffn_conv
    hs, gate0 = _sample_call(_layer0_sample_kernel, "layer0_sample",
                             (xs, o_s.reshape(db, d), wo_attn, *ffn_vecs[0], st[0, :, 0], st[0, :, 1]),
                             (D_MODEL, D_FF))
    cst = state_lru_conv[0]
    ys, hnew, xbs, gate1 = _sample_call(
        _layer1_sample_kernel, "layer1_sample",
        (hs, vec(norm_mix[1]), w_rin, *rec_vecs, cst[:, 0], cst[:, 1], cst[:, 2], state_lru_h[0], *ffn_vecs[1],
         st[1, :, 0], st[1, :, 1], gfin),
        (D_MODEL, D_MODEL, D_MODEL, D_FF), extra_scratch=(pltpu.VMEM((db, D_MODEL), BF16),))

    return (
        yp.reshape(bsz, seq, d),
        ys.reshape(db, 1, d),
        kf.reshape(1, bsz, seq, *heads),
        vf.reshape(1, bsz, seq, *heads),
        kiw[:, :IDX_DIM].reshape(1, bsz, seq, IDX_DIM),
        kfs.reshape(1, db, 1, *heads),
        vfs.reshape(1, db, 1, *heads),
        kiws[:, :IDX_DIM].reshape(1, db, 1, IDX_DIM),
        hlast[None, :, 0, :],
        ctail[None, :, SUBLANES - 3:, :],
        hnew[None],
        jnp.stack([cst[:, 1], cst[:, 2], xbs], axis=1)[None],
        jnp.stack([ftail0[:, SUBLANES - 2:], ftail1[:, SUBLANES - 2:]]),
        jnp.stack([jnp.stack([st[0, :, 1], gate0], axis=1), jnp.stack([st[1, :, 1], gate1], axis=1)]),
    )
```

```python
import functools

import jax
import jax.numpy as jnp
from jax import lax
from jax.experimental import pallas as pl
from jax.experimental.pallas import tpu as pltpu
from jax.experimental.pallas import tpu_sc as plsc

F32 = jnp.float32
BF16 = jnp.bfloat16
I32 = jnp.int32
I16 = jnp.int16

D_MODEL = 1024
N_HEADS = 8
HEAD_DIM = 128
IDX_HEADS = 4
IDX_DIM = 64
TOPK = 256
PAGE = 128
ROPE_THETA = 10000.0
IDX_SCALE = (IDX_DIM * IDX_HEADS) ** -0.5
Q_SCALE = HEAD_DIM ** -0.5 * 1.4426950408889634
RG_C = 8.0
D_FF = 2816
RMS_EPS = 1e-6

LANES = 128
SUBLANES = 8
MXU_N = 256
IDX_W = (IDX_HEADS + 1) * LANES
INT_MIN = -(2 ** 31)
NEG = -1e30
VMEM_LIMIT = 52 * 1024 * 1024

ROW_BLOCK = 512
TQ = 256
TK = 256
COUNT_ROWS = 32
COUNT_ROWS_16 = 64
IDX_SEQS = 4
ATT_SEQS = 4
SC_GATHER_ROWS = 32
FF_CHUNK = 256
N_FF_CHUNKS = D_FF // FF_CHUNK
NT_DIMS = (((1,), (1,)), ((), ()))


def _cparams(*sem):
    return pltpu.CompilerParams(dimension_semantics=sem if sem else None, vmem_limit_bytes=VMEM_LIMIT)


def _const_spec(shape):
    nd = len(shape)
    return pl.BlockSpec(shape, lambda *_: (0,) * nd, pipeline_mode=pl.Buffered(1))


def _rms(x, g):
    return x * lax.rsqrt(jnp.mean(x * x, axis=-1, keepdims=True) + RMS_EPS) * g


def _shift_rows(x, s, prev_rows):
    r = pltpu.roll(x, s, 0)
    top = r[:SUBLANES]
    row = lax.broadcasted_iota(I32, top.shape, 0)
    for k in range(s):
        top = jnp.where(row == k, prev_rows[k], top)
    return jnp.concatenate([top, r[SUBLANES:]], axis=0)


def _sortable_key(score):
    bits = pltpu.bitcast(score + 0.0, I32)
    return jnp.where(bits < 0, bits ^ jnp.int32(0x7FFFFFFF), bits)


def _attn_in_kernel(x_ref, g_ref, wqk_ref, wv_ref, widx_ref, cos_ref, sin_ref, icos_ref, isa_ref, isb_ref,
                    q_ref, kf_ref, kb_ref, vf_ref, vt_ref, qi_ref, kiw_ref, a_scr):
    a_scr[...] = _rms(x_ref[...], g_ref[...]).astype(BF16)
    cos = cos_ref[...]
    sin = sin_ref[...]
    for c in range(2 * D_MODEL // MXU_N):
        r = jnp.dot(a_scr[...], wqk_ref[:, c * MXU_N:(c + 1) * MXU_N], preferred_element_type=F32)
        for hh in range(MXU_N // HEAD_DIM):
            xh = r[:, hh * HEAD_DIM:(hh + 1) * HEAD_DIM]
            y = xh * cos + pltpu.roll(xh, HEAD_DIM // 2, 1) * sin
            col = (c * MXU_N) % D_MODEL + hh * HEAD_DIM
            if c < D_MODEL // MXU_N:
                q_ref[:, col:col + HEAD_DIM] = (y * Q_SCALE).astype(BF16)
            else:
                kf_ref[:, col:col + HEAD_DIM] = y
                kb_ref[:, col:col + HEAD_DIM] = y.astype(BF16)
    for c in range(D_MODEL // MXU_N):
        r = jnp.dot(a_scr[...], wv_ref[:, c * MXU_N:(c + 1) * MXU_N], preferred_element_type=F32)
        vf_ref[:, c * MXU_N:(c + 1) * MXU_N] = r
        vt_ref[c * MXU_N:(c + 1) * MXU_N, :] = r.T.astype(BF16)
    ri = jnp.dot(a_scr[...], widx_ref[...], preferred_element_type=F32)
    yi = (ri * icos_ref[...] + pltpu.roll(ri, IDX_W - IDX_DIM // 2, 1) * isa_ref[...]
          + pltpu.roll(ri, IDX_DIM // 2, 1) * isb_ref[...])
    qi_ref[...] = yi[:, :IDX_HEADS * LANES].astype(BF16)
    kiw_ref[...] = yi[:, IDX_HEADS * LANES:]


def _attn_in(x, g, wqk, wv, widx, tabs, rows, n_pos_blocks):
    n = x.shape[0]
    cos, sin, icos, isa, isb = tabs
    row_spec = lambda w: pl.BlockSpec((rows, w), lambda i: (i, 0))
    tab_spec = lambda w: pl.BlockSpec((rows, w), lambda i: (i % n_pos_blocks, 0))
    vt_spec = pl.BlockSpec((None, D_MODEL, rows), lambda i: (i // n_pos_blocks, 0, i % n_pos_blocks))
    return pl.pallas_call(
        _attn_in_kernel,
        grid=(n // rows,),
        in_specs=[row_spec(D_MODEL), _const_spec((1, D_MODEL)), _const_spec(wqk.shape), _const_spec(wv.shape),
                  _const_spec(widx.shape), tab_spec(HEAD_DIM), tab_spec(HEAD_DIM), tab_spec(IDX_W),
                  tab_spec(IDX_W), tab_spec(IDX_W)],
        out_specs=[row_spec(D_MODEL)] * 4 + [vt_spec, row_spec(IDX_HEADS * LANES), row_spec(LANES)],
        out_shape=[jax.ShapeDtypeStruct((n, D_MODEL), BF16), jax.ShapeDtypeStruct((n, D_MODEL), F32),
                   jax.ShapeDtypeStruct((n, D_MODEL), BF16), jax.ShapeDtypeStruct((n, D_MODEL), F32),
                   jax.ShapeDtypeStruct((n // (rows * n_pos_blocks), D_MODEL, rows * n_pos_blocks), BF16),
                   jax.ShapeDtypeStruct((n, IDX_HEADS * LANES), BF16),
                   jax.ShapeDtypeStruct((n, LANES), F32)],
        scratch_shapes=[pltpu.VMEM((rows, D_MODEL), BF16)],
        compiler_params=_cparams("arbitrary"),
        name="attn_in",
    )(x, g, wqk, wv, widx, cos, sin, icos, isa, isb)


def _kth_largest_key(count_ge, n_total):
    kf = float(TOPK)
    c0 = count_ge(0)
    t = jnp.where(c0 >= kf, jnp.int32(0), jnp.int32(INT_MIN))
    ct = jnp.where(c0 >= kf, c0, n_total)

    def bit_body(it, carry):
        t, ct = carry
        cand = t + (jnp.int32(1) << (30 - it))
        cnt = count_ge(cand)
        return jnp.where(cnt >= kf, cand, t), jnp.where(cnt >= kf, cnt, ct)

    return lax.fori_loop(0, 31, bit_body, (t, ct))


def _kth_largest_half(count_ge, above, count_all):
    kf = float(TOPK)
    c0 = above + count_ge(0)
    t = jnp.where(c0 >= kf, jnp.int32(0), jnp.int32(-(2 ** 15)))
    ct = jnp.where(c0 >= kf, c0, count_all)

    def bit_body(it, carry):
        t, ct = carry
        cand = t + (jnp.int32(1) << (14 - it))
        cnt = above + count_ge(cand)
        return jnp.where(cnt >= kf, cand, t), jnp.where(cnt >= kf, cnt, ct)

    return lax.fori_loop(0, 15, bit_body, (t, ct))


def _tie_cutoff(count_tie_le, need, n_bits):
    def bit_body(it, c):
        cand = c + (jnp.int32(1) << (n_bits - 1 - it))
        return jnp.where(count_tie_le(cand - 1) < need, cand, c)

    return lax.fori_loop(0, n_bits, bit_body, jnp.zeros_like(need, dtype=I32))


def _dsa_prompt_kernel(q_ref, kb_ref, vt_ref, qi_ref, kiwk_ref, wit_ref, x_ref, wo_ref, out_ref,
                       key_scr, bias_scr, kib_scr, o_scr, acc_scr, s_scr, hi_scr, lo_scr):
    i = pl.program_id(1)
    nk = i + 1

    @pl.when(i == 0)
    def _():
        kib_scr[...] = kiwk_ref[...].astype(BF16)

    qpos = i * TQ + lax.broadcasted_iota(I32, (1, TQ), 1)

    def rows(kc):
        return pl.ds(pl.multiple_of(kc * TK, TK), TK)

    def spos(kc):
        return kc * TK + lax.broadcasted_iota(I32, (TK, 1), 0)

    def colsum(a):
        return jnp.sum(a.reshape(TK // COUNT_ROWS, COUNT_ROWS, TQ), axis=0)

    wis = wit_ref[...] * IDX_SCALE

    def score_body(kc, carry):
        kic = kib_scr[rows(kc), :]
        sc = jnp.zeros((TK, TQ), F32)
        for h in range(IDX_HEADS):
            s = lax.dot_general(kic, qi_ref[:, h * LANES:(h + 1) * LANES], NT_DIMS, preferred_element_type=F32)
            sc = sc + wis[h:h + 1, :] * jnp.maximum(s, 0.0)
        key = jnp.where(spos(kc) <= qpos, _sortable_key(sc), jnp.int32(INT_MIN))
        key_scr[rows(kc), :] = key
        hi_scr[rows(kc), :] = (key >> 16).astype(I16)
        lo_scr[rows(kc), :] = ((key & 0xFFFF) - 2 ** 15).astype(I16)
        return carry

    lax.fori_loop(0, nk, score_body, 0)

    def count_ge(cand):
        def body(kc, acc):
            return acc + colsum(jnp.where(key_scr[rows(kc), :] >= cand, 1.0, 0.0))
        acc = lax.fori_loop(0, nk, body, jnp.zeros((COUNT_ROWS, TQ), F32))
        return jnp.sum(acc, axis=0, keepdims=True)

    def count_ge_half(half_scr):
        def count(cand):
            c16 = jnp.asarray(cand, I32).astype(I16)

            def body(kc, acc):
                ind = jnp.where(half_scr[rows(kc), :] >= c16, jnp.int16(1), jnp.int16(0))
                for r in range(TK // COUNT_ROWS_16):
                    acc = acc + ind[r * COUNT_ROWS_16:(r + 1) * COUNT_ROWS_16]
                return acc

            acc = lax.fori_loop(0, nk, body, jnp.zeros((COUNT_ROWS_16, TQ), I16))
            return jnp.sum(acc.astype(F32), axis=0, keepdims=True)
        return count

    count_hi, count_lo = count_ge_half(hi_scr), count_ge_half(lo_scr)
    thi, cnt_hi = _kth_largest_half(count_hi, 0.0, (nk * TK).astype(F32))
    above = jnp.where(thi == 2 ** 15 - 1, 0.0, count_hi(jnp.minimum(thi + 1, 2 ** 15 - 1)))
    thi16 = thi.astype(I16)

    def lo_body(kc, carry):
        lo_scr[rows(kc), :] = jnp.where(hi_scr[rows(kc), :] == thi16, lo_scr[rows(kc), :], jnp.int16(-(2 ** 15)))
        return carry

    lax.fori_loop(0, nk, lo_body, 0)
    tlo, cnt_thr = _kth_largest_half(count_lo, above, cnt_hi)
    thr = (thi << 16) | (tlo + 2 ** 15)
    short = thr == INT_MIN
    has_ties = jnp.max(jnp.where(short, 0.0, cnt_thr)) > float(TOPK)

    @pl.when(jnp.logical_not(has_ties))
    def _():
        thr_vis = jnp.maximum(thr, jnp.int32(INT_MIN + 1))

        def bias_body(kc, carry):
            bias_scr[rows(kc), :] = jnp.where(key_scr[rows(kc), :] >= thr_vis, 0.0, NEG)
            return carry

        lax.fori_loop(0, nk, bias_body, 0)

    @pl.when(has_ties)
    def _():
        need = float(TOPK) - count_ge(thr + 1)

        def eq_body(kc, carry):
            bias_scr[rows(kc), :] = jnp.where(key_scr[rows(kc), :] == thr, 1.0, 0.0)
            return carry

        lax.fori_loop(0, nk, eq_body, 0)

        def count_tie_le(cm):
            def body(kc, acc):
                return acc + colsum(jnp.where(spos(kc) <= cm, bias_scr[rows(kc), :], 0.0))
            acc = lax.fori_loop(0, nk, body, jnp.zeros((COUNT_ROWS, TQ), F32))
            return jnp.sum(acc, axis=0, keepdims=True)

        cut = jnp.where(short, jnp.int32(-1), _tie_cutoff(count_tie_le, need, 11))

        def bias_body(kc, carry):
            sel = (key_scr[rows(kc), :] > thr) | ((bias_scr[rows(kc), :] > 0.0) & (spos(kc) <= cut))
            bias_scr[rows(kc), :] = jnp.where(sel, 0.0, NEG)
            return carry

        lax.fori_loop(0, nk, bias_body, 0)

    acc_scr[...] = jnp.zeros(acc_scr.shape, F32)

    def att_body(kc, carry):
        ms, ls = carry
        for h in range(N_HEADS):
            hs = slice(h * HEAD_DIM, (h + 1) * HEAD_DIM)
            s_scr[h] = lax.dot_general(kb_ref[rows(kc), hs], q_ref[:, hs], NT_DIMS, preferred_element_type=F32)
        bias = bias_scr[rows(kc), :]
        new_ms, new_ls = [], []
        for h in range(N_HEADS):
            hs = slice(h * HEAD_DIM, (h + 1) * HEAD_DIM)
            s = s_scr[h] + bias
            m_new = jnp.maximum(ms[h], jnp.max(s, axis=0, keepdims=True))
            alpha = jnp.exp2(ms[h] - m_new)
            p = jnp.exp2(s - m_new)
            new_ms.append(m_new)
            new_ls.append(alpha * ls[h] + jnp.sum(p, axis=0, keepdims=True))
            pv = jnp.dot(vt_ref[hs, rows(kc)], p.astype(BF16), preferred_element_type=F32)
            acc_scr[hs, :] = alpha * acc_scr[hs, :] + pv
        return tuple(new_ms), tuple(new_ls)

    init = ((jnp.full((1, TQ), NEG, F32),) * N_HEADS, (jnp.zeros((1, TQ), F32),) * N_HEADS)
    _, ls = lax.fori_loop(0, nk, att_body, init)
    for h in range(N_HEADS):
        hs = slice(h * HEAD_DIM, (h + 1) * HEAD_DIM)
        o_scr[:, hs] = (acc_scr[hs, :] / ls[h]).T.astype(BF16)
    out_ref[...] = x_ref[...] + jnp.dot(o_scr[...], wo_ref[...], preferred_element_type=F32)


def _dsa_prompt(q, kb, vt, qi, kiw, wit, x, wo, bsz, seq):
    n = x.shape[0]
    nq = seq // TQ
    qrow = lambda w: pl.BlockSpec((TQ, w), lambda b, i: (b * nq + i, 0))
    brow = lambda w: pl.BlockSpec((seq, w), lambda b, i: (b, 0))
    return pl.pallas_call(
        _dsa_prompt_kernel,
        grid=(bsz, nq),
        in_specs=[qrow(D_MODEL), brow(D_MODEL), pl.BlockSpec((None, D_MODEL, seq), lambda b, i: (b, 0, 0)),
                  qrow(IDX_HEADS * LANES), brow(LANES), pl.BlockSpec((IDX_HEADS, TQ), lambda b, i: (0, b * nq + i)),
                  qrow(D_MODEL), _const_spec(wo.shape)],
        out_specs=qrow(D_MODEL),
        out_shape=jax.ShapeDtypeStruct((n, D_MODEL), F32),
        scratch_shapes=[pltpu.VMEM((seq, TQ), I32), pltpu.VMEM((seq, TQ), F32), pltpu.VMEM((seq, LANES), BF16),
                        pltpu.VMEM((TQ, D_MODEL), BF16), pltpu.VMEM((D_MODEL, TQ), F32),
                        pltpu.VMEM((N_HEADS, TK, TQ), F32), pltpu.VMEM((seq, TQ), I16),
                        pltpu.VMEM((seq, TQ), I16)],
        compiler_params=_cparams("arbitrary", "arbitrary"),
        name="dsa_prompt",
    )(q, kb, vt, qi, kiw, wit, x, wo)


def _glu_hidden(a_scr, wup_ref, wc_ref, bc_ref, h_scr, gate_taps):
    for c in range(N_FF_CHUNKS):
        cs = slice(c * FF_CHUNK, (c + 1) * FF_CHUNK)
        gate = jnp.dot(a_scr[...], wup_ref[:, cs], preferred_element_type=F32)
        val = jnp.dot(a_scr[...], wup_ref[:, D_FF + c * FF_CHUNK:D_FF + (c + 1) * FF_CHUNK],
                      preferred_element_type=F32)
        g2, g1 = gate_taps(c, gate)
        gc = bc_ref[:, cs] + wc_ref[0:1, cs] * g2 + wc_ref[1:2, cs] * g1 + wc_ref[2:3, cs] * gate
        h_scr[:, cs] = (jax.nn.gelu(gc) * val).astype(BF16)


def _ffn_prompt_kernel(x_ref, g_ref, wup_ref, wc_ref, bc_ref, wdn_ref, gf_ref, out_ref, tail_ref,
                       a_scr, h_scr, carry_scr, *, final_norm):
    j = pl.program_id(1)
    rows = x_ref.shape[0]
    a_scr[...] = _rms(x_ref[...], g_ref[...]).astype(BF16)

    @pl.when(j == 0)
    def _():
        carry_scr[...] = jnp.zeros(carry_scr.shape, F32)

    def gate_taps(c, gate):
        cs = slice(c * FF_CHUNK, (c + 1) * FF_CHUNK)
        p0 = carry_scr[SUBLANES - 2:SUBLANES - 1, cs]
        p1 = carry_scr[SUBLANES - 1:SUBLANES, cs]
        g1 = _shift_rows(gate, 1, [p1])
        g2 = _shift_rows(gate, 2, [p0, p1])
        carry_scr[:, cs] = gate[rows - SUBLANES:, :]
        tail_ref[0, :, cs] = gate[rows - SUBLANES:, :]
        return g2, g1

    _glu_hidden(a_scr, wup_ref, wc_ref, bc_ref, h_scr, gate_taps)
    y = x_ref[...] + jnp.dot(h_scr[...], wdn_ref[...], preferred_element_type=F32)
    out_ref[...] = _rms(y, gf_ref[...]) if final_norm else y


def _ffn_prompt(x, g, wup, wc, bc, wdn, gf, bsz, seq, final_norm):
    n = x.shape[0]
    nb = seq // ROW_BLOCK
    row = pl.BlockSpec((ROW_BLOCK, D_MODEL), lambda b, j: (b * nb + j, 0))
    (wup, lu), (wdn, ld) = wup, wdn
    return pl.pallas_call(
        functools.partial(_ffn_prompt_kernel, final_norm=final_norm),
        grid=(bsz, nb),
        in_specs=[row, _const_spec((1, D_MODEL)), _layer_spec(wup, lu), _const_spec(wc.shape),
                  _const_spec(bc.shape), _layer_spec(wdn, ld), _const_spec((1, D_MODEL))],
        out_specs=[row, pl.BlockSpec((1, SUBLANES, D_FF), lambda b, j: (b, 0, 0))],
        out_shape=[jax.ShapeDtypeStruct((n, D_MODEL), F32), jax.ShapeDtypeStruct((bsz, SUBLANES, D_FF), F32)],
        scratch_shapes=[pltpu.VMEM((ROW_BLOCK, D_MODEL), BF16), pltpu.VMEM((ROW_BLOCK, D_FF), BF16),
                        pltpu.VMEM((SUBLANES, D_FF), F32)],
        compiler_params=_cparams("arbitrary", "arbitrary"),
        name="ffn_prompt_final" if final_norm else "ffn_prompt",
    )(x, g, wup, wc, bc, wdn, gf)


def _sigmoid(x):
    return 0.5 * jnp.tanh(0.5 * x) + 0.5


def _lru_coeffs(xc, gg, bga, bgx, neg_c_softplus):
    r = _sigmoid(gg[:, :LANES] + bga)
    ig = _sigmoid(gg[:, LANES:] + bgx)
    log_a = r * neg_c_softplus
    a = jnp.exp(log_a)
    return a, jnp.sqrt(-jnp.tanh(log_a) * (a * a + 1.0)) * ig * xc


def _rec_prompt_kernel(x_ref, g_ref, win_ref, wc_ref, bc_ref, wg_ref, bga_ref, bgx_ref, lam_ref, wo_ref,
                       out_ref, hlast_ref, ctail_ref, a_scr, at_scr, bt_scr, u_scr, hcar_scr, ccar_scr):
    j = pl.program_id(1)
    rows = x_ref.shape[0]
    a_scr[...] = _rms(x_ref[...], g_ref[...]).astype(BF16)

    @pl.when(j == 0)
    def _():
        ccar_scr[...] = jnp.zeros(ccar_scr.shape, F32)
        hcar_scr[...] = jnp.zeros(hcar_scr.shape, F32)

    nblk = MXU_N // LANES
    ncs = -RG_C * jax.nn.softplus(-lam_ref[...])
    for c in range(D_MODEL // MXU_N):
        cs = slice(c * MXU_N, (c + 1) * MXU_N)
        xb = jnp.dot(a_scr[...], win_ref[:, D_MODEL + c * MXU_N:D_MODEL + (c + 1) * MXU_N],
                     preferred_element_type=F32)
        prev = [ccar_scr[SUBLANES - 3 + k:SUBLANES - 2 + k, cs] for k in range(3)]
        xc = (bc_ref[:, cs] + wc_ref[0:1, cs] * _shift_rows(xb, 3, prev) + wc_ref[1:2, cs] * _shift_rows(xb, 2, prev[1:])
              + wc_ref[2:3, cs] * _shift_rows(xb, 1, prev[2:]) + wc_ref[3:4, cs] * xb)
        ccar_scr[:, cs] = xb[rows - SUBLANES:, :]
        ctail_ref[0, :, cs] = xb[rows - SUBLANES:, :]
        for k in range(nblk):
            n = c * nblk + k
            ls = slice(n * LANES, (n + 1) * LANES)
            xcn = xc[:, k * LANES:(k + 1) * LANES]
            gg = jnp.dot(xcn.astype(BF16), wg_ref[n], preferred_element_type=F32)
            a_t, b_t = _lru_coeffs(xcn, gg, bga_ref[:, ls], bgx_ref[:, ls], ncs[:, ls])
            at_scr[:, ls] = a_t
            bt_scr[:, ls] = b_t

    def step(t, h):
        h = at_scr[pl.ds(t, 1), :] * h + bt_scr[pl.ds(t, 1), :]
        bt_scr[pl.ds(t, 1), :] = h
        return h

    h_last = lax.fori_loop(0, rows, step, hcar_scr[0:1, :], unroll=8)
    hcar_scr[...] = jnp.broadcast_to(h_last, hcar_scr.shape)
    hlast_ref[0] = jnp.broadcast_to(h_last, hcar_scr.shape)
    for c in range(D_MODEL // MXU_N):
        cs = slice(c * MXU_N, (c + 1) * MXU_N)
        gate = jnp.dot(a_scr[...], win_ref[:, cs], preferred_element_type=F32)
        u_scr[:, cs] = (jax.nn.gelu(gate) * bt_scr[:, cs]).astype(BF16)
    out_ref[...] = x_ref[...] + jnp.dot(u_scr[...], wo_ref[...], preferred_element_type=F32)


def _rec_prompt(x, g, win, wc, bc, wg, bga, bgx, lam, wo, bsz, seq):
    n = x.shape[0]
    nb = seq // ROW_BLOCK
    row = pl.BlockSpec((ROW_BLOCK, D_MODEL), lambda b, j: (b * nb + j, 0))
    tail = pl.BlockSpec((1, SUBLANES, D_MODEL), lambda b, j: (b, 0, 0))
    vec = _const_spec((1, D_MODEL))
    return pl.pallas_call(
        _rec_prompt_kernel,
        grid=(bsz, nb),
        in_specs=[row, vec, _const_spec(win.shape), _const_spec(wc.shape), vec, _const_spec(wg.shape), vec, vec,
                  vec, _const_spec(wo.shape)],
        out_specs=[row, tail, tail],
        out_shape=[jax.ShapeDtypeStruct((n, D_MODEL), F32), jax.ShapeDtypeStruct((bsz, SUBLANES, D_MODEL), F32),
                   jax.ShapeDtypeStruct((bsz, SUBLANES, D_MODEL), F32)],
        scratch_shapes=[pltpu.VMEM((ROW_BLOCK, D_MODEL), BF16), pltpu.VMEM((ROW_BLOCK, D_MODEL), F32),
                        pltpu.VMEM((ROW_BLOCK, D_MODEL), F32), pltpu.VMEM((ROW_BLOCK, D_MODEL), BF16),
                        pltpu.VMEM((SUBLANES, D_MODEL), F32), pltpu.VMEM((SUBLANES, D_MODEL), F32)],
        compiler_params=_cparams("arbitrary", "arbitrary"),
        name="rec_prompt",
    )(x, g, win, wc, bc, wg, bga, bgx, lam, wo)


def _idx_sample_kernel(pt_ref, qi_ref, wi_ref, kin_ref, *refs, n_pages):
    pages, out_ref = refs[:-1], refs[-1]
    lane = lax.broadcasted_iota(I32, (1, LANES), 1)
    for g in range(IDX_SEQS):
        qi = qi_ref[g]
        wi = wi_ref[g]
        for p in range(n_pages):
            page_t = pages[g * n_pages + p][...].astype(BF16)
            s = jnp.dot(qi[:, :IDX_DIM], page_t, preferred_element_type=F32)
            out_ref[g, :, p * PAGE:(p + 1) * PAGE] = (
                jnp.sum(wi * jnp.maximum(s, 0.0), axis=0, keepdims=True) * IDX_SCALE)
        s_new = jnp.sum(qi.astype(F32) * kin_ref[g].astype(BF16).astype(F32), axis=-1, keepdims=True)
        sc_new = jnp.sum(wi * jnp.maximum(s_new, 0.0), axis=0, keepdims=True) * IDX_SCALE
        out_ref[g, :, n_pages * PAGE:] = jnp.where(lane == 0, sc_new, -jnp.inf)


def _idx_sample(pt, qi3, wi3, kin3, pool_ki_t, n_pages):
    db = qi3.shape[0]
    assert db % IDX_SEQS == 0
    per_step = lambda shape: pl.BlockSpec((IDX_SEQS,) + shape, lambda d, pt: (d, 0, 0))
    page_spec = lambda g, p: pl.BlockSpec((None, IDX_DIM, PAGE),
                                          lambda d, pt: (pt[(d * IDX_SEQS + g) * n_pages + p], 0, 0))
    width = n_pages * PAGE + LANES
    return pl.pallas_call(
        functools.partial(_idx_sample_kernel, n_pages=n_pages),
        grid_spec=pltpu.PrefetchScalarGridSpec(
            num_scalar_prefetch=1, grid=(db // IDX_SEQS,),
            in_specs=[per_step((IDX_HEADS, LANES)), per_step((IDX_HEADS, 1)), per_step((1, LANES))]
            + [page_spec(g, p) for g in range(IDX_SEQS) for p in range(n_pages)],
            out_specs=per_step((1, width))),
        out_shape=jax.ShapeDtypeStruct((db, 1, width), F32),
        compiler_params=_cparams("arbitrary"),
        name="idx_sample",
    )(pt, qi3, wi3, kin3, *([pool_ki_t] * (IDX_SEQS * n_pages)))


def _select_sample_kernel(sc_ref, rows_ref, ids_ref, bias_ref, key_scr, tri_scr, rank_scr, ids_scr, *, n_keys):
    lane = lax.broadcasted_iota(I32, (1, sc_ref.shape[1]), 1)
    key_scr[...] = jnp.where(lane < n_keys, _sortable_key(sc_ref[...]), jnp.int32(INT_MIN))

    def count_ge(cand):
        return jnp.sum(jnp.where(key_scr[...] >= cand, 1.0, 0.0), axis=-1, keepdims=True)

    thr, _ = _kth_largest_key(count_ge, float(sc_ref.shape[1]))
    need = float(TOPK) - count_ge(thr + 1)

    def count_tie_le(cm):
        return jnp.sum(jnp.where((key_scr[...] == thr) & (lane <= cm), 1.0, 0.0), axis=-1, keepdims=True)

    cut = _tie_cutoff(count_tie_le, need, 12)
    cut = jnp.where(thr == INT_MIN, jnp.int32(-1), cut)
    key = key_scr[...]
    sel = (key > thr) | ((key == thr) & (lane <= cut))

    past = n_keys - 1
    n_seq = sc_ref.shape[0]
    taken = jnp.where(sel, 1.0, 0.0)[:, :past]
    for r in range(past // TK):
        upper = (lax.broadcasted_iota(I32, (TK, past), 0) + r * TK) <= lax.broadcasted_iota(I32, (TK, past), 1)
        tri_scr[r * TK:(r + 1) * TK, :] = jnp.where(upper, 1.0, 0.0).astype(BF16)
    rank = jnp.dot(taken.astype(BF16), tri_scr[...], preferred_element_type=F32)
    rank_scr[...] = rank * taken
    n_taken = rank[:, past - 1:past]
    slot = (lax.broadcasted_iota(I32, (TOPK, 1), 0) + 1).astype(F32)
    seq_lane = lax.broadcasted_iota(I32, (1, n_seq), 1)
    ids_scr[...] = jnp.zeros(ids_scr.shape, F32)

    def seq_body(d, carry):
        hit = rank_scr[pl.ds(d, 1), :] == slot
        row = jnp.sum(jnp.where(hit, rows_ref[pl.ds(d, 1), :], 0.0), axis=-1, keepdims=True)
        ids_scr[...] += jnp.where(seq_lane == d, row, 0.0)
        return carry

    lax.fori_loop(0, n_seq, seq_body, 0)
    ids_ref[...] = ids_scr[...].T.astype(I32)
    blane = lax.broadcasted_iota(I32, bias_ref.shape, 1)
    new_taken = jnp.sum(jnp.where(sel & (lane == past), 1.0, 0.0), axis=-1, keepdims=True)
    live = (blane.astype(F32) < n_taken) | ((blane == TOPK) & (new_taken > 0.0))
    bias_ref[...] = jnp.where(live, 0.0, NEG)


def _select_sample(scores, cache_rows, n_keys):
    n_seq = scores.shape[0]
    past = n_keys - 1
    return pl.pallas_call(
        functools.partial(_select_sample_kernel, n_keys=n_keys),
        out_shape=[jax.ShapeDtypeStruct((n_seq, TOPK), I32), jax.ShapeDtypeStruct((n_seq, TOPK + LANES), F32)],
        scratch_shapes=[pltpu.VMEM(scores.shape, I32), pltpu.VMEM((past, past), BF16),
                        pltpu.VMEM((n_seq, past), F32), pltpu.VMEM((TOPK, n_seq), F32)],
        compiler_params=_cparams(),
        name="select_sample",
    )(scores, cache_rows)


def _gather_kv_rows(pool_k, pool_v, ids):
    n = ids.shape[0]
    mesh = plsc.VectorSubcoreMesh(core_axis_name="core", subcore_axis_name="subcore")
    n_workers = mesh.num_cores * mesh.num_subcores
    per_worker = n // n_workers
    assert n % (n_workers * SC_GATHER_ROWS) == 0
    row = pool_k.shape[1:]
    out = jax.ShapeDtypeStruct((n,) + row, pool_k.dtype)

    @functools.partial(
        pl.kernel, mesh=mesh, out_type=[out, out],
        scratch_types=[pltpu.VMEM((SC_GATHER_ROWS,), I32), pltpu.VMEM((SC_GATHER_ROWS,) + row, pool_k.dtype),
                       pltpu.VMEM((SC_GATHER_ROWS,) + row, pool_v.dtype), pltpu.SemaphoreType.DMA,
                       pltpu.SemaphoreType.DMA],
        compiler_params=pltpu.CompilerParams(use_tc_tiling_on_sc=True),
        name="gather_kv_rows",
    )
    def gather(k_hbm, v_hbm, ids_hbm, ko_hbm, vo_hbm, ids_v, k_v, v_v, ksem, vsem):
        worker = lax.axis_index("subcore") * mesh.num_cores + lax.axis_index("core")

        @pl.loop(0, per_worker // SC_GATHER_ROWS)
        def _(j):
            off = pl.multiple_of(worker * per_worker + j * SC_GATHER_ROWS, SC_GATHER_ROWS)
            pltpu.sync_copy(ids_hbm.at[pl.ds(off, SC_GATHER_ROWS)], ids_v)
            kcopy = pltpu.async_copy(k_hbm.at[ids_v], k_v, ksem)
            vcopy = pltpu.async_copy(v_hbm.at[ids_v], v_v, vsem)
            kcopy.wait()
            pltpu.sync_copy(k_v, ko_hbm.at[pl.ds(off, SC_GATHER_ROWS)])
            vcopy.wait()
            pltpu.sync_copy(v_v, vo_hbm.at[pl.ds(off, SC_GATHER_ROWS)])

    return gather(pool_k, pool_v, ids)


def _attn_sample_kernel(q_ref, kn_ref, vn_ref, bias_ref, kg_ref, vg_ref, o_ref, kall, vall):
    past_rows = kg_ref.shape[1]
    zeros = jnp.zeros((LANES - N_HEADS, HEAD_DIM), F32)
    for g in range(ATT_SEQS):
        kall[:past_rows, :] = kg_ref[g].astype(BF16)
        vall[:past_rows, :] = vg_ref[g].astype(BF16)
        kall[past_rows:, :] = jnp.concatenate([kn_ref[g], zeros], axis=0).astype(BF16)
        vall[past_rows:, :] = jnp.concatenate([vn_ref[g], zeros], axis=0).astype(BF16)
        s = lax.dot_general(q_ref[g], kall[...], NT_DIMS, preferred_element_type=F32)
        own = (lax.broadcasted_iota(I32, s.shape, 1) & (N_HEADS - 1)) == lax.broadcasted_iota(I32, s.shape, 0)
        s = jnp.where(own, s + bias_ref[g], NEG)
        m = jnp.max(s, axis=-1, keepdims=True)
        p = jnp.exp2(s - m)
        l = jnp.sum(p, axis=-1, keepdims=True)
        o_ref[g] = jnp.dot(p.astype(BF16), vall[...], preferred_element_type=F32) / l


def _attn_sample(q3, kn3, vn3, bias3, kg, vg):
    db, past_rows, _ = kg.shape
    width = past_rows + LANES
    assert db % ATT_SEQS == 0 and bias3.shape == (db, 1, width)
    per_step = lambda r, w: pl.BlockSpec((ATT_SEQS, r, w), lambda d: (d, 0, 0))
    return pl.pallas_call(
        _attn_sample_kernel,
        grid=(db // ATT_SEQS,),
        in_specs=[per_step(N_HEADS, HEAD_DIM)] * 3 + [per_step(1, width)] + [per_step(past_rows, HEAD_DIM)] * 2,
        out_specs=per_step(N_HEADS, HEAD_DIM),
        out_shape=jax.ShapeDtypeStruct((db, N_HEADS, HEAD_DIM), F32),
        scratch_shapes=[pltpu.VMEM((width, HEAD_DIM), BF16), pltpu.VMEM((width, HEAD_DIM), BF16)],
        compiler_params=_cparams("arbitrary"),
        name="attn_sample",
    )(q3, kn3, vn3, bias3, kg, vg)


def _ffn_sample_body(x, g_ref, wup_ref, wc_ref, bc_ref, wdn_ref, st0_ref, st1_ref, gate_ref, a_scr, h_scr):
    a_scr[...] = _rms(x, g_ref[...]).astype(BF16)

    def gate_taps(c, gate):
        cs = slice(c * FF_CHUNK, (c + 1) * FF_CHUNK)
        gate_ref[:, cs] = gate
        return st0_ref[:, cs], st1_ref[:, cs]

    _glu_hidden(a_scr, wup_ref, wc_ref, bc_ref, h_scr, gate_taps)
    return x + jnp.dot(h_scr[...], wdn_ref[...], preferred_element_type=F32)


def _layer0_sample_kernel(x_ref, o_ref, wo_ref, g_ref, wup_ref, wc_ref, bc_ref, wdn_ref, st0_ref, st1_ref,
                          out_ref, gate_ref, a_scr, h_scr):
    x = x_ref[...] + jnp.dot(o_ref[...].astype(BF16), wo_ref[...], preferred_element_type=F32)
    out_ref[...] = _ffn_sample_body(x, g_ref, wup_ref, wc_ref, bc_ref, wdn_ref, st0_ref, st1_ref, gate_ref,
                                    a_scr, h_scr)


def _layer1_sample_kernel(x_ref, gm_ref, win_ref, wcr_ref, bcr_ref, wg_ref, bga_ref, bgx_ref, lam_ref, wor_ref,
                          cs0_ref, cs1_ref, cs2_ref, h0_ref, g_ref, wup_ref, wc_ref, bc_ref, wdn_ref, st0_ref,
                          st1_ref, gf_ref, out_ref, hnew_ref, xb_ref, gate_ref, a_scr, h_scr, u_scr):
    x = x_ref[...]
    a_scr[...] = _rms(x, gm_ref[...]).astype(BF16)
    nblk = MXU_N // LANES
    ncs = -RG_C * jax.nn.softplus(-lam_ref[...])
    for c in range(D_MODEL // MXU_N):
        cs = slice(c * MXU_N, (c + 1) * MXU_N)
        xb = jnp.dot(a_scr[...], win_ref[:, D_MODEL + c * MXU_N:D_MODEL + (c + 1) * MXU_N],
                     preferred_element_type=F32)
        gate = jnp.dot(a_scr[...], win_ref[:, cs], preferred_element_type=F32)
        xb_ref[:, cs] = xb
        xc = (bcr_ref[:, cs] + wcr_ref[0:1, cs] * cs0_ref[:, cs] + wcr_ref[1:2, cs] * cs1_ref[:, cs]
              + wcr_ref[2:3, cs] * cs2_ref[:, cs] + wcr_ref[3:4, cs] * xb)
        for k in range(nblk):
            n = c * nblk + k
            ls = slice(n * LANES, (n + 1) * LANES)
            xcn = xc[:, k * LANES:(k + 1) * LANES]
            gg = jnp.dot(xcn.astype(BF16), wg_ref[n], preferred_element_type=F32)
            a_t, b_t = _lru_coeffs(xcn, gg, bga_ref[:, ls], bgx_ref[:, ls], ncs[:, ls])
            h = a_t * h0_ref[:, ls] + b_t
            hnew_ref[:, ls] = h
            u_scr[:, ls] = (jax.nn.gelu(gate[:, k * LANES:(k + 1) * LANES]) * h).astype(BF16)
    x = x + jnp.dot(u_scr[...], wor_ref[...], preferred_element_type=F32)
    y = _ffn_sample_body(x, g_ref, wup_ref, wc_ref, bc_ref, wdn_ref, st0_ref, st1_ref, gate_ref, a_scr, h_scr)
    out_ref[...] = _rms(y, gf_ref[...])


def _layer_spec(stacked, layer):
    nd = stacked.ndim - 1
    return pl.BlockSpec((None,) + stacked.shape[1:], lambda *_: (layer,) + (0,) * nd, pipeline_mode=pl.Buffered(1))


def _sample_call(kernel, name, args, out_widths, extra_scratch=()):
    db = args[0].shape[0]
    specs = [_layer_spec(*a) if isinstance(a, tuple) else _const_spec(a.shape) for a in args]
    arrays = [a[0] if isinstance(a, tuple) else a for a in args]
    return pl.pallas_call(
        kernel,
        grid=(1,),
        in_specs=specs,
        out_specs=[pl.BlockSpec((db, w), lambda i: (0, 0)) for w in out_widths],
        out_shape=[jax.ShapeDtypeStruct((db, w), F32) for w in out_widths],
        scratch_shapes=[pltpu.VMEM((db, D_MODEL), BF16), pltpu.VMEM((db, D_FF), BF16), *extra_scratch],
        compiler_params=_cparams("arbitrary"),
        name=name,
    )(*arrays)


def _rope_tables(pos):
    posf = pos.astype(F32)[:, None]

    def cs(d):
        half = d // 2
        inv = ROPE_THETA ** (-jnp.arange(half, dtype=F32) * 2.0 / d)
        ang = posf * inv[None, :]
        return jnp.cos(ang), jnp.sin(ang)

    c, s = cs(HEAD_DIM)
    cos = jnp.concatenate([c, c], axis=-1)
    sin = jnp.concatenate([-s, s], axis=-1)
    c, s = cs(IDX_DIM)
    one, zero = jnp.ones_like(c), jnp.zeros_like(c)
    tile = lambda parts: jnp.concatenate(parts * (IDX_HEADS + 1), axis=-1)
    icos = tile([c, c, one, one])
    isa = tile([-s, zero, zero, zero])
    isb = tile([zero, s, zero, zero])
    return cos, sin, icos, isa, isb


def _split_attn_in(w):
    qkv = N_HEADS * HEAD_DIM
    wqk, wv, wi = w[:, :2 * qkv], w[:, 2 * qkv:3 * qkv], w[:, 3 * qkv:]
    pad = lambda a: jnp.pad(a, ((0, 0), (0, LANES - a.shape[1])))
    groups = [pad(wi[:, h * IDX_DIM:(h + 1) * IDX_DIM]) for h in range(IDX_HEADS)]
    groups.append(pad(wi[:, IDX_HEADS * IDX_DIM:]))
    return wqk.astype(BF16), wv.astype(BF16), jnp.concatenate(groups, axis=-1).astype(BF16)


def kernel(x_prompt, x_sample, cache_k, cache_v, cache_kidx, state_lru_h, state_lru_conv, state_ffn_conv,
           page_table, norm_mix, norm_ffn, norm_final, w_attn_in, w_attn_out, w_rec_in, w_rec_conv, b_rec_conv,
           w_gate_a, b_gate_a, w_gate_x, b_gate_x, lru_lambda, w_rec_out, w_ffn_up, w_ffn_conv, b_ffn_conv,
           w_ffn_down):
    bsz, seq, d = x_prompt.shape
    db = x_sample.shape[0]
    n_pages = page_table.shape[1]
    past = n_pages * PAGE
    assert d == D_MODEL and x_sample.shape[1] == 1 and seq % ROW_BLOCK == 0 and seq % TQ == 0
    assert min(TOPK, seq // 4) == TOPK and min(TOPK, (past + 1) // 4) == TOPK

    vec = lambda a: a.reshape(1, -1)
    wqk, wv, widx = _split_attn_in(w_attn_in[0])
    wo_attn = w_attn_out[0].astype(BF16)
    w_rin = w_rec_in[0].astype(BF16)
    w_gates = jnp.concatenate([w_gate_a[0], w_gate_x[0]], axis=-1).astype(BF16)
    wo_rec = w_rec_out[0].astype(BF16)
    wup_all, wdn_all = w_ffn_up.astype(BF16), w_ffn_down.astype(BF16)
    wup = [(wup_all, i) for i in range(2)]
    wdn = [(wdn_all, i) for i in range(2)]
    rec_vecs = (w_rec_conv[0], vec(b_rec_conv[0]), w_gates, vec(b_gate_a[0]), vec(b_gate_x[0]),
                vec(lru_lambda[0]), wo_rec)
    ffn_vecs = [(vec(norm_ffn[i]), wup[i], w_ffn_conv[i], vec(b_ffn_conv[i]), wdn[i]) for i in range(2)]
    gfin = vec(norm_final)

    xp = x_prompt.reshape(bsz * seq, d)
    q, kf, kb, vf, vt, qi, kiw = _attn_in(xp, vec(norm_mix[0]), wqk, wv, widx, _rope_tables(jnp.arange(seq)),
                                          ROW_BLOCK, seq // ROW_BLOCK)
    wit = kiw[:, IDX_DIM:IDX_DIM + IDX_HEADS].T
    hp_attn = _dsa_prompt(q, kb, vt, qi, kiw, wit, xp, wo_attn, bsz, seq)
    hp, ftail0 = _ffn_prompt(hp_attn, *ffn_vecs[0], gfin, bsz, seq, False)
    hp, hlast, ctail = _rec_prompt(hp, vec(norm_mix[1]), w_rin, *rec_vecs, bsz, seq)
    yp, ftail1 = _ffn_prompt(hp, *ffn_vecs[1], gfin, bsz, seq, True)

    xs = x_sample.reshape(db, d)
    tabs = tuple(jnp.broadcast_to(t, (db, t.shape[1])) for t in _rope_tables(jnp.full((1,), past)))
    qs, kfs, _, vfs, _, qis, kiws = _attn_in(xs, vec(norm_mix[0]), wqk, wv, widx, tabs, db, 1)
    pt = page_table.reshape(-1)
    scores = _idx_sample(pt, qis.reshape(db, IDX_HEADS, LANES),
                         kiws[:, IDX_DIM:IDX_DIM + IDX_HEADS].reshape(db, IDX_HEADS, 1),
                         kiws.reshape(db, 1, LANES), jnp.swapaxes(cache_kidx[0], 1, 2), n_pages)
    cache_rows = (page_table[:, :, None] * PAGE + jnp.arange(PAGE, dtype=I32)).reshape(db, past).astype(F32)
    ids, bias = _select_sample(scores.reshape(db, -1), cache_rows, past + 1)
    heads = (N_HEADS, HEAD_DIM)
    key_rows = lambda pool: pool[0].reshape(-1, *heads)
    ids, _ = lax.optimization_barrier((ids, kb))
    kg, vg = _gather_kv_rows(key_rows(cache_k), key_rows(cache_v), ids.reshape(-1))
    bias_rows = jnp.pad(jnp.repeat(bias[:, :TOPK + 1], N_HEADS, axis=1), ((0, 0), (0, LANES - N_HEADS)),
                        constant_values=NEG)
    bias_rows, _ = lax.optimization_barrier((bias_rows, hp_attn))
    seq_rows = lambda g: g.reshape(db, TOPK * N_HEADS, HEAD_DIM)
    o_s = _attn_sample(qs.reshape(db, *heads), kfs.reshape(db, *heads), vfs.reshape(db, *heads),
                       bias_rows.reshape(db, 1, -1), seq_rows(kg), seq_rows(vg))
    st = state_ffn_conv
    hs, gate0 = _sample_call(_layer0_sample_kernel, "layer0_sample",
                             (xs, o_s.reshape(db, d), wo_attn, *ffn_vecs[0], st[0, :, 0], st[0, :, 1]),
                             (D_MODEL, D_FF))
    cst = state_lru_conv[0]
    ys, hnew, xbs, gate1 = _sample_call(
        _layer1_sample_kernel, "layer1_sample",
        (hs, vec(norm_mix[1]), w_rin, *rec_vecs, cst[:, 0], cst[:, 1], cst[:, 2], state_lru_h[0], *ffn_vecs[1],
         st[1, :, 0], st[1, :, 1], gfin),
        (D_MODEL, D_MODEL, D_MODEL, D_FF), extra_scratch=(pltpu.VMEM((db, D_MODEL), BF16),))

    return (
        yp.reshape(bsz, seq, d),
        ys.reshape(db, 1, d),
        kf.reshape(1, bsz, seq, *heads),
        vf.reshape(1, bsz, seq, *heads),
        kiw[:, :IDX_DIM].reshape(1, bsz, seq, IDX_DIM),
        kfs.reshape(1, db, 1, *heads),
        vfs.reshape(1, db, 1, *heads),
        kiws[:, :IDX_DIM].reshape(1, db, 1, IDX_DIM),
        hlast[None, :, 0, :],
        ctail[None, :, SUBLANES - 3:, :],
        hnew[None],
        jnp.stack([cst[:, 1], cst[:, 2], xbs], axis=1)[None],
        jnp.stack([ftail0[:, SUBLANES - 2:], ftail1[:, SUBLANES - 2:]]),
        jnp.stack([jnp.stack([st[0, :, 1], gate0], axis=1), jnp.stack([st[1, :, 1], gate1], axis=1)]),
    )
```

```python
import functools

import jax
import jax.numpy as jnp
from jax import lax
from jax.experimental import pallas as pl
from jax.experimental.pallas import tpu as pltpu
from jax.experimental.pallas import tpu_sc as plsc

F32 = jnp.float32
BF16 = jnp.bfloat16
I32 = jnp.int32
I16 = jnp.int16

D_MODEL = 1024
N_HEADS = 8
HEAD_DIM = 128
IDX_HEADS = 4
IDX_DIM = 64
TOPK = 256
PAGE = 128
ROPE_THETA = 10000.0
IDX_SCALE = (IDX_DIM * IDX_HEADS) ** -0.5
Q_SCALE = HEAD_DIM ** -0.5 * 1.4426950408889634
RG_C = 8.0
D_FF = 2816
RMS_EPS = 1e-6

LANES = 128
SUBLANES = 8
MXU_N = 256
IDX_W = (IDX_HEADS + 1) * LANES
INT_MIN = -(2 ** 31)
NEG = -1e30
VMEM_LIMIT = 52 * 1024 * 1024

ROW_BLOCK = 512
FFN_ROWS = 1024
TQ = 256
TK = 256
COUNT_ROWS = 32
COUNT_ROWS_16 = 64
IDX_SEQS = 8
ATT_SEQS = 4
SC_GATHER_ROWS = 32
FF_CHUNK = 256
N_FF_CHUNKS = D_FF // FF_CHUNK
NT_DIMS = (((1,), (1,)), ((), ()))


def _cparams(*sem):
    return pltpu.CompilerParams(dimension_semantics=sem if sem else None, vmem_limit_bytes=VMEM_LIMIT)


def _const_spec(shape):
    nd = len(shape)
    return pl.BlockSpec(shape, lambda *_: (0,) * nd, pipeline_mode=pl.Buffered(1))


def _rms(x, g):
    return x * lax.rsqrt(jnp.mean(x * x, axis=-1, keepdims=True) + RMS_EPS) * g


def _shift_rows(x, s, prev_rows):
    r = pltpu.roll(x, s, 0)
    top = r[:SUBLANES]
    row = lax.broadcasted_iota(I32, top.shape, 0)
    for k in range(s):
        top = jnp.where(row == k, prev_rows[k], top)
    return jnp.concatenate([top, r[SUBLANES:]], axis=0)


def _sortable_key(score):
    bits = pltpu.bitcast(score + 0.0, I32)
    return jnp.where(bits < 0, bits ^ jnp.int32(0x7FFFFFFF), bits)


def _attn_in_kernel(x_ref, g_ref, wqk_ref, wv_ref, widx_ref, cos_ref, sin_ref, icos_ref, isa_ref, isb_ref,
                    q_ref, kf_ref, kb_ref, vf_ref, vt_ref, qi_ref, kiw_ref, a_scr):
    a_scr[...] = _rms(x_ref[...], g_ref[...]).astype(BF16)
    cos = cos_ref[...]
    sin = sin_ref[...]
    for c in range(2 * D_MODEL // MXU_N):
        r = jnp.dot(a_scr[...], wqk_ref[:, c * MXU_N:(c + 1) * MXU_N], preferred_element_type=F32)
        for hh in range(MXU_N // HEAD_DIM):
            xh = r[:, hh * HEAD_DIM:(hh + 1) * HEAD_DIM]
            y = xh * cos + pltpu.roll(xh, HEAD_DIM // 2, 1) * sin
            col = (c * MXU_N) % D_MODEL + hh * HEAD_DIM
            if c < D_MODEL // MXU_N:
                q_ref[:, col:col + HEAD_DIM] = (y * Q_SCALE).astype(BF16)
            else:
                kf_ref[:, col:col + HEAD_DIM] = y
                kb_ref[:, col:col + HEAD_DIM] = y.astype(BF16)
    for c in range(D_MODEL // MXU_N):
        r = jnp.dot(a_scr[...], wv_ref[:, c * MXU_N:(c + 1) * MXU_N], preferred_element_type=F32)
        vf_ref[:, c * MXU_N:(c + 1) * MXU_N] = r
        vt_ref[c * MXU_N:(c + 1) * MXU_N, :] = r.T.astype(BF16)
    ri = jnp.dot(a_scr[...], widx_ref[...], preferred_element_type=F32)
    yi = (ri * icos_ref[...] + pltpu.roll(ri, IDX_W - IDX_DIM // 2, 1) * isa_ref[...]
          + pltpu.roll(ri, IDX_DIM // 2, 1) * isb_ref[...])
    qi_ref[...] = yi[:, :IDX_HEADS * LANES].astype(BF16)
    kiw_ref[...] = yi[:, IDX_HEADS * LANES:]


def _attn_in(x, g, wqk, wv, widx, tabs, rows, n_pos_blocks):
    n = x.shape[0]
    cos, sin, icos, isa, isb = tabs
    row_spec = lambda w: pl.BlockSpec((rows, w), lambda i: (i, 0))
    tab_spec = lambda w: pl.BlockSpec((rows, w), lambda i: (i % n_pos_blocks, 0))
    vt_spec = pl.BlockSpec((None, D_MODEL, rows), lambda i: (i // n_pos_blocks, 0, i % n_pos_blocks))
    return pl.pallas_call(
        _attn_in_kernel,
        grid=(n // rows,),
        in_specs=[row_spec(D_MODEL), _const_spec((1, D_MODEL)), _const_spec(wqk.shape), _const_spec(wv.shape),
                  _const_spec(widx.shape), tab_spec(HEAD_DIM), tab_spec(HEAD_DIM), tab_spec(IDX_W),
                  tab_spec(IDX_W), tab_spec(IDX_W)],
        out_specs=[row_spec(D_MODEL)] * 4 + [vt_spec, row_spec(IDX_HEADS * LANES), row_spec(LANES)],
        out_shape=[jax.ShapeDtypeStruct((n, D_MODEL), BF16), jax.ShapeDtypeStruct((n, D_MODEL), F32),
                   jax.ShapeDtypeStruct((n, D_MODEL), BF16), jax.ShapeDtypeStruct((n, D_MODEL), F32),
                   jax.ShapeDtypeStruct((n // (rows * n_pos_blocks), D_MODEL, rows * n_pos_blocks), BF16),
                   jax.ShapeDtypeStruct((n, IDX_HEADS * LANES), BF16),
                   jax.ShapeDtypeStruct((n, LANES), F32)],
        scratch_shapes=[pltpu.VMEM((rows, D_MODEL), BF16)],
        compiler_params=_cparams("arbitrary"),
        name="attn_in",
    )(x, g, wqk, wv, widx, cos, sin, icos, isa, isb)


def _kth_largest_key(count_ge, n_total):
    kf = float(TOPK)
    c0 = count_ge(0)
    t = jnp.where(c0 >= kf, jnp.int32(0), jnp.int32(INT_MIN))
    ct = jnp.where(c0 >= kf, c0, n_total)

    def bit_body(it, carry):
        t, ct = carry
        cand = t + (jnp.int32(1) << (30 - it))
        cnt = count_ge(cand)
        return jnp.where(cnt >= kf, cand, t), jnp.where(cnt >= kf, cnt, ct)

    return lax.fori_loop(0, 31, bit_body, (t, ct))


def _kth_largest_half(count_ge, above, count_all):
    kf = float(TOPK)
    c0 = above + count_ge(0)
    t = jnp.where(c0 >= kf, jnp.int32(0), jnp.int32(-(2 ** 15)))
    ct = jnp.where(c0 >= kf, c0, count_all)

    def bit_body(it, carry):
        t, ct = carry
        cand = t + (jnp.int32(1) << (14 - it))
        cnt = above + count_ge(cand)
        return jnp.where(cnt >= kf, cand, t), jnp.where(cnt >= kf, cnt, ct)

    return lax.fori_loop(0, 15, bit_body, (t, ct))


def _tie_cutoff(count_tie_le, need, n_bits):
    def bit_body(it, c):
        cand = c + (jnp.int32(1) << (n_bits - 1 - it))
        return jnp.where(count_tie_le(cand - 1) < need, cand, c)

    return lax.fori_loop(0, n_bits, bit_body, jnp.zeros_like(need, dtype=I32))


def _dsa_prompt_kernel(q_ref, kb_ref, vt_ref, qi_ref, kiwk_ref, wit_ref, x_ref, wo_ref, out_ref,
                       key_scr, bias_scr, kib_scr, o_scr, acc_scr, s_scr, hi_scr, lo_scr, p_scr):
    i = pl.program_id(1)
    nk = i + 1

    @pl.when(i == 0)
    def _():
        kib_scr[...] = kiwk_ref[...].astype(BF16)

    qpos = i * TQ + lax.broadcasted_iota(I32, (1, TQ), 1)

    def rows(kc):
        return pl.ds(pl.multiple_of(kc * TK, TK), TK)

    def spos(kc):
        return kc * TK + lax.broadcasted_iota(I32, (TK, 1), 0)

    def colsum(a):
        return jnp.sum(a.reshape(TK // COUNT_ROWS, COUNT_ROWS, TQ), axis=0)

    wis = wit_ref[...] * IDX_SCALE

    def score_body(kc, carry):
        kic = kib_scr[rows(kc), :]
        sc = jnp.zeros((TK, TQ), F32)
        for h in range(IDX_HEADS):
            s = lax.dot_general(kic, qi_ref[:, h * LANES:(h + 1) * LANES], NT_DIMS, preferred_element_type=F32)
            sc = sc + wis[h:h + 1, :] * jnp.maximum(s, 0.0)
        key = jnp.where(spos(kc) <= qpos, _sortable_key(sc), jnp.int32(INT_MIN))
        key_scr[rows(kc), :] = key
        hi_scr[rows(kc), :] = (key >> 16).astype(I16)
        lo_scr[rows(kc), :] = ((key & 0xFFFF) - 2 ** 15).astype(I16)
        return carry

    lax.fori_loop(0, nk, score_body, 0)

    def count_ge(cand):
        def body(kc, acc):
            return acc + colsum(jnp.where(key_scr[rows(kc), :] >= cand, 1.0, 0.0))
        acc = lax.fori_loop(0, nk, body, jnp.zeros((COUNT_ROWS, TQ), F32))
        return jnp.sum(acc, axis=0, keepdims=True)

    def count_ge_half(half_scr):
        def count(cand):
            c16 = jnp.asarray(cand, I32).astype(I16)

            def body(kc, acc):
                ind = jnp.where(half_scr[rows(kc), :] >= c16, jnp.int16(1), jnp.int16(0))
                for r in range(TK // COUNT_ROWS_16):
                    acc = acc + ind[r * COUNT_ROWS_16:(r + 1) * COUNT_ROWS_16]
                return acc

            acc = lax.fori_loop(0, nk, body, jnp.zeros((COUNT_ROWS_16, TQ), I16))
            return jnp.sum(acc.astype(F32), axis=0, keepdims=True)
        return count

    count_hi, count_lo = count_ge_half(hi_scr), count_ge_half(lo_scr)
    thi, cnt_hi = _kth_largest_half(count_hi, 0.0, (nk * TK).astype(F32))
    above = jnp.where(thi == 2 ** 15 - 1, 0.0, count_hi(jnp.minimum(thi + 1, 2 ** 15 - 1)))
    thi16 = thi.astype(I16)

    def lo_body(kc, carry):
        lo_scr[rows(kc), :] = jnp.where(hi_scr[rows(kc), :] == thi16, lo_scr[rows(kc), :], jnp.int16(-(2 ** 15)))
        return carry

    lax.fori_loop(0, nk, lo_body, 0)
    tlo, cnt_thr = _kth_largest_half(count_lo, above, cnt_hi)
    thr = (thi << 16) | (tlo + 2 ** 15)
    short = thr == INT_MIN
    has_ties = jnp.max(jnp.where(short, 0.0, cnt_thr)) > float(TOPK)

    @pl.when(jnp.logical_not(has_ties))
    def _():
        thr_vis = jnp.maximum(thr, jnp.int32(INT_MIN + 1))

        def bias_body(kc, carry):
            bias_scr[rows(kc), :] = jnp.where(key_scr[rows(kc), :] >= thr_vis, 0.0, NEG)
            return carry

        lax.fori_loop(0, nk, bias_body, 0)

    @pl.when(has_ties)
    def _():
        need = float(TOPK) - count_ge(thr + 1)

        def eq_body(kc, carry):
            bias_scr[rows(kc), :] = jnp.where(key_scr[rows(kc), :] == thr, 1.0, 0.0)
            return carry

        lax.fori_loop(0, nk, eq_body, 0)

        def count_tie_le(cm):
            def body(kc, acc):
                return acc + colsum(jnp.where(spos(kc) <= cm, bias_scr[rows(kc), :], 0.0))
            acc = lax.fori_loop(0, nk, body, jnp.zeros((COUNT_ROWS, TQ), F32))
            return jnp.sum(acc, axis=0, keepdims=True)

        cut = jnp.where(short, jnp.int32(-1), _tie_cutoff(count_tie_le, need, 11))

        def bias_body(kc, carry):
            sel = (key_scr[rows(kc), :] > thr) | ((bias_scr[rows(kc), :] > 0.0) & (spos(kc) <= cut))
            bias_scr[rows(kc), :] = jnp.where(sel, 0.0, NEG)
            return carry

        lax.fori_loop(0, nk, bias_body, 0)

    acc_scr[...] = jnp.zeros(acc_scr.shape, F32)

    def att_body(kc, carry):
        ms, ls = carry
        for h in range(N_HEADS):
            hs = slice(h * HEAD_DIM, (h + 1) * HEAD_DIM)
            s_scr[h] = lax.dot_general(kb_ref[rows(kc), hs], q_ref[:, hs], NT_DIMS, preferred_element_type=F32)
        new_ms, new_ls = [], []
        for h in range(N_HEADS):
            hs = slice(h * HEAD_DIM, (h + 1) * HEAD_DIM)
            alphas = []
            for t in range(TQ // LANES):
                ts = slice(t * LANES, (t + 1) * LANES)
                s = s_scr[h, :, ts] + bias_scr[rows(kc), ts]
                m_old = ms[h][:, ts]
                m_new = jnp.maximum(m_old, jnp.max(s, axis=0, keepdims=True))
                alpha = jnp.exp2(m_old - m_new)
                p = jnp.exp2(s - m_new)
                p_scr[h, :, ts] = p.astype(BF16)
                alphas.append(alpha)
                new_ms.append(m_new)
                new_ls.append(alpha * ls[h][:, ts] + jnp.sum(p, axis=0, keepdims=True))
            pv = jnp.dot(vt_ref[hs, rows(kc)], p_scr[h], preferred_element_type=F32)
            acc_scr[hs, :] = jnp.concatenate(alphas, axis=1) * acc_scr[hs, :] + pv
        nt = TQ // LANES
        regroup = lambda parts: tuple(jnp.concatenate(parts[h * nt:(h + 1) * nt], axis=1) for h in range(N_HEADS))
        return regroup(new_ms), regroup(new_ls)

    init = ((jnp.full((1, TQ), NEG, F32),) * N_HEADS, (jnp.zeros((1, TQ), F32),) * N_HEADS)
    _, ls = lax.fori_loop(0, nk, att_body, init)
    for h in range(N_HEADS):
        hs = slice(h * HEAD_DIM, (h + 1) * HEAD_DIM)
        o_scr[:, hs] = (acc_scr[hs, :] / ls[h]).T.astype(BF16)
    out_ref[...] = x_ref[...] + jnp.dot(o_scr[...], wo_ref[...], preferred_element_type=F32)


def _dsa_prompt(q, kb, vt, qi, kiw, wit, x, wo, bsz, seq):
    n = x.shape[0]
    nq = seq // TQ
    qrow = lambda w: pl.BlockSpec((TQ, w), lambda b, i: (b * nq + i, 0))
    brow = lambda w: pl.BlockSpec((seq, w), lambda b, i: (b, 0))
    return pl.pallas_call(
        _dsa_prompt_kernel,
        grid=(bsz, nq),
        in_specs=[qrow(D_MODEL), brow(D_MODEL), pl.BlockSpec((None, D_MODEL, seq), lambda b, i: (b, 0, 0)),
                  qrow(IDX_HEADS * LANES), brow(LANES), pl.BlockSpec((IDX_HEADS, TQ), lambda b, i: (0, b * nq + i)),
                  qrow(D_MODEL), _const_spec(wo.shape)],
        out_specs=qrow(D_MODEL),
        out_shape=jax.ShapeDtypeStruct((n, D_MODEL), F32),
        scratch_shapes=[pltpu.VMEM((seq, TQ), I32), pltpu.VMEM((seq, TQ), F32), pltpu.VMEM((seq, LANES), BF16),
                        pltpu.VMEM((TQ, D_MODEL), BF16), pltpu.VMEM((D_MODEL, TQ), F32),
                        pltpu.VMEM((N_HEADS, TK, TQ), F32), pltpu.VMEM((seq, TQ), I16),
                        pltpu.VMEM((seq, TQ), I16), pltpu.VMEM((N_HEADS, TK, TQ), BF16)],
        compiler_params=_cparams("arbitrary", "arbitrary"),
        name="dsa_prompt",
    )(q, kb, vt, qi, kiw, wit, x, wo)


def _glu_hidden(a_scr, wup_ref, wc_ref, bc_ref, h_scr, gate_taps):
    for c in range(N_FF_CHUNKS):
        cs = slice(c * FF_CHUNK, (c + 1) * FF_CHUNK)
        gate = jnp.dot(a_scr[...], wup_ref[:, cs], preferred_element_type=F32)
        val = jnp.dot(a_scr[...], wup_ref[:, D_FF + c * FF_CHUNK:D_FF + (c + 1) * FF_CHUNK],
                      preferred_element_type=F32)
        g2, g1 = gate_taps(c, gate)
        gc = bc_ref[:, cs] + wc_ref[0:1, cs] * g2 + wc_ref[1:2, cs] * g1 + wc_ref[2:3, cs] * gate
        h_scr[:, cs] = (jax.nn.gelu(gc) * val).astype(BF16)


def _ffn_prompt_kernel(x_ref, g_ref, wup_ref, wc_ref, bc_ref, wdn_ref, gf_ref, out_ref, tail_ref,
                       a_scr, h_scr, carry_scr, *, final_norm):
    j = pl.program_id(1)
    rows = x_ref.shape[0]
    a_scr[...] = _rms(x_ref[...], g_ref[...]).astype(BF16)

    @pl.when(j == 0)
    def _():
        carry_scr[...] = jnp.zeros(carry_scr.shape, F32)

    def gate_taps(c, gate):
        cs = slice(c * FF_CHUNK, (c + 1) * FF_CHUNK)
        p0 = carry_scr[SUBLANES - 2:SUBLANES - 1, cs]
        p1 = carry_scr[SUBLANES - 1:SUBLANES, cs]
        g1 = _shift_rows(gate, 1, [p1])
        g2 = _shift_rows(gate, 2, [p0, p1])
        carry_scr[:, cs] = gate[rows - SUBLANES:, :]
        tail_ref[0, :, cs] = gate[rows - SUBLANES:, :]
        return g2, g1

    _glu_hidden(a_scr, wup_ref, wc_ref, bc_ref, h_scr, gate_taps)
    y = x_ref[...] + jnp.dot(h_scr[...], wdn_ref[...], preferred_element_type=F32)
    out_ref[...] = _rms(y, gf_ref[...]) if final_norm else y


def _ffn_prompt(x, g, wup, wc, bc, wdn, gf, bsz, seq, final_norm):
    n = x.shape[0]
    assert seq % FFN_ROWS == 0
    nb = seq // FFN_ROWS
    row = pl.BlockSpec((FFN_ROWS, D_MODEL), lambda b, j: (b * nb + j, 0))
    (wup, lu), (wdn, ld) = wup, wdn
    return pl.pallas_call(
        functools.partial(_ffn_prompt_kernel, final_norm=final_norm),
        grid=(bsz, nb),
        in_specs=[row, _const_spec((1, D_MODEL)), _layer_spec(wup, lu), _const_spec(wc.shape),
                  _const_spec(bc.shape), _layer_spec(wdn, ld), _const_spec((1, D_MODEL))],
        out_specs=[row, pl.BlockSpec((1, SUBLANES, D_FF), lambda b, j: (b, 0, 0))],
        out_shape=[jax.ShapeDtypeStruct((n, D_MODEL), F32), jax.ShapeDtypeStruct((bsz, SUBLANES, D_FF), F32)],
        scratch_shapes=[pltpu.VMEM((FFN_ROWS, D_MODEL), BF16), pltpu.VMEM((FFN_ROWS, D_FF), BF16),
                        pltpu.VMEM((SUBLANES, D_FF), F32)],
        compiler_params=_cparams("arbitrary", "arbitrary"),
        name="ffn_prompt_final" if final_norm else "ffn_prompt",
    )(x, g, wup, wc, bc, wdn, gf)


def _sigmoid(x):
    return 0.5 * jnp.tanh(0.5 * x) + 0.5


def _lru_coeffs(xc, gg, bga, bgx, neg_c_softplus):
    r = _sigmoid(gg[:, :LANES] + bga)
    ig = _sigmoid(gg[:, LANES:] + bgx)
    log_a = r * neg_c_softplus
    a = jnp.exp(log_a)
    z = -jnp.tanh(log_a) * (a * a + 1.0)
    root = jnp.where(z > 0.0, z * lax.rsqrt(z), 0.0)
    return a, root * ig * xc


def _rec_prompt_kernel(x_ref, g_ref, win_ref, wc_ref, bc_ref, wg_ref, bga_ref, bgx_ref, lam_ref, wo_ref,
                       out_ref, hlast_ref, ctail_ref, a_scr, at_scr, bt_scr, u_scr, hcar_scr, ccar_scr):
    j = pl.program_id(1)
    rows = x_ref.shape[0]
    a_scr[...] = _rms(x_ref[...], g_ref[...]).astype(BF16)

    @pl.when(j == 0)
    def _():
        ccar_scr[...] = jnp.zeros(ccar_scr.shape, F32)
        hcar_scr[...] = jnp.zeros(hcar_scr.shape, F32)

    nblk = MXU_N // LANES
    ncs = -RG_C * jax.nn.softplus(-lam_ref[...])
    for c in range(D_MODEL // MXU_N):
        cs = slice(c * MXU_N, (c + 1) * MXU_N)
        xb = jnp.dot(a_scr[...], win_ref[:, D_MODEL + c * MXU_N:D_MODEL + (c + 1) * MXU_N],
                     preferred_element_type=F32)
        prev = [ccar_scr[SUBLANES - 3 + k:SUBLANES - 2 + k, cs] for k in range(3)]
        xc = (bc_ref[:, cs] + wc_ref[0:1, cs] * _shift_rows(xb, 3, prev) + wc_ref[1:2, cs] * _shift_rows(xb, 2, prev[1:])
              + wc_ref[2:3, cs] * _shift_rows(xb, 1, prev[2:]) + wc_ref[3:4, cs] * xb)
        ccar_scr[:, cs] = xb[rows - SUBLANES:, :]
        ctail_ref[0, :, cs] = xb[rows - SUBLANES:, :]
        for k in range(nblk):
            n = c * nblk + k
            ls = slice(n * LANES, (n + 1) * LANES)
            xcn = xc[:, k * LANES:(k + 1) * LANES]
            gg = jnp.dot(xcn.astype(BF16), wg_ref[n], preferred_element_type=F32)
            a_t, b_t = _lru_coeffs(xcn, gg, bga_ref[:, ls], bgx_ref[:, ls], ncs[:, ls])
            at_scr[:, ls] = a_t
            bt_scr[:, ls] = b_t

    def step(t, h):
        h = at_scr[pl.ds(t, 1), :] * h + bt_scr[pl.ds(t, 1), :]
        bt_scr[pl.ds(t, 1), :] = h
        return h

    h_last = lax.fori_loop(0, rows, step, hcar_scr[0:1, :], unroll=8)
    hcar_scr[...] = jnp.broadcast_to(h_last, hcar_scr.shape)
    hlast_ref[0] = jnp.broadcast_to(h_last, hcar_scr.shape)
    for c in range(D_MODEL // MXU_N):
        cs = slice(c * MXU_N, (c + 1) * MXU_N)
        gate = jnp.dot(a_scr[...], win_ref[:, cs], preferred_element_type=F32)
        u_scr[:, cs] = (jax.nn.gelu(gate) * bt_scr[:, cs]).astype(BF16)
    out_ref[...] = x_ref[...] + jnp.dot(u_scr[...], wo_ref[...], preferred_element_type=F32)


def _rec_prompt(x, g, win, wc, bc, wg, bga, bgx, lam, wo, bsz, seq):
    n = x.shape[0]
    nb = seq // ROW_BLOCK
    row = pl.BlockSpec((ROW_BLOCK, D_MODEL), lambda b, j: (b * nb + j, 0))
    tail = pl.BlockSpec((1, SUBLANES, D_MODEL), lambda b, j: (b, 0, 0))
    vec = _const_spec((1, D_MODEL))
    return pl.pallas_call(
        _rec_prompt_kernel,
        grid=(bsz, nb),
        in_specs=[row, vec, _const_spec(win.shape), _const_spec(wc.shape), vec, _const_spec(wg.shape), vec, vec,
                  vec, _const_spec(wo.shape)],
        out_specs=[row, tail, tail],
        out_shape=[jax.ShapeDtypeStruct((n, D_MODEL), F32), jax.ShapeDtypeStruct((bsz, SUBLANES, D_MODEL), F32),
                   jax.ShapeDtypeStruct((bsz, SUBLANES, D_MODEL), F32)],
        scratch_shapes=[pltpu.VMEM((ROW_BLOCK, D_MODEL), BF16), pltpu.VMEM((ROW_BLOCK, D_MODEL), F32),
                        pltpu.VMEM((ROW_BLOCK, D_MODEL), F32), pltpu.VMEM((ROW_BLOCK, D_MODEL), BF16),
                        pltpu.VMEM((SUBLANES, D_MODEL), F32), pltpu.VMEM((SUBLANES, D_MODEL), F32)],
        compiler_params=_cparams("arbitrary", "arbitrary"),
        name="rec_prompt",
    )(x, g, win, wc, bc, wg, bga, bgx, lam, wo)


def _idx_sample_kernel(pt_ref, qi_ref, wi_ref, kin_ref, *refs, n_pages):
    pages, out_ref = refs[:-1], refs[-1]
    lane = lax.broadcasted_iota(I32, (1, LANES), 1)
    for g in range(IDX_SEQS):
        qi = qi_ref[g]
        wi = wi_ref[g]
        for p in range(n_pages):
            page_t = pages[g * n_pages + p][...].astype(BF16)
            s = jnp.dot(qi[:, :IDX_DIM], page_t, preferred_element_type=F32)
            out_ref[g, :, p * PAGE:(p + 1) * PAGE] = (
                jnp.sum(wi * jnp.maximum(s, 0.0), axis=0, keepdims=True) * IDX_SCALE)
        s_new = jnp.sum(qi.astype(F32) * kin_ref[g].astype(BF16).astype(F32), axis=-1, keepdims=True)
        sc_new = jnp.sum(wi * jnp.maximum(s_new, 0.0), axis=0, keepdims=True) * IDX_SCALE
        out_ref[g, :, n_pages * PAGE:] = jnp.where(lane == 0, sc_new, -jnp.inf)


def _idx_sample(pt, qi3, wi3, kin3, pool_ki_t, n_pages):
    db = qi3.shape[0]
    assert db % IDX_SEQS == 0
    per_step = lambda shape: pl.BlockSpec((IDX_SEQS,) + shape, lambda d, pt: (d, 0, 0))
    page_spec = lambda g, p: pl.BlockSpec((None, IDX_DIM, PAGE),
                                          lambda d, pt: (pt[(d * IDX_SEQS + g) * n_pages + p], 0, 0))
    width = n_pages * PAGE + LANES
    return pl.pallas_call(
        functools.partial(_idx_sample_kernel, n_pages=n_pages),
        grid_spec=pltpu.PrefetchScalarGridSpec(
            num_scalar_prefetch=1, grid=(db // IDX_SEQS,),
            in_specs=[per_step((IDX_HEADS, LANES)), per_step((IDX_HEADS, 1)), per_step((1, LANES))]
            + [page_spec(g, p) for g in range(IDX_SEQS) for p in range(n_pages)],
            out_specs=per_step((1, width))),
        out_shape=jax.ShapeDtypeStruct((db, 1, width), F32),
        compiler_params=_cparams("arbitrary"),
        name="idx_sample",
    )(pt, qi3, wi3, kin3, *([pool_ki_t] * (IDX_SEQS * n_pages)))


def _select_sample_kernel(sc_ref, rows_ref, ids_ref, bias_ref, key_scr, tri_scr, rank_scr, ids_scr, *, n_keys):
    lane = lax.broadcasted_iota(I32, (1, sc_ref.shape[1]), 1)
    key_scr[...] = jnp.where(lane < n_keys, _sortable_key(sc_ref[...]), jnp.int32(INT_MIN))

    def count_ge(cand):
        return jnp.sum(jnp.where(key_scr[...] >= cand, 1.0, 0.0), axis=-1, keepdims=True)

    thr, _ = _kth_largest_key(count_ge, float(sc_ref.shape[1]))
    need = float(TOPK) - count_ge(thr + 1)

    def count_tie_le(cm):
        return jnp.sum(jnp.where((key_scr[...] == thr) & (lane <= cm), 1.0, 0.0), axis=-1, keepdims=True)

    cut = _tie_cutoff(count_tie_le, need, 12)
    cut = jnp.where(thr == INT_MIN, jnp.int32(-1), cut)
    key = key_scr[...]
    sel = (key > thr) | ((key == thr) & (lane <= cut))

    past = n_keys - 1
    n_seq = sc_ref.shape[0]
    taken = jnp.where(sel, 1.0, 0.0)[:, :past]
    for r in range(past // TK):
        upper = (lax.broadcasted_iota(I32, (TK, past), 0) + r * TK) <= lax.broadcasted_iota(I32, (TK, past), 1)
        tri_scr[r * TK:(r + 1) * TK, :] = jnp.where(upper, 1.0, 0.0).astype(BF16)
    rank = jnp.dot(taken.astype(BF16), tri_scr[...], preferred_element_type=F32)
    rank_scr[...] = rank * taken
    n_taken = rank[:, past - 1:past]
    slot = (lax.broadcasted_iota(I32, (TOPK, 1), 0) + 1).astype(F32)
    seq_lane = lax.broadcasted_iota(I32, (1, n_seq), 1)
    ids_scr[...] = jnp.zeros(ids_scr.shape, F32)

    def seq_body(d, carry):
        hit = rank_scr[pl.ds(d, 1), :] == slot
        row = jnp.sum(jnp.where(hit, rows_ref[pl.ds(d, 1), :], 0.0), axis=-1, keepdims=True)
        ids_scr[...] += jnp.where(seq_lane == d, row, 0.0)
        return carry

    lax.fori_loop(0, n_seq, seq_body, 0)
    ids_ref[...] = ids_scr[...].T.astype(I32)
    blane = lax.broadcasted_iota(I32, bias_ref.shape, 1)
    new_taken = jnp.sum(jnp.where(sel & (lane == past), 1.0, 0.0), axis=-1, keepdims=True)
    live = (blane.astype(F32) < n_taken) | ((blane == TOPK) & (new_taken > 0.0))
    bias_ref[...] = jnp.where(live, 0.0, NEG)


def _select_sample(scores, cache_rows, n_keys):
    n_seq = scores.shape[0]
    past = n_keys - 1
    return pl.pallas_call(
        functools.partial(_select_sample_kernel, n_keys=n_keys),
        out_shape=[jax.ShapeDtypeStruct((n_seq, TOPK), I32), jax.ShapeDtypeStruct((n_seq, TOPK + LANES), F32)],
        scratch_shapes=[pltpu.VMEM(scores.shape, I32), pltpu.VMEM((past, past), BF16),
                        pltpu.VMEM((n_seq, past), F32), pltpu.VMEM((TOPK, n_seq), F32)],
        compiler_params=_cparams(),
        name="select_sample",
    )(scores, cache_rows)


def _gather_kv_rows(pool_k, pool_v, ids):
    n = ids.shape[0]
    mesh = plsc.VectorSubcoreMesh(core_axis_name="core", subcore_axis_name="subcore")
    n_workers = mesh.num_cores * mesh.num_subcores
    per_worker = n // n_workers
    assert n % (n_workers * SC_GATHER_ROWS) == 0
    row = pool_k.shape[1:]
    out = jax.ShapeDtypeStruct((n,) + row, pool_k.dtype)

    @functools.partial(
        pl.kernel, mesh=mesh, out_type=[out, out],
        scratch_types=[pltpu.VMEM((SC_GATHER_ROWS,), I32), pltpu.VMEM((SC_GATHER_ROWS,) + row, pool_k.dtype),
                       pltpu.VMEM((SC_GATHER_ROWS,) + row, pool_v.dtype), pltpu.SemaphoreType.DMA,
                       pltpu.SemaphoreType.DMA],
        compiler_params=pltpu.CompilerParams(use_tc_tiling_on_sc=True),
        name="gather_kv_rows",
    )
    def gather(k_hbm, v_hbm, ids_hbm, ko_hbm, vo_hbm, ids_v, k_v, v_v, ksem, vsem):
        worker = lax.axis_index("subcore") * mesh.num_cores + lax.axis_index("core")

        @pl.loop(0, per_worker // SC_GATHER_ROWS)
        def _(j):
            off = pl.multiple_of(worker * per_worker + j * SC_GATHER_ROWS, SC_GATHER_ROWS)
            pltpu.sync_copy(ids_hbm.at[pl.ds(off, SC_GATHER_ROWS)], ids_v)
            kcopy = pltpu.async_copy(k_hbm.at[ids_v], k_v, ksem)
            vcopy = pltpu.async_copy(v_hbm.at[ids_v], v_v, vsem)
            kcopy.wait()
            pltpu.sync_copy(k_v, ko_hbm.at[pl.ds(off, SC_GATHER_ROWS)])
            vcopy.wait()
            pltpu.sync_copy(v_v, vo_hbm.at[pl.ds(off, SC_GATHER_ROWS)])

    return gather(pool_k, pool_v, ids)


def _attn_sample_kernel(q_ref, kn_ref, vn_ref, bias_ref, kg_ref, vg_ref, o_ref, kall, vall):
    past_rows = kg_ref.shape[1]
    zeros = jnp.zeros((LANES - N_HEADS, HEAD_DIM), F32)
    for g in range(ATT_SEQS):
        kall[:past_rows, :] = kg_ref[g].astype(BF16)
        vall[:past_rows, :] = vg_ref[g].astype(BF16)
        kall[past_rows:, :] = jnp.concatenate([kn_ref[g], zeros], axis=0).astype(BF16)
        vall[past_rows:, :] = jnp.concatenate([vn_ref[g], zeros], axis=0).astype(BF16)
        s = lax.dot_general(q_ref[g], kall[...], NT_DIMS, preferred_element_type=F32)
        own = (lax.broadcasted_iota(I32, s.shape, 1) & (N_HEADS - 1)) == lax.broadcasted_iota(I32, s.shape, 0)
        s = jnp.where(own, s + bias_ref[g], NEG)
        m = jnp.max(s, axis=-1, keepdims=True)
        p = jnp.exp2(s - m)
        l = jnp.sum(p, axis=-1, keepdims=True)
        o_ref[g] = jnp.dot(p.astype(BF16), vall[...], preferred_element_type=F32) / l


def _attn_sample(q3, kn3, vn3, bias3, kg, vg):
    db, past_rows, _ = kg.shape
    width = past_rows + LANES
    assert db % ATT_SEQS == 0 and bias3.shape == (db, 1, width)
    per_step = lambda r, w: pl.BlockSpec((ATT_SEQS, r, w), lambda d: (d, 0, 0))
    return pl.pallas_call(
        _attn_sample_kernel,
        grid=(db // ATT_SEQS,),
        in_specs=[per_step(N_HEADS, HEAD_DIM)] * 3 + [per_step(1, width)] + [per_step(past_rows, HEAD_DIM)] * 2,
        out_specs=per_step(N_HEADS, HEAD_DIM),
        out_shape=jax.ShapeDtypeStruct((db, N_HEADS, HEAD_DIM), F32),
        scratch_shapes=[pltpu.VMEM((width, HEAD_DIM), BF16), pltpu.VMEM((width, HEAD_DIM), BF16)],
        compiler_params=_cparams("arbitrary"),
        name="attn_sample",
    )(q3, kn3, vn3, bias3, kg, vg)


def _ffn_sample_body(x, g_ref, wup_ref, wc_ref, bc_ref, wdn_ref, st0_ref, st1_ref, gate_ref, a_scr, h_scr):
    a_scr[...] = _rms(x, g_ref[...]).astype(BF16)

    def gate_taps(c, gate):
        cs = slice(c * FF_CHUNK, (c + 1) * FF_CHUNK)
        gate_ref[:, cs] = gate
        return st0_ref[:, cs], st1_ref[:, cs]

    _glu_hidden(a_scr, wup_ref, wc_ref, bc_ref, h_scr, gate_taps)
    return x + jnp.dot(h_scr[...], wdn_ref[...], preferred_element_type=F32)


def _layer0_sample_kernel(x_ref, o_ref, wo_ref, g_ref, wup_ref, wc_ref, bc_ref, wdn_ref, st0_ref, st1_ref,
                          out_ref, gate_ref, a_scr, h_scr):
    x = x_ref[...] + jnp.dot(o_ref[...].astype(BF16), wo_ref[...], preferred_element_type=F32)
    out_ref[...] = _ffn_sample_body(x, g_ref, wup_ref, wc_ref, bc_ref, wdn_ref, st0_ref, st1_ref, gate_ref,
                                    a_scr, h_scr)


def _layer1_sample_kernel(x_ref, gm_ref, win_ref, wcr_ref, bcr_ref, wg_ref, bga_ref, bgx_ref, lam_ref, wor_ref,
                          cs0_ref, cs1_ref, cs2_ref, h0_ref, g_ref, wup_ref, wc_ref, bc_ref, wdn_ref, st0_ref,
                          st1_ref, gf_ref, out_ref, hnew_ref, xb_ref, gate_ref, a_scr, h_scr, u_scr):
    x = x_ref[...]
    a_scr[...] = _rms(x, gm_ref[...]).astype(BF16)
    nblk = MXU_N // LANES
    ncs = -RG_C * jax.nn.softplus(-lam_ref[...])
    for c in range(D_MODEL // MXU_N):
        cs = slice(c * MXU_N, (c + 1) * MXU_N)
        xb = jnp.dot(a_scr[...], win_ref[:, D_MODEL + c * MXU_N:D_MODEL + (c + 1) * MXU_N],
                     preferred_element_type=F32)
        gate = jnp.dot(a_scr[...], win_ref[:, cs], preferred_element_type=F32)
        xb_ref[:, cs] = xb
        xc = (bcr_ref[:, cs] + wcr_ref[0:1, cs] * cs0_ref[:, cs] + wcr_ref[1:2, cs] * cs1_ref[:, cs]
              + wcr_ref[2:3, cs] * cs2_ref[:, cs] + wcr_ref[3:4, cs] * xb)
        for k in range(nblk):
            n = c * nblk + k
            ls = slice(n * LANES, (n + 1) * LANES)
            xcn = xc[:, k * LANES:(k + 1) * LANES]
            gg = jnp.dot(xcn.astype(BF16), wg_ref[n], preferred_element_type=F32)
            a_t, b_t = _lru_coeffs(xcn, gg, bga_ref[:, ls], bgx_ref[:, ls], ncs[:, ls])
            h = a_t * h0_ref[:, ls] + b_t
            hnew_ref[:, ls] = h
            u_scr[:, ls] = (jax.nn.gelu(gate[:, k * LANES:(k + 1) * LANES]) * h).astype(BF16)
    x = x + jnp.dot(u_scr[...], wor_ref[...], preferred_element_type=F32)
    y = _ffn_sample_body(x, g_ref, wup_ref, wc_ref, bc_ref, wdn_ref, st0_ref, st1_ref, gate_ref, a_scr, h_scr)
    out_ref[...] = _rms(y, gf_ref[...])


def _layer_spec(stacked, layer):
    nd = stacked.ndim - 1
    return pl.BlockSpec((None,) + stacked.shape[1:], lambda *_: (layer,) + (0,) * nd, pipeline_mode=pl.Buffered(1))


def _sample_call(kernel, name, args, out_widths, extra_scratch=()):
    db = args[0].shape[0]
    specs = [_layer_spec(*a) if isinstance(a, tuple) else _const_spec(a.shape) for a in args]
    arrays = [a[0] if isinstance(a, tuple) else a for a in args]
    return pl.pallas_call(
        kernel,
        grid=(1,),
        in_specs=specs,
        out_specs=[pl.BlockSpec((db, w), lambda i: (0, 0)) for w in out_widths],
        out_shape=[jax.ShapeDtypeStruct((db, w), F32) for w in out_widths],
        scratch_shapes=[pltpu.VMEM((db, D_MODEL), BF16), pltpu.VMEM((db, D_FF), BF16), *extra_scratch],
        compiler_params=_cparams("arbitrary"),
        name=name,
    )(*arrays)


def _rope_tables(pos):
    posf = pos.astype(F32)[:, None]

    def cs(d):
        half = d // 2
        inv = ROPE_THETA ** (-jnp.arange(half, dtype=F32) * 2.0 / d)
        ang = posf * inv[None, :]
        return jnp.cos(ang), jnp.sin(ang)

    c, s = cs(HEAD_DIM)
    cos = jnp.concatenate([c, c], axis=-1)
    sin = jnp.concatenate([-s, s], axis=-1)
    c, s = cs(IDX_DIM)
    one, zero = jnp.ones_like(c), jnp.zeros_like(c)
    tile = lambda parts: jnp.concatenate(parts * (IDX_HEADS + 1), axis=-1)
    icos = tile([c, c, one, one])
    isa = tile([-s, zero, zero, zero])
    isb = tile([zero, s, zero, zero])
    return cos, sin, icos, isa, isb


def _split_attn_in(w):
    qkv = N_HEADS * HEAD_DIM
    wqk, wv, wi = w[:, :2 * qkv], w[:, 2 * qkv:3 * qkv], w[:, 3 * qkv:]
    pad = lambda a: jnp.pad(a, ((0, 0), (0, LANES - a.shape[1])))
    groups = [pad(wi[:, h * IDX_DIM:(h + 1) * IDX_DIM]) for h in range(IDX_HEADS)]
    groups.append(pad(wi[:, IDX_HEADS * IDX_DIM:]))
    return wqk.astype(BF16), wv.astype(BF16), jnp.concatenate(groups, axis=-1).astype(BF16)


def kernel(x_prompt, x_sample, cache_k, cache_v, cache_kidx, state_lru_h, state_lru_conv, state_ffn_conv,
           page_table, norm_mix, norm_ffn, norm_final, w_attn_in, w_attn_out, w_rec_in, w_rec_conv, b_rec_conv,
           w_gate_a, b_gate_a, w_gate_x, b_gate_x, lru_lambda, w_rec_out, w_ffn_up, w_ffn_conv, b_ffn_conv,
           w_ffn_down):
    bsz, seq, d = x_prompt.shape
    db = x_sample.shape[0]
    n_pages = page_table.shape[1]
    past = n_pages * PAGE
    assert d == D_MODEL and x_sample.shape[1] == 1 and seq % ROW_BLOCK == 0 and seq % TQ == 0
    assert min(TOPK, seq // 4) == TOPK and min(TOPK, (past + 1) // 4) == TOPK

    vec = lambda a: a.reshape(1, -1)
    wqk, wv, widx = _split_attn_in(w_attn_in[0])
    wo_attn = w_attn_out[0].astype(BF16)
    w_rin = w_rec_in[0].astype(BF16)
    w_gates = jnp.concatenate([w_gate_a[0], w_gate_x[0]], axis=-1).astype(BF16)
    wo_rec = w_rec_out[0].astype(BF16)
    wup_all, wdn_all = w_ffn_up.astype(BF16), w_ffn_down.astype(BF16)
    wup = [(wup_all, i) for i in range(2)]
    wdn = [(wdn_all, i) for i in range(2)]
    rec_vecs = (w_rec_conv[0], vec(b_rec_conv[0]), w_gates, vec(b_gate_a[0]), vec(b_gate_x[0]),
                vec(lru_lambda[0]), wo_rec)
    ffn_vecs = [(vec(norm_ffn[i]), wup[i], w_ffn_conv[i], vec(b_ffn_conv[i]), wdn[i]) for i in range(2)]
    gfin = vec(norm_final)

    xp = x_prompt.reshape(bsz * seq, d)
    q, kf, kb, vf, vt, qi, kiw = _attn_in(xp, vec(norm_mix[0]), wqk, wv, widx, _rope_tables(jnp.arange(seq)),
                                          ROW_BLOCK, seq // ROW_BLOCK)
    wit = kiw[:, IDX_DIM:IDX_DIM + IDX_HEADS].T
    hp_attn = _dsa_prompt(q, kb, vt, qi, kiw, wit, xp, wo_attn, bsz, seq)
    hp, ftail0 = _ffn_prompt(hp_attn, *ffn_vecs[0], gfin, bsz, seq, False)
    hp, hlast, ctail = _rec_prompt(hp, vec(norm_mix[1]), w_rin, *rec_vecs, bsz, seq)
    yp, ftail1 = _ffn_prompt(hp, *ffn_vecs[1], gfin, bsz, seq, True)

    xs = x_sample.reshape(db, d)
    tabs = tuple(jnp.broadcast_to(t, (db, t.shape[1])) for t in _rope_tables(jnp.full((1,), past)))
    qs, kfs, _, vfs, _, qis, kiws = _attn_in(xs, vec(norm_mix[0]), wqk, wv, widx, tabs, db, 1)
    pt = page_table.reshape(-1)
    scores = _idx_sample(pt, qis.reshape(db, IDX_HEADS, LANES),
                         kiws[:, IDX_DIM:IDX_DIM + IDX_HEADS].reshape(db, IDX_HEADS, 1),
                         kiws.reshape(db, 1, LANES), jnp.swapaxes(cache_kidx[0], 1, 2), n_pages)
    cache_rows = (page_table[:, :, None] * PAGE + jnp.arange(PAGE, dtype=I32)).reshape(db, past).astype(F32)
    ids, bias = _select_sample(scores.reshape(db, -1), cache_rows, past + 1)
    heads = (N_HEADS, HEAD_DIM)
    key_rows = lambda pool: pool[0].reshape(-1, *heads)
    ids, _ = lax.optimization_barrier((ids, kb))
    kg, vg = _gather_kv_rows(key_rows(cache_k), key_rows(cache_v), ids.reshape(-1))
    bias_rows = jnp.pad(jnp.repeat(bias[:, :TOPK + 1], N_HEADS, axis=1), ((0, 0), (0, LANES - N_HEADS)),
                        constant_values=NEG)
    bias_rows, _ = lax.optimization_barrier((bias_rows, hp_attn))
    seq_rows = lambda g: g.reshape(db, TOPK * N_HEADS, HEAD_DIM)
    o_s = _attn_sample(qs.reshape(db, *heads), kfs.reshape(db, *heads), vfs.reshape(db, *heads),
                       bias_rows.reshape(db, 1, -1), seq_rows(kg), seq_rows(vg))
    st = state_ffn_conv
    hs, gate0 = _sample_call(_layer0_sample_kernel, "layer0_sample",
                             (xs, o_s.reshape(db, d), wo_attn, *ffn_vecs[0], st[0, :, 0], st[0, :, 1]),
                             (D_MODEL, D_FF))
    cst = state_lru_conv[0]
    ys, hnew, xbs, gate1 = _sample_call(
        _layer1_sample_kernel, "layer1_sample",
        (hs, vec(norm_mix[1]), w_rin, *rec_vecs, cst[:, 0], cst[:, 1], cst[:, 2], state_lru_h[0], *ffn_vecs[1],
         st[1, :, 0], st[1, :, 1], gfin),
        (D_MODEL, D_MODEL, D_MODEL, D_FF), extra_scratch=(pltpu.VMEM((db, D_MODEL), BF16),))

    return (
        yp.reshape(bsz, seq, d),
        ys.reshape(db, 1, d),
        kf.reshape(1, bsz, seq, *heads),
        vf.reshape(1, bsz, seq, *heads),
        kiw[:, :IDX_DIM].reshape(1, bsz, seq, IDX_DIM),
        kfs.reshape(1, db, 1, *heads),
        vfs.reshape(1, db, 1, *heads),
        kiws[:, :IDX_DIM].reshape(1, db, 1, IDX_DIM),
        hlast[None, :, 0, :],
        ctail[None, :, SUBLANES - 3:, :],
        hnew[None],
        jnp.stack([cst[:, 1], cst[:, 2], xbs], axis=1)[None],
        jnp.stack([ftail0[:, SUBLANES - 2:], ftail1[:, SUBLANES - 2:]]),
        jnp.stack([jnp.stack([st[0, :, 1], gate0], axis=1), jnp.stack([st[1, :, 1], gate1], axis=1)]),
    )
```

```python
import functools

import jax
import jax.numpy as jnp
from jax import lax
from jax.experimental import pallas as pl
from jax.experimental.pallas import tpu as pltpu
from jax.experimental.pallas import tpu_sc as plsc

F32 = jnp.float32
BF16 = jnp.bfloat16
I32 = jnp.int32
I16 = jnp.int16

D_MODEL = 1024
N_HEADS = 8
HEAD_DIM = 128
IDX_HEADS = 4
IDX_DIM = 64
TOPK = 256
PAGE = 128
ROPE_THETA = 10000.0
IDX_SCALE = (IDX_DIM * IDX_HEADS) ** -0.5
Q_SCALE = HEAD_DIM ** -0.5 * 1.4426950408889634
RG_C = 8.0
D_FF = 2816
RMS_EPS = 1e-6

LANES = 128
SUBLANES = 8
MXU_N = 256
IDX_W = (IDX_HEADS + 1) * LANES
INT_MIN = -(2 ** 31)
NEG = -1e30
VMEM_LIMIT = 52 * 1024 * 1024

ROW_BLOCK = 512
CAST_ROWS = 256
FFN_ROWS = 1024
TQ = 256
TK = 256
COUNT_ROWS = 32
COUNT_ROWS_16 = 64
IDX_SEQS = 8
ATT_SEQS = 4
SC_GATHER_ROWS = 32
FF_CHUNK = 256
N_FF_CHUNKS = D_FF // FF_CHUNK
NT_DIMS = (((1,), (1,)), ((), ()))


def _cparams(*sem):
    return pltpu.CompilerParams(dimension_semantics=sem if sem else None, vmem_limit_bytes=VMEM_LIMIT)


def _const_spec(shape):
    nd = len(shape)
    return pl.BlockSpec(shape, lambda *_: (0,) * nd, pipeline_mode=pl.Buffered(1))


def _cast_kernel(w_ref, o_ref):
    o_ref[...] = w_ref[...].astype(o_ref.dtype)


def _to_bf16(w):
    n_layers, n_rows, n_cols = w.shape
    assert n_rows % CAST_ROWS == 0
    spec = pl.BlockSpec((None, CAST_ROWS, n_cols), lambda l, r: (l, r, 0))
    return pl.pallas_call(
        _cast_kernel, grid=(n_layers, n_rows // CAST_ROWS), in_specs=[spec], out_specs=spec,
        out_shape=jax.ShapeDtypeStruct(w.shape, BF16), compiler_params=_cparams("arbitrary", "arbitrary"),
        name="cast_bf16",
    )(w)


def _rms(x, g):
    return x * lax.rsqrt(jnp.mean(x * x, axis=-1, keepdims=True) + RMS_EPS) * g


def _shift_rows(x, s, prev_rows):
    r = pltpu.roll(x, s, 0)
    top = r[:SUBLANES]
    row = lax.broadcasted_iota(I32, top.shape, 0)
    for k in range(s):
        top = jnp.where(row == k, prev_rows[k], top)
    return jnp.concatenate([top, r[SUBLANES:]], axis=0)


def _sortable_key(score):
    bits = pltpu.bitcast(score + 0.0, I32)
    return jnp.where(bits < 0, bits ^ jnp.int32(0x7FFFFFFF), bits)


def _attn_in_kernel(x_ref, g_ref, wqk_ref, wv_ref, widx_ref, cos_ref, sin_ref, icos_ref, isa_ref, isb_ref,
                    q_ref, kf_ref, kb_ref, vf_ref, vt_ref, qi_ref, kiw_ref, a_scr):
    a_scr[...] = _rms(x_ref[...], g_ref[...]).astype(BF16)
    cos = cos_ref[...]
    sin = sin_ref[...]
    for c in range(2 * D_MODEL // MXU_N):
        r = jnp.dot(a_scr[...], wqk_ref[:, c * MXU_N:(c + 1) * MXU_N], preferred_element_type=F32)
        for hh in range(MXU_N // HEAD_DIM):
            xh = r[:, hh * HEAD_DIM:(hh + 1) * HEAD_DIM]
            y = xh * cos + pltpu.roll(xh, HEAD_DIM // 2, 1) * sin
            col = (c * MXU_N) % D_MODEL + hh * HEAD_DIM
            if c < D_MODEL // MXU_N:
                q_ref[:, col:col + HEAD_DIM] = (y * Q_SCALE).astype(BF16)
            else:
                kf_ref[:, col:col + HEAD_DIM] = y
                kb_ref[:, col:col + HEAD_DIM] = y.astype(BF16)
    for c in range(D_MODEL // MXU_N):
        r = jnp.dot(a_scr[...], wv_ref[:, c * MXU_N:(c + 1) * MXU_N], preferred_element_type=F32)
        vf_ref[:, c * MXU_N:(c + 1) * MXU_N] = r
        vt_ref[c * MXU_N:(c + 1) * MXU_N, :] = r.T.astype(BF16)
    ri = jnp.dot(a_scr[...], widx_ref[...], preferred_element_type=F32)
    groups = lambda t: jnp.concatenate([t] * (IDX_W // LANES), axis=1)
    yi = (ri * groups(icos_ref[...]) + pltpu.roll(ri, IDX_W - IDX_DIM // 2, 1) * groups(isa_ref[...])
          + pltpu.roll(ri, IDX_DIM // 2, 1) * groups(isb_ref[...]))
    qi_ref[...] = yi[:, :IDX_HEADS * LANES].astype(BF16)
    kiw_ref[...] = yi[:, IDX_HEADS * LANES:]


def _attn_in(x, g, wqk, wv, widx, tabs, rows, n_pos_blocks):
    n = x.shape[0]
    cos, sin, icos, isa, isb = tabs
    row_spec = lambda w: pl.BlockSpec((rows, w), lambda i: (i, 0))
    tab_spec = lambda w: pl.BlockSpec((rows, w), lambda i: (i % n_pos_blocks, 0))
    vt_spec = pl.BlockSpec((None, D_MODEL, rows), lambda i: (i // n_pos_blocks, 0, i % n_pos_blocks))
    return pl.pallas_call(
        _attn_in_kernel,
        grid=(n // rows,),
        in_specs=[row_spec(D_MODEL), _const_spec((1, D_MODEL)), _const_spec(wqk.shape), _const_spec(wv.shape),
                  _const_spec(widx.shape), tab_spec(HEAD_DIM), tab_spec(HEAD_DIM), tab_spec(LANES),
                  tab_spec(LANES), tab_spec(LANES)],
        out_specs=[row_spec(D_MODEL)] * 4 + [vt_spec, row_spec(IDX_HEADS * LANES), row_spec(LANES)],
        out_shape=[jax.ShapeDtypeStruct((n, D_MODEL), BF16), jax.ShapeDtypeStruct((n, D_MODEL), F32),
                   jax.ShapeDtypeStruct((n, D_MODEL), BF16), jax.ShapeDtypeStruct((n, D_MODEL), F32),
                   jax.ShapeDtypeStruct((n // (rows * n_pos_blocks), D_MODEL, rows * n_pos_blocks), BF16),
                   jax.ShapeDtypeStruct((n, IDX_HEADS * LANES), BF16),
                   jax.ShapeDtypeStruct((n, LANES), F32)],
        scratch_shapes=[pltpu.VMEM((rows, D_MODEL), BF16)],
        compiler_params=_cparams("arbitrary"),
        name="attn_in",
    )(x, g, wqk, wv, widx, cos, sin, icos, isa, isb)


def _kth_largest_key(count_ge, n_total):
    kf = float(TOPK)
    c0 = count_ge(0)
    t = jnp.where(c0 >= kf, jnp.int32(0), jnp.int32(INT_MIN))
    ct = jnp.where(c0 >= kf, c0, n_total)

    def bit_body(it, carry):
        t, ct = carry
        cand = t + (jnp.int32(1) << (30 - it))
        cnt = count_ge(cand)
        return jnp.where(cnt >= kf, cand, t), jnp.where(cnt >= kf, cnt, ct)

    return lax.fori_loop(0, 31, bit_body, (t, ct))


def _kth_largest_half(count_ge, above, count_all):
    kf = float(TOPK)
    c0 = above + count_ge(0)
    t = jnp.where(c0 >= kf, jnp.int32(0), jnp.int32(-(2 ** 15)))
    ct = jnp.where(c0 >= kf, c0, count_all)

    def bit_body(it, carry):
        t, ct = carry
        cand = t + (jnp.int32(1) << (14 - it))
        cnt = above + count_ge(cand)
        return jnp.where(cnt >= kf, cand, t), jnp.where(cnt >= kf, cnt, ct)

    return lax.fori_loop(0, 15, bit_body, (t, ct))


def _tie_cutoff(count_tie_le, need, n_bits):
    def bit_body(it, c):
        cand = c + (jnp.int32(1) << (n_bits - 1 - it))
        return jnp.where(count_tie_le(cand - 1) < need, cand, c)

    return lax.fori_loop(0, n_bits, bit_body, jnp.zeros_like(need, dtype=I32))


def _dsa_prompt_kernel(q_ref, kb_ref, vt_ref, qi_ref, kiwk_ref, wit_ref, x_ref, wo_ref, out_ref,
                       key_scr, bias_scr, kib_scr, o_scr, acc_scr, s_scr, hi_scr, lo_scr):
    i = pl.program_id(1)
    nk = i + 1

    @pl.when(i == 0)
    def _():
        kib_scr[...] = kiwk_ref[...].astype(BF16)

    qpos = i * TQ + lax.broadcasted_iota(I32, (1, TQ), 1)

    def rows(kc):
        return pl.ds(pl.multiple_of(kc * TK, TK), TK)

    def spos(kc):
        return kc * TK + lax.broadcasted_iota(I32, (TK, 1), 0)

    def colsum(a):
        return jnp.sum(a.reshape(TK // COUNT_ROWS, COUNT_ROWS, TQ), axis=0)

    wis = wit_ref[...] * IDX_SCALE

    def score_body(kc, carry):
        kic = kib_scr[rows(kc), :]
        sc = jnp.zeros((TK, TQ), F32)
        for h in range(IDX_HEADS):
            s = lax.dot_general(kic, qi_ref[:, h * LANES:(h + 1) * LANES], NT_DIMS, preferred_element_type=F32)
            sc = sc + wis[h:h + 1, :] * jnp.maximum(s, 0.0)
        key = jnp.where(spos(kc) <= qpos, _sortable_key(sc), jnp.int32(INT_MIN))
        key_scr[rows(kc), :] = key
        hi_scr[rows(kc), :] = (key >> 16).astype(I16)
        lo_scr[rows(kc), :] = ((key & 0xFFFF) - 2 ** 15).astype(I16)
        return carry

    lax.fori_loop(0, nk, score_body, 0)

    def count_ge(cand):
        def body(kc, acc):
            return acc + colsum(jnp.where(key_scr[rows(kc), :] >= cand, 1.0, 0.0))
        acc = lax.fori_loop(0, nk, body, jnp.zeros((COUNT_ROWS, TQ), F32))
        return jnp.sum(acc, axis=0, keepdims=True)

    def count_ge_half(half_scr):
        def count(cand):
            c16 = jnp.asarray(cand, I32).astype(I16)

            def body(kc, acc):
                ind = jnp.where(half_scr[rows(kc), :] >= c16, jnp.int16(1), jnp.int16(0))
                for r in range(TK // COUNT_ROWS_16):
                    acc = acc + ind[r * COUNT_ROWS_16:(r + 1) * COUNT_ROWS_16]
                return acc

            acc = lax.fori_loop(0, nk, body, jnp.zeros((COUNT_ROWS_16, TQ), I16))
            return jnp.sum(acc.astype(F32), axis=0, keepdims=True)
        return count

    count_hi, count_lo = count_ge_half(hi_scr), count_ge_half(lo_scr)
    thi, cnt_hi = _kth_largest_half(count_hi, 0.0, (nk * TK).astype(F32))
    above = jnp.where(thi == 2 ** 15 - 1, 0.0, count_hi(jnp.minimum(thi + 1, 2 ** 15 - 1)))
    thi16 = thi.astype(I16)

    def lo_body(kc, carry):
        lo_scr[rows(kc), :] = jnp.where(hi_scr[rows(kc), :] == thi16, lo_scr[rows(kc), :], jnp.int16(-(2 ** 15)))
        return carry

    lax.fori_loop(0, nk, lo_body, 0)
    tlo, cnt_thr = _kth_largest_half(count_lo, above, cnt_hi)
    thr = (thi << 16) | (tlo + 2 ** 15)
    short = thr == INT_MIN
    has_ties = jnp.max(jnp.where(short, 0.0, cnt_thr)) > float(TOPK)

    @pl.when(jnp.logical_not(has_ties))
    def _():
        thr_vis = jnp.maximum(thr, jnp.int32(INT_MIN + 1))

        def bias_body(kc, carry):
            bias_scr[rows(kc), :] = jnp.where(key_scr[rows(kc), :] >= thr_vis, 0.0, NEG)
            return carry

        lax.fori_loop(0, nk, bias_body, 0)

    @pl.when(has_ties)
    def _():
        need = float(TOPK) - count_ge(thr + 1)

        def eq_body(kc, carry):
            bias_scr[rows(kc), :] = jnp.where(key_scr[rows(kc), :] == thr, 1.0, 0.0)
            return carry

        lax.fori_loop(0, nk, eq_body, 0)

        def count_tie_le(cm):
            def body(kc, acc):
                return acc + colsum(jnp.where(spos(kc) <= cm, bias_scr[rows(kc), :], 0.0))
            acc = lax.fori_loop(0, nk, body, jnp.zeros((COUNT_ROWS, TQ), F32))
            return jnp.sum(acc, axis=0, keepdims=True)

        cut = jnp.where(short, jnp.int32(-1), _tie_cutoff(count_tie_le, need, 11))

        def bias_body(kc, carry):
            sel = (key_scr[rows(kc), :] > thr) | ((bias_scr[rows(kc), :] > 0.0) & (spos(kc) <= cut))
            bias_scr[rows(kc), :] = jnp.where(sel, 0.0, NEG)
            return carry

        lax.fori_loop(0, nk, bias_body, 0)

    acc_scr[...] = jnp.zeros(acc_scr.shape, F32)

    def att_body(kc, carry):
        ms, ls = carry
        for h in range(N_HEADS):
            hs = slice(h * HEAD_DIM, (h + 1) * HEAD_DIM)
            s_scr[h] = lax.dot_general(kb_ref[rows(kc), hs], q_ref[:, hs], NT_DIMS, preferred_element_type=F32)
        bias = bias_scr[rows(kc), :]
        new_ms, new_ls = [], []
        for h in range(N_HEADS):
            hs = slice(h * HEAD_DIM, (h + 1) * HEAD_DIM)
            s = s_scr[h] + bias
            m_new = jnp.maximum(ms[h], jnp.max(s, axis=0, keepdims=True))
            alpha = jnp.exp2(ms[h] - m_new)
            p = jnp.exp2(s - m_new)
            new_ms.append(m_new)
            new_ls.append(alpha * ls[h] + jnp.sum(p, axis=0, keepdims=True))
            pv = jnp.dot(vt_ref[hs, rows(kc)], p.astype(BF16), preferred_element_type=F32)
            acc_scr[hs, :] = alpha * acc_scr[hs, :] + pv
        return tuple(new_ms), tuple(new_ls)

    init = ((jnp.full((1, TQ), NEG, F32),) * N_HEADS, (jnp.zeros((1, TQ), F32),) * N_HEADS)
    _, ls = lax.fori_loop(0, nk, att_body, init)
    for h in range(N_HEADS):
        hs = slice(h * HEAD_DIM, (h + 1) * HEAD_DIM)
        o_scr[:, hs] = (acc_scr[hs, :] / ls[h]).T.astype(BF16)
    out_ref[...] = x_ref[...] + jnp.dot(o_scr[...], wo_ref[...], preferred_element_type=F32)


def _dsa_prompt(q, kb, vt, qi, kiw, wit, x, wo, bsz, seq):
    n = x.shape[0]
    nq = seq // TQ
    qrow = lambda w: pl.BlockSpec((TQ, w), lambda b, i: (b * nq + i, 0))
    brow = lambda w: pl.BlockSpec((seq, w), lambda b, i: (b, 0))
    return pl.pallas_call(
        _dsa_prompt_kernel,
        grid=(bsz, nq),
        in_specs=[qrow(D_MODEL), brow(D_MODEL), pl.BlockSpec((None, D_MODEL, seq), lambda b, i: (b, 0, 0)),
                  qrow(IDX_HEADS * LANES), brow(LANES), pl.BlockSpec((IDX_HEADS, TQ), lambda b, i: (0, b * nq + i)),
                  qrow(D_MODEL), _const_spec(wo.shape)],
        out_specs=qrow(D_MODEL),
        out_shape=jax.ShapeDtypeStruct((n, D_MODEL), F32),
        scratch_shapes=[pltpu.VMEM((seq, TQ), I32), pltpu.VMEM((seq, TQ), F32), pltpu.VMEM((seq, LANES), BF16),
                        pltpu.VMEM((TQ, D_MODEL), BF16), pltpu.VMEM((D_MODEL, TQ), F32),
                        pltpu.VMEM((N_HEADS, TK, TQ), F32), pltpu.VMEM((seq, TQ), I16),
                        pltpu.VMEM((seq, TQ), I16)],
        compiler_params=_cparams("arbitrary", "arbitrary"),
        name="dsa_prompt",
    )(q, kb, vt, qi, kiw, wit, x, wo)


def _glu_hidden(a_scr, wup_ref, wc_ref, bc_ref, h_scr, gate_taps):
    for c in range(N_FF_CHUNKS):
        cs = slice(c * FF_CHUNK, (c + 1) * FF_CHUNK)
        gate = jnp.dot(a_scr[...], wup_ref[:, cs], preferred_element_type=F32)
        val = jnp.dot(a_scr[...], wup_ref[:, D_FF + c * FF_CHUNK:D_FF + (c + 1) * FF_CHUNK],
                      preferred_element_type=F32)
        g2, g1 = gate_taps(c, gate)
        gc = bc_ref[:, cs] + wc_ref[0:1, cs] * g2 + wc_ref[1:2, cs] * g1 + wc_ref[2:3, cs] * gate
        h_scr[:, cs] = (jax.nn.gelu(gc) * val).astype(BF16)


def _ffn_prompt_kernel(x_ref, g_ref, wup_ref, wc_ref, bc_ref, wdn_ref, gf_ref, out_ref, tail_ref,
                       a_scr, h_scr, carry_scr, *, final_norm):
    j = pl.program_id(1)
    rows = x_ref.shape[0]
    a_scr[...] = _rms(x_ref[...], g_ref[...]).astype(BF16)

    @pl.when(j == 0)
    def _():
        carry_scr[...] = jnp.zeros(carry_scr.shape, F32)

    def gate_taps(c, gate):
        cs = slice(c * FF_CHUNK, (c + 1) * FF_CHUNK)
        p0 = carry_scr[SUBLANES - 2:SUBLANES - 1, cs]
        p1 = carry_scr[SUBLANES - 1:SUBLANES, cs]
        g1 = _shift_rows(gate, 1, [p1])
        g2 = _shift_rows(gate, 2, [p0, p1])
        carry_scr[:, cs] = gate[rows - SUBLANES:, :]
        tail_ref[0, :, cs] = gate[rows - SUBLANES:, :]
        return g2, g1

    _glu_hidden(a_scr, wup_ref, wc_ref, bc_ref, h_scr, gate_taps)
    y = x_ref[...] + jnp.dot(h_scr[...], wdn_ref[...], preferred_element_type=F32)
    out_ref[...] = _rms(y, gf_ref[...]) if final_norm else y


def _ffn_prompt(x, g, wup, wc, bc, wdn, gf, bsz, seq, final_norm):
    n = x.shape[0]
    assert seq % FFN_ROWS == 0
    nb = seq // FFN_ROWS
    row = pl.BlockSpec((FFN_ROWS, D_MODEL), lambda b, j: (b * nb + j, 0))
    (wup, lu), (wdn, ld) = wup, wdn
    return pl.pallas_call(
        functools.partial(_ffn_prompt_kernel, final_norm=final_norm),
        grid=(bsz, nb),
        in_specs=[row, _const_spec((1, D_MODEL)), _layer_spec(wup, lu), _const_spec(wc.shape),
                  _const_spec(bc.shape), _layer_spec(wdn, ld), _const_spec((1, D_MODEL))],
        out_specs=[row, pl.BlockSpec((1, SUBLANES, D_FF), lambda b, j: (b, 0, 0))],
        out_shape=[jax.ShapeDtypeStruct((n, D_MODEL), F32), jax.ShapeDtypeStruct((bsz, SUBLANES, D_FF), F32)],
        scratch_shapes=[pltpu.VMEM((FFN_ROWS, D_MODEL), BF16), pltpu.VMEM((FFN_ROWS, D_FF), BF16),
                        pltpu.VMEM((SUBLANES, D_FF), F32)],
        compiler_params=_cparams("arbitrary", "arbitrary"),
        name="ffn_prompt_final" if final_norm else "ffn_prompt",
    )(x, g, wup, wc, bc, wdn, gf)


def _sigmoid(x):
    return 0.5 * jnp.tanh(0.5 * x) + 0.5


def _lru_coeffs(xc, gg, bga, bgx, neg_c_softplus):
    r = _sigmoid(gg[:, :LANES] + bga)
    ig = _sigmoid(gg[:, LANES:] + bgx)
    log_a = r * neg_c_softplus
    a = jnp.exp(log_a)
    z = -jnp.tanh(log_a) * (a * a + 1.0)
    root = jnp.where(z > 0.0, z * lax.rsqrt(z), 0.0)
    return a, root * ig * xc


def _group_scan(a, b):
    row = lax.broadcasted_iota(I32, a.shape, 0) & (SUBLANES - 1)
    for d in (1, 2, 4):
        inside = row >= d
        a_prev = jnp.where(inside, pltpu.roll(a, d, 0), 1.0)
        b_prev = jnp.where(inside, pltpu.roll(b, d, 0), 0.0)
        b = a * b_prev + b
        a = a * a_prev
    return a, b


def _rec_prompt_kernel(x_ref, g_ref, win_ref, wc_ref, bc_ref, wg_ref, bga_ref, bgx_ref, lam_ref, wo_ref,
                       out_ref, hlast_ref, ctail_ref, a_scr, at_scr, bt_scr, u_scr, hcar_scr, ccar_scr):
    j = pl.program_id(1)
    rows = x_ref.shape[0]
    a_scr[...] = _rms(x_ref[...], g_ref[...]).astype(BF16)

    @pl.when(j == 0)
    def _():
        ccar_scr[...] = jnp.zeros(ccar_scr.shape, F32)
        hcar_scr[...] = jnp.zeros(hcar_scr.shape, F32)

    nblk = MXU_N // LANES
    ncs = -RG_C * jax.nn.softplus(-lam_ref[...])
    for c in range(D_MODEL // MXU_N):
        cs = slice(c * MXU_N, (c + 1) * MXU_N)
        xb = jnp.dot(a_scr[...], win_ref[:, D_MODEL + c * MXU_N:D_MODEL + (c + 1) * MXU_N],
                     preferred_element_type=F32)
        prev = [ccar_scr[SUBLANES - 3 + k:SUBLANES - 2 + k, cs] for k in range(3)]
        xc = (bc_ref[:, cs] + wc_ref[0:1, cs] * _shift_rows(xb, 3, prev) + wc_ref[1:2, cs] * _shift_rows(xb, 2, prev[1:])
              + wc_ref[2:3, cs] * _shift_rows(xb, 1, prev[2:]) + wc_ref[3:4, cs] * xb)
        ccar_scr[:, cs] = xb[rows - SUBLANES:, :]
        ctail_ref[0, :, cs] = xb[rows - SUBLANES:, :]
        for k in range(nblk):
            n = c * nblk + k
            ls = slice(n * LANES, (n + 1) * LANES)
            xcn = xc[:, k * LANES:(k + 1) * LANES]
            gg = jnp.dot(xcn.astype(BF16), wg_ref[n], preferred_element_type=F32)
            a_t, b_t = _group_scan(*_lru_coeffs(xcn, gg, bga_ref[:, ls], bgx_ref[:, ls], ncs[:, ls]))
            at_scr[:, ls] = a_t
            bt_scr[:, ls] = b_t

    def group_step(g, h):
        r = pl.ds(pl.multiple_of(g * SUBLANES, SUBLANES), SUBLANES)
        hg = at_scr[r, :] * h + bt_scr[r, :]
        bt_scr[r, :] = hg
        return hg[SUBLANES - 1:, :]

    h_last = lax.fori_loop(0, rows // SUBLANES, group_step, hcar_scr[0:1, :], unroll=4)
    hcar_scr[...] = jnp.broadcast_to(h_last, hcar_scr.shape)
    hlast_ref[0] = jnp.broadcast_to(h_last, hcar_scr.shape)
    for c in range(D_MODEL // MXU_N):
        cs = slice(c * MXU_N, (c + 1) * MXU_N)
        gate = jnp.dot(a_scr[...], win_ref[:, cs], preferred_element_type=F32)
        u_scr[:, cs] = (jax.nn.gelu(gate) * bt_scr[:, cs]).astype(BF16)
    out_ref[...] = x_ref[...] + jnp.dot(u_scr[...], wo_ref[...], preferred_element_type=F32)


def _rec_prompt(x, g, win, wc, bc, wg, bga, bgx, lam, wo, bsz, seq):
    n = x.shape[0]
    nb = seq // ROW_BLOCK
    row = pl.BlockSpec((ROW_BLOCK, D_MODEL), lambda b, j: (b * nb + j, 0))
    tail = pl.BlockSpec((1, SUBLANES, D_MODEL), lambda b, j: (b, 0, 0))
    vec = _const_spec((1, D_MODEL))
    return pl.pallas_call(
        _rec_prompt_kernel,
        grid=(bsz, nb),
        in_specs=[row, vec, _const_spec(win.shape), _const_spec(wc.shape), vec, _const_spec(wg.shape), vec, vec,
                  vec, _const_spec(wo.shape)],
        out_specs=[row, tail, tail],
        out_shape=[jax.ShapeDtypeStruct((n, D_MODEL), F32), jax.ShapeDtypeStruct((bsz, SUBLANES, D_MODEL), F32),
                   jax.ShapeDtypeStruct((bsz, SUBLANES, D_MODEL), F32)],
        scratch_shapes=[pltpu.VMEM((ROW_BLOCK, D_MODEL), BF16), pltpu.VMEM((ROW_BLOCK, D_MODEL), F32),
                        pltpu.VMEM((ROW_BLOCK, D_MODEL), F32), pltpu.VMEM((ROW_BLOCK, D_MODEL), BF16),
                        pltpu.VMEM((SUBLANES, D_MODEL), F32), pltpu.VMEM((SUBLANES, D_MODEL), F32)],
        compiler_params=_cparams("arbitrary", "arbitrary"),
        name="rec_prompt",
    )(x, g, win, wc, bc, wg, bga, bgx, lam, wo)


def _idx_sample_kernel(pt_ref, qi_ref, wi_ref, kin_ref, *refs, n_pages):
    pages, out_ref = refs[:-1], refs[-1]
    lane = lax.broadcasted_iota(I32, (1, LANES), 1)
    for g in range(IDX_SEQS):
        qi = qi_ref[g]
        wi = wi_ref[g]
        for p in range(n_pages):
            page_t = pages[g * n_pages + p][...].astype(BF16)
            s = jnp.dot(qi[:, :IDX_DIM], page_t, preferred_element_type=F32)
            out_ref[g, :, p * PAGE:(p + 1) * PAGE] = (
                jnp.sum(wi * jnp.maximum(s, 0.0), axis=0, keepdims=True) * IDX_SCALE)
        s_new = jnp.sum(qi.astype(F32) * kin_ref[g].astype(BF16).astype(F32), axis=-1, keepdims=True)
        sc_new = jnp.sum(wi * jnp.maximum(s_new, 0.0), axis=0, keepdims=True) * IDX_SCALE
        out_ref[g, :, n_pages * PAGE:] = jnp.where(lane == 0, sc_new, -jnp.inf)


def _idx_sample(pt, qi3, wi3, kin3, pool_ki_t, n_pages):
    db = qi3.shape[0]
    assert db % IDX_SEQS == 0
    per_step = lambda shape: pl.BlockSpec((IDX_SEQS,) + shape, lambda d, pt: (d, 0, 0))
    page_spec = lambda g, p: pl.BlockSpec((None, IDX_DIM, PAGE),
                                          lambda d, pt: (pt[(d * IDX_SEQS + g) * n_pages + p], 0, 0))
    width = n_pages * PAGE + LANES
    return pl.pallas_call(
        functools.partial(_idx_sample_kernel, n_pages=n_pages),
        grid_spec=pltpu.PrefetchScalarGridSpec(
            num_scalar_prefetch=1, grid=(db // IDX_SEQS,),
            in_specs=[per_step((IDX_HEADS, LANES)), per_step((IDX_HEADS, 1)), per_step((1, LANES))]
            + [page_spec(g, p) for g in range(IDX_SEQS) for p in range(n_pages)],
            out_specs=per_step((1, width))),
        out_shape=jax.ShapeDtypeStruct((db, 1, width), F32),
        compiler_params=_cparams("arbitrary"),
        name="idx_sample",
    )(pt, qi3, wi3, kin3, *([pool_ki_t] * (IDX_SEQS * n_pages)))


def _select_sample_kernel(sc_ref, rows_ref, ids_ref, bias_ref, key_scr, tri_scr, rank_scr, ids_scr, *, n_keys):
    lane = lax.broadcasted_iota(I32, (1, sc_ref.shape[1]), 1)
    key_scr[...] = jnp.where(lane < n_keys, _sortable_key(sc_ref[...]), jnp.int32(INT_MIN))

    def count_ge(cand):
        return jnp.sum(jnp.where(key_scr[...] >= cand, 1.0, 0.0), axis=-1, keepdims=True)

    thr, _ = _kth_largest_key(count_ge, float(sc_ref.shape[1]))
    need = float(TOPK) - count_ge(thr + 1)

    def count_tie_le(cm):
        return jnp.sum(jnp.where((key_scr[...] == thr) & (lane <= cm), 1.0, 0.0), axis=-1, keepdims=True)

    cut = _tie_cutoff(count_tie_le, need, 12)
    cut = jnp.where(thr == INT_MIN, jnp.int32(-1), cut)
    key = key_scr[...]
    sel = (key > thr) | ((key == thr) & (lane <= cut))

    past = n_keys - 1
    n_seq = sc_ref.shape[0]
    taken = jnp.where(sel, 1.0, 0.0)[:, :past]
    for r in range(past // TK):
        upper = (lax.broadcasted_iota(I32, (TK, past), 0) + r * TK) <= lax.broadcasted_iota(I32, (TK, past), 1)
        tri_scr[r * TK:(r + 1) * TK, :] = jnp.where(upper, 1.0, 0.0).astype(BF16)
    rank = jnp.dot(taken.astype(BF16), tri_scr[...], preferred_element_type=F32)
    rank_scr[...] = rank * taken
    n_taken = rank[:, past - 1:past]
    slot = (lax.broadcasted_iota(I32, (TOPK, 1), 0) + 1).astype(F32)
    seq_lane = lax.broadcasted_iota(I32, (1, n_seq), 1)
    ids_scr[...] = jnp.zeros(ids_scr.shape, F32)

    def seq_body(d, carry):
        hit = rank_scr[pl.ds(d, 1), :] == slot
        row = jnp.sum(jnp.where(hit, rows_ref[pl.ds(d, 1), :], 0.0), axis=-1, keepdims=True)
        ids_scr[...] += jnp.where(seq_lane == d, row, 0.0)
        return carry

    lax.fori_loop(0, n_seq, seq_body, 0)
    ids_ref[...] = ids_scr[...].T.astype(I32)
    blane = lax.broadcasted_iota(I32, bias_ref.shape, 1)
    new_taken = jnp.sum(jnp.where(sel & (lane == past), 1.0, 0.0), axis=-1, keepdims=True)
    live = (blane.astype(F32) < n_taken) | ((blane == TOPK) & (new_taken > 0.0))
    bias_ref[...] = jnp.where(live, 0.0, NEG)


def _select_sample(scores, cache_rows, n_keys):
    n_seq = scores.shape[0]
    past = n_keys - 1
    return pl.pallas_call(
        functools.partial(_select_sample_kernel, n_keys=n_keys),
        out_shape=[jax.ShapeDtypeStruct((n_seq, TOPK), I32), jax.ShapeDtypeStruct((n_seq, TOPK + LANES), F32)],
        scratch_shapes=[pltpu.VMEM(scores.shape, I32), pltpu.VMEM((past, past), BF16),
                        pltpu.VMEM((n_seq, past), F32), pltpu.VMEM((TOPK, n_seq), F32)],
        compiler_params=_cparams(),
        name="select_sample",
    )(scores, cache_rows)


def _gather_kv_rows(pool_k, pool_v, ids):
    n = ids.shape[0]
    mesh = plsc.VectorSubcoreMesh(core_axis_name="core", subcore_axis_name="subcore")
    n_workers = mesh.num_cores * mesh.num_subcores
    per_worker = n // n_workers
    assert n % (n_workers * SC_GATHER_ROWS) == 0
    row = pool_k.shape[1:]
    out = jax.ShapeDtypeStruct((n,) + row, pool_k.dtype)

    @functools.partial(
        pl.kernel, mesh=mesh, out_type=[out, out],
        scratch_types=[pltpu.VMEM((SC_GATHER_ROWS,), I32), pltpu.VMEM((SC_GATHER_ROWS,) + row, pool_k.dtype),
                       pltpu.VMEM((SC_GATHER_ROWS,) + row, pool_v.dtype), pltpu.SemaphoreType.DMA,
                       pltpu.SemaphoreType.DMA],
        compiler_params=pltpu.CompilerParams(use_tc_tiling_on_sc=True),
        name="gather_kv_rows",
    )
    def gather(k_hbm, v_hbm, ids_hbm, ko_hbm, vo_hbm, ids_v, k_v, v_v, ksem, vsem):
        worker = lax.axis_index("subcore") * mesh.num_cores + lax.axis_index("core")

        @pl.loop(0, per_worker // SC_GATHER_ROWS)
        def _(j):
            off = pl.multiple_of(worker * per_worker + j * SC_GATHER_ROWS, SC_GATHER_ROWS)
            pltpu.sync_copy(ids_hbm.at[pl.ds(off, SC_GATHER_ROWS)], ids_v)
            kcopy = pltpu.async_copy(k_hbm.at[ids_v], k_v, ksem)
            vcopy = pltpu.async_copy(v_hbm.at[ids_v], v_v, vsem)
            kcopy.wait()
            pltpu.sync_copy(k_v, ko_hbm.at[pl.ds(off, SC_GATHER_ROWS)])
            vcopy.wait()
            pltpu.sync_copy(v_v, vo_hbm.at[pl.ds(off, SC_GATHER_ROWS)])

    return gather(pool_k, pool_v, ids)


def _attn_sample_kernel(q_ref, kn_ref, vn_ref, bias_ref, kg_ref, vg_ref, o_ref, kall, vall):
    past_rows = kg_ref.shape[1]
    zeros = jnp.zeros((LANES - N_HEADS, HEAD_DIM), F32)
    for g in range(ATT_SEQS):
        kall[:past_rows, :] = kg_ref[g].astype(BF16)
        vall[:past_rows, :] = vg_ref[g].astype(BF16)
        kall[past_rows:, :] = jnp.concatenate([kn_ref[g], zeros], axis=0).astype(BF16)
        vall[past_rows:, :] = jnp.concatenate([vn_ref[g], zeros], axis=0).astype(BF16)
        s = lax.dot_general(q_ref[g], kall[...], NT_DIMS, preferred_element_type=F32)
        own = (lax.broadcasted_iota(I32, s.shape, 1) & (N_HEADS - 1)) == lax.broadcasted_iota(I32, s.shape, 0)
        s = jnp.where(own, s + bias_ref[g], NEG)
        m = jnp.max(s, axis=-1, keepdims=True)
        p = jnp.exp2(s - m)
        l = jnp.sum(p, axis=-1, keepdims=True)
        o_ref[g] = jnp.dot(p.astype(BF16), vall[...], preferred_element_type=F32) / l


def _attn_sample(q3, kn3, vn3, bias3, kg, vg):
    db, past_rows, _ = kg.shape
    width = past_rows + LANES
    assert db % ATT_SEQS == 0 and bias3.shape == (db, 1, width)
    per_step = lambda r, w: pl.BlockSpec((ATT_SEQS, r, w), lambda d: (d, 0, 0))
    return pl.pallas_call(
        _attn_sample_kernel,
        grid=(db // ATT_SEQS,),
        in_specs=[per_step(N_HEADS, HEAD_DIM)] * 3 + [per_step(1, width)] + [per_step(past_rows, HEAD_DIM)] * 2,
        out_specs=per_step(N_HEADS, HEAD_DIM),
        out_shape=jax.ShapeDtypeStruct((db, N_HEADS, HEAD_DIM), F32),
        scratch_shapes=[pltpu.VMEM((width, HEAD_DIM), BF16), pltpu.VMEM((width, HEAD_DIM), BF16)],
        compiler_params=_cparams("arbitrary"),
        name="attn_sample",
    )(q3, kn3, vn3, bias3, kg, vg)


def _ffn_sample_body(x, g_ref, wup_ref, wc_ref, bc_ref, wdn_ref, st0_ref, st1_ref, gate_ref, a_scr, h_scr):
    a_scr[...] = _rms(x, g_ref[...]).astype(BF16)

    def gate_taps(c, gate):
        cs = slice(c * FF_CHUNK, (c + 1) * FF_CHUNK)
        gate_ref[:, cs] = gate
        return st0_ref[:, cs], st1_ref[:, cs]

    _glu_hidden(a_scr, wup_ref, wc_ref, bc_ref, h_scr, gate_taps)
    return x + jnp.dot(h_scr[...], wdn_ref[...], preferred_element_type=F32)


def _layer0_sample_kernel(x_ref, o_ref, wo_ref, g_ref, wup_ref, wc_ref, bc_ref, wdn_ref, st0_ref, st1_ref,
                          out_ref, gate_ref, a_scr, h_scr):
    x = x_ref[...] + jnp.dot(o_ref[...].astype(BF16), wo_ref[...], preferred_element_type=F32)
    out_ref[...] = _ffn_sample_body(x, g_ref, wup_ref, wc_ref, bc_ref, wdn_ref, st0_ref, st1_ref, gate_ref,
                                    a_scr, h_scr)


def _layer1_sample_kernel(x_ref, gm_ref, win_ref, wcr_ref, bcr_ref, wg_ref, bga_ref, bgx_ref, lam_ref, wor_ref,
                          cs0_ref, cs1_ref, cs2_ref, h0_ref, g_ref, wup_ref, wc_ref, bc_ref, wdn_ref, st0_ref,
                          st1_ref, gf_ref, out_ref, hnew_ref, xb_ref, gate_ref, a_scr, h_scr, u_scr):
    x = x_ref[...]
    a_scr[...] = _rms(x, gm_ref[...]).astype(BF16)
    nblk = MXU_N // LANES
    ncs = -RG_C * jax.nn.softplus(-lam_ref[...])
    for c in range(D_MODEL // MXU_N):
        cs = slice(c * MXU_N, (c + 1) * MXU_N)
        xb = jnp.dot(a_scr[...], win_ref[:, D_MODEL + c * MXU_N:D_MODEL + (c + 1) * MXU_N],
                     preferred_element_type=F32)
        gate = jnp.dot(a_scr[...], win_ref[:, cs], preferred_element_type=F32)
        xb_ref[:, cs] = xb
        xc = (bcr_ref[:, cs] + wcr_ref[0:1, cs] * cs0_ref[:, cs] + wcr_ref[1:2, cs] * cs1_ref[:, cs]
              + wcr_ref[2:3, cs] * cs2_ref[:, cs] + wcr_ref[3:4, cs] * xb)
        for k in range(nblk):
            n = c * nblk + k
            ls = slice(n * LANES, (n + 1) * LANES)
            xcn = xc[:, k * LANES:(k + 1) * LANES]
            gg = jnp.dot(xcn.astype(BF16), wg_ref[n], preferred_element_type=F32)
            a_t, b_t = _lru_coeffs(xcn, gg, bga_ref[:, ls], bgx_ref[:, ls], ncs[:, ls])
            h = a_t * h0_ref[:, ls] + b_t
            hnew_ref[:, ls] = h
            u_scr[:, ls] = (jax.nn.gelu(gate[:, k * LANES:(k + 1) * LANES]) * h).astype(BF16)
    x = x + jnp.dot(u_scr[...], wor_ref[...], preferred_element_type=F32)
    y = _ffn_sample_body(x, g_ref, wup_ref, wc_ref, bc_ref, wdn_ref, st0_ref, st1_ref, gate_ref, a_scr, h_scr)
    out_ref[...] = _rms(y, gf_ref[...])


def _layer_spec(stacked, layer):
    nd = stacked.ndim - 1
    return pl.BlockSpec((None,) + stacked.shape[1:], lambda *_: (layer,) + (0,) * nd, pipeline_mode=pl.Buffered(1))


def _sample_call(kernel, name, args, out_widths, extra_scratch=()):
    db = args[0].shape[0]
    specs = [_layer_spec(*a) if isinstance(a, tuple) else _const_spec(a.shape) for a in args]
    arrays = [a[0] if isinstance(a, tuple) else a for a in args]
    return pl.pallas_call(
        kernel,
        grid=(1,),
        in_specs=specs,
        out_specs=[pl.BlockSpec((db, w), lambda i: (0, 0)) for w in out_widths],
        out_shape=[jax.ShapeDtypeStruct((db, w), F32) for w in out_widths],
        scratch_shapes=[pltpu.VMEM((db, D_MODEL), BF16), pltpu.VMEM((db, D_FF), BF16), *extra_scratch],
        compiler_params=_cparams("arbitrary"),
        name=name,
    )(*arrays)


def _rope_tables(pos):
    posf = pos.astype(F32)[:, None]

    def cs(d):
        half = d // 2
        inv = ROPE_THETA ** (-jnp.arange(half, dtype=F32) * 2.0 / d)
        ang = posf * inv[None, :]
        return jnp.cos(ang), jnp.sin(ang)

    c, s = cs(HEAD_DIM)
    cos = jnp.concatenate([c, c], axis=-1)
    sin = jnp.concatenate([-s, s], axis=-1)
    c, s = cs(IDX_DIM)
    one, zero = jnp.ones_like(c), jnp.zeros_like(c)
    tile = lambda parts: jnp.concatenate(parts, axis=-1)
    icos = tile([c, c, one, one])
    isa = tile([-s, zero, zero, zero])
    isb = tile([zero, s, zero, zero])
    return cos, sin, icos, isa, isb


def _split_attn_in(w):
    qkv = N_HEADS * HEAD_DIM
    wqk, wv, wi = w[:, :2 * qkv], w[:, 2 * qkv:3 * qkv], w[:, 3 * qkv:]
    pad = lambda a: jnp.pad(a, ((0, 0), (0, LANES - a.shape[1])))
    groups = [pad(wi[:, h * IDX_DIM:(h + 1) * IDX_DIM]) for h in range(IDX_HEADS)]
    groups.append(pad(wi[:, IDX_HEADS * IDX_DIM:]))
    return wqk.astype(BF16), wv.astype(BF16), jnp.concatenate(groups, axis=-1).astype(BF16)


def kernel(x_prompt, x_sample, cache_k, cache_v, cache_kidx, state_lru_h, state_lru_conv, state_ffn_conv,
           page_table, norm_mix, norm_ffn, norm_final, w_attn_in, w_attn_out, w_rec_in, w_rec_conv, b_rec_conv,
           w_gate_a, b_gate_a, w_gate_x, b_gate_x, lru_lambda, w_rec_out, w_ffn_up, w_ffn_conv, b_ffn_conv,
           w_ffn_down):
    bsz, seq, d = x_prompt.shape
    db = x_sample.shape[0]
    n_pages = page_table.shape[1]
    past = n_pages * PAGE
    assert d == D_MODEL and x_sample.shape[1] == 1 and seq % ROW_BLOCK == 0 and seq % TQ == 0
    assert min(TOPK, seq // 4) == TOPK and min(TOPK, (past + 1) // 4) == TOPK

    vec = lambda a: a.reshape(1, -1)
    wqk, wv, widx = _split_attn_in(w_attn_in[0])
    wo_attn = w_attn_out[0].astype(BF16)
    w_rin = w_rec_in[0].astype(BF16)
    w_gates = jnp.concatenate([w_gate_a[0], w_gate_x[0]], axis=-1).astype(BF16)
    wo_rec = w_rec_out[0].astype(BF16)
    wup_all, wdn_all = _to_bf16(w_ffn_up), _to_bf16(w_ffn_down)
    wup = [(wup_all, i) for i in range(2)]
    wdn = [(wdn_all, i) for i in range(2)]
    rec_vecs = (w_rec_conv[0], vec(b_rec_conv[0]), w_gates, vec(b_gate_a[0]), vec(b_gate_x[0]),
                vec(lru_lambda[0]), wo_rec)
    ffn_vecs = [(vec(norm_ffn[i]), wup[i], w_ffn_conv[i], vec(b_ffn_conv[i]), wdn[i]) for i in range(2)]
    gfin = vec(norm_final)

    xp = x_prompt.reshape(bsz * seq, d)
    q, kf, kb, vf, vt, qi, kiw = _attn_in(xp, vec(norm_mix[0]), wqk, wv, widx, _rope_tables(jnp.arange(seq)),
                                          ROW_BLOCK, seq // ROW_BLOCK)
    wit = kiw[:, IDX_DIM:IDX_DIM + IDX_HEADS].T
    hp_attn = _dsa_prompt(q, kb, vt, qi, kiw, wit, xp, wo_attn, bsz, seq)
    hp, ftail0 = _ffn_prompt(hp_attn, *ffn_vecs[0], gfin, bsz, seq, False)
    hp, hlast, ctail = _rec_prompt(hp, vec(norm_mix[1]), w_rin, *rec_vecs, bsz, seq)
    yp, ftail1 = _ffn_prompt(hp, *ffn_vecs[1], gfin, bsz, seq, True)

    xs = x_sample.reshape(db, d)
    tabs = tuple(jnp.broadcast_to(t, (db, t.shape[1])) for t in _rope_tables(jnp.full((1,), past)))
    qs, kfs, _, vfs, _, qis, kiws = _attn_in(xs, vec(norm_mix[0]), wqk, wv, widx, tabs, db, 1)
    pt = page_table.reshape(-1)
    scores = _idx_sample(pt, qis.reshape(db, IDX_HEADS, LANES),
                         kiws[:, IDX_DIM:IDX_DIM + IDX_HEADS].reshape(db, IDX_HEADS, 1),
                         kiws.reshape(db, 1, LANES), jnp.swapaxes(cache_kidx[0], 1, 2), n_pages)
    cache_rows = (page_table[:, :, None] * PAGE + jnp.arange(PAGE, dtype=I32)).reshape(db, past).astype(F32)
    ids, bias = _select_sample(scores.reshape(db, -1), cache_rows, past + 1)
    heads = (N_HEADS, HEAD_DIM)
    key_rows = lambda pool: pool[0].reshape(-1, *heads)
    ids, _ = lax.optimization_barrier((ids, kb))
    kg, vg = _gather_kv_rows(key_rows(cache_k), key_rows(cache_v), ids.reshape(-1))
    bias_rows = jnp.pad(jnp.repeat(bias[:, :TOPK + 1], N_HEADS, axis=1), ((0, 0), (0, LANES - N_HEADS)),
                        constant_values=NEG)
    bias_rows, _ = lax.optimization_barrier((bias_rows, hp_attn))
    seq_rows = lambda g: g.reshape(db, TOPK * N_HEADS, HEAD_DIM)
    o_s = _attn_sample(qs.reshape(db, *heads), kfs.reshape(db, *heads), vfs.reshape(db, *heads),
                       bias_rows.reshape(db, 1, -1), seq_rows(kg), seq_rows(vg))
    st = state_ffn_conv
    hs, gate0 = _sample_call(_layer0_sample_kernel, "layer0_sample",
                             (xs, o_s.reshape(db, d), wo_attn, *ffn_vecs[0], st[0, :, 0], st[0, :, 1]),
                             (D_MODEL, D_FF))
    cst = state_lru_conv[0]
    ys, hnew, xbs, gate1 = _sample_call(
        _layer1_sample_kernel, "layer1_sample",
        (hs, vec(norm_mix[1]), w_rin, *rec_vecs, cst[:, 0], cst[:, 1], cst[:, 2], state_lru_h[0], *ffn_vecs[1],
         st[1, :, 0], st[1, :, 1], gfin),
        (D_MODEL, D_MODEL, D_MODEL, D_FF), extra_scratch=(pltpu.VMEM((db, D_MODEL), BF16),))

    return (
        yp.reshape(bsz, seq, d),
        ys.reshape(db, 1, d),
        kf.reshape(1, bsz, seq, *heads),
        vf.reshape(1, bsz, seq, *heads),
        kiw[:, :IDX_DIM].reshape(1, bsz, seq, IDX_DIM),
        kfs.reshape(1, db, 1, *heads),
        vfs.reshape(1, db, 1, *heads),
        kiws[:, :IDX_DIM].reshape(1, db, 1, IDX_DIM),
        hlast[None, :, 0, :],
        ctail[None, :, SUBLANES - 3:, :],
        hnew[None],
        jnp.stack([cst[:, 1], cst[:, 2], xbs], axis=1)[None],
        jnp.stack([ftail0[:, SUBLANES - 2:], ftail1[:, SUBLANES - 2:]]),
        jnp.stack([jnp.stack([st[0, :, 1], gate0], axis=1), jnp.stack([st[1, :, 1], gate1], axis=1)]),
    )
```

```python
import functools

import jax
import jax.numpy as jnp
from jax import lax
from jax.experimental import pallas as pl
from jax.experimental.pallas import tpu as pltpu
from jax.experimental.pallas import tpu_sc as plsc

F32 = jnp.float32
BF16 = jnp.bfloat16
I32 = jnp.int32
I16 = jnp.int16

D_MODEL = 1024
N_HEADS = 8
HEAD_DIM = 128
IDX_HEADS = 4
IDX_DIM = 64
TOPK = 256
PAGE = 128
ROPE_THETA = 10000.0
IDX_SCALE = (IDX_DIM * IDX_HEADS) ** -0.5
Q_SCALE = HEAD_DIM ** -0.5 * 1.4426950408889634
RG_C = 8.0
D_FF = 2816
RMS_EPS = 1e-6

LANES = 128
SUBLANES = 8
MXU_N = 256
IDX_W = (IDX_HEADS + 1) * LANES
INT_MIN = -(2 ** 31)
NEG = -1e30
VMEM_LIMIT = 52 * 1024 * 1024

ROW_BLOCK = 512
FFN_ROWS = 1024
TQ = 256
TK = 256
COUNT_ROWS = 32
COUNT_ROWS_16 = 64
IDX_SEQS = 8
ATT_SEQS = 4
SC_GATHER_ROWS = 32
FF_CHUNK = 256
N_FF_CHUNKS = D_FF // FF_CHUNK
NT_DIMS = (((1,), (1,)), ((), ()))


def _cparams(*sem):
    return pltpu.CompilerParams(dimension_semantics=sem if sem else None, vmem_limit_bytes=VMEM_LIMIT)


def _const_spec(shape):
    nd = len(shape)
    return pl.BlockSpec(shape, lambda *_: (0,) * nd, pipeline_mode=pl.Buffered(1))


def _rms(x, g):
    return x * lax.rsqrt(jnp.mean(x * x, axis=-1, keepdims=True) + RMS_EPS) * g


def _shift_rows(x, s, prev_rows):
    r = pltpu.roll(x, s, 0)
    top = r[:SUBLANES]
    row = lax.broadcasted_iota(I32, top.shape, 0)
    for k in range(s):
        top = jnp.where(row == k, prev_rows[k], top)
    return jnp.concatenate([top, r[SUBLANES:]], axis=0)


def _sortable_key(score):
    bits = pltpu.bitcast(score + 0.0, I32)
    return jnp.where(bits < 0, bits ^ jnp.int32(0x7FFFFFFF), bits)


def _attn_in_kernel(x_ref, g_ref, wqk_ref, wv_ref, widx_ref, cos_ref, sin_ref, icos_ref, isa_ref, isb_ref,
                    q_ref, kf_ref, kb_ref, vf_ref, vt_ref, qi_ref, kiw_ref, a_scr):
    a_scr[...] = _rms(x_ref[...], g_ref[...]).astype(BF16)
    cos = cos_ref[...]
    sin = sin_ref[...]
    for c in range(2 * D_MODEL // MXU_N):
        r = jnp.dot(a_scr[...], wqk_ref[:, c * MXU_N:(c + 1) * MXU_N], preferred_element_type=F32)
        for hh in range(MXU_N // HEAD_DIM):
            xh = r[:, hh * HEAD_DIM:(hh + 1) * HEAD_DIM]
            y = xh * cos + pltpu.roll(xh, HEAD_DIM // 2, 1) * sin
            col = (c * MXU_N) % D_MODEL + hh * HEAD_DIM
            if c < D_MODEL // MXU_N:
                q_ref[:, col:col + HEAD_DIM] = (y * Q_SCALE).astype(BF16)
            else:
                kf_ref[:, col:col + HEAD_DIM] = y
                kb_ref[:, col:col + HEAD_DIM] = y.astype(BF16)
    for c in range(D_MODEL // MXU_N):
        r = jnp.dot(a_scr[...], wv_ref[:, c * MXU_N:(c + 1) * MXU_N], preferred_element_type=F32)
        vf_ref[:, c * MXU_N:(c + 1) * MXU_N] = r
        vt_ref[c * MXU_N:(c + 1) * MXU_N, :] = r.T.astype(BF16)
    ri = jnp.dot(a_scr[...], widx_ref[...], preferred_element_type=F32)
    groups = lambda t: jnp.concatenate([t] * (IDX_W // LANES), axis=1)
    yi = (ri * groups(icos_ref[...]) + pltpu.roll(ri, IDX_W - IDX_DIM // 2, 1) * groups(isa_ref[...])
          + pltpu.roll(ri, IDX_DIM // 2, 1) * groups(isb_ref[...]))
    qi_ref[...] = yi[:, :IDX_HEADS * LANES].astype(BF16)
    kiw_ref[...] = yi[:, IDX_HEADS * LANES:]


def _attn_in(x, g, wqk, wv, widx, tabs, rows, n_pos_blocks):
    n = x.shape[0]
    cos, sin, icos, isa, isb = tabs
    row_spec = lambda w: pl.BlockSpec((rows, w), lambda i: (i, 0))
    tab_spec = lambda w: pl.BlockSpec((rows, w), lambda i: (i % n_pos_blocks, 0))
    vt_spec = pl.BlockSpec((None, D_MODEL, rows), lambda i: (i // n_pos_blocks, 0, i % n_pos_blocks))
    return pl.pallas_call(
        _attn_in_kernel,
        grid=(n // rows,),
        in_specs=[row_spec(D_MODEL), _const_spec((1, D_MODEL)), _const_spec(wqk.shape), _const_spec(wv.shape),
                  _const_spec(widx.shape), tab_spec(HEAD_DIM), tab_spec(HEAD_DIM), tab_spec(LANES),
                  tab_spec(LANES), tab_spec(LANES)],
        out_specs=[row_spec(D_MODEL)] * 4 + [vt_spec, row_spec(IDX_HEADS * LANES), row_spec(LANES)],
        out_shape=[jax.ShapeDtypeStruct((n, D_MODEL), BF16), jax.ShapeDtypeStruct((n, D_MODEL), F32),
                   jax.ShapeDtypeStruct((n, D_MODEL), BF16), jax.ShapeDtypeStruct((n, D_MODEL), F32),
                   jax.ShapeDtypeStruct((n // (rows * n_pos_blocks), D_MODEL, rows * n_pos_blocks), BF16),
                   jax.ShapeDtypeStruct((n, IDX_HEADS * LANES), BF16),
                   jax.ShapeDtypeStruct((n, LANES), F32)],
        scratch_shapes=[pltpu.VMEM((rows, D_MODEL), BF16)],
        compiler_params=_cparams("arbitrary"),
        name="attn_in",
    )(x, g, wqk, wv, widx, cos, sin, icos, isa, isb)


def _kth_largest_key(count_ge, n_total):
    kf = float(TOPK)
    c0 = count_ge(0)
    t = jnp.where(c0 >= kf, jnp.int32(0), jnp.int32(INT_MIN))
    ct = jnp.where(c0 >= kf, c0, n_total)

    def bit_body(it, carry):
        t, ct = carry
        cand = t + (jnp.int32(1) << (30 - it))
        cnt = count_ge(cand)
        return jnp.where(cnt >= kf, cand, t), jnp.where(cnt >= kf, cnt, ct)

    return lax.fori_loop(0, 31, bit_body, (t, ct))


def _kth_largest_half(count_ge, above, count_all):
    kf = float(TOPK)
    c0 = above + count_ge(0)
    t = jnp.where(c0 >= kf, jnp.int32(0), jnp.int32(-(2 ** 15)))
    ct = jnp.where(c0 >= kf, c0, count_all)

    def bit_body(it, carry):
        t, ct = carry
        cand = t + (jnp.int32(1) << (14 - it))
        cnt = above + count_ge(cand)
        return jnp.where(cnt >= kf, cand, t), jnp.where(cnt >= kf, cnt, ct)

    return lax.fori_loop(0, 15, bit_body, (t, ct))


def _tie_cutoff(count_tie_le, need, n_bits):
    def bit_body(it, c):
        cand = c + (jnp.int32(1) << (n_bits - 1 - it))
        return jnp.where(count_tie_le(cand - 1) < need, cand, c)

    return lax.fori_loop(0, n_bits, bit_body, jnp.zeros_like(need, dtype=I32))


def _dsa_prompt_kernel(q_ref, kb_ref, vt_ref, qi_ref, qin_ref, kiwk_ref, wit_ref, witn_ref, x_ref, wo_ref, out_ref,
                       key_scr, bias_scr, kib_scr, o_scr, acc_scr, s_scr, hi_scr, lo_scr):
    i = pl.program_id(1)
    nk = i + 1
    lane_q = lax.broadcasted_iota(I32, (1, TQ), 1)

    def rows(kc):
        return pl.ds(kc * TK if isinstance(kc, int) else pl.multiple_of(kc * TK, TK), TK)

    def spos(kc):
        return kc * TK + lax.broadcasted_iota(I32, (TK, 1), 0)

    def colsum(a):
        return jnp.sum(a.reshape(TK // COUNT_ROWS, COUNT_ROWS, TQ), axis=0)

    def score_chunk(kc, qidx_ref, wis, qpos):
        kic = kib_scr[rows(kc), :]
        sc = jnp.zeros((TK, TQ), F32)
        for h in range(IDX_HEADS):
            s = lax.dot_general(kic, qidx_ref[:, h * LANES:(h + 1) * LANES], NT_DIMS, preferred_element_type=F32)
            sc = sc + wis[h:h + 1, :] * jnp.maximum(s, 0.0)
        key = _sortable_key(sc)
        if qpos is not None:
            key = jnp.where(spos(kc) <= qpos, key, jnp.int32(INT_MIN))
        key_scr[rows(kc), :] = key
        hi_scr[rows(kc), :] = (key >> 16).astype(I16)
        lo_scr[rows(kc), :] = ((key & 0xFFFF) - 2 ** 15).astype(I16)

    @pl.when(i == 0)
    def _():
        kib_scr[...] = kiwk_ref[...].astype(BF16)
        score_chunk(0, qi_ref, wit_ref[...] * IDX_SCALE, lane_q)

    wis_next = witn_ref[...] * IDX_SCALE

    def count_ge(cand):
        def body(kc, acc):
            return acc + colsum(jnp.where(key_scr[rows(kc), :] >= cand, 1.0, 0.0))
        acc = lax.fori_loop(0, nk, body, jnp.zeros((COUNT_ROWS, TQ), F32))
        return jnp.sum(acc, axis=0, keepdims=True)

    def count_ge_half(half_scr):
        def count(cand):
            c16 = jnp.asarray(cand, I32).astype(I16)

            def body(kc, acc):
                ind = jnp.where(half_scr[rows(kc), :] >= c16, jnp.int16(1), jnp.int16(0))
                for r in range(TK // COUNT_ROWS_16):
                    acc = acc + ind[r * COUNT_ROWS_16:(r + 1) * COUNT_ROWS_16]
                return acc

            acc = lax.fori_loop(0, nk, body, jnp.zeros((COUNT_ROWS_16, TQ), I16))
            return jnp.sum(acc.astype(F32), axis=0, keepdims=True)
        return count

    count_hi, count_lo = count_ge_half(hi_scr), count_ge_half(lo_scr)
    thi, cnt_hi = _kth_largest_half(count_hi, 0.0, (nk * TK).astype(F32))
    above = jnp.where(thi == 2 ** 15 - 1, 0.0, count_hi(jnp.minimum(thi + 1, 2 ** 15 - 1)))
    thi16 = thi.astype(I16)

    def lo_body(kc, carry):
        lo_scr[rows(kc), :] = jnp.where(hi_scr[rows(kc), :] == thi16, lo_scr[rows(kc), :], jnp.int16(-(2 ** 15)))
        return carry

    lax.fori_loop(0, nk, lo_body, 0)
    tlo, cnt_thr = _kth_largest_half(count_lo, above, cnt_hi)
    thr = (thi << 16) | (tlo + 2 ** 15)
    short = thr == INT_MIN
    has_ties = jnp.max(jnp.where(short, 0.0, cnt_thr)) > float(TOPK)

    @pl.when(jnp.logical_not(has_ties))
    def _():
        thr_vis = jnp.maximum(thr, jnp.int32(INT_MIN + 1))

        def bias_body(kc, carry):
            bias_scr[rows(kc), :] = jnp.where(key_scr[rows(kc), :] >= thr_vis, 0.0, NEG)
            return carry

        lax.fori_loop(0, nk, bias_body, 0)

    @pl.when(has_ties)
    def _():
        need = float(TOPK) - count_ge(thr + 1)

        def eq_body(kc, carry):
            bias_scr[rows(kc), :] = jnp.where(key_scr[rows(kc), :] == thr, 1.0, 0.0)
            return carry

        lax.fori_loop(0, nk, eq_body, 0)

        def count_tie_le(cm):
            def body(kc, acc):
                return acc + colsum(jnp.where(spos(kc) <= cm, bias_scr[rows(kc), :], 0.0))
            acc = lax.fori_loop(0, nk, body, jnp.zeros((COUNT_ROWS, TQ), F32))
            return jnp.sum(acc, axis=0, keepdims=True)

        cut = jnp.where(short, jnp.int32(-1), _tie_cutoff(count_tie_le, need, 11))

        def bias_body(kc, carry):
            sel = (key_scr[rows(kc), :] > thr) | ((bias_scr[rows(kc), :] > 0.0) & (spos(kc) <= cut))
            bias_scr[rows(kc), :] = jnp.where(sel, 0.0, NEG)
            return carry

        lax.fori_loop(0, nk, bias_body, 0)

    acc_scr[...] = jnp.zeros(acc_scr.shape, F32)

    def att_body(kc, carry):
        ms, ls = carry
        for h in range(N_HEADS):
            hs = slice(h * HEAD_DIM, (h + 1) * HEAD_DIM)
            s_scr[h] = lax.dot_general(kb_ref[rows(kc), hs], q_ref[:, hs], NT_DIMS, preferred_element_type=F32)
        bias = bias_scr[rows(kc), :]
        score_chunk(kc, qin_ref, wis_next, None)
        new_ms, new_ls = [], []
        for h in range(N_HEADS):
            hs = slice(h * HEAD_DIM, (h + 1) * HEAD_DIM)
            s = s_scr[h] + bias
            m_new = jnp.maximum(ms[h], jnp.max(s, axis=0, keepdims=True))
            alpha = jnp.exp2(ms[h] - m_new)
            p = jnp.exp2(s - m_new)
            new_ms.append(m_new)
            new_ls.append(alpha * ls[h] + jnp.sum(p, axis=0, keepdims=True))
            pv = jnp.dot(vt_ref[hs, rows(kc)], p.astype(BF16), preferred_element_type=F32)
            acc_scr[hs, :] = alpha * acc_scr[hs, :] + pv
        return tuple(new_ms), tuple(new_ls)

    init = ((jnp.full((1, TQ), NEG, F32),) * N_HEADS, (jnp.zeros((1, TQ), F32),) * N_HEADS)
    _, ls = lax.fori_loop(0, nk, att_body, init)

    @pl.when(i < pl.num_programs(1) - 1)
    def _():
        score_chunk(nk, qin_ref, wis_next, (i + 1) * TQ + lane_q)

    for h in range(N_HEADS):
        hs = slice(h * HEAD_DIM, (h + 1) * HEAD_DIM)
        o_scr[:, hs] = (acc_scr[hs, :] / ls[h]).T.astype(BF16)
    out_ref[...] = x_ref[...] + jnp.dot(o_scr[...], wo_ref[...], preferred_element_type=F32)


def _dsa_prompt(q, kb, vt, qi, kiw, wit, x, wo, bsz, seq):
    n = x.shape[0]
    nq = seq // TQ
    qrow = lambda w: pl.BlockSpec((TQ, w), lambda b, i: (b * nq + i, 0))
    brow = lambda w: pl.BlockSpec((seq, w), lambda b, i: (b, 0))
    nxt = lambda b, i: b * nq + jnp.minimum(i + 1, nq - 1)
    return pl.pallas_call(
        _dsa_prompt_kernel,
        grid=(bsz, nq),
        in_specs=[qrow(D_MODEL), brow(D_MODEL), pl.BlockSpec((None, D_MODEL, seq), lambda b, i: (b, 0, 0)),
                  qrow(IDX_HEADS * LANES), pl.BlockSpec((TQ, IDX_HEADS * LANES), lambda b, i: (nxt(b, i), 0)),
                  brow(LANES), pl.BlockSpec((IDX_HEADS, TQ), lambda b, i: (0, b * nq + i)),
                  pl.BlockSpec((IDX_HEADS, TQ), lambda b, i: (0, nxt(b, i))),
                  qrow(D_MODEL), _const_spec(wo.shape)],
        out_specs=qrow(D_MODEL),
        out_shape=jax.ShapeDtypeStruct((n, D_MODEL), F32),
        scratch_shapes=[pltpu.VMEM((seq, TQ), I32), pltpu.VMEM((seq, TQ), F32), pltpu.VMEM((seq, LANES), BF16),
                        pltpu.VMEM((TQ, D_MODEL), BF16), pltpu.VMEM((D_MODEL, TQ), F32),
                        pltpu.VMEM((N_HEADS, TK, TQ), F32), pltpu.VMEM((seq, TQ), I16),
                        pltpu.VMEM((seq, TQ), I16)],
        compiler_params=_cparams("arbitrary", "arbitrary"),
        name="dsa_prompt",
    )(q, kb, vt, qi, qi, kiw, wit, wit, x, wo)


def _glu_hidden(a_scr, wup_ref, wc_ref, bc_ref, h_scr, gate_taps):
    for c in range(N_FF_CHUNKS):
        cs = slice(c * FF_CHUNK, (c + 1) * FF_CHUNK)
        gate = jnp.dot(a_scr[...], wup_ref[:, cs], preferred_element_type=F32)
        val = jnp.dot(a_scr[...], wup_ref[:, D_FF + c * FF_CHUNK:D_FF + (c + 1) * FF_CHUNK],
                      preferred_element_type=F32)
        g2, g1 = gate_taps(c, gate)
        gc = bc_ref[:, cs] + wc_ref[0:1, cs] * g2 + wc_ref[1:2, cs] * g1 + wc_ref[2:3, cs] * gate
        h_scr[:, cs] = (jax.nn.gelu(gc) * val).astype(BF16)


def _ffn_prompt_kernel(x_ref, g_ref, wup_ref, wc_ref, bc_ref, wdn_ref, gf_ref, out_ref, tail_ref,
                       a_scr, h_scr, carry_scr, *, final_norm):
    j = pl.program_id(1)
    rows = x_ref.shape[0]
    a_scr[...] = _rms(x_ref[...], g_ref[...]).astype(BF16)

    @pl.when(j == 0)
    def _():
        carry_scr[...] = jnp.zeros(carry_scr.shape, F32)

    def gate_taps(c, gate):
        cs = slice(c * FF_CHUNK, (c + 1) * FF_CHUNK)
        p0 = carry_scr[SUBLANES - 2:SUBLANES - 1, cs]
        p1 = carry_scr[SUBLANES - 1:SUBLANES, cs]
        g1 = _shift_rows(gate, 1, [p1])
        g2 = _shift_rows(gate, 2, [p0, p1])
        carry_scr[:, cs] = gate[rows - SUBLANES:, :]
        tail_ref[0, :, cs] = gate[rows - SUBLANES:, :]
        return g2, g1

    _glu_hidden(a_scr, wup_ref, wc_ref, bc_ref, h_scr, gate_taps)
    y = x_ref[...] + jnp.dot(h_scr[...], wdn_ref[...], preferred_element_type=F32)
    out_ref[...] = _rms(y, gf_ref[...]) if final_norm else y


def _ffn_prompt(x, g, wup, wc, bc, wdn, gf, bsz, seq, final_norm):
    n = x.shape[0]
    assert seq % FFN_ROWS == 0
    nb = seq // FFN_ROWS
    row = pl.BlockSpec((FFN_ROWS, D_MODEL), lambda b, j: (b * nb + j, 0))
    (wup, lu), (wdn, ld) = wup, wdn
    return pl.pallas_call(
        functools.partial(_ffn_prompt_kernel, final_norm=final_norm),
        grid=(bsz, nb),
        in_specs=[row, _const_spec((1, D_MODEL)), _layer_spec(wup, lu), _const_spec(wc.shape),
                  _const_spec(bc.shape), _layer_spec(wdn, ld), _const_spec((1, D_MODEL))],
        out_specs=[row, pl.BlockSpec((1, SUBLANES, D_FF), lambda b, j: (b, 0, 0))],
        out_shape=[jax.ShapeDtypeStruct((n, D_MODEL), F32), jax.ShapeDtypeStruct((bsz, SUBLANES, D_FF), F32)],
        scratch_shapes=[pltpu.VMEM((FFN_ROWS, D_MODEL), BF16), pltpu.VMEM((FFN_ROWS, D_FF), BF16),
                        pltpu.VMEM((SUBLANES, D_FF), F32)],
        compiler_params=_cparams("arbitrary", "arbitrary"),
        name="ffn_prompt_final" if final_norm else "ffn_prompt",
    )(x, g, wup, wc, bc, wdn, gf)


def _sigmoid(x):
    return 0.5 * jnp.tanh(0.5 * x) + 0.5


def _lru_coeffs(xc, gg, bga, bgx, neg_c_softplus):
    r = _sigmoid(gg[:, :LANES] + bga)
    ig = _sigmoid(gg[:, LANES:] + bgx)
    log_a = r * neg_c_softplus
    a = jnp.exp(log_a)
    z = -jnp.tanh(log_a) * (a * a + 1.0)
    root = jnp.where(z > 0.0, z * lax.rsqrt(z), 0.0)
    return a, root * ig * xc


def _group_scan(a, b):
    row = lax.broadcasted_iota(I32, a.shape, 0) & (SUBLANES - 1)
    for d in (1, 2, 4):
        inside = row >= d
        a_prev = jnp.where(inside, pltpu.roll(a, d, 0), 1.0)
        b_prev = jnp.where(inside, pltpu.roll(b, d, 0), 0.0)
        b = a * b_prev + b
        a = a * a_prev
    return a, b


def _rec_prompt_kernel(x_ref, g_ref, win_ref, wc_ref, bc_ref, wg_ref, bga_ref, bgx_ref, lam_ref, wo_ref,
                       out_ref, hlast_ref, ctail_ref, a_scr, at_scr, bt_scr, u_scr, hcar_scr, ccar_scr):
    j = pl.program_id(1)
    rows = x_ref.shape[0]
    a_scr[...] = _rms(x_ref[...], g_ref[...]).astype(BF16)

    @pl.when(j == 0)
    def _():
        ccar_scr[...] = jnp.zeros(ccar_scr.shape, F32)
        hcar_scr[...] = jnp.zeros(hcar_scr.shape, F32)

    nblk = MXU_N // LANES
    ncs = -RG_C * jax.nn.softplus(-lam_ref[...])
    for c in range(D_MODEL // MXU_N):
        cs = slice(c * MXU_N, (c + 1) * MXU_N)
        xb = jnp.dot(a_scr[...], win_ref[:, D_MODEL + c * MXU_N:D_MODEL + (c + 1) * MXU_N],
                     preferred_element_type=F32)
        prev = [ccar_scr[SUBLANES - 3 + k:SUBLANES - 2 + k, cs] for k in range(3)]
        xc = (bc_ref[:, cs] + wc_ref[0:1, cs] * _shift_rows(xb, 3, prev) + wc_ref[1:2, cs] * _shift_rows(xb, 2, prev[1:])
              + wc_ref[2:3, cs] * _shift_rows(xb, 1, prev[2:]) + wc_ref[3:4, cs] * xb)
        ccar_scr[:, cs] = xb[rows - SUBLANES:, :]
        ctail_ref[0, :, cs] = xb[rows - SUBLANES:, :]
        for k in range(nblk):
            n = c * nblk + k
            ls = slice(n * LANES, (n + 1) * LANES)
            xcn = xc[:, k * LANES:(k + 1) * LANES]
            gg = jnp.dot(xcn.astype(BF16), wg_ref[n], preferred_element_type=F32)
            a_t, b_t = _group_scan(*_lru_coeffs(xcn, gg, bga_ref[:, ls], bgx_ref[:, ls], ncs[:, ls]))
            at_scr[:, ls] = a_t
            bt_scr[:, ls] = b_t

    def group_step(g, h):
        r = pl.ds(pl.multiple_of(g * SUBLANES, SUBLANES), SUBLANES)
        hg = at_scr[r, :] * h + bt_scr[r, :]
        bt_scr[r, :] = hg
        return hg[SUBLANES - 1:, :]

    h_last = lax.fori_loop(0, rows // SUBLANES, group_step, hcar_scr[0:1, :], unroll=4)
    hcar_scr[...] = jnp.broadcast_to(h_last, hcar_scr.shape)
    hlast_ref[0] = jnp.broadcast_to(h_last, hcar_scr.shape)
    for c in range(D_MODEL // MXU_N):
        cs = slice(c * MXU_N, (c + 1) * MXU_N)
        gate = jnp.dot(a_scr[...], win_ref[:, cs], preferred_element_type=F32)
        u_scr[:, cs] = (jax.nn.gelu(gate) * bt_scr[:, cs]).astype(BF16)
    out_ref[...] = x_ref[...] + jnp.dot(u_scr[...], wo_ref[...], preferred_element_type=F32)


def _rec_prompt(x, g, win, wc, bc, wg, bga, bgx, lam, wo, bsz, seq):
    n = x.shape[0]
    nb = seq // ROW_BLOCK
    row = pl.BlockSpec((ROW_BLOCK, D_MODEL), lambda b, j: (b * nb + j, 0))
    tail = pl.BlockSpec((1, SUBLANES, D_MODEL), lambda b, j: (b, 0, 0))
    vec = _const_spec((1, D_MODEL))
    return pl.pallas_call(
        _rec_prompt_kernel,
        grid=(bsz, nb),
        in_specs=[row, vec, _const_spec(win.shape), _const_spec(wc.shape), vec, _const_spec(wg.shape), vec, vec,
                  vec, _const_spec(wo.shape)],
        out_specs=[row, tail, tail],
        out_shape=[jax.ShapeDtypeStruct((n, D_MODEL), F32), jax.ShapeDtypeStruct((bsz, SUBLANES, D_MODEL), F32),
                   jax.ShapeDtypeStruct((bsz, SUBLANES, D_MODEL), F32)],
        scratch_shapes=[pltpu.VMEM((ROW_BLOCK, D_MODEL), BF16), pltpu.VMEM((ROW_BLOCK, D_MODEL), F32),
                        pltpu.VMEM((ROW_BLOCK, D_MODEL), F32), pltpu.VMEM((ROW_BLOCK, D_MODEL), BF16),
                        pltpu.VMEM((SUBLANES, D_MODEL), F32), pltpu.VMEM((SUBLANES, D_MODEL), F32)],
        compiler_params=_cparams("arbitrary", "arbitrary"),
        name="rec_prompt",
    )(x, g, win, wc, bc, wg, bga, bgx, lam, wo)


def _idx_sample_kernel(pt_ref, qi_ref, wi_ref, kin_ref, *refs, n_pages):
    pages, out_ref = refs[:-1], refs[-1]
    lane = lax.broadcasted_iota(I32, (1, LANES), 1)
    for g in range(IDX_SEQS):
        qi = qi_ref[g]
        wi = wi_ref[g]
        for p in range(n_pages):
            page_t = pages[g * n_pages + p][...].astype(BF16)
            s = jnp.dot(qi[:, :IDX_DIM], page_t, preferred_element_type=F32)
            out_ref[g, :, p * PAGE:(p + 1) * PAGE] = (
                jnp.sum(wi * jnp.maximum(s, 0.0), axis=0, keepdims=True) * IDX_SCALE)
        s_new = jnp.sum(qi.astype(F32) * kin_ref[g].astype(BF16).astype(F32), axis=-1, keepdims=True)
        sc_new = jnp.sum(wi * jnp.maximum(s_new, 0.0), axis=0, keepdims=True) * IDX_SCALE
        out_ref[g, :, n_pages * PAGE:] = jnp.where(lane == 0, sc_new, -jnp.inf)


def _idx_sample(pt, qi3, wi3, kin3, pool_ki_t, n_pages):
    db = qi3.shape[0]
    assert db % IDX_SEQS == 0
    per_step = lambda shape: pl.BlockSpec((IDX_SEQS,) + shape, lambda d, pt: (d, 0, 0))
    page_spec = lambda g, p: pl.BlockSpec((None, IDX_DIM, PAGE),
                                          lambda d, pt: (pt[(d * IDX_SEQS + g) * n_pages + p], 0, 0))
    width = n_pages * PAGE + LANES
    return pl.pallas_call(
        functools.partial(_idx_sample_kernel, n_pages=n_pages),
        grid_spec=pltpu.PrefetchScalarGridSpec(
            num_scalar_prefetch=1, grid=(db // IDX_SEQS,),
            in_specs=[per_step((IDX_HEADS, LANES)), per_step((IDX_HEADS, 1)), per_step((1, LANES))]
            + [page_spec(g, p) for g in range(IDX_SEQS) for p in range(n_pages)],
            out_specs=per_step((1, width))),
        out_shape=jax.ShapeDtypeStruct((db, 1, width), F32),
        compiler_params=_cparams("arbitrary"),
        name="idx_sample",
    )(pt, qi3, wi3, kin3, *([pool_ki_t] * (IDX_SEQS * n_pages)))


def _select_sample_kernel(sc_ref, rows_ref, ids_ref, bias_ref, key_scr, tri_scr, rank_scr, ids_scr, *, n_keys):
    lane = lax.broadcasted_iota(I32, (1, sc_ref.shape[1]), 1)
    key_scr[...] = jnp.where(lane < n_keys, _sortable_key(sc_ref[...]), jnp.int32(INT_MIN))

    def count_ge(cand):
        return jnp.sum(jnp.where(key_scr[...] >= cand, 1.0, 0.0), axis=-1, keepdims=True)

    thr, _ = _kth_largest_key(count_ge, float(sc_ref.shape[1]))
    need = float(TOPK) - count_ge(thr + 1)

    def count_tie_le(cm):
        return jnp.sum(jnp.where((key_scr[...] == thr) & (lane <= cm), 1.0, 0.0), axis=-1, keepdims=True)

    cut = _tie_cutoff(count_tie_le, need, 12)
    cut = jnp.where(thr == INT_MIN, jnp.int32(-1), cut)
    key = key_scr[...]
    sel = (key > thr) | ((key == thr) & (lane <= cut))

    past = n_keys - 1
    n_seq = sc_ref.shape[0]
    taken = jnp.where(sel, 1.0, 0.0)[:, :past]
    for r in range(past // TK):
        upper = (lax.broadcasted_iota(I32, (TK, past), 0) + r * TK) <= lax.broadcasted_iota(I32, (TK, past), 1)
        tri_scr[r * TK:(r + 1) * TK, :] = jnp.where(upper, 1.0, 0.0).astype(BF16)
    rank = jnp.dot(taken.astype(BF16), tri_scr[...], preferred_element_type=F32)
    rank_scr[...] = rank * taken
    n_taken = rank[:, past - 1:past]
    slot = (lax.broadcasted_iota(I32, (TOPK, 1), 0) + 1).astype(F32)
    seq_lane = lax.broadcasted_iota(I32, (1, n_seq), 1)
    ids_scr[...] = jnp.zeros(ids_scr.shape, F32)

    def seq_body(d, carry):
        hit = rank_scr[pl.ds(d, 1), :] == slot
        row = jnp.sum(jnp.where(hit, rows_ref[pl.ds(d, 1), :], 0.0), axis=-1, keepdims=True)
        ids_scr[...] += jnp.where(seq_lane == d, row, 0.0)
        return carry

    lax.fori_loop(0, n_seq, seq_body, 0)
    ids_ref[...] = ids_scr[...].T.astype(I32)
    blane = lax.broadcasted_iota(I32, bias_ref.shape, 1)
    new_taken = jnp.sum(jnp.where(sel & (lane == past), 1.0, 0.0), axis=-1, keepdims=True)
    live = (blane.astype(F32) < n_taken) | ((blane == TOPK) & (new_taken > 0.0))
    bias_ref[...] = jnp.where(live, 0.0, NEG)


def _select_sample(scores, cache_rows, n_keys):
    n_seq = scores.shape[0]
    past = n_keys - 1
    return pl.pallas_call(
        functools.partial(_select_sample_kernel, n_keys=n_keys),
        out_shape=[jax.ShapeDtypeStruct((n_seq, TOPK), I32), jax.ShapeDtypeStruct((n_seq, TOPK + LANES), F32)],
        scratch_shapes=[pltpu.VMEM(scores.shape, I32), pltpu.VMEM((past, past), BF16),
                        pltpu.VMEM((n_seq, past), F32), pltpu.VMEM((TOPK, n_seq), F32)],
        compiler_params=_cparams(),
        name="select_sample",
    )(scores, cache_rows)


def _gather_kv_rows(pool_k, pool_v, ids):
    n = ids.shape[0]
    mesh = plsc.VectorSubcoreMesh(core_axis_name="core", subcore_axis_name="subcore")
    n_workers = mesh.num_cores * mesh.num_subcores
    per_worker = n // n_workers
    assert n % (n_workers * SC_GATHER_ROWS) == 0
    row = pool_k.shape[1:]
    out = jax.ShapeDtypeStruct((n,) + row, pool_k.dtype)

    @functools.partial(
        pl.kernel, mesh=mesh, out_type=[out, out],
        scratch_types=[pltpu.VMEM((SC_GATHER_ROWS,), I32), pltpu.VMEM((SC_GATHER_ROWS,) + row, pool_k.dtype),
                       pltpu.VMEM((SC_GATHER_ROWS,) + row, pool_v.dtype), pltpu.SemaphoreType.DMA,
                       pltpu.SemaphoreType.DMA],
        compiler_params=pltpu.CompilerParams(use_tc_tiling_on_sc=True),
        name="gather_kv_rows",
    )
    def gather(k_hbm, v_hbm, ids_hbm, ko_hbm, vo_hbm, ids_v, k_v, v_v, ksem, vsem):
        worker = lax.axis_index("subcore") * mesh.num_cores + lax.axis_index("core")

        @pl.loop(0, per_worker // SC_GATHER_ROWS)
        def _(j):
            off = pl.multiple_of(worker * per_worker + j * SC_GATHER_ROWS, SC_GATHER_ROWS)
            pltpu.sync_copy(ids_hbm.at[pl.ds(off, SC_GATHER_ROWS)], ids_v)
            kcopy = pltpu.async_copy(k_hbm.at[ids_v], k_v, ksem)
            vcopy = pltpu.async_copy(v_hbm.at[ids_v], v_v, vsem)
            kcopy.wait()
            pltpu.sync_copy(k_v, ko_hbm.at[pl.ds(off, SC_GATHER_ROWS)])
            vcopy.wait()
            pltpu.sync_copy(v_v, vo_hbm.at[pl.ds(off, SC_GATHER_ROWS)])

    return gather(pool_k, pool_v, ids)


def _attn_sample_kernel(q_ref, kn_ref, vn_ref, bias_ref, kg_ref, vg_ref, o_ref, kall, vall):
    past_rows = kg_ref.shape[1]
    zeros = jnp.zeros((LANES - N_HEADS, HEAD_DIM), F32)
    for g in range(ATT_SEQS):
        kall[:past_rows, :] = kg_ref[g].astype(BF16)
        vall[:past_rows, :] = vg_ref[g].astype(BF16)
        kall[past_rows:, :] = jnp.concatenate([kn_ref[g], zeros], axis=0).astype(BF16)
        vall[past_rows:, :] = jnp.concatenate([vn_ref[g], zeros], axis=0).astype(BF16)
        s = lax.dot_general(q_ref[g], kall[...], NT_DIMS, preferred_element_type=F32)
        own = (lax.broadcasted_iota(I32, s.shape, 1) & (N_HEADS - 1)) == lax.broadcasted_iota(I32, s.shape, 0)
        s = jnp.where(own, s + bias_ref[g], NEG)
        m = jnp.max(s, axis=-1, keepdims=True)
        p = jnp.exp2(s - m)
        l = jnp.sum(p, axis=-1, keepdims=True)
        o_ref[g] = jnp.dot(p.astype(BF16), vall[...], preferred_element_type=F32) / l


def _attn_sample(q3, kn3, vn3, bias3, kg, vg):
    db, past_rows, _ = kg.shape
    width = past_rows + LANES
    assert db % ATT_SEQS == 0 and bias3.shape == (db, 1, width)
    per_step = lambda r, w: pl.BlockSpec((ATT_SEQS, r, w), lambda d: (d, 0, 0))
    return pl.pallas_call(
        _attn_sample_kernel,
        grid=(db // ATT_SEQS,),
        in_specs=[per_step(N_HEADS, HEAD_DIM)] * 3 + [per_step(1, width)] + [per_step(past_rows, HEAD_DIM)] * 2,
        out_specs=per_step(N_HEADS, HEAD_DIM),
        out_shape=jax.ShapeDtypeStruct((db, N_HEADS, HEAD_DIM), F32),
        scratch_shapes=[pltpu.VMEM((width, HEAD_DIM), BF16), pltpu.VMEM((width, HEAD_DIM), BF16)],
        compiler_params=_cparams("arbitrary"),
        name="attn_sample",
    )(q3, kn3, vn3, bias3, kg, vg)


def _ffn_sample_body(x, g_ref, wup_ref, wc_ref, bc_ref, wdn_ref, st0_ref, st1_ref, gate_ref, a_scr, h_scr):
    a_scr[...] = _rms(x, g_ref[...]).astype(BF16)

    def gate_taps(c, gate):
        cs = slice(c * FF_CHUNK, (c + 1) * FF_CHUNK)
        gate_ref[:, cs] = gate
        return st0_ref[:, cs], st1_ref[:, cs]

    _glu_hidden(a_scr, wup_ref, wc_ref, bc_ref, h_scr, gate_taps)
    return x + jnp.dot(h_scr[...], wdn_ref[...], preferred_element_type=F32)


def _layer0_sample_kernel(x_ref, o_ref, wo_ref, g_ref, wup_ref, wc_ref, bc_ref, wdn_ref, st0_ref, st1_ref,
                          out_ref, gate_ref, a_scr, h_scr):
    x = x_ref[...] + jnp.dot(o_ref[...].astype(BF16), wo_ref[...], preferred_element_type=F32)
    out_ref[...] = _ffn_sample_body(x, g_ref, wup_ref, wc_ref, bc_ref, wdn_ref, st0_ref, st1_ref, gate_ref,
                                    a_scr, h_scr)


def _layer1_sample_kernel(x_ref, gm_ref, win_ref, wcr_ref, bcr_ref, wg_ref, bga_ref, bgx_ref, lam_ref, wor_ref,
                          cs0_ref, cs1_ref, cs2_ref, h0_ref, g_ref, wup_ref, wc_ref, bc_ref, wdn_ref, st0_ref,
                          st1_ref, gf_ref, out_ref, hnew_ref, xb_ref, gate_ref, a_scr, h_scr, u_scr):
    x = x_ref[...]
    a_scr[...] = _rms(x, gm_ref[...]).astype(BF16)
    nblk = MXU_N // LANES
    ncs = -RG_C * jax.nn.softplus(-lam_ref[...])
    for c in range(D_MODEL // MXU_N):
        cs = slice(c * MXU_N, (c + 1) * MXU_N)
        xb = jnp.dot(a_scr[...], win_ref[:, D_MODEL + c * MXU_N:D_MODEL + (c + 1) * MXU_N],
                     preferred_element_type=F32)
        gate = jnp.dot(a_scr[...], win_ref[:, cs], preferred_element_type=F32)
        xb_ref[:, cs] = xb
        xc = (bcr_ref[:, cs] + wcr_ref[0:1, cs] * cs0_ref[:, cs] + wcr_ref[1:2, cs] * cs1_ref[:, cs]
              + wcr_ref[2:3, cs] * cs2_ref[:, cs] + wcr_ref[3:4, cs] * xb)
        for k in range(nblk):
            n = c * nblk + k
            ls = slice(n * LANES, (n + 1) * LANES)
            xcn = xc[:, k * LANES:(k + 1) * LANES]
            gg = jnp.dot(xcn.astype(BF16), wg_ref[n], preferred_element_type=F32)
            a_t, b_t = _lru_coeffs(xcn, gg, bga_ref[:, ls], bgx_ref[:, ls], ncs[:, ls])
            h = a_t * h0_ref[:, ls] + b_t
            hnew_ref[:, ls] = h
            u_scr[:, ls] = (jax.nn.gelu(gate[:, k * LANES:(k + 1) * LANES]) * h).astype(BF16)
    x = x + jnp.dot(u_scr[...], wor_ref[...], preferred_element_type=F32)
    y = _ffn_sample_body(x, g_ref, wup_ref, wc_ref, bc_ref, wdn_ref, st0_ref, st1_ref, gate_ref, a_scr, h_scr)
    out_ref[...] = _rms(y, gf_ref[...])


def _layer_spec(stacked, layer):
    nd = stacked.ndim - 1
    return pl.BlockSpec((None,) + stacked.shape[1:], lambda *_: (layer,) + (0,) * nd, pipeline_mode=pl.Buffered(1))


def _sample_call(kernel, name, args, out_widths, extra_scratch=()):
    db = args[0].shape[0]
    specs = [_layer_spec(*a) if isinstance(a, tuple) else _const_spec(a.shape) for a in args]
    arrays = [a[0] if isinstance(a, tuple) else a for a in args]
    return pl.pallas_call(
        kernel,
        grid=(1,),
        in_specs=specs,
        out_specs=[pl.BlockSpec((db, w), lambda i: (0, 0)) for w in out_widths],
        out_shape=[jax.ShapeDtypeStruct((db, w), F32) for w in out_widths],
        scratch_shapes=[pltpu.VMEM((db, D_MODEL), BF16), pltpu.VMEM((db, D_FF), BF16), *extra_scratch],
        compiler_params=_cparams("arbitrary"),
        name=name,
    )(*arrays)


def _rope_tables(pos):
    posf = pos.astype(F32)[:, None]

    def cs(d):
        half = d // 2
        inv = ROPE_THETA ** (-jnp.arange(half, dtype=F32) * 2.0 / d)
        ang = posf * inv[None, :]
        return jnp.cos(ang), jnp.sin(ang)

    c, s = cs(HEAD_DIM)
    cos = jnp.concatenate([c, c], axis=-1)
    sin = jnp.concatenate([-s, s], axis=-1)
    c, s = cs(IDX_DIM)
    one, zero = jnp.ones_like(c), jnp.zeros_like(c)
    tile = lambda parts: jnp.concatenate(parts, axis=-1)
    icos = tile([c, c, one, one])
    isa = tile([-s, zero, zero, zero])
    isb = tile([zero, s, zero, zero])
    return cos, sin, icos, isa, isb


def _split_attn_in(w):
    qkv = N_HEADS * HEAD_DIM
    wqk, wv, wi = w[:, :2 * qkv], w[:, 2 * qkv:3 * qkv], w[:, 3 * qkv:]
    pad = lambda a: jnp.pad(a, ((0, 0), (0, LANES - a.shape[1])))
    groups = [pad(wi[:, h * IDX_DIM:(h + 1) * IDX_DIM]) for h in range(IDX_HEADS)]
    groups.append(pad(wi[:, IDX_HEADS * IDX_DIM:]))
    return wqk.astype(BF16), wv.astype(BF16), jnp.concatenate(groups, axis=-1).astype(BF16)


def kernel(x_prompt, x_sample, cache_k, cache_v, cache_kidx, state_lru_h, state_lru_conv, state_ffn_conv,
           page_table, norm_mix, norm_ffn, norm_final, w_attn_in, w_attn_out, w_rec_in, w_rec_conv, b_rec_conv,
           w_gate_a, b_gate_a, w_gate_x, b_gate_x, lru_lambda, w_rec_out, w_ffn_up, w_ffn_conv, b_ffn_conv,
           w_ffn_down):
    bsz, seq, d = x_prompt.shape
    db = x_sample.shape[0]
    n_pages = page_table.shape[1]
    past = n_pages * PAGE
    assert d == D_MODEL and x_sample.shape[1] == 1 and seq % ROW_BLOCK == 0 and seq % TQ == 0
    assert min(TOPK, seq // 4) == TOPK and min(TOPK, (past + 1) // 4) == TOPK

    vec = lambda a: a.reshape(1, -1)
    wqk, wv, widx = _split_attn_in(w_attn_in[0])
    wo_attn = w_attn_out[0].astype(BF16)
    w_rin = w_rec_in[0].astype(BF16)
    w_gates = jnp.concatenate([w_gate_a[0], w_gate_x[0]], axis=-1).astype(BF16)
    wo_rec = w_rec_out[0].astype(BF16)
    wup_all, wdn_all = w_ffn_up.astype(BF16), w_ffn_down.astype(BF16)
    wup = [(wup_all, i) for i in range(2)]
    wdn = [(wdn_all, i) for i in range(2)]
    rec_vecs = (w_rec_conv[0], vec(b_rec_conv[0]), w_gates, vec(b_gate_a[0]), vec(b_gate_x[0]),
                vec(lru_lambda[0]), wo_rec)
    ffn_vecs = [(vec(norm_ffn[i]), wup[i], w_ffn_conv[i], vec(b_ffn_conv[i]), wdn[i]) for i in range(2)]
    gfin = vec(norm_final)

    xp = x_prompt.reshape(bsz * seq, d)
    q, kf, kb, vf, vt, qi, kiw = _attn_in(xp, vec(norm_mix[0]), wqk, wv, widx, _rope_tables(jnp.arange(seq)),
                                          ROW_BLOCK, seq // ROW_BLOCK)
    wit = kiw[:, IDX_DIM:IDX_DIM + IDX_HEADS].T
    hp_attn = _dsa_prompt(q, kb, vt, qi, kiw, wit, xp, wo_attn, bsz, seq)
    hp, ftail0 = _ffn_prompt(hp_attn, *ffn_vecs[0], gfin, bsz, seq, False)
    hp, hlast, ctail = _rec_prompt(hp, vec(norm_mix[1]), w_rin, *rec_vecs, bsz, seq)
    yp, ftail1 = _ffn_prompt(hp, *ffn_vecs[1], gfin, bsz, seq, True)

    xs = x_sample.reshape(db, d)
    tabs = tuple(jnp.broadcast_to(t, (db, t.shape[1])) for t in _rope_tables(jnp.full((1,), past)))
    qs, kfs, _, vfs, _, qis, kiws = _attn_in(xs, vec(norm_mix[0]), wqk, wv, widx, tabs, db, 1)
    pt = page_table.reshape(-1)
    scores = _idx_sample(pt, qis.reshape(db, IDX_HEADS, LANES),
                         kiws[:, IDX_DIM:IDX_DIM + IDX_HEADS].reshape(db, IDX_HEADS, 1),
                         kiws.reshape(db, 1, LANES), jnp.swapaxes(cache_kidx[0], 1, 2), n_pages)
    cache_rows = (page_table[:, :, None] * PAGE + jnp.arange(PAGE, dtype=I32)).reshape(db, past).astype(F32)
    ids, bias = _select_sample(scores.reshape(db, -1), cache_rows, past + 1)
    heads = (N_HEADS, HEAD_DIM)
    key_rows = lambda pool: pool[0].reshape(-1, *heads)
    ids, _ = lax.optimization_barrier((ids, kb))
    kg, vg = _gather_kv_rows(key_rows(cache_k), key_rows(cache_v), ids.reshape(-1))
    bias_rows = jnp.pad(jnp.repeat(bias[:, :TOPK + 1], N_HEADS, axis=1), ((0, 0), (0, LANES - N_HEADS)),
                        constant_values=NEG)
    bias_rows, _ = lax.optimization_barrier((bias_rows, hp_attn))
    seq_rows = lambda g: g.reshape(db, TOPK * N_HEADS, HEAD_DIM)
    o_s = _attn_sample(qs.reshape(db, *heads), kfs.reshape(db, *heads), vfs.reshape(db, *heads),
                       bias_rows.reshape(db, 1, -1), seq_rows(kg), seq_rows(vg))
    st = state_ffn_conv
    hs, gate0 = _sample_call(_layer0_sample_kernel, "layer0_sample",
                             (xs, o_s.reshape(db, d), wo_attn, *ffn_vecs[0], st[0, :, 0], st[0, :, 1]),
                             (D_MODEL, D_FF))
    cst = state_lru_conv[0]
    ys, hnew, xbs, gate1 = _sample_call(
        _layer1_sample_kernel, "layer1_sample",
        (hs, vec(norm_mix[1]), w_rin, *rec_vecs, cst[:, 0], cst[:, 1], cst[:, 2], state_lru_h[0], *ffn_vecs[1],
         st[1, :, 0], st[1, :, 1], gfin),
        (D_MODEL, D_MODEL, D_MODEL, D_FF), extra_scratch=(pltpu.VMEM((db, D_MODEL), BF16),))

    return (
        yp.reshape(bsz, seq, d),
        ys.reshape(db, 1, d),
        kf.reshape(1, bsz, seq, *heads),
        vf.reshape(1, bsz, seq, *heads),
        kiw[:, :IDX_DIM].reshape(1, bsz, seq, IDX_DIM),
        kfs.reshape(1, db, 1, *heads),
        vfs.reshape(1, db, 1, *heads),
        kiws[:, :IDX_DIM].reshape(1, db, 1, IDX_DIM),
        hlast[None, :, 0, :],
        ctail[None, :, SUBLANES - 3:, :],
        hnew[None],
        jnp.stack([cst[:, 1], cst[:, 2], xbs], axis=1)[None],
        jnp.stack([ftail0[:, SUBLANES - 2:], ftail1[:, SUBLANES - 2:]]),
        jnp.stack([jnp.stack([st[0, :, 1], gate0], axis=1), jnp.stack([st[1, :, 1], gate1], axis=1)]),
    )
```

```python
import functools

import jax
import jax.numpy as jnp
from jax import lax
from jax.experimental import pallas as pl
from jax.experimental.pallas import tpu as pltpu
from jax.experimental.pallas import tpu_sc as plsc

F32 = jnp.float32
BF16 = jnp.bfloat16
I32 = jnp.int32
I16 = jnp.int16

D_MODEL = 1024
N_HEADS = 8
HEAD_DIM = 128
IDX_HEADS = 4
IDX_DIM = 64
TOPK = 256
PAGE = 128
ROPE_THETA = 10000.0
IDX_SCALE = (IDX_DIM * IDX_HEADS) ** -0.5
Q_SCALE = HEAD_DIM ** -0.5 * 1.4426950408889634
RG_C = 8.0
D_FF = 2816
RMS_EPS = 1e-6

LANES = 128
SUBLANES = 8
MXU_N = 256
IDX_W = (IDX_HEADS + 1) * LANES
INT_MIN = -(2 ** 31)
NEG = -1e30
VMEM_LIMIT = 52 * 1024 * 1024

ROW_BLOCK = 512
FFN_ROWS = 1024
TQ = 256
TK = 256
COUNT_ROWS = 32
COUNT_ROWS_16 = 64
IDX_SEQS = 8
ATT_SEQS = 4
SC_GATHER_ROWS = 32
FF_CHUNK = 256
N_FF_CHUNKS = D_FF // FF_CHUNK
NT_DIMS = (((1,), (1,)), ((), ()))


def _cparams(*sem):
    return pltpu.CompilerParams(dimension_semantics=sem if sem else None, vmem_limit_bytes=VMEM_LIMIT)


def _const_spec(shape):
    nd = len(shape)
    return pl.BlockSpec(shape, lambda *_: (0,) * nd, pipeline_mode=pl.Buffered(1))


def _rms(x, g):
    return x * lax.rsqrt(jnp.mean(x * x, axis=-1, keepdims=True) + RMS_EPS) * g


def _shift_rows(x, s, prev_rows):
    r = pltpu.roll(x, s, 0)
    top = r[:SUBLANES]
    row = lax.broadcasted_iota(I32, top.shape, 0)
    for k in range(s):
        top = jnp.where(row == k, prev_rows[k], top)
    return jnp.concatenate([top, r[SUBLANES:]], axis=0)


def _sortable_key(score):
    bits = pltpu.bitcast(score + 0.0, I32)
    return jnp.where(bits < 0, bits ^ jnp.int32(0x7FFFFFFF), bits)


def _attn_in_kernel(x_ref, g_ref, wqk_ref, wv_ref, widx_ref, cos_ref, sin_ref, icos_ref, isa_ref, isb_ref,
                    q_ref, kf_ref, kb_ref, vf_ref, vt_ref, qi_ref, kiw_ref, a_scr):
    a_scr[...] = _rms(x_ref[...], g_ref[...]).astype(BF16)
    cos = cos_ref[...]
    sin = sin_ref[...]
    for c in range(2 * D_MODEL // MXU_N):
        r = jnp.dot(a_scr[...], wqk_ref[:, c * MXU_N:(c + 1) * MXU_N], preferred_element_type=F32)
        for hh in range(MXU_N // HEAD_DIM):
            xh = r[:, hh * HEAD_DIM:(hh + 1) * HEAD_DIM]
            y = xh * cos + pltpu.roll(xh, HEAD_DIM // 2, 1) * sin
            col = (c * MXU_N) % D_MODEL + hh * HEAD_DIM
            if c < D_MODEL // MXU_N:
                q_ref[:, col:col + HEAD_DIM] = (y * Q_SCALE).astype(BF16)
            else:
                kf_ref[:, col:col + HEAD_DIM] = y
                kb_ref[:, col:col + HEAD_DIM] = y.astype(BF16)
    for c in range(D_MODEL // MXU_N):
        r = jnp.dot(a_scr[...], wv_ref[:, c * MXU_N:(c + 1) * MXU_N], preferred_element_type=F32)
        vf_ref[:, c * MXU_N:(c + 1) * MXU_N] = r
        vt_ref[c * MXU_N:(c + 1) * MXU_N, :] = r.T.astype(BF16)
    ri = jnp.dot(a_scr[...], widx_ref[...], preferred_element_type=F32)
    groups = lambda t: jnp.concatenate([t] * (IDX_W // LANES), axis=1)
    yi = (ri * groups(icos_ref[...]) + pltpu.roll(ri, IDX_W - IDX_DIM // 2, 1) * groups(isa_ref[...])
          + pltpu.roll(ri, IDX_DIM // 2, 1) * groups(isb_ref[...]))
    qi_ref[...] = yi[:, :IDX_HEADS * LANES].astype(BF16)
    kiw_ref[...] = yi[:, IDX_HEADS * LANES:]


def _attn_in(x, g, wqk, wv, widx, tabs, rows, n_pos_blocks):
    n = x.shape[0]
    cos, sin, icos, isa, isb = tabs
    row_spec = lambda w: pl.BlockSpec((rows, w), lambda i: (i, 0))
    tab_spec = lambda w: pl.BlockSpec((rows, w), lambda i: (i % n_pos_blocks, 0))
    vt_spec = pl.BlockSpec((None, D_MODEL, rows), lambda i: (i // n_pos_blocks, 0, i % n_pos_blocks))
    return pl.pallas_call(
        _attn_in_kernel,
        grid=(n // rows,),
        in_specs=[row_spec(D_MODEL), _const_spec((1, D_MODEL)), _const_spec(wqk.shape), _const_spec(wv.shape),
                  _const_spec(widx.shape), tab_spec(HEAD_DIM), tab_spec(HEAD_DIM), tab_spec(LANES),
                  tab_spec(LANES), tab_spec(LANES)],
        out_specs=[row_spec(D_MODEL)] * 4 + [vt_spec, row_spec(IDX_HEADS * LANES), row_spec(LANES)],
        out_shape=[jax.ShapeDtypeStruct((n, D_MODEL), BF16), jax.ShapeDtypeStruct((n, D_MODEL), F32),
                   jax.ShapeDtypeStruct((n, D_MODEL), BF16), jax.ShapeDtypeStruct((n, D_MODEL), F32),
                   jax.ShapeDtypeStruct((n // (rows * n_pos_blocks), D_MODEL, rows * n_pos_blocks), BF16),
                   jax.ShapeDtypeStruct((n, IDX_HEADS * LANES), BF16),
                   jax.ShapeDtypeStruct((n, LANES), F32)],
        scratch_shapes=[pltpu.VMEM((rows, D_MODEL), BF16)],
        compiler_params=_cparams("arbitrary"),
        name="attn_in",
    )(x, g, wqk, wv, widx, cos, sin, icos, isa, isb)


def _kth_largest_key(count_ge, n_total):
    kf = float(TOPK)
    c0 = count_ge(0)
    t = jnp.where(c0 >= kf, jnp.int32(0), jnp.int32(INT_MIN))
    ct = jnp.where(c0 >= kf, c0, n_total)

    def bit_body(it, carry):
        t, ct = carry
        cand = t + (jnp.int32(1) << (30 - it))
        cnt = count_ge(cand)
        return jnp.where(cnt >= kf, cand, t), jnp.where(cnt >= kf, cnt, ct)

    return lax.fori_loop(0, 31, bit_body, (t, ct))


def _kth_largest_half(count_ge, above, count_all):
    kf = float(TOPK)
    c0 = above + count_ge(0)
    t = jnp.where(c0 >= kf, jnp.int32(0), jnp.int32(-(2 ** 15)))
    ct = jnp.where(c0 >= kf, c0, count_all)

    def bit_body(it, carry):
        t, ct = carry
        cand = t + (jnp.int32(1) << (14 - it))
        cnt = above + count_ge(cand)
        return jnp.where(cnt >= kf, cand, t), jnp.where(cnt >= kf, cnt, ct)

    return lax.fori_loop(0, 15, bit_body, (t, ct))


def _tie_cutoff(count_tie_le, need, n_bits):
    def bit_body(it, c):
        cand = c + (jnp.int32(1) << (n_bits - 1 - it))
        return jnp.where(count_tie_le(cand - 1) < need, cand, c)

    return lax.fori_loop(0, n_bits, bit_body, jnp.zeros_like(need, dtype=I32))


def _dsa_prompt_kernel(q_ref, kb_ref, vt_ref, qi_ref, qin_ref, kiwk_ref, wit_ref, witn_ref, x_ref, wo_ref, out_ref,
                       key_scr, bias_scr, kib_scr, o_scr, acc_scr, s_scr, hi_scr, lo_scr):
    i = pl.program_id(1)
    nk = i + 1
    lane_q = lax.broadcasted_iota(I32, (1, TQ), 1)

    def rows(kc):
        return pl.ds(kc * TK if isinstance(kc, int) else pl.multiple_of(kc * TK, TK), TK)

    def spos(kc):
        return kc * TK + lax.broadcasted_iota(I32, (TK, 1), 0)

    def colsum(a):
        return jnp.sum(a.reshape(TK // COUNT_ROWS, COUNT_ROWS, TQ), axis=0)

    def score_chunk(kc, qidx_ref, wis, qpos):
        kic = kib_scr[rows(kc), :]
        sc = jnp.zeros((TK, TQ), F32)
        for h in range(IDX_HEADS):
            s = lax.dot_general(kic, qidx_ref[:, h * LANES:(h + 1) * LANES], NT_DIMS, preferred_element_type=F32)
            sc = sc + wis[h:h + 1, :] * jnp.maximum(s, 0.0)
        key = _sortable_key(sc)
        if qpos is not None:
            key = jnp.where(spos(kc) <= qpos, key, jnp.int32(INT_MIN))
        key_scr[rows(kc), :] = key
        hi_scr[rows(kc), :] = (key >> 16).astype(I16)
        lo_scr[rows(kc), :] = ((key & 0xFFFF) - 2 ** 15).astype(I16)

    @pl.when(i == 0)
    def _():
        kib_scr[...] = kiwk_ref[...].astype(BF16)
        score_chunk(0, qi_ref, wit_ref[...] * IDX_SCALE, lane_q)

    wis_next = witn_ref[...] * IDX_SCALE

    def count_ge(cand):
        def body(kc, acc):
            return acc + colsum(jnp.where(key_scr[rows(kc), :] >= cand, 1.0, 0.0))
        acc = lax.fori_loop(0, nk, body, jnp.zeros((COUNT_ROWS, TQ), F32))
        return jnp.sum(acc, axis=0, keepdims=True)

    def count_ge_half(half_scr):
        def count(cand):
            c16 = jnp.asarray(cand, I32).astype(I16)

            def body(kc, acc):
                ind = jnp.where(half_scr[rows(kc), :] >= c16, jnp.int16(1), jnp.int16(0))
                for r in range(TK // COUNT_ROWS_16):
                    acc = acc + ind[r * COUNT_ROWS_16:(r + 1) * COUNT_ROWS_16]
                return acc

            acc = lax.fori_loop(0, nk, body, jnp.zeros((COUNT_ROWS_16, TQ), I16))
            return jnp.sum(acc.astype(F32), axis=0, keepdims=True)
        return count

    count_hi, count_lo = count_ge_half(hi_scr), count_ge_half(lo_scr)
    thi, cnt_hi = _kth_largest_half(count_hi, 0.0, (nk * TK).astype(F32))
    above = jnp.where(thi == 2 ** 15 - 1, 0.0, count_hi(jnp.minimum(thi + 1, 2 ** 15 - 1)))
    thi16 = thi.astype(I16)

    def lo_body(kc, carry):
        lo_scr[rows(kc), :] = jnp.where(hi_scr[rows(kc), :] == thi16, lo_scr[rows(kc), :], jnp.int16(-(2 ** 15)))
        return carry

    lax.fori_loop(0, nk, lo_body, 0)
    tlo, cnt_thr = _kth_largest_half(count_lo, above, cnt_hi)
    thr = (thi << 16) | (tlo + 2 ** 15)
    short = thr == INT_MIN
    has_ties = jnp.max(jnp.where(short, 0.0, cnt_thr)) > float(TOPK)

    @pl.when(jnp.logical_not(has_ties))
    def _():
        thr_vis = jnp.maximum(thr, jnp.int32(INT_MIN + 1))

        def bias_body(kc, carry):
            bias_scr[rows(kc), :] = jnp.where(key_scr[rows(kc), :] >= thr_vis, 0.0, NEG)
            return carry

        lax.fori_loop(0, nk, bias_body, 0)

    @pl.when(has_ties)
    def _():
        need = float(TOPK) - count_ge(thr + 1)

        def eq_body(kc, carry):
            bias_scr[rows(kc), :] = jnp.where(key_scr[rows(kc), :] == thr, 1.0, 0.0)
            return carry

        lax.fori_loop(0, nk, eq_body, 0)

        def count_tie_le(cm):
            def body(kc, acc):
                return acc + colsum(jnp.where(spos(kc) <= cm, bias_scr[rows(kc), :], 0.0))
            acc = lax.fori_loop(0, nk, body, jnp.zeros((COUNT_ROWS, TQ), F32))
            return jnp.sum(acc, axis=0, keepdims=True)

        cut = jnp.where(short, jnp.int32(-1), _tie_cutoff(count_tie_le, need, 11))

        def bias_body(kc, carry):
            sel = (key_scr[rows(kc), :] > thr) | ((bias_scr[rows(kc), :] > 0.0) & (spos(kc) <= cut))
            bias_scr[rows(kc), :] = jnp.where(sel, 0.0, NEG)
            return carry

        lax.fori_loop(0, nk, bias_body, 0)

    acc_scr[...] = jnp.zeros(acc_scr.shape, F32)

    def att_body(kc, carry):
        ms, ls = carry
        for h in range(N_HEADS):
            hs = slice(h * HEAD_DIM, (h + 1) * HEAD_DIM)
            s_scr[h] = lax.dot_general(kb_ref[rows(kc), hs], q_ref[:, hs], NT_DIMS, preferred_element_type=F32)
        bias = bias_scr[rows(kc), :]
        score_chunk(kc, qin_ref, wis_next, None)
        new_ms, new_ls = [], []
        for h in range(N_HEADS):
            hs = slice(h * HEAD_DIM, (h + 1) * HEAD_DIM)
            s = s_scr[h] + bias
            m_new = jnp.maximum(ms[h], jnp.max(s, axis=0, keepdims=True))
            alpha = jnp.exp2(ms[h] - m_new)
            p = jnp.exp2(s - m_new)
            new_ms.append(m_new)
            new_ls.append(alpha * ls[h] + jnp.sum(p, axis=0, keepdims=True))
            pv = jnp.dot(vt_ref[hs, rows(kc)], p.astype(BF16), preferred_element_type=F32)
            acc_scr[hs, :] = alpha * acc_scr[hs, :] + pv
        return tuple(new_ms), tuple(new_ls)

    init = ((jnp.full((1, TQ), NEG, F32),) * N_HEADS, (jnp.zeros((1, TQ), F32),) * N_HEADS)
    _, ls = lax.fori_loop(0, nk, att_body, init)

    @pl.when(i < pl.num_programs(1) - 1)
    def _():
        score_chunk(nk, qin_ref, wis_next, (i + 1) * TQ + lane_q)

    for h in range(N_HEADS):
        hs = slice(h * HEAD_DIM, (h + 1) * HEAD_DIM)
        o_scr[:, hs] = (acc_scr[hs, :] / ls[h]).T.astype(BF16)
    out_ref[...] = x_ref[...] + jnp.dot(o_scr[...], wo_ref[...], preferred_element_type=F32)


def _dsa_prompt(q, kb, vt, qi, kiw, wit, x, wo, bsz, seq):
    n = x.shape[0]
    nq = seq // TQ
    qrow = lambda w: pl.BlockSpec((TQ, w), lambda b, i: (b * nq + i, 0))
    brow = lambda w: pl.BlockSpec((seq, w), lambda b, i: (b, 0))
    nxt = lambda b, i: b * nq + jnp.minimum(i + 1, nq - 1)
    return pl.pallas_call(
        _dsa_prompt_kernel,
        grid=(bsz, nq),
        in_specs=[qrow(D_MODEL), brow(D_MODEL), pl.BlockSpec((None, D_MODEL, seq), lambda b, i: (b, 0, 0)),
                  qrow(IDX_HEADS * LANES), pl.BlockSpec((TQ, IDX_HEADS * LANES), lambda b, i: (nxt(b, i), 0)),
                  brow(LANES), pl.BlockSpec((IDX_HEADS, TQ), lambda b, i: (0, b * nq + i)),
                  pl.BlockSpec((IDX_HEADS, TQ), lambda b, i: (0, nxt(b, i))),
                  qrow(D_MODEL), _const_spec(wo.shape)],
        out_specs=qrow(D_MODEL),
        out_shape=jax.ShapeDtypeStruct((n, D_MODEL), F32),
        scratch_shapes=[pltpu.VMEM((seq, TQ), I32), pltpu.VMEM((seq, TQ), F32), pltpu.VMEM((seq, LANES), BF16),
                        pltpu.VMEM((TQ, D_MODEL), BF16), pltpu.VMEM((D_MODEL, TQ), F32),
                        pltpu.VMEM((N_HEADS, TK, TQ), F32), pltpu.VMEM((seq, TQ), I16),
                        pltpu.VMEM((seq, TQ), I16)],
        compiler_params=_cparams("arbitrary", "arbitrary"),
        name="dsa_prompt",
    )(q, kb, vt, qi, qi, kiw, wit, wit, x, wo)


def _glu_hidden(a_scr, wup_ref, wc_ref, bc_ref, h_scr, gate_taps):
    for c in range(N_FF_CHUNKS):
        cs = slice(c * FF_CHUNK, (c + 1) * FF_CHUNK)
        gate = jnp.dot(a_scr[...], wup_ref[:, cs], preferred_element_type=F32)
        val = jnp.dot(a_scr[...], wup_ref[:, D_FF + c * FF_CHUNK:D_FF + (c + 1) * FF_CHUNK],
                      preferred_element_type=F32)
        g2, g1 = gate_taps(c, gate)
        gc = bc_ref[:, cs] + wc_ref[0:1, cs] * g2 + wc_ref[1:2, cs] * g1 + wc_ref[2:3, cs] * gate
        h_scr[:, cs] = (jax.nn.gelu(gc) * val).astype(BF16)


def _ffn_prompt_kernel(x_ref, g_ref, wup_ref, wc_ref, bc_ref, wdn_ref, gf_ref, out_ref, tail_ref,
                       a_scr, h_scr, carry_scr, *, final_norm):
    j = pl.program_id(1)
    rows = x_ref.shape[0]
    a_scr[...] = _rms(x_ref[...], g_ref[...]).astype(BF16)

    @pl.when(j == 0)
    def _():
        carry_scr[...] = jnp.zeros(carry_scr.shape, F32)

    def gate_taps(c, gate):
        cs = slice(c * FF_CHUNK, (c + 1) * FF_CHUNK)
        p0 = carry_scr[SUBLANES - 2:SUBLANES - 1, cs]
        p1 = carry_scr[SUBLANES - 1:SUBLANES, cs]
        g1 = _shift_rows(gate, 1, [p1])
        g2 = _shift_rows(gate, 2, [p0, p1])
        carry_scr[:, cs] = gate[rows - SUBLANES:, :]
        tail_ref[0, :, cs] = gate[rows - SUBLANES:, :]
        return g2, g1

    _glu_hidden(a_scr, wup_ref, wc_ref, bc_ref, h_scr, gate_taps)
    y = x_ref[...] + jnp.dot(h_scr[...], wdn_ref[...], preferred_element_type=F32)
    out_ref[...] = _rms(y, gf_ref[...]) if final_norm else y


def _ffn_prompt(x, g, wup, wc, bc, wdn, gf, bsz, seq, final_norm):
    n = x.shape[0]
    assert seq % FFN_ROWS == 0
    nb = seq // FFN_ROWS
    row = pl.BlockSpec((FFN_ROWS, D_MODEL), lambda b, j: (b * nb + j, 0))
    (wup, lu), (wdn, ld) = wup, wdn
    return pl.pallas_call(
        functools.partial(_ffn_prompt_kernel, final_norm=final_norm),
        grid=(bsz, nb),
        in_specs=[row, _const_spec((1, D_MODEL)), _layer_spec(wup, lu), _const_spec(wc.shape),
                  _const_spec(bc.shape), _layer_spec(wdn, ld), _const_spec((1, D_MODEL))],
        out_specs=[row, pl.BlockSpec((1, SUBLANES, D_FF), lambda b, j: (b, 0, 0))],
        out_shape=[jax.ShapeDtypeStruct((n, D_MODEL), F32), jax.ShapeDtypeStruct((bsz, SUBLANES, D_FF), F32)],
        scratch_shapes=[pltpu.VMEM((FFN_ROWS, D_MODEL), BF16), pltpu.VMEM((FFN_ROWS, D_FF), BF16),
                        pltpu.VMEM((SUBLANES, D_FF), F32)],
        compiler_params=_cparams("arbitrary", "arbitrary"),
        name="ffn_prompt_final" if final_norm else "ffn_prompt",
    )(x, g, wup, wc, bc, wdn, gf)


def _sigmoid(x):
    return 0.5 * jnp.tanh(0.5 * x) + 0.5


def _lru_coeffs(xc, gg, bga, bgx, neg_c_softplus):
    r = _sigmoid(gg[:, :LANES] + bga)
    ig = _sigmoid(gg[:, LANES:] + bgx)
    log_a = r * neg_c_softplus
    a = jnp.exp(log_a)
    z = -jnp.tanh(log_a) * (a * a + 1.0)
    root = jnp.where(z > 0.0, z * lax.rsqrt(z), 0.0)
    return a, root * ig * xc


def _group_scan(a, b):
    shape = a.shape
    grouped = (shape[0] // SUBLANES, SUBLANES, shape[1])
    a, b = a.reshape(grouped), b.reshape(grouped)
    row = lax.broadcasted_iota(I32, grouped, 1)
    for d in (1, 2, 4):
        inside = row >= d
        a_prev = jnp.where(inside, pltpu.roll(a, d, 1), 1.0)
        b_prev = jnp.where(inside, pltpu.roll(b, d, 1), 0.0)
        b = a * b_prev + b
        a = a * a_prev
    return a.reshape(shape), b.reshape(shape)


def _rec_prompt_kernel(x_ref, g_ref, win_ref, wc_ref, bc_ref, wg_ref, bga_ref, bgx_ref, lam_ref, wo_ref,
                       out_ref, hlast_ref, ctail_ref, a_scr, at_scr, bt_scr, u_scr, hcar_scr, ccar_scr, xb_scr,
                       gelu_scr):
    j = pl.program_id(1)
    rows = x_ref.shape[0]
    a_scr[...] = _rms(x_ref[...], g_ref[...]).astype(BF16)

    @pl.when(j == 0)
    def _():
        ccar_scr[...] = jnp.zeros(ccar_scr.shape, F32)
        hcar_scr[...] = jnp.zeros(hcar_scr.shape, F32)

    nblk = MXU_N // LANES
    ncs = -RG_C * jax.nn.softplus(-lam_ref[...])
    for c in range(D_MODEL // MXU_N):
        cs = slice(c * MXU_N, (c + 1) * MXU_N)
        xb_scr[:, cs] = jnp.dot(a_scr[...], win_ref[:, D_MODEL + c * MXU_N:D_MODEL + (c + 1) * MXU_N],
                                preferred_element_type=F32)
    for c in range(D_MODEL // MXU_N):
        cs = slice(c * MXU_N, (c + 1) * MXU_N)
        gelu_scr[:, cs] = jax.nn.gelu(jnp.dot(a_scr[...], win_ref[:, cs], preferred_element_type=F32))
    for c in range(D_MODEL // MXU_N):
        cs = slice(c * MXU_N, (c + 1) * MXU_N)
        xb = xb_scr[:, cs]
        prev = [ccar_scr[SUBLANES - 3 + k:SUBLANES - 2 + k, cs] for k in range(3)]
        xc = (bc_ref[:, cs] + wc_ref[0:1, cs] * _shift_rows(xb, 3, prev) + wc_ref[1:2, cs] * _shift_rows(xb, 2, prev[1:])
              + wc_ref[2:3, cs] * _shift_rows(xb, 1, prev[2:]) + wc_ref[3:4, cs] * xb)
        ccar_scr[:, cs] = xb[rows - SUBLANES:, :]
        ctail_ref[0, :, cs] = xb[rows - SUBLANES:, :]
        for k in range(nblk):
            n = c * nblk + k
            ls = slice(n * LANES, (n + 1) * LANES)
            xcn = xc[:, k * LANES:(k + 1) * LANES]
            gg = jnp.dot(xcn.astype(BF16), wg_ref[n], preferred_element_type=F32)
            a_t, b_t = _group_scan(*_lru_coeffs(xcn, gg, bga_ref[:, ls], bgx_ref[:, ls], ncs[:, ls]))
            at_scr[:, ls] = a_t
            bt_scr[:, ls] = b_t

    def group_step(g, h):
        r = pl.ds(pl.multiple_of(g * SUBLANES, SUBLANES), SUBLANES)
        hg = at_scr[r, :] * h + bt_scr[r, :]
        bt_scr[r, :] = hg
        return hg[SUBLANES - 1:, :]

    h_last = lax.fori_loop(0, rows // SUBLANES, group_step, hcar_scr[0:1, :], unroll=4)
    hcar_scr[...] = jnp.broadcast_to(h_last, hcar_scr.shape)
    hlast_ref[0] = jnp.broadcast_to(h_last, hcar_scr.shape)
    u_scr[...] = (gelu_scr[...] * bt_scr[...]).astype(BF16)
    out_ref[...] = x_ref[...] + jnp.dot(u_scr[...], wo_ref[...], preferred_element_type=F32)


def _rec_prompt(x, g, win, wc, bc, wg, bga, bgx, lam, wo, bsz, seq):
    n = x.shape[0]
    nb = seq // ROW_BLOCK
    row = pl.BlockSpec((ROW_BLOCK, D_MODEL), lambda b, j: (b * nb + j, 0))
    tail = pl.BlockSpec((1, SUBLANES, D_MODEL), lambda b, j: (b, 0, 0))
    vec = _const_spec((1, D_MODEL))
    return pl.pallas_call(
        _rec_prompt_kernel,
        grid=(bsz, nb),
        in_specs=[row, vec, _const_spec(win.shape), _const_spec(wc.shape), vec, _const_spec(wg.shape), vec, vec,
                  vec, _const_spec(wo.shape)],
        out_specs=[row, tail, tail],
        out_shape=[jax.ShapeDtypeStruct((n, D_MODEL), F32), jax.ShapeDtypeStruct((bsz, SUBLANES, D_MODEL), F32),
                   jax.ShapeDtypeStruct((bsz, SUBLANES, D_MODEL), F32)],
        scratch_shapes=[pltpu.VMEM((ROW_BLOCK, D_MODEL), BF16), pltpu.VMEM((ROW_BLOCK, D_MODEL), F32),
                        pltpu.VMEM((ROW_BLOCK, D_MODEL), F32), pltpu.VMEM((ROW_BLOCK, D_MODEL), BF16),
                        pltpu.VMEM((SUBLANES, D_MODEL), F32), pltpu.VMEM((SUBLANES, D_MODEL), F32),
                        pltpu.VMEM((ROW_BLOCK, D_MODEL), F32), pltpu.VMEM((ROW_BLOCK, D_MODEL), F32)],
        compiler_params=_cparams("arbitrary", "arbitrary"),
        name="rec_prompt",
    )(x, g, win, wc, bc, wg, bga, bgx, lam, wo)


def _idx_sample_kernel(pt_ref, qi_ref, wi_ref, kin_ref, *refs, n_pages):
    pages, out_ref = refs[:-1], refs[-1]
    lane = lax.broadcasted_iota(I32, (1, LANES), 1)
    for g in range(IDX_SEQS):
        qi = qi_ref[g]
        wi = wi_ref[g]
        for p in range(n_pages):
            page_t = pages[g * n_pages + p][...].astype(BF16)
            s = jnp.dot(qi[:, :IDX_DIM], page_t, preferred_element_type=F32)
            out_ref[g, :, p * PAGE:(p + 1) * PAGE] = (
                jnp.sum(wi * jnp.maximum(s, 0.0), axis=0, keepdims=True) * IDX_SCALE)
        s_new = jnp.sum(qi.astype(F32) * kin_ref[g].astype(BF16).astype(F32), axis=-1, keepdims=True)
        sc_new = jnp.sum(wi * jnp.maximum(s_new, 0.0), axis=0, keepdims=True) * IDX_SCALE
        out_ref[g, :, n_pages * PAGE:] = jnp.where(lane == 0, sc_new, -jnp.inf)


def _idx_sample(pt, qi3, wi3, kin3, pool_ki_t, n_pages):
    db = qi3.shape[0]
    assert db % IDX_SEQS == 0
    per_step = lambda shape: pl.BlockSpec((IDX_SEQS,) + shape, lambda d, pt: (d, 0, 0))
    page_spec = lambda g, p: pl.BlockSpec((None, IDX_DIM, PAGE),
                                          lambda d, pt: (pt[(d * IDX_SEQS + g) * n_pages + p], 0, 0))
    width = n_pages * PAGE + LANES
    return pl.pallas_call(
        functools.partial(_idx_sample_kernel, n_pages=n_pages),
        grid_spec=pltpu.PrefetchScalarGridSpec(
            num_scalar_prefetch=1, grid=(db // IDX_SEQS,),
            in_specs=[per_step((IDX_HEADS, LANES)), per_step((IDX_HEADS, 1)), per_step((1, LANES))]
            + [page_spec(g, p) for g in range(IDX_SEQS) for p in range(n_pages)],
            out_specs=per_step((1, width))),
        out_shape=jax.ShapeDtypeStruct((db, 1, width), F32),
        compiler_params=_cparams("arbitrary"),
        name="idx_sample",
    )(pt, qi3, wi3, kin3, *([pool_ki_t] * (IDX_SEQS * n_pages)))


def _select_sample_kernel(sc_ref, rows_ref, ids_ref, bias_ref, key_scr, tri_scr, rank_scr, ids_scr, *, n_keys):
    lane = lax.broadcasted_iota(I32, (1, sc_ref.shape[1]), 1)
    key_scr[...] = jnp.where(lane < n_keys, _sortable_key(sc_ref[...]), jnp.int32(INT_MIN))

    def count_ge(cand):
        return jnp.sum(jnp.where(key_scr[...] >= cand, 1.0, 0.0), axis=-1, keepdims=True)

    thr, _ = _kth_largest_key(count_ge, float(sc_ref.shape[1]))
    need = float(TOPK) - count_ge(thr + 1)

    def count_tie_le(cm):
        return jnp.sum(jnp.where((key_scr[...] == thr) & (lane <= cm), 1.0, 0.0), axis=-1, keepdims=True)

    cut = _tie_cutoff(count_tie_le, need, 12)
    cut = jnp.where(thr == INT_MIN, jnp.int32(-1), cut)
    key = key_scr[...]
    sel = (key > thr) | ((key == thr) & (lane <= cut))

    past = n_keys - 1
    n_seq = sc_ref.shape[0]
    taken = jnp.where(sel, 1.0, 0.0)[:, :past]
    for r in range(past // TK):
        upper = (lax.broadcasted_iota(I32, (TK, past), 0) + r * TK) <= lax.broadcasted_iota(I32, (TK, past), 1)
        tri_scr[r * TK:(r + 1) * TK, :] = jnp.where(upper, 1.0, 0.0).astype(BF16)
    rank = jnp.dot(taken.astype(BF16), tri_scr[...], preferred_element_type=F32)
    rank_scr[...] = rank * taken
    n_taken = rank[:, past - 1:past]
    slot = (lax.broadcasted_iota(I32, (TOPK, 1), 0) + 1).astype(F32)
    seq_lane = lax.broadcasted_iota(I32, (1, n_seq), 1)
    ids_scr[...] = jnp.zeros(ids_scr.shape, F32)

    def seq_body(d, carry):
        hit = rank_scr[pl.ds(d, 1), :] == slot
        row = jnp.sum(jnp.where(hit, rows_ref[pl.ds(d, 1), :], 0.0), axis=-1, keepdims=True)
        ids_scr[...] += jnp.where(seq_lane == d, row, 0.0)
        return carry

    lax.fori_loop(0, n_seq, seq_body, 0)
    ids_ref[...] = ids_scr[...].T.astype(I32)
    blane = lax.broadcasted_iota(I32, bias_ref.shape, 1)
    new_taken = jnp.sum(jnp.where(sel & (lane == past), 1.0, 0.0), axis=-1, keepdims=True)
    live = (blane.astype(F32) < n_taken) | ((blane == TOPK) & (new_taken > 0.0))
    bias_ref[...] = jnp.where(live, 0.0, NEG)


def _select_sample(scores, cache_rows, n_keys):
    n_seq = scores.shape[0]
    past = n_keys - 1
    return pl.pallas_call(
        functools.partial(_select_sample_kernel, n_keys=n_keys),
        out_shape=[jax.ShapeDtypeStruct((n_seq, TOPK), I32), jax.ShapeDtypeStruct((n_seq, TOPK + LANES), F32)],
        scratch_shapes=[pltpu.VMEM(scores.shape, I32), pltpu.VMEM((past, past), BF16),
                        pltpu.VMEM((n_seq, past), F32), pltpu.VMEM((TOPK, n_seq), F32)],
        compiler_params=_cparams(),
        name="select_sample",
    )(scores, cache_rows)


def _gather_kv_rows(pool_k, pool_v, ids):
    n = ids.shape[0]
    mesh = plsc.VectorSubcoreMesh(core_axis_name="core", subcore_axis_name="subcore")
    n_workers = mesh.num_cores * mesh.num_subcores
    per_worker = n // n_workers
    assert n % (n_workers * SC_GATHER_ROWS) == 0
    row = pool_k.shape[1:]
    out = jax.ShapeDtypeStruct((n,) + row, pool_k.dtype)

    @functools.partial(
        pl.kernel, mesh=mesh, out_type=[out, out],
        scratch_types=[pltpu.VMEM((SC_GATHER_ROWS,), I32), pltpu.VMEM((SC_GATHER_ROWS,) + row, pool_k.dtype),
                       pltpu.VMEM((SC_GATHER_ROWS,) + row, pool_v.dtype), pltpu.SemaphoreType.DMA,
                       pltpu.SemaphoreType.DMA],
        compiler_params=pltpu.CompilerParams(use_tc_tiling_on_sc=True),
        name="gather_kv_rows",
    )
    def gather(k_hbm, v_hbm, ids_hbm, ko_hbm, vo_hbm, ids_v, k_v, v_v, ksem, vsem):
        worker = lax.axis_index("subcore") * mesh.num_cores + lax.axis_index("core")

        @pl.loop(0, per_worker // SC_GATHER_ROWS)
        def _(j):
            off = pl.multiple_of(worker * per_worker + j * SC_GATHER_ROWS, SC_GATHER_ROWS)
            pltpu.sync_copy(ids_hbm.at[pl.ds(off, SC_GATHER_ROWS)], ids_v)
            kcopy = pltpu.async_copy(k_hbm.at[ids_v], k_v, ksem)
            vcopy = pltpu.async_copy(v_hbm.at[ids_v], v_v, vsem)
            kcopy.wait()
            pltpu.sync_copy(k_v, ko_hbm.at[pl.ds(off, SC_GATHER_ROWS)])
            vcopy.wait()
            pltpu.sync_copy(v_v, vo_hbm.at[pl.ds(off, SC_GATHER_ROWS)])

    return gather(pool_k, pool_v, ids)


def _attn_sample_kernel(q_ref, kn_ref, vn_ref, bias_ref, kg_ref, vg_ref, o_ref, kall, vall):
    past_rows = kg_ref.shape[1]
    zeros = jnp.zeros((LANES - N_HEADS, HEAD_DIM), F32)
    for g in range(ATT_SEQS):
        kall[:past_rows, :] = kg_ref[g].astype(BF16)
        vall[:past_rows, :] = vg_ref[g].astype(BF16)
        kall[past_rows:, :] = jnp.concatenate([kn_ref[g], zeros], axis=0).astype(BF16)
        vall[past_rows:, :] = jnp.concatenate([vn_ref[g], zeros], axis=0).astype(BF16)
        s = lax.dot_general(q_ref[g], kall[...], NT_DIMS, preferred_element_type=F32)
        own = (lax.broadcasted_iota(I32, s.shape, 1) & (N_HEADS - 1)) == lax.broadcasted_iota(I32, s.shape, 0)
        s = jnp.where(own, s + bias_ref[g], NEG)
        m = jnp.max(s, axis=-1, keepdims=True)
        p = jnp.exp2(s - m)
        l = jnp.sum(p, axis=-1, keepdims=True)
        o_ref[g] = jnp.dot(p.astype(BF16), vall[...], preferred_element_type=F32) / l


def _attn_sample(q3, kn3, vn3, bias3, kg, vg):
    db, past_rows, _ = kg.shape
    width = past_rows + LANES
    assert db % ATT_SEQS == 0 and bias3.shape == (db, 1, width)
    per_step = lambda r, w: pl.BlockSpec((ATT_SEQS, r, w), lambda d: (d, 0, 0))
    return pl.pallas_call(
        _attn_sample_kernel,
        grid=(db // ATT_SEQS,),
        in_specs=[per_step(N_HEADS, HEAD_DIM)] * 3 + [per_step(1, width)] + [per_step(past_rows, HEAD_DIM)] * 2,
        out_specs=per_step(N_HEADS, HEAD_DIM),
        out_shape=jax.ShapeDtypeStruct((db, N_HEADS, HEAD_DIM), F32),
        scratch_shapes=[pltpu.VMEM((width, HEAD_DIM), BF16), pltpu.VMEM((width, HEAD_DIM), BF16)],
        compiler_params=_cparams("arbitrary"),
        name="attn_sample",
    )(q3, kn3, vn3, bias3, kg, vg)


def _ffn_sample_body(x, g_ref, wup_ref, wc_ref, bc_ref, wdn_ref, st0_ref, st1_ref, gate_ref, a_scr, h_scr):
    a_scr[...] = _rms(x, g_ref[...]).astype(BF16)

    def gate_taps(c, gate):
        cs = slice(c * FF_CHUNK, (c + 1) * FF_CHUNK)
        gate_ref[:, cs] = gate
        return st0_ref[:, cs], st1_ref[:, cs]

    _glu_hidden(a_scr, wup_ref, wc_ref, bc_ref, h_scr, gate_taps)
    return x + jnp.dot(h_scr[...], wdn_ref[...], preferred_element_type=F32)


def _layer0_sample_kernel(x_ref, o_ref, wo_ref, g_ref, wup_ref, wc_ref, bc_ref, wdn_ref, st0_ref, st1_ref,
                          out_ref, gate_ref, a_scr, h_scr):
    x = x_ref[...] + jnp.dot(o_ref[...].astype(BF16), wo_ref[...], preferred_element_type=F32)
    out_ref[...] = _ffn_sample_body(x, g_ref, wup_ref, wc_ref, bc_ref, wdn_ref, st0_ref, st1_ref, gate_ref,
                                    a_scr, h_scr)


def _layer1_sample_kernel(x_ref, gm_ref, win_ref, wcr_ref, bcr_ref, wg_ref, bga_ref, bgx_ref, lam_ref, wor_ref,
                          cs0_ref, cs1_ref, cs2_ref, h0_ref, g_ref, wup_ref, wc_ref, bc_ref, wdn_ref, st0_ref,
                          st1_ref, gf_ref, out_ref, hnew_ref, xb_ref, gate_ref, a_scr, h_scr, u_scr):
    x = x_ref[...]
    a_scr[...] = _rms(x, gm_ref[...]).astype(BF16)
    nblk = MXU_N // LANES
    ncs = -RG_C * jax.nn.softplus(-lam_ref[...])
    for c in range(D_MODEL // MXU_N):
        cs = slice(c * MXU_N, (c + 1) * MXU_N)
        xb = jnp.dot(a_scr[...], win_ref[:, D_MODEL + c * MXU_N:D_MODEL + (c + 1) * MXU_N],
                     preferred_element_type=F32)
        gate = jnp.dot(a_scr[...], win_ref[:, cs], preferred_element_type=F32)
        xb_ref[:, cs] = xb
        xc = (bcr_ref[:, cs] + wcr_ref[0:1, cs] * cs0_ref[:, cs] + wcr_ref[1:2, cs] * cs1_ref[:, cs]
              + wcr_ref[2:3, cs] * cs2_ref[:, cs] + wcr_ref[3:4, cs] * xb)
        for k in range(nblk):
            n = c * nblk + k
            ls = slice(n * LANES, (n + 1) * LANES)
            xcn = xc[:, k * LANES:(k + 1) * LANES]
            gg = jnp.dot(xcn.astype(BF16), wg_ref[n], preferred_element_type=F32)
            a_t, b_t = _lru_coeffs(xcn, gg, bga_ref[:, ls], bgx_ref[:, ls], ncs[:, ls])
            h = a_t * h0_ref[:, ls] + b_t
            hnew_ref[:, ls] = h
            u_scr[:, ls] = (jax.nn.gelu(gate[:, k * LANES:(k + 1) * LANES]) * h).astype(BF16)
    x = x + jnp.dot(u_scr[...], wor_ref[...], preferred_element_type=F32)
    y = _ffn_sample_body(x, g_ref, wup_ref, wc_ref, bc_ref, wdn_ref, st0_ref, st1_ref, gate_ref, a_scr, h_scr)
    out_ref[...] = _rms(y, gf_ref[...])


def _layer_spec(stacked, layer):
    nd = stacked.ndim - 1
    return pl.BlockSpec((None,) + stacked.shape[1:], lambda *_: (layer,) + (0,) * nd, pipeline_mode=pl.Buffered(1))


def _sample_call(kernel, name, args, out_widths, extra_scratch=()):
    db = args[0].shape[0]
    specs = [_layer_spec(*a) if isinstance(a, tuple) else _const_spec(a.shape) for a in args]
    arrays = [a[0] if isinstance(a, tuple) else a for a in args]
    return pl.pallas_call(
        kernel,
        grid=(1,),
        in_specs=specs,
        out_specs=[pl.BlockSpec((db, w), lambda i: (0, 0)) for w in out_widths],
        out_shape=[jax.ShapeDtypeStruct((db, w), F32) for w in out_widths],
        scratch_shapes=[pltpu.VMEM((db, D_MODEL), BF16), pltpu.VMEM((db, D_FF), BF16), *extra_scratch],
        compiler_params=_cparams("arbitrary"),
        name=name,
    )(*arrays)


def _rope_tables(pos):
    posf = pos.astype(F32)[:, None]

    def cs(d):
        half = d // 2
        inv = ROPE_THETA ** (-jnp.arange(half, dtype=F32) * 2.0 / d)
        ang = posf * inv[None, :]
        return jnp.cos(ang), jnp.sin(ang)

    c, s = cs(HEAD_DIM)
    cos = jnp.concatenate([c, c], axis=-1)
    sin = jnp.concatenate([-s, s], axis=-1)
    c, s = cs(IDX_DIM)
    one, zero = jnp.ones_like(c), jnp.zeros_like(c)
    tile = lambda parts: jnp.concatenate(parts, axis=-1)
    icos = tile([c, c, one, one])
    isa = tile([-s, zero, zero, zero])
    isb = tile([zero, s, zero, zero])
    return cos, sin, icos, isa, isb


def _split_attn_in(w):
    qkv = N_HEADS * HEAD_DIM
    wqk, wv, wi = w[:, :2 * qkv], w[:, 2 * qkv:3 * qkv], w[:, 3 * qkv:]
    pad = lambda a: jnp.pad(a, ((0, 0), (0, LANES - a.shape[1])))
    groups = [pad(wi[:, h * IDX_DIM:(h + 1) * IDX_DIM]) for h in range(IDX_HEADS)]
    groups.append(pad(wi[:, IDX_HEADS * IDX_DIM:]))
    return wqk.astype(BF16), wv.astype(BF16), jnp.concatenate(groups, axis=-1).astype(BF16)


def kernel(x_prompt, x_sample, cache_k, cache_v, cache_kidx, state_lru_h, state_lru_conv, state_ffn_conv,
           page_table, norm_mix, norm_ffn, norm_final, w_attn_in, w_attn_out, w_rec_in, w_rec_conv, b_rec_conv,
           w_gate_a, b_gate_a, w_gate_x, b_gate_x, lru_lambda, w_rec_out, w_ffn_up, w_ffn_conv, b_ffn_conv,
           w_ffn_down):
    bsz, seq, d = x_prompt.shape
    db = x_sample.shape[0]
    n_pages = page_table.shape[1]
    past = n_pages * PAGE
    assert d == D_MODEL and x_sample.shape[1] == 1 and seq % ROW_BLOCK == 0 and seq % TQ == 0
    assert min(TOPK, seq // 4) == TOPK and min(TOPK, (past + 1) // 4) == TOPK

    vec = lambda a: a.reshape(1, -1)
    wqk, wv, widx = _split_attn_in(w_attn_in[0])
    wo_attn = w_attn_out[0].astype(BF16)
    w_rin = w_rec_in[0].astype(BF16)
    w_gates = jnp.concatenate([w_gate_a[0], w_gate_x[0]], axis=-1).astype(BF16)
    wo_rec = w_rec_out[0].astype(BF16)
    wup_all, wdn_all = w_ffn_up.astype(BF16), w_ffn_down.astype(BF16)
    wup = [(wup_all, i) for i in range(2)]
    wdn = [(wdn_all, i) for i in range(2)]
    rec_vecs = (w_rec_conv[0], vec(b_rec_conv[0]), w_gates, vec(b_gate_a[0]), vec(b_gate_x[0]),
                vec(lru_lambda[0]), wo_rec)
    ffn_vecs = [(vec(norm_ffn[i]), wup[i], w_ffn_conv[i], vec(b_ffn_conv[i]), wdn[i]) for i in range(2)]
    gfin = vec(norm_final)

    xp = x_prompt.reshape(bsz * seq, d)
    q, kf, kb, vf, vt, qi, kiw = _attn_in(xp, vec(norm_mix[0]), wqk, wv, widx, _rope_tables(jnp.arange(seq)),
                                          ROW_BLOCK, seq // ROW_BLOCK)
    wit = kiw[:, IDX_DIM:IDX_DIM + IDX_HEADS].T
    hp_attn = _dsa_prompt(q, kb, vt, qi, kiw, wit, xp, wo_attn, bsz, seq)
    hp, ftail0 = _ffn_prompt(hp_attn, *ffn_vecs[0], gfin, bsz, seq, False)
    hp, hlast, ctail = _rec_prompt(hp, vec(norm_mix[1]), w_rin, *rec_vecs, bsz, seq)
    yp, ftail1 = _ffn_prompt(hp, *ffn_vecs[1], gfin, bsz, seq, True)

    xs = x_sample.reshape(db, d)
    tabs = tuple(jnp.broadcast_to(t, (db, t.shape[1])) for t in _rope_tables(jnp.full((1,), past)))
    qs, kfs, _, vfs, _, qis, kiws = _attn_in(xs, vec(norm_mix[0]), wqk, wv, widx, tabs, db, 1)
    pt = page_table.reshape(-1)
    scores = _idx_sample(pt, qis.reshape(db, IDX_HEADS, LANES),
                         kiws[:, IDX_DIM:IDX_DIM + IDX_HEADS].reshape(db, IDX_HEADS, 1),
                         kiws.reshape(db, 1, LANES), jnp.swapaxes(cache_kidx[0], 1, 2), n_pages)
    cache_rows = (page_table[:, :, None] * PAGE + jnp.arange(PAGE, dtype=I32)).reshape(db, past).astype(F32)
    ids, bias = _select_sample(scores.reshape(db, -1), cache_rows, past + 1)
    heads = (N_HEADS, HEAD_DIM)
    key_rows = lambda pool: pool[0].reshape(-1, *heads)
    ids, _ = lax.optimization_barrier((ids, kb))
    kg, vg = _gather_kv_rows(key_rows(cache_k), key_rows(cache_v), ids.reshape(-1))
    bias_rows = jnp.pad(jnp.repeat(bias[:, :TOPK + 1], N_HEADS, axis=1), ((0, 0), (0, LANES - N_HEADS)),
                        constant_values=NEG)
    bias_rows, _ = lax.optimization_barrier((bias_rows, hp_attn))
    seq_rows = lambda g: g.reshape(db, TOPK * N_HEADS, HEAD_DIM)
    o_s = _attn_sample(qs.reshape(db, *heads), kfs.reshape(db, *heads), vfs.reshape(db, *heads),
                       bias_rows.reshape(db, 1, -1), seq_rows(kg), seq_rows(vg))
    st = state_ffn_conv
    hs, gate0 = _sample_call(_layer0_sample_kernel, "layer0_sample",
                             (xs, o_s.reshape(db, d), wo_attn, *ffn_vecs[0], st[0, :, 0], st[0, :, 1]),
                             (D_MODEL, D_FF))
    cst = state_lru_conv[0]
    ys, hnew, xbs, gate1 = _sample_call(
        _layer1_sample_kernel, "layer1_sample",
        (hs, vec(norm_mix[1]), w_rin, *rec_vecs, cst[:, 0], cst[:, 1], cst[:, 2], state_lru_h[0], *ffn_vecs[1],
         st[1, :, 0], st[1, :, 1], gfin),
        (D_MODEL, D_MODEL, D_MODEL, D_FF), extra_scratch=(pltpu.VMEM((db, D_MODEL), BF16),))

    return (
        yp.reshape(bsz, seq, d),
        ys.reshape(db, 1, d),
        kf.reshape(1, bsz, seq, *heads),
        vf.reshape(1, bsz, seq, *heads),
        kiw[:, :IDX_DIM].reshape(1, bsz, seq, IDX_DIM),
        kfs.reshape(1, db, 1, *heads),
        vfs.reshape(1, db, 1, *heads),
        kiws[:, :IDX_DIM].reshape(1, db, 1, IDX_DIM),
        hlast[None, :, 0, :],
        ctail[None, :, SUBLANES - 3:, :],
        hnew[None],
        jnp.stack([cst[:, 1], cst[:, 2], xbs], axis=1)[None],
        jnp.stack([ftail0[:, SUBLANES - 2:], ftail1[:, SUBLANES - 2:]]),
        jnp.stack([jnp.stack([st[0, :, 1], gate0], axis=1), jnp.stack([st[1, :, 1], gate1], axis=1)]),
    )
```

```python
import functools

import jax
import jax.numpy as jnp
from jax import lax
from jax.experimental import pallas as pl
from jax.experimental.pallas import tpu as pltpu
from jax.experimental.pallas import tpu_sc as plsc

F32 = jnp.float32
BF16 = jnp.bfloat16
I32 = jnp.int32
I16 = jnp.int16

D_MODEL = 1024
N_HEADS = 8
HEAD_DIM = 128
IDX_HEADS = 4
IDX_DIM = 64
TOPK = 256
PAGE = 128
ROPE_THETA = 10000.0
IDX_SCALE = (IDX_DIM * IDX_HEADS) ** -0.5
Q_SCALE = HEAD_DIM ** -0.5 * 1.4426950408889634
RG_C = 8.0
D_FF = 2816
RMS_EPS = 1e-6

LANES = 128
SUBLANES = 8
MXU_N = 256
IDX_W = (IDX_HEADS + 1) * LANES
INT_MIN = -(2 ** 31)
NEG = -1e30
VMEM_LIMIT = 52 * 1024 * 1024

ROW_BLOCK = 512
FFN_ROWS = 1024
TQ = 256
TK = 256
COUNT_ROWS = 32
COUNT_ROWS_16 = 64
IDX_SEQS = 8
ATT_SEQS = 4
SC_GATHER_ROWS = 32
FF_CHUNK = 256
N_FF_CHUNKS = D_FF // FF_CHUNK
NT_DIMS = (((1,), (1,)), ((), ()))


def _cparams(*sem):
    return pltpu.CompilerParams(dimension_semantics=sem if sem else None, vmem_limit_bytes=VMEM_LIMIT)


def _const_spec(shape):
    nd = len(shape)
    return pl.BlockSpec(shape, lambda *_: (0,) * nd, pipeline_mode=pl.Buffered(1))


def _rms(x, g):
    return x * lax.rsqrt(jnp.mean(x * x, axis=-1, keepdims=True) + RMS_EPS) * g


def _shift_rows(x, s, prev_rows):
    r = pltpu.roll(x, s, 0)
    top = r[:SUBLANES]
    row = lax.broadcasted_iota(I32, top.shape, 0)
    for k in range(s):
        top = jnp.where(row == k, prev_rows[k], top)
    return jnp.concatenate([top, r[SUBLANES:]], axis=0)


def _sortable_key(score):
    bits = pltpu.bitcast(score + 0.0, I32)
    return jnp.where(bits < 0, bits ^ jnp.int32(0x7FFFFFFF), bits)


def _attn_in_kernel(x_ref, g_ref, wqk_ref, wv_ref, widx_ref, cos_ref, sin_ref, icos_ref, isa_ref, isb_ref,
                    q_ref, kf_ref, kb_ref, vf_ref, vt_ref, qi_ref, kiw_ref, a_scr):
    a_scr[...] = _rms(x_ref[...], g_ref[...]).astype(BF16)
    cos = cos_ref[...]
    sin = sin_ref[...]
    for c in range(2 * D_MODEL // MXU_N):
        r = jnp.dot(a_scr[...], wqk_ref[:, c * MXU_N:(c + 1) * MXU_N], preferred_element_type=F32)
        for hh in range(MXU_N // HEAD_DIM):
            xh = r[:, hh * HEAD_DIM:(hh + 1) * HEAD_DIM]
            y = xh * cos + pltpu.roll(xh, HEAD_DIM // 2, 1) * sin
            col = (c * MXU_N) % D_MODEL + hh * HEAD_DIM
            if c < D_MODEL // MXU_N:
                q_ref[:, col:col + HEAD_DIM] = (y * Q_SCALE).astype(BF16)
            else:
                kf_ref[:, col:col + HEAD_DIM] = y
                kb_ref[:, col:col + HEAD_DIM] = y.astype(BF16)
    for c in range(D_MODEL // MXU_N):
        r = jnp.dot(a_scr[...], wv_ref[:, c * MXU_N:(c + 1) * MXU_N], preferred_element_type=F32)
        vf_ref[:, c * MXU_N:(c + 1) * MXU_N] = r
        vt_ref[c * MXU_N:(c + 1) * MXU_N, :] = r.T.astype(BF16)
    ri = jnp.dot(a_scr[...], widx_ref[...], preferred_element_type=F32)
    groups = lambda t: jnp.concatenate([t] * (IDX_W // LANES), axis=1)
    yi = (ri * groups(icos_ref[...]) + pltpu.roll(ri, IDX_W - IDX_DIM // 2, 1) * groups(isa_ref[...])
          + pltpu.roll(ri, IDX_DIM // 2, 1) * groups(isb_ref[...]))
    qi_ref[...] = yi[:, :IDX_HEADS * LANES].astype(BF16)
    kiw_ref[...] = yi[:, IDX_HEADS * LANES:]


def _attn_in(x, g, wqk, wv, widx, tabs, rows, n_pos_blocks):
    n = x.shape[0]
    cos, sin, icos, isa, isb = tabs
    row_spec = lambda w: pl.BlockSpec((rows, w), lambda i: (i, 0))
    tab_spec = lambda w: pl.BlockSpec((rows, w), lambda i: (i % n_pos_blocks, 0))
    vt_spec = pl.BlockSpec((None, D_MODEL, rows), lambda i: (i // n_pos_blocks, 0, i % n_pos_blocks))
    return pl.pallas_call(
        _attn_in_kernel,
        grid=(n // rows,),
        in_specs=[row_spec(D_MODEL), _const_spec((1, D_MODEL)), _const_spec(wqk.shape), _const_spec(wv.shape),
                  _const_spec(widx.shape), tab_spec(HEAD_DIM), tab_spec(HEAD_DIM), tab_spec(LANES),
                  tab_spec(LANES), tab_spec(LANES)],
        out_specs=[row_spec(D_MODEL)] * 4 + [vt_spec, row_spec(IDX_HEADS * LANES), row_spec(LANES)],
        out_shape=[jax.ShapeDtypeStruct((n, D_MODEL), BF16), jax.ShapeDtypeStruct((n, D_MODEL), F32),
                   jax.ShapeDtypeStruct((n, D_MODEL), BF16), jax.ShapeDtypeStruct((n, D_MODEL), F32),
                   jax.ShapeDtypeStruct((n // (rows * n_pos_blocks), D_MODEL, rows * n_pos_blocks), BF16),
                   jax.ShapeDtypeStruct((n, IDX_HEADS * LANES), BF16),
                   jax.ShapeDtypeStruct((n, LANES), F32)],
        scratch_shapes=[pltpu.VMEM((rows, D_MODEL), BF16)],
        compiler_params=_cparams("arbitrary"),
        name="attn_in",
    )(x, g, wqk, wv, widx, cos, sin, icos, isa, isb)


def _kth_largest_key(count_ge, n_total):
    kf = float(TOPK)
    c0 = count_ge(0)
    t = jnp.where(c0 >= kf, jnp.int32(0), jnp.int32(INT_MIN))
    ct = jnp.where(c0 >= kf, c0, n_total)

    def bit_body(it, carry):
        t, ct = carry
        cand = t + (jnp.int32(1) << (30 - it))
        cnt = count_ge(cand)
        return jnp.where(cnt >= kf, cand, t), jnp.where(cnt >= kf, cnt, ct)

    return lax.fori_loop(0, 31, bit_body, (t, ct))


def _kth_largest_half(count_ge, above, count_all):
    kf = float(TOPK)
    c0 = above + count_ge(0)
    t = jnp.where(c0 >= kf, jnp.int32(0), jnp.int32(-(2 ** 15)))
    ct = jnp.where(c0 >= kf, c0, count_all)

    def bit_body(it, carry):
        t, ct = carry
        cand = t + (jnp.int32(1) << (14 - it))
        cnt = above + count_ge(cand)
        return jnp.where(cnt >= kf, cand, t), jnp.where(cnt >= kf, cnt, ct)

    return lax.fori_loop(0, 15, bit_body, (t, ct))


def _tie_cutoff(count_tie_le, need, n_bits):
    def bit_body(it, c):
        cand = c + (jnp.int32(1) << (n_bits - 1 - it))
        return jnp.where(count_tie_le(cand - 1) < need, cand, c)

    return lax.fori_loop(0, n_bits, bit_body, jnp.zeros_like(need, dtype=I32))


def _dsa_prompt_kernel(q_ref, kb_ref, vt_ref, qi_ref, qin_ref, kiwk_ref, wit_ref, witn_ref, x_ref, wo_ref, out_ref,
                       key_scr, bias_scr, kib_scr, o_scr, acc_scr, s_scr, hi_scr, lo_scr):
    i = pl.program_id(1)
    nk = i + 1
    lane_q = lax.broadcasted_iota(I32, (1, TQ), 1)

    def rows(kc):
        return pl.ds(kc * TK if isinstance(kc, int) else pl.multiple_of(kc * TK, TK), TK)

    def spos(kc):
        return kc * TK + lax.broadcasted_iota(I32, (TK, 1), 0)

    def colsum(a):
        return jnp.sum(a.reshape(TK // COUNT_ROWS, COUNT_ROWS, TQ), axis=0)

    def score_chunk(kc, qidx_ref, wis, qpos):
        kic = kib_scr[rows(kc), :]
        sc = jnp.zeros((TK, TQ), F32)
        for h in range(IDX_HEADS):
            s = lax.dot_general(kic, qidx_ref[:, h * LANES:(h + 1) * LANES], NT_DIMS, preferred_element_type=F32)
            sc = sc + wis[h:h + 1, :] * jnp.maximum(s, 0.0)
        key = _sortable_key(sc)
        if qpos is not None:
            key = jnp.where(spos(kc) <= qpos, key, jnp.int32(INT_MIN))
        key_scr[rows(kc), :] = key
        hi_scr[rows(kc), :] = (key >> 16).astype(I16)
        lo_scr[rows(kc), :] = ((key & 0xFFFF) - 2 ** 15).astype(I16)

    @pl.when(i == 0)
    def _():
        kib_scr[...] = kiwk_ref[...].astype(BF16)
        score_chunk(0, qi_ref, wit_ref[...] * IDX_SCALE, lane_q)

    wis_next = witn_ref[...] * IDX_SCALE

    def count_ge(cand):
        def body(kc, acc):
            return acc + colsum(jnp.where(key_scr[rows(kc), :] >= cand, 1.0, 0.0))
        acc = lax.fori_loop(0, nk, body, jnp.zeros((COUNT_ROWS, TQ), F32))
        return jnp.sum(acc, axis=0, keepdims=True)

    def count_ge_half(half_scr):
        def count(cand):
            c16 = jnp.asarray(cand, I32).astype(I16)

            def body(kc, acc):
                ind = jnp.where(half_scr[rows(kc), :] >= c16, jnp.int16(1), jnp.int16(0))
                for r in range(TK // COUNT_ROWS_16):
                    acc = acc + ind[r * COUNT_ROWS_16:(r + 1) * COUNT_ROWS_16]
                return acc

            acc = lax.fori_loop(0, nk, body, jnp.zeros((COUNT_ROWS_16, TQ), I16))
            return jnp.sum(acc.astype(F32), axis=0, keepdims=True)
        return count

    count_hi, count_lo = count_ge_half(hi_scr), count_ge_half(lo_scr)
    thi, cnt_hi = _kth_largest_half(count_hi, 0.0, (nk * TK).astype(F32))
    above = jnp.where(thi == 2 ** 15 - 1, 0.0, count_hi(jnp.minimum(thi + 1, 2 ** 15 - 1)))
    thi16 = thi.astype(I16)

    def lo_body(kc, carry):
        lo_scr[rows(kc), :] = jnp.where(hi_scr[rows(kc), :] == thi16, lo_scr[rows(kc), :], jnp.int16(-(2 ** 15)))
        return carry

    lax.fori_loop(0, nk, lo_body, 0)
    tlo, cnt_thr = _kth_largest_half(count_lo, above, cnt_hi)
    thr = (thi << 16) | (tlo + 2 ** 15)
    short = thr == INT_MIN
    has_ties = jnp.max(jnp.where(short, 0.0, cnt_thr)) > float(TOPK)

    @pl.when(jnp.logical_not(has_ties))
    def _():
        thr_vis = jnp.maximum(thr, jnp.int32(INT_MIN + 1))

        def bias_body(kc, carry):
            bias_scr[rows(kc), :] = jnp.where(key_scr[rows(kc), :] >= thr_vis, 0.0, NEG)
            return carry

        lax.fori_loop(0, nk, bias_body, 0)

    @pl.when(has_ties)
    def _():
        need = float(TOPK) - count_ge(thr + 1)

        def eq_body(kc, carry):
            bias_scr[rows(kc), :] = jnp.where(key_scr[rows(kc), :] == thr, 1.0, 0.0)
            return carry

        lax.fori_loop(0, nk, eq_body, 0)

        def count_tie_le(cm):
            def body(kc, acc):
                return acc + colsum(jnp.where(spos(kc) <= cm, bias_scr[rows(kc), :], 0.0))
            acc = lax.fori_loop(0, nk, body, jnp.zeros((COUNT_ROWS, TQ), F32))
            return jnp.sum(acc, axis=0, keepdims=True)

        cut = jnp.where(short, jnp.int32(-1), _tie_cutoff(count_tie_le, need, 11))

        def bias_body(kc, carry):
            sel = (key_scr[rows(kc), :] > thr) | ((bias_scr[rows(kc), :] > 0.0) & (spos(kc) <= cut))
            bias_scr[rows(kc), :] = jnp.where(sel, 0.0, NEG)
            return carry

        lax.fori_loop(0, nk, bias_body, 0)

    acc_scr[...] = jnp.zeros(acc_scr.shape, F32)

    def att_body(kc, carry):
        ms, ls = carry
        for h in range(N_HEADS):
            hs = slice(h * HEAD_DIM, (h + 1) * HEAD_DIM)
            s_scr[h] = lax.dot_general(kb_ref[rows(kc), hs], q_ref[:, hs], NT_DIMS, preferred_element_type=F32)
        bias = bias_scr[rows(kc), :]
        score_chunk(kc, qin_ref, wis_next, None)
        new_ms, new_ls = [], []
        for h in range(N_HEADS):
            hs = slice(h * HEAD_DIM, (h + 1) * HEAD_DIM)
            s = s_scr[h] + bias
            m_new = jnp.maximum(ms[h], jnp.max(s, axis=0, keepdims=True))
            alpha = jnp.exp2(ms[h] - m_new)
            p = jnp.exp2(s - m_new)
            new_ms.append(m_new)
            new_ls.append(alpha * ls[h] + jnp.sum(p, axis=0, keepdims=True))
            pv = jnp.dot(vt_ref[hs, rows(kc)], p.astype(BF16), preferred_element_type=F32)
            acc_scr[hs, :] = alpha * acc_scr[hs, :] + pv
        return tuple(new_ms), tuple(new_ls)

    init = ((jnp.full((1, TQ), NEG, F32),) * N_HEADS, (jnp.zeros((1, TQ), F32),) * N_HEADS)
    _, ls = lax.fori_loop(0, nk, att_body, init)

    @pl.when(i < pl.num_programs(1) - 1)
    def _():
        score_chunk(nk, qin_ref, wis_next, (i + 1) * TQ + lane_q)

    for h in range(N_HEADS):
        hs = slice(h * HEAD_DIM, (h + 1) * HEAD_DIM)
        o_scr[:, hs] = (acc_scr[hs, :] / ls[h]).T.astype(BF16)
    out_ref[...] = x_ref[...] + jnp.dot(o_scr[...], wo_ref[...], preferred_element_type=F32)


def _dsa_prompt(q, kb, vt, qi, kiw, wit, x, wo, bsz, seq):
    n = x.shape[0]
    nq = seq // TQ
    qrow = lambda w: pl.BlockSpec((TQ, w), lambda b, i: (b * nq + i, 0))
    brow = lambda w: pl.BlockSpec((seq, w), lambda b, i: (b, 0))
    nxt = lambda b, i: b * nq + jnp.minimum(i + 1, nq - 1)
    return pl.pallas_call(
        _dsa_prompt_kernel,
        grid=(bsz, nq),
        in_specs=[qrow(D_MODEL), brow(D_MODEL), pl.BlockSpec((None, D_MODEL, seq), lambda b, i: (b, 0, 0)),
                  qrow(IDX_HEADS * LANES), pl.BlockSpec((TQ, IDX_HEADS * LANES), lambda b, i: (nxt(b, i), 0)),
                  brow(LANES), pl.BlockSpec((IDX_HEADS, TQ), lambda b, i: (0, b * nq + i)),
                  pl.BlockSpec((IDX_HEADS, TQ), lambda b, i: (0, nxt(b, i))),
                  qrow(D_MODEL), _const_spec(wo.shape)],
        out_specs=qrow(D_MODEL),
        out_shape=jax.ShapeDtypeStruct((n, D_MODEL), F32),
        scratch_shapes=[pltpu.VMEM((seq, TQ), I32), pltpu.VMEM((seq, TQ), F32), pltpu.VMEM((seq, LANES), BF16),
                        pltpu.VMEM((TQ, D_MODEL), BF16), pltpu.VMEM((D_MODEL, TQ), F32),
                        pltpu.VMEM((N_HEADS, TK, TQ), F32), pltpu.VMEM((seq, TQ), I16),
                        pltpu.VMEM((seq, TQ), I16)],
        compiler_params=_cparams("arbitrary", "arbitrary"),
        name="dsa_prompt",
    )(q, kb, vt, qi, qi, kiw, wit, wit, x, wo)


def _glu_hidden(a_scr, wup_ref, wc_ref, bc_ref, h_scr, gate_taps):
    for c in range(N_FF_CHUNKS):
        cs = slice(c * FF_CHUNK, (c + 1) * FF_CHUNK)
        gate = jnp.dot(a_scr[...], wup_ref[:, cs], preferred_element_type=F32)
        val = jnp.dot(a_scr[...], wup_ref[:, D_FF + c * FF_CHUNK:D_FF + (c + 1) * FF_CHUNK],
                      preferred_element_type=F32)
        g2, g1 = gate_taps(c, gate)
        gc = bc_ref[:, cs] + wc_ref[0:1, cs] * g2 + wc_ref[1:2, cs] * g1 + wc_ref[2:3, cs] * gate
        h_scr[:, cs] = (jax.nn.gelu(gc) * val).astype(BF16)


def _ffn_prompt_kernel(x_ref, g_ref, wup_ref, wc_ref, bc_ref, wdn_ref, gf_ref, out_ref, tail_ref,
                       a_scr, h_scr, carry_scr, *, final_norm):
    j = pl.program_id(1)
    rows = x_ref.shape[0]
    a_scr[...] = _rms(x_ref[...], g_ref[...]).astype(BF16)

    @pl.when(j == 0)
    def _():
        carry_scr[...] = jnp.zeros(carry_scr.shape, F32)

    def gate_taps(c, gate):
        cs = slice(c * FF_CHUNK, (c + 1) * FF_CHUNK)
        p0 = carry_scr[SUBLANES - 2:SUBLANES - 1, cs]
        p1 = carry_scr[SUBLANES - 1:SUBLANES, cs]
        g1 = _shift_rows(gate, 1, [p1])
        g2 = _shift_rows(gate, 2, [p0, p1])
        carry_scr[:, cs] = gate[rows - SUBLANES:, :]
        tail_ref[0, :, cs] = gate[rows - SUBLANES:, :]
        return g2, g1

    _glu_hidden(a_scr, wup_ref, wc_ref, bc_ref, h_scr, gate_taps)
    y = x_ref[...] + jnp.dot(h_scr[...], wdn_ref[...], preferred_element_type=F32)
    out_ref[...] = _rms(y, gf_ref[...]) if final_norm else y


def _ffn_prompt(x, g, wup, wc, bc, wdn, gf, bsz, seq, final_norm):
    n = x.shape[0]
    assert seq % FFN_ROWS == 0
    nb = seq // FFN_ROWS
    row = pl.BlockSpec((FFN_ROWS, D_MODEL), lambda b, j: (b * nb + j, 0))
    return pl.pallas_call(
        functools.partial(_ffn_prompt_kernel, final_norm=final_norm),
        grid=(bsz, nb),
        in_specs=[row, _const_spec((1, D_MODEL)), _const_spec(wup.shape), _const_spec(wc.shape),
                  _const_spec(bc.shape), _const_spec(wdn.shape), _const_spec((1, D_MODEL))],
        out_specs=[row, pl.BlockSpec((1, SUBLANES, D_FF), lambda b, j: (b, 0, 0))],
        out_shape=[jax.ShapeDtypeStruct((n, D_MODEL), F32), jax.ShapeDtypeStruct((bsz, SUBLANES, D_FF), F32)],
        scratch_shapes=[pltpu.VMEM((FFN_ROWS, D_MODEL), BF16), pltpu.VMEM((FFN_ROWS, D_FF), BF16),
                        pltpu.VMEM((SUBLANES, D_FF), F32)],
        compiler_params=_cparams("arbitrary", "arbitrary"),
        name="ffn_prompt_final" if final_norm else "ffn_prompt",
    )(x, g, wup, wc, bc, wdn, gf)


def _sigmoid(x):
    return 0.5 * jnp.tanh(0.5 * x) + 0.5


def _lru_coeffs(xc, gg, bga, bgx, neg_c_softplus):
    r = _sigmoid(gg[:, :LANES] + bga)
    ig = _sigmoid(gg[:, LANES:] + bgx)
    log_a = r * neg_c_softplus
    a = jnp.exp(log_a)
    z = -jnp.tanh(log_a) * (a * a + 1.0)
    root = jnp.where(z > 0.0, z * lax.rsqrt(z), 0.0)
    return a, root * ig * xc


def _group_scan(a, b):
    shape = a.shape
    grouped = (shape[0] // SUBLANES, SUBLANES, shape[1])
    a, b = a.reshape(grouped), b.reshape(grouped)
    row = lax.broadcasted_iota(I32, grouped, 1)
    for d in (1, 2, 4):
        inside = row >= d
        a_prev = jnp.where(inside, pltpu.roll(a, d, 1), 1.0)
        b_prev = jnp.where(inside, pltpu.roll(b, d, 1), 0.0)
        b = a * b_prev + b
        a = a * a_prev
    return a.reshape(shape), b.reshape(shape)


def _rec_prompt_kernel(x_ref, g_ref, win_ref, wc_ref, bc_ref, wg_ref, bga_ref, bgx_ref, lam_ref, wo_ref,
                       out_ref, hlast_ref, ctail_ref, a_scr, at_scr, bt_scr, u_scr, hcar_scr, ccar_scr, xb_scr,
                       gelu_scr):
    j = pl.program_id(1)
    rows = x_ref.shape[0]
    a_scr[...] = _rms(x_ref[...], g_ref[...]).astype(BF16)

    @pl.when(j == 0)
    def _():
        ccar_scr[...] = jnp.zeros(ccar_scr.shape, F32)
        hcar_scr[...] = jnp.zeros(hcar_scr.shape, F32)

    nblk = MXU_N // LANES
    ncs = -RG_C * jax.nn.softplus(-lam_ref[...])
    for c in range(D_MODEL // MXU_N):
        cs = slice(c * MXU_N, (c + 1) * MXU_N)
        xb_scr[:, cs] = jnp.dot(a_scr[...], win_ref[:, D_MODEL + c * MXU_N:D_MODEL + (c + 1) * MXU_N],
                                preferred_element_type=F32)
    for c in range(D_MODEL // MXU_N):
        cs = slice(c * MXU_N, (c + 1) * MXU_N)
        gelu_scr[:, cs] = jax.nn.gelu(jnp.dot(a_scr[...], win_ref[:, cs], preferred_element_type=F32))
    for c in range(D_MODEL // MXU_N):
        cs = slice(c * MXU_N, (c + 1) * MXU_N)
        xb = xb_scr[:, cs]
        prev = [ccar_scr[SUBLANES - 3 + k:SUBLANES - 2 + k, cs] for k in range(3)]
        xc = (bc_ref[:, cs] + wc_ref[0:1, cs] * _shift_rows(xb, 3, prev) + wc_ref[1:2, cs] * _shift_rows(xb, 2, prev[1:])
              + wc_ref[2:3, cs] * _shift_rows(xb, 1, prev[2:]) + wc_ref[3:4, cs] * xb)
        ccar_scr[:, cs] = xb[rows - SUBLANES:, :]
        ctail_ref[0, :, cs] = xb[rows - SUBLANES:, :]
        for k in range(nblk):
            n = c * nblk + k
            ls = slice(n * LANES, (n + 1) * LANES)
            xcn = xc[:, k * LANES:(k + 1) * LANES]
            gg = jnp.dot(xcn.astype(BF16), wg_ref[n], preferred_element_type=F32)
            a_t, b_t = _group_scan(*_lru_coeffs(xcn, gg, bga_ref[:, ls], bgx_ref[:, ls], ncs[:, ls]))
            at_scr[:, ls] = a_t
            bt_scr[:, ls] = b_t

    def group_step(g, h):
        r = pl.ds(pl.multiple_of(g * SUBLANES, SUBLANES), SUBLANES)
        hg = at_scr[r, :] * h + bt_scr[r, :]
        bt_scr[r, :] = hg
        return hg[SUBLANES - 1:, :]

    h_last = lax.fori_loop(0, rows // SUBLANES, group_step, hcar_scr[0:1, :], unroll=4)
    hcar_scr[...] = jnp.broadcast_to(h_last, hcar_scr.shape)
    hlast_ref[0] = jnp.broadcast_to(h_last, hcar_scr.shape)
    u_scr[...] = (gelu_scr[...] * bt_scr[...]).astype(BF16)
    out_ref[...] = x_ref[...] + jnp.dot(u_scr[...], wo_ref[...], preferred_element_type=F32)


def _rec_prompt(x, g, win, wc, bc, wg, bga, bgx, lam, wo, bsz, seq):
    n = x.shape[0]
    nb = seq // ROW_BLOCK
    row = pl.BlockSpec((ROW_BLOCK, D_MODEL), lambda b, j: (b * nb + j, 0))
    tail = pl.BlockSpec((1, SUBLANES, D_MODEL), lambda b, j: (b, 0, 0))
    vec = _const_spec((1, D_MODEL))
    return pl.pallas_call(
        _rec_prompt_kernel,
        grid=(bsz, nb),
        in_specs=[row, vec, _const_spec(win.shape), _const_spec(wc.shape), vec, _const_spec(wg.shape), vec, vec,
                  vec, _const_spec(wo.shape)],
        out_specs=[row, tail, tail],
        out_shape=[jax.ShapeDtypeStruct((n, D_MODEL), F32), jax.ShapeDtypeStruct((bsz, SUBLANES, D_MODEL), F32),
                   jax.ShapeDtypeStruct((bsz, SUBLANES, D_MODEL), F32)],
        scratch_shapes=[pltpu.VMEM((ROW_BLOCK, D_MODEL), BF16), pltpu.VMEM((ROW_BLOCK, D_MODEL), F32),
                        pltpu.VMEM((ROW_BLOCK, D_MODEL), F32), pltpu.VMEM((ROW_BLOCK, D_MODEL), BF16),
                        pltpu.VMEM((SUBLANES, D_MODEL), F32), pltpu.VMEM((SUBLANES, D_MODEL), F32),
                        pltpu.VMEM((ROW_BLOCK, D_MODEL), F32), pltpu.VMEM((ROW_BLOCK, D_MODEL), F32)],
        compiler_params=_cparams("arbitrary", "arbitrary"),
        name="rec_prompt",
    )(x, g, win, wc, bc, wg, bga, bgx, lam, wo)


def _idx_sample_kernel(pt_ref, qi_ref, wi_ref, kin_ref, *refs, n_pages):
    pages, out_ref = refs[:-1], refs[-1]
    lane = lax.broadcasted_iota(I32, (1, LANES), 1)
    for g in range(IDX_SEQS):
        qi = qi_ref[g]
        wi = wi_ref[g]
        for p in range(n_pages):
            page_t = pages[g * n_pages + p][...].astype(BF16)
            s = jnp.dot(qi[:, :IDX_DIM], page_t, preferred_element_type=F32)
            out_ref[g, :, p * PAGE:(p + 1) * PAGE] = (
                jnp.sum(wi * jnp.maximum(s, 0.0), axis=0, keepdims=True) * IDX_SCALE)
        s_new = jnp.sum(qi.astype(F32) * kin_ref[g].astype(BF16).astype(F32), axis=-1, keepdims=True)
        sc_new = jnp.sum(wi * jnp.maximum(s_new, 0.0), axis=0, keepdims=True) * IDX_SCALE
        out_ref[g, :, n_pages * PAGE:] = jnp.where(lane == 0, sc_new, -jnp.inf)


def _idx_sample(pt, qi3, wi3, kin3, pool_ki_t, n_pages):
    db = qi3.shape[0]
    assert db % IDX_SEQS == 0
    per_step = lambda shape: pl.BlockSpec((IDX_SEQS,) + shape, lambda d, pt: (d, 0, 0))
    page_spec = lambda g, p: pl.BlockSpec((None, IDX_DIM, PAGE),
                                          lambda d, pt: (pt[(d * IDX_SEQS + g) * n_pages + p], 0, 0))
    width = n_pages * PAGE + LANES
    return pl.pallas_call(
        functools.partial(_idx_sample_kernel, n_pages=n_pages),
        grid_spec=pltpu.PrefetchScalarGridSpec(
            num_scalar_prefetch=1, grid=(db // IDX_SEQS,),
            in_specs=[per_step((IDX_HEADS, LANES)), per_step((IDX_HEADS, 1)), per_step((1, LANES))]
            + [page_spec(g, p) for g in range(IDX_SEQS) for p in range(n_pages)],
            out_specs=per_step((1, width))),
        out_shape=jax.ShapeDtypeStruct((db, 1, width), F32),
        compiler_params=_cparams("arbitrary"),
        name="idx_sample",
    )(pt, qi3, wi3, kin3, *([pool_ki_t] * (IDX_SEQS * n_pages)))


def _select_sample_kernel(sc_ref, rows_ref, ids_ref, bias_ref, key_scr, tri_scr, rank_scr, ids_scr, *, n_keys):
    lane = lax.broadcasted_iota(I32, (1, sc_ref.shape[1]), 1)
    key_scr[...] = jnp.where(lane < n_keys, _sortable_key(sc_ref[...]), jnp.int32(INT_MIN))

    def count_ge(cand):
        return jnp.sum(jnp.where(key_scr[...] >= cand, 1.0, 0.0), axis=-1, keepdims=True)

    thr, _ = _kth_largest_key(count_ge, float(sc_ref.shape[1]))
    need = float(TOPK) - count_ge(thr + 1)

    def count_tie_le(cm):
        return jnp.sum(jnp.where((key_scr[...] == thr) & (lane <= cm), 1.0, 0.0), axis=-1, keepdims=True)

    cut = _tie_cutoff(count_tie_le, need, 12)
    cut = jnp.where(thr == INT_MIN, jnp.int32(-1), cut)
    key = key_scr[...]
    sel = (key > thr) | ((key == thr) & (lane <= cut))

    past = n_keys - 1
    n_seq = sc_ref.shape[0]
    taken = jnp.where(sel, 1.0, 0.0)[:, :past]
    for r in range(past // TK):
        upper = (lax.broadcasted_iota(I32, (TK, past), 0) + r * TK) <= lax.broadcasted_iota(I32, (TK, past), 1)
        tri_scr[r * TK:(r + 1) * TK, :] = jnp.where(upper, 1.0, 0.0).astype(BF16)
    rank = jnp.dot(taken.astype(BF16), tri_scr[...], preferred_element_type=F32)
    rank_scr[...] = rank * taken
    n_taken = rank[:, past - 1:past]
    slot = (lax.broadcasted_iota(I32, (TOPK, 1), 0) + 1).astype(F32)
    seq_lane = lax.broadcasted_iota(I32, (1, n_seq), 1)
    ids_scr[...] = jnp.zeros(ids_scr.shape, F32)

    def seq_body(d, carry):
        hit = rank_scr[pl.ds(d, 1), :] == slot
        row = jnp.sum(jnp.where(hit, rows_ref[pl.ds(d, 1), :], 0.0), axis=-1, keepdims=True)
        ids_scr[...] += jnp.where(seq_lane == d, row, 0.0)
        return carry

    lax.fori_loop(0, n_seq, seq_body, 0)
    ids_ref[...] = ids_scr[...].T.astype(I32)
    blane = lax.broadcasted_iota(I32, bias_ref.shape, 1)
    new_taken = jnp.sum(jnp.where(sel & (lane == past), 1.0, 0.0), axis=-1, keepdims=True)
    live = (blane.astype(F32) < n_taken) | ((blane == TOPK) & (new_taken > 0.0))
    bias_ref[...] = jnp.where(live, 0.0, NEG)


def _select_sample(scores, cache_rows, n_keys):
    n_seq = scores.shape[0]
    past = n_keys - 1
    return pl.pallas_call(
        functools.partial(_select_sample_kernel, n_keys=n_keys),
        out_shape=[jax.ShapeDtypeStruct((n_seq, TOPK), I32), jax.ShapeDtypeStruct((n_seq, TOPK + LANES), F32)],
        scratch_shapes=[pltpu.VMEM(scores.shape, I32), pltpu.VMEM((past, past), BF16),
                        pltpu.VMEM((n_seq, past), F32), pltpu.VMEM((TOPK, n_seq), F32)],
        compiler_params=_cparams(),
        name="select_sample",
    )(scores, cache_rows)


def _gather_kv_rows(pool_k, pool_v, ids):
    n = ids.shape[0]
    mesh = plsc.VectorSubcoreMesh(core_axis_name="core", subcore_axis_name="subcore")
    n_workers = mesh.num_cores * mesh.num_subcores
    per_worker = n // n_workers
    assert n % (n_workers * SC_GATHER_ROWS) == 0
    row = pool_k.shape[1:]
    out = jax.ShapeDtypeStruct((n,) + row, pool_k.dtype)

    @functools.partial(
        pl.kernel, mesh=mesh, out_type=[out, out],
        scratch_types=[pltpu.VMEM((SC_GATHER_ROWS,), I32), pltpu.VMEM((SC_GATHER_ROWS,) + row, pool_k.dtype),
                       pltpu.VMEM((SC_GATHER_ROWS,) + row, pool_v.dtype), pltpu.SemaphoreType.DMA,
                       pltpu.SemaphoreType.DMA],
        compiler_params=pltpu.CompilerParams(use_tc_tiling_on_sc=True),
        name="gather_kv_rows",
    )
    def gather(k_hbm, v_hbm, ids_hbm, ko_hbm, vo_hbm, ids_v, k_v, v_v, ksem, vsem):
        worker = lax.axis_index("subcore") * mesh.num_cores + lax.axis_index("core")

        @pl.loop(0, per_worker // SC_GATHER_ROWS)
        def _(j):
            off = pl.multiple_of(worker * per_worker + j * SC_GATHER_ROWS, SC_GATHER_ROWS)
            pltpu.sync_copy(ids_hbm.at[pl.ds(off, SC_GATHER_ROWS)], ids_v)
            kcopy = pltpu.async_copy(k_hbm.at[ids_v], k_v, ksem)
            vcopy = pltpu.async_copy(v_hbm.at[ids_v], v_v, vsem)
            kcopy.wait()
            pltpu.sync_copy(k_v, ko_hbm.at[pl.ds(off, SC_GATHER_ROWS)])
            vcopy.wait()
            pltpu.sync_copy(v_v, vo_hbm.at[pl.ds(off, SC_GATHER_ROWS)])

    return gather(pool_k, pool_v, ids)


def _attn_sample_kernel(q_ref, kn_ref, vn_ref, bias_ref, kg_ref, vg_ref, o_ref, kall, vall):
    past_rows = kg_ref.shape[1]
    zeros = jnp.zeros((LANES - N_HEADS, HEAD_DIM), F32)
    for g in range(ATT_SEQS):
        kall[:past_rows, :] = kg_ref[g].astype(BF16)
        vall[:past_rows, :] = vg_ref[g].astype(BF16)
        kall[past_rows:, :] = jnp.concatenate([kn_ref[g], zeros], axis=0).astype(BF16)
        vall[past_rows:, :] = jnp.concatenate([vn_ref[g], zeros], axis=0).astype(BF16)
        s = lax.dot_general(q_ref[g], kall[...], NT_DIMS, preferred_element_type=F32)
        own = (lax.broadcasted_iota(I32, s.shape, 1) & (N_HEADS - 1)) == lax.broadcasted_iota(I32, s.shape, 0)
        s = jnp.where(own, s + bias_ref[g], NEG)
        m = jnp.max(s, axis=-1, keepdims=True)
        p = jnp.exp2(s - m)
        l = jnp.sum(p, axis=-1, keepdims=True)
        o_ref[g] = jnp.dot(p.astype(BF16), vall[...], preferred_element_type=F32) / l


def _attn_sample(q3, kn3, vn3, bias3, kg, vg):
    db, past_rows, _ = kg.shape
    width = past_rows + LANES
    assert db % ATT_SEQS == 0 and bias3.shape == (db, 1, width)
    per_step = lambda r, w: pl.BlockSpec((ATT_SEQS, r, w), lambda d: (d, 0, 0))
    return pl.pallas_call(
        _attn_sample_kernel,
        grid=(db // ATT_SEQS,),
        in_specs=[per_step(N_HEADS, HEAD_DIM)] * 3 + [per_step(1, width)] + [per_step(past_rows, HEAD_DIM)] * 2,
        out_specs=per_step(N_HEADS, HEAD_DIM),
        out_shape=jax.ShapeDtypeStruct((db, N_HEADS, HEAD_DIM), F32),
        scratch_shapes=[pltpu.VMEM((width, HEAD_DIM), BF16), pltpu.VMEM((width, HEAD_DIM), BF16)],
        compiler_params=_cparams("arbitrary"),
        name="attn_sample",
    )(q3, kn3, vn3, bias3, kg, vg)


def _ffn_sample_kernel(x_ref, g_ref, wup_ref, wc_ref, bc_ref, wdn_ref, st0_ref, st1_ref, gf_ref,
                       out_ref, gate_ref, wupb_ref, wdnb_ref, a_scr, acc_scr, *, final_norm):
    t = pl.program_id(0)

    @pl.when(t == 0)
    def _():
        a_scr[...] = _rms(x_ref[...], g_ref[...]).astype(BF16)
        acc_scr[...] = jnp.zeros(acc_scr.shape, F32)

    wup_b = wup_ref[...].astype(BF16)
    wupb_ref[...] = wup_b
    r = jnp.dot(a_scr[...], wup_b, preferred_element_type=F32)

    @pl.when(t < N_FF_CHUNKS)
    def _():
        gate_ref[:, pl.ds(pl.multiple_of(t * FF_CHUNK, FF_CHUNK), FF_CHUNK)] = r

    @pl.when(t >= N_FF_CHUNKS)
    def _():
        cs = pl.ds(pl.multiple_of((t - N_FF_CHUNKS) * FF_CHUNK, FF_CHUNK), FF_CHUNK)
        gc = (bc_ref[:, cs] + wc_ref[0:1, cs] * st0_ref[:, cs] + wc_ref[1:2, cs] * st1_ref[:, cs]
              + wc_ref[2:3, cs] * gate_ref[:, cs])
        wdn_b = wdn_ref[...].astype(BF16)
        wdnb_ref[...] = wdn_b
        acc_scr[...] += jnp.dot((jax.nn.gelu(gc) * r).astype(BF16), wdn_b, preferred_element_type=F32)

    @pl.when(t == 2 * N_FF_CHUNKS - 1)
    def _():
        y = x_ref[...] + acc_scr[...]
        out_ref[...] = _rms(y, gf_ref[...]) if final_norm else y


def _ffn_sample(x, g, wup, wc, bc, wdn, st0, st1, gf, layer, final_norm):
    db = x.shape[0]
    full = lambda a: pl.BlockSpec(a.shape, lambda t: (0,) * a.ndim)
    dn_blk = lambda t: jnp.maximum(t - N_FF_CHUNKS, 0)
    return pl.pallas_call(
        functools.partial(_ffn_sample_kernel, final_norm=final_norm),
        grid=(2 * N_FF_CHUNKS,),
        in_specs=[full(x), full(g), pl.BlockSpec((None, D_MODEL, FF_CHUNK), lambda t: (layer, 0, t)), full(wc),
                  full(bc), pl.BlockSpec((None, FF_CHUNK, D_MODEL), lambda t: (layer, dn_blk(t), 0)), full(st0),
                  full(st1), full(gf)],
        out_specs=[pl.BlockSpec((db, D_MODEL), lambda t: (0, 0)), pl.BlockSpec((db, D_FF), lambda t: (0, 0)),
                   pl.BlockSpec((D_MODEL, FF_CHUNK), lambda t: (0, t)),
                   pl.BlockSpec((FF_CHUNK, D_MODEL), lambda t: (dn_blk(t), 0))],
        out_shape=[jax.ShapeDtypeStruct((db, D_MODEL), F32), jax.ShapeDtypeStruct((db, D_FF), F32),
                   jax.ShapeDtypeStruct((D_MODEL, 2 * D_FF), BF16), jax.ShapeDtypeStruct((D_FF, D_MODEL), BF16)],
        scratch_shapes=[pltpu.VMEM((db, D_MODEL), BF16), pltpu.VMEM((db, D_MODEL), F32)],
        compiler_params=_cparams("arbitrary"),
        name="ffn_sample_final" if final_norm else "ffn_sample",
    )(x, g, wup, wc, bc, wdn, st0, st1, gf)


def _attn_out_sample_kernel(x_ref, o_ref, wo_ref, out_ref):
    out_ref[...] = x_ref[...] + jnp.dot(o_ref[...].astype(BF16), wo_ref[...], preferred_element_type=F32)


def _rec_sample_kernel(x_ref, gm_ref, win_ref, wcr_ref, bcr_ref, wg_ref, bga_ref, bgx_ref, lam_ref, wor_ref,
                       cs0_ref, cs1_ref, cs2_ref, h0_ref, out_ref, hnew_ref, xb_ref, a_scr, u_scr):
    x = x_ref[...]
    a_scr[...] = _rms(x, gm_ref[...]).astype(BF16)
    nblk = MXU_N // LANES
    ncs = -RG_C * jax.nn.softplus(-lam_ref[...])
    for c in range(D_MODEL // MXU_N):
        cs = slice(c * MXU_N, (c + 1) * MXU_N)
        xb = jnp.dot(a_scr[...], win_ref[:, D_MODEL + c * MXU_N:D_MODEL + (c + 1) * MXU_N],
                     preferred_element_type=F32)
        gate = jnp.dot(a_scr[...], win_ref[:, cs], preferred_element_type=F32)
        xb_ref[:, cs] = xb
        xc = (bcr_ref[:, cs] + wcr_ref[0:1, cs] * cs0_ref[:, cs] + wcr_ref[1:2, cs] * cs1_ref[:, cs]
              + wcr_ref[2:3, cs] * cs2_ref[:, cs] + wcr_ref[3:4, cs] * xb)
        for k in range(nblk):
            n = c * nblk + k
            ls = slice(n * LANES, (n + 1) * LANES)
            xcn = xc[:, k * LANES:(k + 1) * LANES]
            gg = jnp.dot(xcn.astype(BF16), wg_ref[n], preferred_element_type=F32)
            a_t, b_t = _lru_coeffs(xcn, gg, bga_ref[:, ls], bgx_ref[:, ls], ncs[:, ls])
            h = a_t * h0_ref[:, ls] + b_t
            hnew_ref[:, ls] = h
            u_scr[:, ls] = (jax.nn.gelu(gate[:, k * LANES:(k + 1) * LANES]) * h).astype(BF16)
    out_ref[...] = x + jnp.dot(u_scr[...], wor_ref[...], preferred_element_type=F32)


def _sample_call(kernel, name, args, out_widths, n_scratch):
    db = args[0].shape[0]
    return pl.pallas_call(
        kernel,
        grid=(1,),
        in_specs=[_const_spec(a.shape) for a in args],
        out_specs=[pl.BlockSpec((db, w), lambda i: (0, 0)) for w in out_widths],
        out_shape=[jax.ShapeDtypeStruct((db, w), F32) for w in out_widths],
        scratch_shapes=[pltpu.VMEM((db, D_MODEL), BF16)] * n_scratch,
        compiler_params=_cparams("arbitrary"),
        name=name,
    )(*args)


def _rope_tables(pos):
    posf = pos.astype(F32)[:, None]

    def cs(d):
        half = d // 2
        inv = ROPE_THETA ** (-jnp.arange(half, dtype=F32) * 2.0 / d)
        ang = posf * inv[None, :]
        return jnp.cos(ang), jnp.sin(ang)

    c, s = cs(HEAD_DIM)
    cos = jnp.concatenate([c, c], axis=-1)
    sin = jnp.concatenate([-s, s], axis=-1)
    c, s = cs(IDX_DIM)
    one, zero = jnp.ones_like(c), jnp.zeros_like(c)
    tile = lambda parts: jnp.concatenate(parts, axis=-1)
    icos = tile([c, c, one, one])
    isa = tile([-s, zero, zero, zero])
    isb = tile([zero, s, zero, zero])
    return cos, sin, icos, isa, isb


def _split_attn_in(w):
    qkv = N_HEADS * HEAD_DIM
    wqk, wv, wi = w[:, :2 * qkv], w[:, 2 * qkv:3 * qkv], w[:, 3 * qkv:]
    pad = lambda a: jnp.pad(a, ((0, 0), (0, LANES - a.shape[1])))
    groups = [pad(wi[:, h * IDX_DIM:(h + 1) * IDX_DIM]) for h in range(IDX_HEADS)]
    groups.append(pad(wi[:, IDX_HEADS * IDX_DIM:]))
    return wqk.astype(BF16), wv.astype(BF16), jnp.concatenate(groups, axis=-1).astype(BF16)


def kernel(x_prompt, x_sample, cache_k, cache_v, cache_kidx, state_lru_h, state_lru_conv, state_ffn_conv,
           page_table, norm_mix, norm_ffn, norm_final, w_attn_in, w_attn_out, w_rec_in, w_rec_conv, b_rec_conv,
           w_gate_a, b_gate_a, w_gate_x, b_gate_x, lru_lambda, w_rec_out, w_ffn_up, w_ffn_conv, b_ffn_conv,
           w_ffn_down):
    bsz, seq, d = x_prompt.shape
    db = x_sample.shape[0]
    n_pages = page_table.shape[1]
    past = n_pages * PAGE
    assert d == D_MODEL and x_sample.shape[1] == 1 and seq % ROW_BLOCK == 0 and seq % TQ == 0
    assert min(TOPK, seq // 4) == TOPK and min(TOPK, (past + 1) // 4) == TOPK

    vec = lambda a: a.reshape(1, -1)
    wqk, wv, widx = _split_attn_in(w_attn_in[0])
    wo_attn = w_attn_out[0].astype(BF16)
    w_rin = w_rec_in[0].astype(BF16)
    w_gates = jnp.concatenate([w_gate_a[0], w_gate_x[0]], axis=-1).astype(BF16)
    wo_rec = w_rec_out[0].astype(BF16)
    rec_vecs = (w_rec_conv[0], vec(b_rec_conv[0]), w_gates, vec(b_gate_a[0]), vec(b_gate_x[0]),
                vec(lru_lambda[0]), wo_rec)
    gfin = vec(norm_final)
    st = state_ffn_conv

    def ffn_sample(x, i, final_norm):
        return _ffn_sample(x, vec(norm_ffn[i]), w_ffn_up, w_ffn_conv[i], vec(b_ffn_conv[i]), w_ffn_down,
                           st[i, :, 0], st[i, :, 1], gfin, i, final_norm)

    def ffn_prompt(x, i, wup_b, wdn_b, final_norm):
        return _ffn_prompt(x, vec(norm_ffn[i]), wup_b, w_ffn_conv[i], vec(b_ffn_conv[i]), wdn_b, gfin, bsz, seq,
                           final_norm)

    xp = x_prompt.reshape(bsz * seq, d)
    q, kf, kb, vf, vt, qi, kiw = _attn_in(xp, vec(norm_mix[0]), wqk, wv, widx, _rope_tables(jnp.arange(seq)),
                                          ROW_BLOCK, seq // ROW_BLOCK)
    wit = kiw[:, IDX_DIM:IDX_DIM + IDX_HEADS].T
    hp_attn = _dsa_prompt(q, kb, vt, qi, kiw, wit, xp, wo_attn, bsz, seq)

    xs = x_sample.reshape(db, d)
    tabs = tuple(jnp.broadcast_to(t, (db, t.shape[1])) for t in _rope_tables(jnp.full((1,), past)))
    qs, kfs, _, vfs, _, qis, kiws = _attn_in(xs, vec(norm_mix[0]), wqk, wv, widx, tabs, db, 1)
    pt = page_table.reshape(-1)
    scores = _idx_sample(pt, qis.reshape(db, IDX_HEADS, LANES),
                         kiws[:, IDX_DIM:IDX_DIM + IDX_HEADS].reshape(db, IDX_HEADS, 1),
                         kiws.reshape(db, 1, LANES), jnp.swapaxes(cache_kidx[0], 1, 2), n_pages)
    cache_rows = (page_table[:, :, None] * PAGE + jnp.arange(PAGE, dtype=I32)).reshape(db, past).astype(F32)
    ids, bias = _select_sample(scores.reshape(db, -1), cache_rows, past + 1)
    heads = (N_HEADS, HEAD_DIM)
    key_rows = lambda pool: pool[0].reshape(-1, *heads)
    ids, _ = lax.optimization_barrier((ids, kb))
    kg, vg = _gather_kv_rows(key_rows(cache_k), key_rows(cache_v), ids.reshape(-1))
    bias_rows = jnp.pad(jnp.repeat(bias[:, :TOPK + 1], N_HEADS, axis=1), ((0, 0), (0, LANES - N_HEADS)),
                        constant_values=NEG)
    bias_rows, _ = lax.optimization_barrier((bias_rows, hp_attn))
    seq_rows = lambda g: g.reshape(db, TOPK * N_HEADS, HEAD_DIM)
    o_s = _attn_sample(qs.reshape(db, *heads), kfs.reshape(db, *heads), vfs.reshape(db, *heads),
                       bias_rows.reshape(db, 1, -1), seq_rows(kg), seq_rows(vg))
    (hs,) = _sample_call(_attn_out_sample_kernel, "attn_out_sample", (xs, o_s.reshape(db, d), wo_attn),
                         (D_MODEL,), 0)
    hs, gate0, wup_b, wdn_b = ffn_sample(hs, 0, False)
    hp, ftail0 = ffn_prompt(hp_attn, 0, wup_b, wdn_b, False)

    hp, hlast, ctail = _rec_prompt(hp, vec(norm_mix[1]), w_rin, *rec_vecs, bsz, seq)
    cst = state_lru_conv[0]
    hs, hnew, xbs = _sample_call(
        _rec_sample_kernel, "rec_sample",
        (hs, vec(norm_mix[1]), w_rin, *rec_vecs, cst[:, 0], cst[:, 1], cst[:, 2], state_lru_h[0]),
        (D_MODEL, D_MODEL, D_MODEL), 2)
    ys, gate1, wup_b, wdn_b = ffn_sample(hs, 1, True)
    yp, ftail1 = ffn_prompt(hp, 1, wup_b, wdn_b, True)

    return (
        yp.reshape(bsz, seq, d),
        ys.reshape(db, 1, d),
        kf.reshape(1, bsz, seq, *heads),
        vf.reshape(1, bsz, seq, *heads),
        kiw[:, :IDX_DIM].reshape(1, bsz, seq, IDX_DIM),
        kfs.reshape(1, db, 1, *heads),
        vfs.reshape(1, db, 1, *heads),
        kiws[:, :IDX_DIM].reshape(1, db, 1, IDX_DIM),
        hlast[None, :, 0, :],
        ctail[None, :, SUBLANES - 3:, :],
        hnew[None],
        jnp.stack([cst[:, 1], cst[:, 2], xbs], axis=1)[None],
        jnp.stack([ftail0[:, SUBLANES - 2:], ftail1[:, SUBLANES - 2:]]),
        jnp.stack([jnp.stack([st[0, :, 1], gate0], axis=1), jnp.stack([st[1, :, 1], gate1], axis=1)]),
    )
```

```python
import functools

import jax
import jax.numpy as jnp
from jax import lax
from jax.experimental import pallas as pl
from jax.experimental.pallas import tpu as pltpu
from jax.experimental.pallas import tpu_sc as plsc

F32 = jnp.float32
BF16 = jnp.bfloat16
I32 = jnp.int32
I16 = jnp.int16

D_MODEL = 1024
N_HEADS = 8
HEAD_DIM = 128
IDX_HEADS = 4
IDX_DIM = 64
TOPK = 256
PAGE = 128
ROPE_THETA = 10000.0
IDX_SCALE = (IDX_DIM * IDX_HEADS) ** -0.5
Q_SCALE = HEAD_DIM ** -0.5 * 1.4426950408889634
RG_C = 8.0
D_FF = 2816
RMS_EPS = 1e-6

LANES = 128
SUBLANES = 8
MXU_N = 256
IDX_W = (IDX_HEADS + 1) * LANES
INT_MIN = -(2 ** 31)
NEG = -1e30
VMEM_LIMIT = 52 * 1024 * 1024

ROW_BLOCK = 512
FFN_ROWS = 1024
TQ = 256
TK = 256
COUNT_ROWS = 32
COUNT_ROWS_16 = 64
IDX_SEQS = 8
ATT_SEQS = 4
SC_GATHER_ROWS = 32
FF_CHUNK = 256
N_FF_CHUNKS = D_FF // FF_CHUNK
FFS_COLS = 1408
N_FFS = D_FF // FFS_COLS
NT_DIMS = (((1,), (1,)), ((), ()))


def _cparams(*sem):
    return pltpu.CompilerParams(dimension_semantics=sem if sem else None, vmem_limit_bytes=VMEM_LIMIT)


def _const_spec(shape):
    nd = len(shape)
    return pl.BlockSpec(shape, lambda *_: (0,) * nd, pipeline_mode=pl.Buffered(1))


def _rms(x, g):
    return x * lax.rsqrt(jnp.mean(x * x, axis=-1, keepdims=True) + RMS_EPS) * g


def _shift_rows(x, s, prev_rows):
    r = pltpu.roll(x, s, 0)
    top = r[:SUBLANES]
    row = lax.broadcasted_iota(I32, top.shape, 0)
    for k in range(s):
        top = jnp.where(row == k, prev_rows[k], top)
    return jnp.concatenate([top, r[SUBLANES:]], axis=0)


def _sortable_key(score):
    bits = pltpu.bitcast(score + 0.0, I32)
    return jnp.where(bits < 0, bits ^ jnp.int32(0x7FFFFFFF), bits)


def _attn_in_kernel(x_ref, g_ref, wqk_ref, wv_ref, widx_ref, cos_ref, sin_ref, icos_ref, isa_ref, isb_ref,
                    q_ref, kf_ref, kb_ref, vf_ref, vt_ref, qi_ref, kiw_ref, a_scr):
    a_scr[...] = _rms(x_ref[...], g_ref[...]).astype(BF16)
    cos = cos_ref[...]
    sin = sin_ref[...]
    for c in range(2 * D_MODEL // MXU_N):
        r = lax.dot_general(a_scr[...], wqk_ref[c * MXU_N:(c + 1) * MXU_N, :], NT_DIMS, preferred_element_type=F32)
        for hh in range(MXU_N // HEAD_DIM):
            xh = r[:, hh * HEAD_DIM:(hh + 1) * HEAD_DIM]
            y = xh * cos + pltpu.roll(xh, HEAD_DIM // 2, 1) * sin
            col = (c * MXU_N) % D_MODEL + hh * HEAD_DIM
            if c < D_MODEL // MXU_N:
                q_ref[:, col:col + HEAD_DIM] = (y * Q_SCALE).astype(BF16)
            else:
                kf_ref[:, col:col + HEAD_DIM] = y
                kb_ref[:, col:col + HEAD_DIM] = y.astype(BF16)
    for c in range(D_MODEL // MXU_N):
        r = lax.dot_general(a_scr[...], wv_ref[c * MXU_N:(c + 1) * MXU_N, :], NT_DIMS, preferred_element_type=F32)
        vf_ref[:, c * MXU_N:(c + 1) * MXU_N] = r
        vt_ref[c * MXU_N:(c + 1) * MXU_N, :] = r.T.astype(BF16)
    ri = lax.dot_general(a_scr[...], widx_ref[...], NT_DIMS, preferred_element_type=F32)
    groups = lambda t: jnp.concatenate([t] * (IDX_W // LANES), axis=1)
    yi = (ri * groups(icos_ref[...]) + pltpu.roll(ri, IDX_W - IDX_DIM // 2, 1) * groups(isa_ref[...])
          + pltpu.roll(ri, IDX_DIM // 2, 1) * groups(isb_ref[...]))
    qi_ref[...] = yi[:, :IDX_HEADS * LANES].astype(BF16)
    kiw_ref[...] = yi[:, IDX_HEADS * LANES:]


def _attn_in(x, g, wqk, wv, widx, tabs, rows, n_pos_blocks):
    n = x.shape[0]
    cos, sin, icos, isa, isb = tabs
    row_spec = lambda w: pl.BlockSpec((rows, w), lambda i: (i, 0))
    tab_spec = lambda w: pl.BlockSpec((rows, w), lambda i: (i % n_pos_blocks, 0))
    vt_spec = pl.BlockSpec((None, D_MODEL, rows), lambda i: (i // n_pos_blocks, 0, i % n_pos_blocks))
    return pl.pallas_call(
        _attn_in_kernel,
        grid=(n // rows,),
        in_specs=[row_spec(D_MODEL), _const_spec((1, D_MODEL)), _const_spec(wqk.shape), _const_spec(wv.shape),
                  _const_spec(widx.shape), tab_spec(HEAD_DIM), tab_spec(HEAD_DIM), tab_spec(LANES),
                  tab_spec(LANES), tab_spec(LANES)],
        out_specs=[row_spec(D_MODEL)] * 4 + [vt_spec, row_spec(IDX_HEADS * LANES), row_spec(LANES)],
        out_shape=[jax.ShapeDtypeStruct((n, D_MODEL), BF16), jax.ShapeDtypeStruct((n, D_MODEL), F32),
                   jax.ShapeDtypeStruct((n, D_MODEL), BF16), jax.ShapeDtypeStruct((n, D_MODEL), F32),
                   jax.ShapeDtypeStruct((n // (rows * n_pos_blocks), D_MODEL, rows * n_pos_blocks), BF16),
                   jax.ShapeDtypeStruct((n, IDX_HEADS * LANES), BF16),
                   jax.ShapeDtypeStruct((n, LANES), F32)],
        scratch_shapes=[pltpu.VMEM((rows, D_MODEL), BF16)],
        compiler_params=_cparams("arbitrary"),
        name="attn_in",
    )(x, g, wqk, wv, widx, cos, sin, icos, isa, isb)


def _kth_largest_key(count_ge, n_total):
    kf = float(TOPK)
    c0 = count_ge(0)
    t = jnp.where(c0 >= kf, jnp.int32(0), jnp.int32(INT_MIN))
    ct = jnp.where(c0 >= kf, c0, n_total)

    def bit_body(it, carry):
        t, ct = carry
        cand = t + (jnp.int32(1) << (30 - it))
        cnt = count_ge(cand)
        return jnp.where(cnt >= kf, cand, t), jnp.where(cnt >= kf, cnt, ct)

    return lax.fori_loop(0, 31, bit_body, (t, ct))


def _kth_largest_half(count_ge, above, count_all):
    kf = float(TOPK)
    c0 = above + count_ge(0)
    t = jnp.where(c0 >= kf, jnp.int32(0), jnp.int32(-(2 ** 15)))
    ct = jnp.where(c0 >= kf, c0, count_all)

    def bit_body(it, carry):
        t, ct = carry
        cand = t + (jnp.int32(1) << (14 - it))
        cnt = above + count_ge(cand)
        return jnp.where(cnt >= kf, cand, t), jnp.where(cnt >= kf, cnt, ct)

    return lax.fori_loop(0, 15, bit_body, (t, ct))


def _tie_cutoff(count_tie_le, need, n_bits):
    def bit_body(it, c):
        cand = c + (jnp.int32(1) << (n_bits - 1 - it))
        return jnp.where(count_tie_le(cand - 1) < need, cand, c)

    return lax.fori_loop(0, n_bits, bit_body, jnp.zeros_like(need, dtype=I32))


def _dsa_prompt_kernel(q_ref, kb_ref, vt_ref, qi_ref, qin_ref, kiwk_ref, wit_ref, witn_ref, x_ref, wo_ref, out_ref,
                       key_scr, bias_scr, kib_scr, o_scr, acc_scr, s_scr, hi_scr, lo_scr):
    i = pl.program_id(1)
    nk = i + 1
    lane_q = lax.broadcasted_iota(I32, (1, TQ), 1)

    def rows(kc):
        return pl.ds(kc * TK if isinstance(kc, int) else pl.multiple_of(kc * TK, TK), TK)

    def spos(kc):
        return kc * TK + lax.broadcasted_iota(I32, (TK, 1), 0)

    def colsum(a):
        return jnp.sum(a.reshape(TK // COUNT_ROWS, COUNT_ROWS, TQ), axis=0)

    def score_chunk(kc, qidx_ref, wis, qpos):
        kic = kib_scr[rows(kc), :]
        sc = jnp.zeros((TK, TQ), F32)
        for h in range(IDX_HEADS):
            s = lax.dot_general(kic, qidx_ref[:, h * LANES:(h + 1) * LANES], NT_DIMS, preferred_element_type=F32)
            sc = sc + wis[h:h + 1, :] * jnp.maximum(s, 0.0)
        key = _sortable_key(sc)
        if qpos is not None:
            key = jnp.where(spos(kc) <= qpos, key, jnp.int32(INT_MIN))
        key_scr[rows(kc), :] = key
        hi_scr[rows(kc), :] = (key >> 16).astype(I16)
        lo_scr[rows(kc), :] = ((key & 0xFFFF) - 2 ** 15).astype(I16)

    @pl.when(i == 0)
    def _():
        kib_scr[...] = kiwk_ref[...].astype(BF16)
        score_chunk(0, qi_ref, wit_ref[...] * IDX_SCALE, lane_q)

    wis_next = witn_ref[...] * IDX_SCALE

    def count_ge(cand):
        def body(kc, acc):
            return acc + colsum(jnp.where(key_scr[rows(kc), :] >= cand, 1.0, 0.0))
        acc = lax.fori_loop(0, nk, body, jnp.zeros((COUNT_ROWS, TQ), F32))
        return jnp.sum(acc, axis=0, keepdims=True)

    def count_ge_half(half_scr):
        def count(cand):
            c16 = jnp.asarray(cand, I32).astype(I16)

            def body(kc, acc):
                ind = jnp.where(half_scr[rows(kc), :] >= c16, jnp.int16(1), jnp.int16(0))
                for r in range(TK // COUNT_ROWS_16):
                    acc = acc + ind[r * COUNT_ROWS_16:(r + 1) * COUNT_ROWS_16]
                return acc

            acc = lax.fori_loop(0, nk, body, jnp.zeros((COUNT_ROWS_16, TQ), I16))
            return jnp.sum(acc.astype(F32), axis=0, keepdims=True)
        return count

    count_hi, count_lo = count_ge_half(hi_scr), count_ge_half(lo_scr)
    thi, cnt_hi = _kth_largest_half(count_hi, 0.0, (nk * TK).astype(F32))
    above = jnp.where(thi == 2 ** 15 - 1, 0.0, count_hi(jnp.minimum(thi + 1, 2 ** 15 - 1)))
    thi16 = thi.astype(I16)

    def lo_body(kc, carry):
        lo_scr[rows(kc), :] = jnp.where(hi_scr[rows(kc), :] == thi16, lo_scr[rows(kc), :], jnp.int16(-(2 ** 15)))
        return carry

    lax.fori_loop(0, nk, lo_body, 0)
    tlo, cnt_thr = _kth_largest_half(count_lo, above, cnt_hi)
    thr = (thi << 16) | (tlo + 2 ** 15)
    short = thr == INT_MIN
    has_ties = jnp.max(jnp.where(short, 0.0, cnt_thr)) > float(TOPK)

    @pl.when(jnp.logical_not(has_ties))
    def _():
        thr_vis = jnp.maximum(thr, jnp.int32(INT_MIN + 1))

        def bias_body(kc, carry):
            bias_scr[rows(kc), :] = jnp.where(key_scr[rows(kc), :] >= thr_vis, 0.0, NEG)
            return carry

        lax.fori_loop(0, nk, bias_body, 0)

    @pl.when(has_ties)
    def _():
        need = float(TOPK) - count_ge(thr + 1)

        def eq_body(kc, carry):
            bias_scr[rows(kc), :] = jnp.where(key_scr[rows(kc), :] == thr, 1.0, 0.0)
            return carry

        lax.fori_loop(0, nk, eq_body, 0)

        def count_tie_le(cm):
            def body(kc, acc):
                return acc + colsum(jnp.where(spos(kc) <= cm, bias_scr[rows(kc), :], 0.0))
            acc = lax.fori_loop(0, nk, body, jnp.zeros((COUNT_ROWS, TQ), F32))
            return jnp.sum(acc, axis=0, keepdims=True)

        cut = jnp.where(short, jnp.int32(-1), _tie_cutoff(count_tie_le, need, 11))

        def bias_body(kc, carry):
            sel = (key_scr[rows(kc), :] > thr) | ((bias_scr[rows(kc), :] > 0.0) & (spos(kc) <= cut))
            bias_scr[rows(kc), :] = jnp.where(sel, 0.0, NEG)
            return carry

        lax.fori_loop(0, nk, bias_body, 0)

    acc_scr[...] = jnp.zeros(acc_scr.shape, F32)

    def att_body(kc, carry):
        ms, ls = carry
        for h in range(N_HEADS):
            hs = slice(h * HEAD_DIM, (h + 1) * HEAD_DIM)
            s_scr[h] = lax.dot_general(kb_ref[rows(kc), hs], q_ref[:, hs], NT_DIMS, preferred_element_type=F32)
        bias = bias_scr[rows(kc), :]
        score_chunk(kc, qin_ref, wis_next, None)
        new_ms, new_ls = [], []
        for h in range(N_HEADS):
            hs = slice(h * HEAD_DIM, (h + 1) * HEAD_DIM)
            s = s_scr[h] + bias
            m_new = jnp.maximum(ms[h], jnp.max(s, axis=0, keepdims=True))
            alpha = jnp.exp2(ms[h] - m_new)
            p = jnp.exp2(s - m_new)
            new_ms.append(m_new)
            new_ls.append(alpha * ls[h] + jnp.sum(p, axis=0, keepdims=True))
            pv = jnp.dot(vt_ref[hs, rows(kc)], p.astype(BF16), preferred_element_type=F32)
            acc_scr[hs, :] = alpha * acc_scr[hs, :] + pv
        return tuple(new_ms), tuple(new_ls)

    init = ((jnp.full((1, TQ), NEG, F32),) * N_HEADS, (jnp.zeros((1, TQ), F32),) * N_HEADS)
    _, ls = lax.fori_loop(0, nk, att_body, init)

    @pl.when(i < pl.num_programs(1) - 1)
    def _():
        score_chunk(nk, qin_ref, wis_next, (i + 1) * TQ + lane_q)

    for h in range(N_HEADS):
        hs = slice(h * HEAD_DIM, (h + 1) * HEAD_DIM)
        o_scr[:, hs] = (acc_scr[hs, :] / ls[h]).T.astype(BF16)
    out_ref[...] = x_ref[...] + jnp.dot(o_scr[...], wo_ref[...], preferred_element_type=F32)


def _dsa_prompt(q, kb, vt, qi, kiw, wit, x, wo, bsz, seq):
    n = x.shape[0]
    nq = seq // TQ
    qrow = lambda w: pl.BlockSpec((TQ, w), lambda b, i: (b * nq + i, 0))
    brow = lambda w: pl.BlockSpec((seq, w), lambda b, i: (b, 0))
    nxt = lambda b, i: b * nq + jnp.minimum(i + 1, nq - 1)
    return pl.pallas_call(
        _dsa_prompt_kernel,
        grid=(bsz, nq),
        in_specs=[qrow(D_MODEL), brow(D_MODEL), pl.BlockSpec((None, D_MODEL, seq), lambda b, i: (b, 0, 0)),
                  qrow(IDX_HEADS * LANES), pl.BlockSpec((TQ, IDX_HEADS * LANES), lambda b, i: (nxt(b, i), 0)),
                  brow(LANES), pl.BlockSpec((IDX_HEADS, TQ), lambda b, i: (0, b * nq + i)),
                  pl.BlockSpec((IDX_HEADS, TQ), lambda b, i: (0, nxt(b, i))),
                  qrow(D_MODEL), _const_spec(wo.shape)],
        out_specs=qrow(D_MODEL),
        out_shape=jax.ShapeDtypeStruct((n, D_MODEL), F32),
        scratch_shapes=[pltpu.VMEM((seq, TQ), I32), pltpu.VMEM((seq, TQ), F32), pltpu.VMEM((seq, LANES), BF16),
                        pltpu.VMEM((TQ, D_MODEL), BF16), pltpu.VMEM((D_MODEL, TQ), F32),
                        pltpu.VMEM((N_HEADS, TK, TQ), F32), pltpu.VMEM((seq, TQ), I16),
                        pltpu.VMEM((seq, TQ), I16)],
        compiler_params=_cparams("arbitrary", "arbitrary"),
        name="dsa_prompt",
    )(q, kb, vt, qi, qi, kiw, wit, wit, x, wo)


def _glu_hidden(a_scr, wup_ref, wc_ref, bc_ref, h_scr, gate_taps):
    for c in range(N_FF_CHUNKS):
        cs = slice(c * FF_CHUNK, (c + 1) * FF_CHUNK)
        gate = jnp.dot(a_scr[...], wup_ref[:, cs], preferred_element_type=F32)
        val = jnp.dot(a_scr[...], wup_ref[:, D_FF + c * FF_CHUNK:D_FF + (c + 1) * FF_CHUNK],
                      preferred_element_type=F32)
        g2, g1 = gate_taps(c, gate)
        gc = bc_ref[:, cs] + wc_ref[0:1, cs] * g2 + wc_ref[1:2, cs] * g1 + wc_ref[2:3, cs] * gate
        h_scr[:, cs] = (jax.nn.gelu(gc) * val).astype(BF16)


def _ffn_prompt_kernel(x_ref, g_ref, wup_ref, wc_ref, bc_ref, wdn_ref, gf_ref, out_ref, tail_ref,
                       a_scr, h_scr, carry_scr, *, final_norm):
    j = pl.program_id(1)
    rows = x_ref.shape[0]
    a_scr[...] = _rms(x_ref[...], g_ref[...]).astype(BF16)

    @pl.when(j == 0)
    def _():
        carry_scr[...] = jnp.zeros(carry_scr.shape, F32)

    def gate_taps(c, gate):
        cs = slice(c * FF_CHUNK, (c + 1) * FF_CHUNK)
        p0 = carry_scr[SUBLANES - 2:SUBLANES - 1, cs]
        p1 = carry_scr[SUBLANES - 1:SUBLANES, cs]
        g1 = _shift_rows(gate, 1, [p1])
        g2 = _shift_rows(gate, 2, [p0, p1])
        carry_scr[:, cs] = gate[rows - SUBLANES:, :]
        tail_ref[0, :, cs] = gate[rows - SUBLANES:, :]
        return g2, g1

    _glu_hidden(a_scr, wup_ref, wc_ref, bc_ref, h_scr, gate_taps)
    y = x_ref[...] + jnp.dot(h_scr[...], wdn_ref[...], preferred_element_type=F32)
    out_ref[...] = _rms(y, gf_ref[...]) if final_norm else y


def _ffn_prompt(x, g, wup, wc, bc, wdn, gf, bsz, seq, final_norm):
    n = x.shape[0]
    assert seq % FFN_ROWS == 0
    nb = seq // FFN_ROWS
    row = pl.BlockSpec((FFN_ROWS, D_MODEL), lambda b, j: (b * nb + j, 0))
    return pl.pallas_call(
        functools.partial(_ffn_prompt_kernel, final_norm=final_norm),
        grid=(bsz, nb),
        in_specs=[row, _const_spec((1, D_MODEL)), _const_spec(wup.shape), _const_spec(wc.shape),
                  _const_spec(bc.shape), _const_spec(wdn.shape), _const_spec((1, D_MODEL))],
        out_specs=[row, pl.BlockSpec((1, SUBLANES, D_FF), lambda b, j: (b, 0, 0))],
        out_shape=[jax.ShapeDtypeStruct((n, D_MODEL), F32), jax.ShapeDtypeStruct((bsz, SUBLANES, D_FF), F32)],
        scratch_shapes=[pltpu.VMEM((FFN_ROWS, D_MODEL), BF16), pltpu.VMEM((FFN_ROWS, D_FF), BF16),
                        pltpu.VMEM((SUBLANES, D_FF), F32)],
        compiler_params=_cparams("arbitrary", "arbitrary"),
        name="ffn_prompt_final" if final_norm else "ffn_prompt",
    )(x, g, wup, wc, bc, wdn, gf)


def _sigmoid(x):
    return 0.5 * jnp.tanh(0.5 * x) + 0.5


def _lru_coeffs(xc, gg, bga, bgx, neg_c_softplus):
    r = _sigmoid(gg[:, :LANES] + bga)
    ig = _sigmoid(gg[:, LANES:] + bgx)
    log_a = r * neg_c_softplus
    a = jnp.exp(log_a)
    z = -jnp.tanh(log_a) * (a * a + 1.0)
    root = jnp.where(z > 0.0, z * lax.rsqrt(z), 0.0)
    return a, root * ig * xc


def _group_scan(a, b):
    shape = a.shape
    grouped = (shape[0] // SUBLANES, SUBLANES, shape[1])
    a, b = a.reshape(grouped), b.reshape(grouped)
    row = lax.broadcasted_iota(I32, grouped, 1)
    for d in (1, 2, 4):
        inside = row >= d
        a_prev = jnp.where(inside, pltpu.roll(a, d, 1), 1.0)
        b_prev = jnp.where(inside, pltpu.roll(b, d, 1), 0.0)
        b = a * b_prev + b
        a = a * a_prev
    return a.reshape(shape), b.reshape(shape)


def _rec_prompt_kernel(x_ref, g_ref, win_ref, wc_ref, bc_ref, wg_ref, bga_ref, bgx_ref, lam_ref, wo_ref,
                       out_ref, hlast_ref, ctail_ref, a_scr, at_scr, bt_scr, u_scr, hcar_scr, ccar_scr, xb_scr,
                       gelu_scr):
    j = pl.program_id(1)
    rows = x_ref.shape[0]
    a_scr[...] = _rms(x_ref[...], g_ref[...]).astype(BF16)

    @pl.when(j == 0)
    def _():
        ccar_scr[...] = jnp.zeros(ccar_scr.shape, F32)
        hcar_scr[...] = jnp.zeros(hcar_scr.shape, F32)

    nblk = MXU_N // LANES
    ncs = -RG_C * jax.nn.softplus(-lam_ref[...])
    for c in range(D_MODEL // MXU_N):
        cs = slice(c * MXU_N, (c + 1) * MXU_N)
        xb_scr[:, cs] = jnp.dot(a_scr[...], win_ref[:, D_MODEL + c * MXU_N:D_MODEL + (c + 1) * MXU_N],
                                preferred_element_type=F32)
    for c in range(D_MODEL // MXU_N):
        cs = slice(c * MXU_N, (c + 1) * MXU_N)
        gelu_scr[:, cs] = jax.nn.gelu(jnp.dot(a_scr[...], win_ref[:, cs], preferred_element_type=F32))
    for c in range(D_MODEL // MXU_N):
        cs = slice(c * MXU_N, (c + 1) * MXU_N)
        xb = xb_scr[:, cs]
        prev = [ccar_scr[SUBLANES - 3 + k:SUBLANES - 2 + k, cs] for k in range(3)]
        xc = (bc_ref[:, cs] + wc_ref[0:1, cs] * _shift_rows(xb, 3, prev) + wc_ref[1:2, cs] * _shift_rows(xb, 2, prev[1:])
              + wc_ref[2:3, cs] * _shift_rows(xb, 1, prev[2:]) + wc_ref[3:4, cs] * xb)
        ccar_scr[:, cs] = xb[rows - SUBLANES:, :]
        ctail_ref[0, :, cs] = xb[rows - SUBLANES:, :]
        for k in range(nblk):
            n = c * nblk + k
            ls = slice(n * LANES, (n + 1) * LANES)
            xcn = xc[:, k * LANES:(k + 1) * LANES]
            gg = jnp.dot(xcn.astype(BF16), wg_ref[n], preferred_element_type=F32)
            a_t, b_t = _group_scan(*_lru_coeffs(xcn, gg, bga_ref[:, ls], bgx_ref[:, ls], ncs[:, ls]))
            at_scr[:, ls] = a_t
            bt_scr[:, ls] = b_t

    def group_step(g, h):
        r = pl.ds(pl.multiple_of(g * SUBLANES, SUBLANES), SUBLANES)
        hg = at_scr[r, :] * h + bt_scr[r, :]
        bt_scr[r, :] = hg
        return hg[SUBLANES - 1:, :]

    h_last = lax.fori_loop(0, rows // SUBLANES, group_step, hcar_scr[0:1, :], unroll=4)
    hcar_scr[...] = jnp.broadcast_to(h_last, hcar_scr.shape)
    hlast_ref[0] = jnp.broadcast_to(h_last, hcar_scr.shape)
    u_scr[...] = (gelu_scr[...] * bt_scr[...]).astype(BF16)
    out_ref[...] = x_ref[...] + jnp.dot(u_scr[...], wo_ref[...], preferred_element_type=F32)


def _rec_prompt(x, g, win, wc, bc, wg, bga, bgx, lam, wo, bsz, seq):
    n = x.shape[0]
    nb = seq // ROW_BLOCK
    row = pl.BlockSpec((ROW_BLOCK, D_MODEL), lambda b, j: (b * nb + j, 0))
    tail = pl.BlockSpec((1, SUBLANES, D_MODEL), lambda b, j: (b, 0, 0))
    vec = _const_spec((1, D_MODEL))
    return pl.pallas_call(
        _rec_prompt_kernel,
        grid=(bsz, nb),
        in_specs=[row, vec, _const_spec(win.shape), _const_spec(wc.shape), vec, _const_spec(wg.shape), vec, vec,
                  vec, _const_spec(wo.shape)],
        out_specs=[row, tail, tail],
        out_shape=[jax.ShapeDtypeStruct((n, D_MODEL), F32), jax.ShapeDtypeStruct((bsz, SUBLANES, D_MODEL), F32),
                   jax.ShapeDtypeStruct((bsz, SUBLANES, D_MODEL), F32)],
        scratch_shapes=[pltpu.VMEM((ROW_BLOCK, D_MODEL), BF16), pltpu.VMEM((ROW_BLOCK, D_MODEL), F32),
                        pltpu.VMEM((ROW_BLOCK, D_MODEL), F32), pltpu.VMEM((ROW_BLOCK, D_MODEL), BF16),
                        pltpu.VMEM((SUBLANES, D_MODEL), F32), pltpu.VMEM((SUBLANES, D_MODEL), F32),
                        pltpu.VMEM((ROW_BLOCK, D_MODEL), F32), pltpu.VMEM((ROW_BLOCK, D_MODEL), F32)],
        compiler_params=_cparams("arbitrary", "arbitrary"),
        name="rec_prompt",
    )(x, g, win, wc, bc, wg, bga, bgx, lam, wo)


def _idx_sample_kernel(pt_ref, qi_ref, wi_ref, kin_ref, *refs, n_pages):
    pages, out_ref = refs[:-1], refs[-1]
    lane = lax.broadcasted_iota(I32, (1, LANES), 1)
    for g in range(IDX_SEQS):
        qi = qi_ref[g]
        wi = wi_ref[g]
        for p in range(n_pages):
            page_t = pages[g * n_pages + p][...].astype(BF16)
            s = jnp.dot(qi[:, :IDX_DIM], page_t, preferred_element_type=F32)
            out_ref[g, :, p * PAGE:(p + 1) * PAGE] = (
                jnp.sum(wi * jnp.maximum(s, 0.0), axis=0, keepdims=True) * IDX_SCALE)
        s_new = jnp.sum(qi.astype(F32) * kin_ref[g].astype(BF16).astype(F32), axis=-1, keepdims=True)
        sc_new = jnp.sum(wi * jnp.maximum(s_new, 0.0), axis=0, keepdims=True) * IDX_SCALE
        out_ref[g, :, n_pages * PAGE:] = jnp.where(lane == 0, sc_new, -jnp.inf)


def _idx_sample(pt, qi3, wi3, kin3, pool_ki_t, n_pages):
    db = qi3.shape[0]
    assert db % IDX_SEQS == 0
    per_step = lambda shape: pl.BlockSpec((IDX_SEQS,) + shape, lambda d, pt: (d, 0, 0))
    page_spec = lambda g, p: pl.BlockSpec((None, IDX_DIM, PAGE),
                                          lambda d, pt: (pt[(d * IDX_SEQS + g) * n_pages + p], 0, 0))
    width = n_pages * PAGE + LANES
    return pl.pallas_call(
        functools.partial(_idx_sample_kernel, n_pages=n_pages),
        grid_spec=pltpu.PrefetchScalarGridSpec(
            num_scalar_prefetch=1, grid=(db // IDX_SEQS,),
            in_specs=[per_step((IDX_HEADS, LANES)), per_step((IDX_HEADS, 1)), per_step((1, LANES))]
            + [page_spec(g, p) for g in range(IDX_SEQS) for p in range(n_pages)],
            out_specs=per_step((1, width))),
        out_shape=jax.ShapeDtypeStruct((db, 1, width), F32),
        compiler_params=_cparams("arbitrary"),
        name="idx_sample",
    )(pt, qi3, wi3, kin3, *([pool_ki_t] * (IDX_SEQS * n_pages)))


def _select_sample_kernel(sc_ref, rows_ref, ids_ref, bias_ref, key_scr, tri_scr, rank_scr, ids_scr, *, n_keys):
    lane = lax.broadcasted_iota(I32, (1, sc_ref.shape[1]), 1)
    key_scr[...] = jnp.where(lane < n_keys, _sortable_key(sc_ref[...]), jnp.int32(INT_MIN))

    def count_ge(cand):
        return jnp.sum(jnp.where(key_scr[...] >= cand, 1.0, 0.0), axis=-1, keepdims=True)

    thr, _ = _kth_largest_key(count_ge, float(sc_ref.shape[1]))
    need = float(TOPK) - count_ge(thr + 1)

    def count_tie_le(cm):
        return jnp.sum(jnp.where((key_scr[...] == thr) & (lane <= cm), 1.0, 0.0), axis=-1, keepdims=True)

    cut = _tie_cutoff(count_tie_le, need, 12)
    cut = jnp.where(thr == INT_MIN, jnp.int32(-1), cut)
    key = key_scr[...]
    sel = (key > thr) | ((key == thr) & (lane <= cut))

    past = n_keys - 1
    n_seq = sc_ref.shape[0]
    taken = jnp.where(sel, 1.0, 0.0)[:, :past]
    for r in range(past // TK):
        upper = (lax.broadcasted_iota(I32, (TK, past), 0) + r * TK) <= lax.broadcasted_iota(I32, (TK, past), 1)
        tri_scr[r * TK:(r + 1) * TK, :] = jnp.where(upper, 1.0, 0.0).astype(BF16)
    rank = jnp.dot(taken.astype(BF16), tri_scr[...], preferred_element_type=F32)
    rank_scr[...] = rank * taken
    n_taken = rank[:, past - 1:past]
    slot = (lax.broadcasted_iota(I32, (TOPK, 1), 0) + 1).astype(F32)
    seq_lane = lax.broadcasted_iota(I32, (1, n_seq), 1)
    ids_scr[...] = jnp.zeros(ids_scr.shape, F32)

    def seq_body(d, carry):
        hit = rank_scr[pl.ds(d, 1), :] == slot
        row = jnp.sum(jnp.where(hit, rows_ref[pl.ds(d, 1), :], 0.0), axis=-1, keepdims=True)
        ids_scr[...] += jnp.where(seq_lane == d, row, 0.0)
        return carry

    lax.fori_loop(0, n_seq, seq_body, 0)
    ids_ref[...] = ids_scr[...].T.astype(I32)
    blane = lax.broadcasted_iota(I32, bias_ref.shape, 1)
    new_taken = jnp.sum(jnp.where(sel & (lane == past), 1.0, 0.0), axis=-1, keepdims=True)
    live = (blane.astype(F32) < n_taken) | ((blane == TOPK) & (new_taken > 0.0))
    bias_ref[...] = jnp.where(live, 0.0, NEG)


def _select_sample(scores, cache_rows, n_keys):
    n_seq = scores.shape[0]
    past = n_keys - 1
    return pl.pallas_call(
        functools.partial(_select_sample_kernel, n_keys=n_keys),
        out_shape=[jax.ShapeDtypeStruct((n_seq, TOPK), I32), jax.ShapeDtypeStruct((n_seq, TOPK + LANES), F32)],
        scratch_shapes=[pltpu.VMEM(scores.shape, I32), pltpu.VMEM((past, past), BF16),
                        pltpu.VMEM((n_seq, past), F32), pltpu.VMEM((TOPK, n_seq), F32)],
        compiler_params=_cparams(),
        name="select_sample",
    )(scores, cache_rows)


def _gather_kv_rows(pool_k, pool_v, ids):
    n = ids.shape[0]
    mesh = plsc.VectorSubcoreMesh(core_axis_name="core", subcore_axis_name="subcore")
    n_workers = mesh.num_cores * mesh.num_subcores
    per_worker = n // n_workers
    assert n % (n_workers * SC_GATHER_ROWS) == 0
    row = pool_k.shape[1:]
    out = jax.ShapeDtypeStruct((n,) + row, pool_k.dtype)

    @functools.partial(
        pl.kernel, mesh=mesh, out_type=[out, out],
        scratch_types=[pltpu.VMEM((SC_GATHER_ROWS,), I32), pltpu.VMEM((SC_GATHER_ROWS,) + row, pool_k.dtype),
                       pltpu.VMEM((SC_GATHER_ROWS,) + row, pool_v.dtype), pltpu.SemaphoreType.DMA,
                       pltpu.SemaphoreType.DMA],
        compiler_params=pltpu.CompilerParams(use_tc_tiling_on_sc=True),
        name="gather_kv_rows",
    )
    def gather(k_hbm, v_hbm, ids_hbm, ko_hbm, vo_hbm, ids_v, k_v, v_v, ksem, vsem):
        worker = lax.axis_index("subcore") * mesh.num_cores + lax.axis_index("core")

        @pl.loop(0, per_worker // SC_GATHER_ROWS)
        def _(j):
            off = pl.multiple_of(worker * per_worker + j * SC_GATHER_ROWS, SC_GATHER_ROWS)
            pltpu.sync_copy(ids_hbm.at[pl.ds(off, SC_GATHER_ROWS)], ids_v)
            kcopy = pltpu.async_copy(k_hbm.at[ids_v], k_v, ksem)
            vcopy = pltpu.async_copy(v_hbm.at[ids_v], v_v, vsem)
            kcopy.wait()
            pltpu.sync_copy(k_v, ko_hbm.at[pl.ds(off, SC_GATHER_ROWS)])
            vcopy.wait()
            pltpu.sync_copy(v_v, vo_hbm.at[pl.ds(off, SC_GATHER_ROWS)])

    return gather(pool_k, pool_v, ids)


def _attn_sample_kernel(q_ref, kn_ref, vn_ref, bias_ref, kg_ref, vg_ref, o_ref, kall, vall):
    past_rows = kg_ref.shape[1]
    zeros = jnp.zeros((LANES - N_HEADS, HEAD_DIM), F32)
    for g in range(ATT_SEQS):
        kall[:past_rows, :] = kg_ref[g].astype(BF16)
        vall[:past_rows, :] = vg_ref[g].astype(BF16)
        kall[past_rows:, :] = jnp.concatenate([kn_ref[g], zeros], axis=0).astype(BF16)
        vall[past_rows:, :] = jnp.concatenate([vn_ref[g], zeros], axis=0).astype(BF16)
        s = lax.dot_general(q_ref[g], kall[...], NT_DIMS, preferred_element_type=F32)
        own = (lax.broadcasted_iota(I32, s.shape, 1) & (N_HEADS - 1)) == lax.broadcasted_iota(I32, s.shape, 0)
        s = jnp.where(own, s + bias_ref[g], NEG)
        m = jnp.max(s, axis=-1, keepdims=True)
        p = jnp.exp2(s - m)
        l = jnp.sum(p, axis=-1, keepdims=True)
        o_ref[g] = jnp.dot(p.astype(BF16), vall[...], preferred_element_type=F32) / l


def _attn_sample(q3, kn3, vn3, bias3, kg, vg):
    db, past_rows, _ = kg.shape
    width = past_rows + LANES
    assert db % ATT_SEQS == 0 and bias3.shape == (db, 1, width)
    per_step = lambda r, w: pl.BlockSpec((ATT_SEQS, r, w), lambda d: (d, 0, 0))
    return pl.pallas_call(
        _attn_sample_kernel,
        grid=(db // ATT_SEQS,),
        in_specs=[per_step(N_HEADS, HEAD_DIM)] * 3 + [per_step(1, width)] + [per_step(past_rows, HEAD_DIM)] * 2,
        out_specs=per_step(N_HEADS, HEAD_DIM),
        out_shape=jax.ShapeDtypeStruct((db, N_HEADS, HEAD_DIM), F32),
        scratch_shapes=[pltpu.VMEM((width, HEAD_DIM), BF16), pltpu.VMEM((width, HEAD_DIM), BF16)],
        compiler_params=_cparams("arbitrary"),
        name="attn_sample",
    )(q3, kn3, vn3, bias3, kg, vg)


def _ffn_sample_kernel(x_ref, g_ref, wup_ref, wc_ref, bc_ref, wdn_ref, st0_ref, st1_ref, gf_ref,
                       out_ref, gate_ref, wupb_ref, wdnb_ref, a_scr, acc_scr, *, final_norm):
    t = pl.program_id(0)

    @pl.when(t == 0)
    def _():
        a_scr[...] = _rms(x_ref[...], g_ref[...]).astype(BF16)
        acc_scr[...] = jnp.zeros(acc_scr.shape, F32)

    wup_b = wup_ref[...].astype(BF16)
    wupb_ref[...] = wup_b
    r = jnp.dot(a_scr[...], wup_b, preferred_element_type=F32)

    @pl.when(t < N_FFS)
    def _():
        gate_ref[:, pl.ds(pl.multiple_of(t * FFS_COLS, LANES), FFS_COLS)] = r

    @pl.when(t >= N_FFS)
    def _():
        cs = pl.ds(pl.multiple_of((t - N_FFS) * FFS_COLS, LANES), FFS_COLS)
        gc = (bc_ref[:, cs] + wc_ref[0:1, cs] * st0_ref[:, cs] + wc_ref[1:2, cs] * st1_ref[:, cs]
              + wc_ref[2:3, cs] * gate_ref[:, cs])
        wdn_b = wdn_ref[...].astype(BF16)
        wdnb_ref[...] = wdn_b
        acc_scr[...] += jnp.dot((jax.nn.gelu(gc) * r).astype(BF16), wdn_b, preferred_element_type=F32)

    @pl.when(t == 2 * N_FFS - 1)
    def _():
        y = x_ref[...] + acc_scr[...]
        out_ref[...] = _rms(y, gf_ref[...]) if final_norm else y


def _ffn_sample(x, g, wup, wc, bc, wdn, st0, st1, gf, layer, final_norm):
    db = x.shape[0]
    full = lambda a: pl.BlockSpec(a.shape, lambda t: (0,) * a.ndim)
    dn_blk = lambda t: jnp.maximum(t - N_FFS, 0)
    return pl.pallas_call(
        functools.partial(_ffn_sample_kernel, final_norm=final_norm),
        grid=(2 * N_FFS,),
        in_specs=[full(x), full(g), pl.BlockSpec((None, D_MODEL, FFS_COLS), lambda t: (layer, 0, t)), full(wc),
                  full(bc), pl.BlockSpec((None, FFS_COLS, D_MODEL), lambda t: (layer, dn_blk(t), 0)), full(st0),
                  full(st1), full(gf)],
        out_specs=[pl.BlockSpec((db, D_MODEL), lambda t: (0, 0)), pl.BlockSpec((db, D_FF), lambda t: (0, 0)),
                   pl.BlockSpec((D_MODEL, FFS_COLS), lambda t: (0, t)),
                   pl.BlockSpec((FFS_COLS, D_MODEL), lambda t: (dn_blk(t), 0))],
        out_shape=[jax.ShapeDtypeStruct((db, D_MODEL), F32), jax.ShapeDtypeStruct((db, D_FF), F32),
                   jax.ShapeDtypeStruct((D_MODEL, 2 * D_FF), BF16), jax.ShapeDtypeStruct((D_FF, D_MODEL), BF16)],
        scratch_shapes=[pltpu.VMEM((db, D_MODEL), BF16), pltpu.VMEM((db, D_MODEL), F32)],
        compiler_params=_cparams("arbitrary"),
        name="ffn_sample_final" if final_norm else "ffn_sample",
    )(x, g, wup, wc, bc, wdn, st0, st1, gf)


def _attn_out_sample_kernel(x_ref, o_ref, wo_ref, out_ref):
    out_ref[...] = x_ref[...] + jnp.dot(o_ref[...].astype(BF16), wo_ref[...], preferred_element_type=F32)


def _rec_sample_kernel(x_ref, gm_ref, win_ref, wcr_ref, bcr_ref, wg_ref, bga_ref, bgx_ref, lam_ref, wor_ref,
                       cs0_ref, cs1_ref, cs2_ref, h0_ref, out_ref, hnew_ref, xb_ref, a_scr, u_scr):
    x = x_ref[...]
    a_scr[...] = _rms(x, gm_ref[...]).astype(BF16)
    nblk = MXU_N // LANES
    ncs = -RG_C * jax.nn.softplus(-lam_ref[...])
    for c in range(D_MODEL // MXU_N):
        cs = slice(c * MXU_N, (c + 1) * MXU_N)
        xb = jnp.dot(a_scr[...], win_ref[:, D_MODEL + c * MXU_N:D_MODEL + (c + 1) * MXU_N],
                     preferred_element_type=F32)
        gate = jnp.dot(a_scr[...], win_ref[:, cs], preferred_element_type=F32)
        xb_ref[:, cs] = xb
        xc = (bcr_ref[:, cs] + wcr_ref[0:1, cs] * cs0_ref[:, cs] + wcr_ref[1:2, cs] * cs1_ref[:, cs]
              + wcr_ref[2:3, cs] * cs2_ref[:, cs] + wcr_ref[3:4, cs] * xb)
        for k in range(nblk):
            n = c * nblk + k
            ls = slice(n * LANES, (n + 1) * LANES)
            xcn = xc[:, k * LANES:(k + 1) * LANES]
            gg = jnp.dot(xcn.astype(BF16), wg_ref[n], preferred_element_type=F32)
            a_t, b_t = _lru_coeffs(xcn, gg, bga_ref[:, ls], bgx_ref[:, ls], ncs[:, ls])
            h = a_t * h0_ref[:, ls] + b_t
            hnew_ref[:, ls] = h
            u_scr[:, ls] = (jax.nn.gelu(gate[:, k * LANES:(k + 1) * LANES]) * h).astype(BF16)
    out_ref[...] = x + jnp.dot(u_scr[...], wor_ref[...], preferred_element_type=F32)


def _sample_call(kernel, name, args, out_widths, n_scratch):
    db = args[0].shape[0]
    return pl.pallas_call(
        kernel,
        grid=(1,),
        in_specs=[_const_spec(a.shape) for a in args],
        out_specs=[pl.BlockSpec((db, w), lambda i: (0, 0)) for w in out_widths],
        out_shape=[jax.ShapeDtypeStruct((db, w), F32) for w in out_widths],
        scratch_shapes=[pltpu.VMEM((db, D_MODEL), BF16)] * n_scratch,
        compiler_params=_cparams("arbitrary"),
        name=name,
    )(*args)


def _rope_tables(pos):
    posf = pos.astype(F32)[:, None]

    def cs(d):
        half = d // 2
        inv = ROPE_THETA ** (-jnp.arange(half, dtype=F32) * 2.0 / d)
        ang = posf * inv[None, :]
        return jnp.cos(ang), jnp.sin(ang)

    c, s = cs(HEAD_DIM)
    cos = jnp.concatenate([c, c], axis=-1)
    sin = jnp.concatenate([-s, s], axis=-1)
    c, s = cs(IDX_DIM)
    one, zero = jnp.ones_like(c), jnp.zeros_like(c)
    tile = lambda parts: jnp.concatenate(parts, axis=-1)
    icos = tile([c, c, one, one])
    isa = tile([-s, zero, zero, zero])
    isb = tile([zero, s, zero, zero])
    return cos, sin, icos, isa, isb


def _split_attn_in(w):
    wt = jnp.swapaxes(w, 0, 1).astype(BF16)
    qkv = N_HEADS * HEAD_DIM
    wqk, wv, wi = wt[:2 * qkv], wt[2 * qkv:3 * qkv], wt[3 * qkv:]
    pad = lambda a: jnp.pad(a, ((0, LANES - a.shape[0]), (0, 0)))
    groups = [pad(wi[h * IDX_DIM:(h + 1) * IDX_DIM]) for h in range(IDX_HEADS)]
    groups.append(pad(wi[IDX_HEADS * IDX_DIM:]))
    return wqk, wv, jnp.concatenate(groups, axis=0)


def kernel(x_prompt, x_sample, cache_k, cache_v, cache_kidx, state_lru_h, state_lru_conv, state_ffn_conv,
           page_table, norm_mix, norm_ffn, norm_final, w_attn_in, w_attn_out, w_rec_in, w_rec_conv, b_rec_conv,
           w_gate_a, b_gate_a, w_gate_x, b_gate_x, lru_lambda, w_rec_out, w_ffn_up, w_ffn_conv, b_ffn_conv,
           w_ffn_down):
    bsz, seq, d = x_prompt.shape
    db = x_sample.shape[0]
    n_pages = page_table.shape[1]
    past = n_pages * PAGE
    assert d == D_MODEL and x_sample.shape[1] == 1 and seq % ROW_BLOCK == 0 and seq % TQ == 0
    assert min(TOPK, seq // 4) == TOPK and min(TOPK, (past + 1) // 4) == TOPK

    vec = lambda a: a.reshape(1, -1)
    wqk, wv, widx = _split_attn_in(w_attn_in[0])
    wo_attn = w_attn_out[0].astype(BF16)
    w_rin = w_rec_in[0].astype(BF16)
    w_gates = jnp.concatenate([w_gate_a[0], w_gate_x[0]], axis=-1).astype(BF16)
    wo_rec = w_rec_out[0].astype(BF16)
    rec_vecs = (w_rec_conv[0], vec(b_rec_conv[0]), w_gates, vec(b_gate_a[0]), vec(b_gate_x[0]),
                vec(lru_lambda[0]), wo_rec)
    gfin = vec(norm_final)
    st = state_ffn_conv

    def ffn_sample(x, i, final_norm):
        return _ffn_sample(x, vec(norm_ffn[i]), w_ffn_up, w_ffn_conv[i], vec(b_ffn_conv[i]), w_ffn_down,
                           st[i, :, 0], st[i, :, 1], gfin, i, final_norm)

    def ffn_prompt(x, i, wup_b, wdn_b, final_norm):
        return _ffn_prompt(x, vec(norm_ffn[i]), wup_b, w_ffn_conv[i], vec(b_ffn_conv[i]), wdn_b, gfin, bsz, seq,
                           final_norm)

    xp = x_prompt.reshape(bsz * seq, d)
    q, kf, kb, vf, vt, qi, kiw = _attn_in(xp, vec(norm_mix[0]), wqk, wv, widx, _rope_tables(jnp.arange(seq)),
                                          ROW_BLOCK, seq // ROW_BLOCK)
    wit = kiw[:, IDX_DIM:IDX_DIM + IDX_HEADS].T
    hp_attn = _dsa_prompt(q, kb, vt, qi, kiw, wit, xp, wo_attn, bsz, seq)

    xs = x_sample.reshape(db, d)
    tabs = tuple(jnp.broadcast_to(t, (db, t.shape[1])) for t in _rope_tables(jnp.full((1,), past)))
    qs, kfs, _, vfs, _, qis, kiws = _attn_in(xs, vec(norm_mix[0]), wqk, wv, widx, tabs, db, 1)
    pt = page_table.reshape(-1)
    scores = _idx_sample(pt, qis.reshape(db, IDX_HEADS, LANES),
                         kiws[:, IDX_DIM:IDX_DIM + IDX_HEADS].reshape(db, IDX_HEADS, 1),
                         kiws.reshape(db, 1, LANES), jnp.swapaxes(cache_kidx[0], 1, 2), n_pages)
    cache_rows = (page_table[:, :, None] * PAGE + jnp.arange(PAGE, dtype=I32)).reshape(db, past).astype(F32)
    ids, bias = _select_sample(scores.reshape(db, -1), cache_rows, past + 1)
    heads = (N_HEADS, HEAD_DIM)
    key_rows = lambda pool: pool[0].reshape(-1, *heads)
    ids, _ = lax.optimization_barrier((ids, kb))
    kg, vg = _gather_kv_rows(key_rows(cache_k), key_rows(cache_v), ids.reshape(-1))
    bias_rows = jnp.pad(jnp.repeat(bias[:, :TOPK + 1], N_HEADS, axis=1), ((0, 0), (0, LANES - N_HEADS)),
                        constant_values=NEG)
    bias_rows, _ = lax.optimization_barrier((bias_rows, hp_attn))
    seq_rows = lambda g: g.reshape(db, TOPK * N_HEADS, HEAD_DIM)
    o_s = _attn_sample(qs.reshape(db, *heads), kfs.reshape(db, *heads), vfs.reshape(db, *heads),
                       bias_rows.reshape(db, 1, -1), seq_rows(kg), seq_rows(vg))
    (hs,) = _sample_call(_attn_out_sample_kernel, "attn_out_sample", (xs, o_s.reshape(db, d), wo_attn),
                         (D_MODEL,), 0)
    hs, gate0, wup_b, wdn_b = ffn_sample(hs, 0, False)
    hp, ftail0 = ffn_prompt(hp_attn, 0, wup_b, wdn_b, False)

    hp, hlast, ctail = _rec_prompt(hp, vec(norm_mix[1]), w_rin, *rec_vecs, bsz, seq)
    cst = state_lru_conv[0]
    hs, hnew, xbs = _sample_call(
        _rec_sample_kernel, "rec_sample",
        (hs, vec(norm_mix[1]), w_rin, *rec_vecs, cst[:, 0], cst[:, 1], cst[:, 2], state_lru_h[0]),
        (D_MODEL, D_MODEL, D_MODEL), 2)
    ys, gate1, wup_b, wdn_b = ffn_sample(hs, 1, True)
    yp, ftail1 = ffn_prompt(hp, 1, wup_b, wdn_b, True)

    return (
        yp.reshape(bsz, seq, d),
        ys.reshape(db, 1, d),
        kf.reshape(1, bsz, seq, *heads),
        vf.reshape(1, bsz, seq, *heads),
        kiw[:, :IDX_DIM].reshape(1, bsz, seq, IDX_DIM),
        kfs.reshape(1, db, 1, *heads),
        vfs.reshape(1, db, 1, *heads),
        kiws[:, :IDX_DIM].reshape(1, db, 1, IDX_DIM),
        hlast[None, :, 0, :],
        ctail[None, :, SUBLANES - 3:, :],
        hnew[None],
        jnp.stack([cst[:, 1], cst[:, 2], xbs], axis=1)[None],
        jnp.stack([ftail0[:, SUBLANES - 2:], ftail1[:, SUBLANES - 2:]]),
        jnp.stack([jnp.stack([st[0, :, 1], gate0], axis=1), jnp.stack([st[1, :, 1], gate1], axis=1)]),
    )
```

```python
import functools

import jax
import jax.numpy as jnp
from jax import lax
from jax.experimental import pallas as pl
from jax.experimental.pallas import tpu as pltpu
from jax.experimental.pallas import tpu_sc as plsc

F32 = jnp.float32
BF16 = jnp.bfloat16
I32 = jnp.int32
I16 = jnp.int16

D_MODEL = 1024
N_HEADS = 8
HEAD_DIM = 128
IDX_HEADS = 4
IDX_DIM = 64
TOPK = 256
PAGE = 128
ROPE_THETA = 10000.0
IDX_SCALE = (IDX_DIM * IDX_HEADS) ** -0.5
Q_SCALE = HEAD_DIM ** -0.5 * 1.4426950408889634
RG_C = 8.0
D_FF = 2816
RMS_EPS = 1e-6

LANES = 128
SUBLANES = 8
MXU_N = 256
IDX_W = (IDX_HEADS + 1) * LANES
INT_MIN = -(2 ** 31)
NEG = -1e30
VMEM_LIMIT = 52 * 1024 * 1024

ROW_BLOCK = 512
FFN_ROWS = 1024
TQ = 256
TK = 256
COUNT_ROWS = 32
COUNT_ROWS_16 = 64
IDX_SEQS = 8
ATT_SEQS = 8
SC_GATHER_ROWS = 32
FF_CHUNK = 512
FFS_COLS = 1408
N_FFS = D_FF // FFS_COLS
NT_DIMS = (((1,), (1,)), ((), ()))


def _cparams(*sem):
    return pltpu.CompilerParams(dimension_semantics=sem if sem else None, vmem_limit_bytes=VMEM_LIMIT)


def _const_spec(shape):
    nd = len(shape)
    return pl.BlockSpec(shape, lambda *_: (0,) * nd, pipeline_mode=pl.Buffered(1))


def _rms(x, g):
    return x * lax.rsqrt(jnp.mean(x * x, axis=-1, keepdims=True) + RMS_EPS) * g


def _shift_rows(x, s, prev_rows):
    r = pltpu.roll(x, s, 0)
    top = r[:SUBLANES]
    row = lax.broadcasted_iota(I32, top.shape, 0)
    for k in range(s):
        top = jnp.where(row == k, prev_rows[k], top)
    return jnp.concatenate([top, r[SUBLANES:]], axis=0)


def _sortable_key(score):
    bits = pltpu.bitcast(score + 0.0, I32)
    return jnp.where(bits < 0, bits ^ jnp.int32(0x7FFFFFFF), bits)


def _attn_in_kernel(x_ref, g_ref, wqk_ref, wv_ref, widx_ref, cos_ref, sin_ref, icos_ref, isa_ref, isb_ref,
                    q_ref, kf_ref, kb_ref, vf_ref, vt_ref, qi_ref, kiw_ref, a_scr):
    a_scr[...] = _rms(x_ref[...], g_ref[...]).astype(BF16)
    cos = cos_ref[...]
    sin = sin_ref[...]
    for c in range(2 * D_MODEL // MXU_N):
        r = lax.dot_general(a_scr[...], wqk_ref[c * MXU_N:(c + 1) * MXU_N, :], NT_DIMS, preferred_element_type=F32)
        for hh in range(MXU_N // HEAD_DIM):
            xh = r[:, hh * HEAD_DIM:(hh + 1) * HEAD_DIM]
            y = xh * cos + pltpu.roll(xh, HEAD_DIM // 2, 1) * sin
            col = (c * MXU_N) % D_MODEL + hh * HEAD_DIM
            if c < D_MODEL // MXU_N:
                q_ref[:, col:col + HEAD_DIM] = (y * Q_SCALE).astype(BF16)
            else:
                kf_ref[:, col:col + HEAD_DIM] = y
                kb_ref[:, col:col + HEAD_DIM] = y.astype(BF16)
    for c in range(D_MODEL // MXU_N):
        r = lax.dot_general(a_scr[...], wv_ref[c * MXU_N:(c + 1) * MXU_N, :], NT_DIMS, preferred_element_type=F32)
        vf_ref[:, c * MXU_N:(c + 1) * MXU_N] = r
        vt_ref[c * MXU_N:(c + 1) * MXU_N, :] = r.T.astype(BF16)
    ri = lax.dot_general(a_scr[...], widx_ref[...], NT_DIMS, preferred_element_type=F32)
    groups = lambda t: jnp.concatenate([t] * (IDX_W // LANES), axis=1)
    yi = (ri * groups(icos_ref[...]) + pltpu.roll(ri, IDX_W - IDX_DIM // 2, 1) * groups(isa_ref[...])
          + pltpu.roll(ri, IDX_DIM // 2, 1) * groups(isb_ref[...]))
    qi_ref[...] = yi[:, :IDX_HEADS * LANES].astype(BF16)
    kiw_ref[...] = yi[:, IDX_HEADS * LANES:]


def _attn_in(x, g, wqk, wv, widx, tabs, rows, n_pos_blocks):
    n = x.shape[0]
    cos, sin, icos, isa, isb = tabs
    row_spec = lambda w: pl.BlockSpec((rows, w), lambda i: (i, 0))
    tab_spec = lambda w: pl.BlockSpec((rows, w), lambda i: (i % n_pos_blocks, 0))
    vt_spec = pl.BlockSpec((None, D_MODEL, rows), lambda i: (i // n_pos_blocks, 0, i % n_pos_blocks))
    return pl.pallas_call(
        _attn_in_kernel,
        grid=(n // rows,),
        in_specs=[row_spec(D_MODEL), _const_spec((1, D_MODEL)), _const_spec(wqk.shape), _const_spec(wv.shape),
                  _const_spec(widx.shape), tab_spec(HEAD_DIM), tab_spec(HEAD_DIM), tab_spec(LANES),
                  tab_spec(LANES), tab_spec(LANES)],
        out_specs=[row_spec(D_MODEL)] * 4 + [vt_spec, row_spec(IDX_HEADS * LANES), row_spec(LANES)],
        out_shape=[jax.ShapeDtypeStruct((n, D_MODEL), BF16), jax.ShapeDtypeStruct((n, D_MODEL), F32),
                   jax.ShapeDtypeStruct((n, D_MODEL), BF16), jax.ShapeDtypeStruct((n, D_MODEL), F32),
                   jax.ShapeDtypeStruct((n // (rows * n_pos_blocks), D_MODEL, rows * n_pos_blocks), BF16),
                   jax.ShapeDtypeStruct((n, IDX_HEADS * LANES), BF16),
                   jax.ShapeDtypeStruct((n, LANES), F32)],
        scratch_shapes=[pltpu.VMEM((rows, D_MODEL), BF16)],
        compiler_params=_cparams("arbitrary"),
        name="attn_in",
    )(x, g, wqk, wv, widx, cos, sin, icos, isa, isb)


def _kth_largest_key(count_ge, n_total):
    kf = float(TOPK)
    c0 = count_ge(0)
    t = jnp.where(c0 >= kf, jnp.int32(0), jnp.int32(INT_MIN))
    ct = jnp.where(c0 >= kf, c0, n_total)

    def bit_body(it, carry):
        t, ct = carry
        cand = t + (jnp.int32(1) << (30 - it))
        cnt = count_ge(cand)
        return jnp.where(cnt >= kf, cand, t), jnp.where(cnt >= kf, cnt, ct)

    return lax.fori_loop(0, 31, bit_body, (t, ct))


def _kth_largest_half(count_ge, above, count_all):
    kf = float(TOPK)
    c0 = above + count_ge(0)
    t = jnp.where(c0 >= kf, jnp.int32(0), jnp.int32(-(2 ** 15)))
    ct = jnp.where(c0 >= kf, c0, count_all)

    def bit_body(it, carry):
        t, ct = carry
        cand = t + (jnp.int32(1) << (14 - it))
        cnt = above + count_ge(cand)
        return jnp.where(cnt >= kf, cand, t), jnp.where(cnt >= kf, cnt, ct)

    return lax.fori_loop(0, 15, bit_body, (t, ct))


def _tie_cutoff(count_tie_le, need, n_bits):
    def bit_body(it, c):
        cand = c + (jnp.int32(1) << (n_bits - 1 - it))
        return jnp.where(count_tie_le(cand - 1) < need, cand, c)

    return lax.fori_loop(0, n_bits, bit_body, jnp.zeros_like(need, dtype=I32))


def _dsa_prompt_kernel(q_ref, kb_ref, vt_ref, qi_ref, qin_ref, kiwk_ref, wit_ref, witn_ref, x_ref, wo_ref, out_ref,
                       key_scr, bias_scr, kib_scr, o_scr, acc_scr, s_scr, hi_scr, lo_scr):
    i = pl.program_id(1)
    nk = i + 1
    lane_q = lax.broadcasted_iota(I32, (1, TQ), 1)

    def rows(kc):
        return pl.ds(kc * TK if isinstance(kc, int) else pl.multiple_of(kc * TK, TK), TK)

    def spos(kc):
        return kc * TK + lax.broadcasted_iota(I32, (TK, 1), 0)

    def colsum(a):
        return jnp.sum(a.reshape(TK // COUNT_ROWS, COUNT_ROWS, TQ), axis=0)

    def score_chunk(kc, qidx_ref, wis, qpos):
        kic = kib_scr[rows(kc), :]
        sc = jnp.zeros((TK, TQ), F32)
        for h in range(IDX_HEADS):
            s = lax.dot_general(kic, qidx_ref[:, h * LANES:(h + 1) * LANES], NT_DIMS, preferred_element_type=F32)
            sc = sc + wis[h:h + 1, :] * jnp.maximum(s, 0.0)
        key = _sortable_key(sc)
        if qpos is not None:
            key = jnp.where(spos(kc) <= qpos, key, jnp.int32(INT_MIN))
        key_scr[rows(kc), :] = key
        hi_scr[rows(kc), :] = (key >> 16).astype(I16)
        lo_scr[rows(kc), :] = ((key & 0xFFFF) - 2 ** 15).astype(I16)

    @pl.when(i == 0)
    def _():
        kib_scr[...] = kiwk_ref[...].astype(BF16)
        score_chunk(0, qi_ref, wit_ref[...] * IDX_SCALE, lane_q)

    wis_next = witn_ref[...] * IDX_SCALE

    def count_ge(cand):
        def body(kc, acc):
            return acc + colsum(jnp.where(key_scr[rows(kc), :] >= cand, 1.0, 0.0))
        acc = lax.fori_loop(0, nk, body, jnp.zeros((COUNT_ROWS, TQ), F32))
        return jnp.sum(acc, axis=0, keepdims=True)

    def count_ge_half(half_scr):
        def count(cand):
            c16 = jnp.asarray(cand, I32).astype(I16)

            def body(kc, acc):
                ind = jnp.where(half_scr[rows(kc), :] >= c16, jnp.int16(1), jnp.int16(0))
                for r in range(TK // COUNT_ROWS_16):
                    acc = acc + ind[r * COUNT_ROWS_16:(r + 1) * COUNT_ROWS_16]
                return acc

            acc = lax.fori_loop(0, nk, body, jnp.zeros((COUNT_ROWS_16, TQ), I16))
            return jnp.sum(acc.astype(F32), axis=0, keepdims=True)
        return count

    count_hi, count_lo = count_ge_half(hi_scr), count_ge_half(lo_scr)
    thi, cnt_hi = _kth_largest_half(count_hi, 0.0, (nk * TK).astype(F32))
    above = jnp.where(thi == 2 ** 15 - 1, 0.0, count_hi(jnp.minimum(thi + 1, 2 ** 15 - 1)))
    thi16 = thi.astype(I16)

    def lo_body(kc, carry):
        lo_scr[rows(kc), :] = jnp.where(hi_scr[rows(kc), :] == thi16, lo_scr[rows(kc), :], jnp.int16(-(2 ** 15)))
        return carry

    lax.fori_loop(0, nk, lo_body, 0)
    tlo, cnt_thr = _kth_largest_half(count_lo, above, cnt_hi)
    thr = (thi << 16) | (tlo + 2 ** 15)
    short = thr == INT_MIN
    has_ties = jnp.max(jnp.where(short, 0.0, cnt_thr)) > float(TOPK)

    @pl.when(jnp.logical_not(has_ties))
    def _():
        thr_vis = jnp.maximum(thr, jnp.int32(INT_MIN + 1))

        def bias_body(kc, carry):
            bias_scr[rows(kc), :] = jnp.where(key_scr[rows(kc), :] >= thr_vis, 0.0, NEG)
            return carry

        lax.fori_loop(0, nk, bias_body, 0)

    @pl.when(has_ties)
    def _():
        need = float(TOPK) - count_ge(thr + 1)

        def eq_body(kc, carry):
            bias_scr[rows(kc), :] = jnp.where(key_scr[rows(kc), :] == thr, 1.0, 0.0)
            return carry

        lax.fori_loop(0, nk, eq_body, 0)

        def count_tie_le(cm):
            def body(kc, acc):
                return acc + colsum(jnp.where(spos(kc) <= cm, bias_scr[rows(kc), :], 0.0))
            acc = lax.fori_loop(0, nk, body, jnp.zeros((COUNT_ROWS, TQ), F32))
            return jnp.sum(acc, axis=0, keepdims=True)

        cut = jnp.where(short, jnp.int32(-1), _tie_cutoff(count_tie_le, need, 11))

        def bias_body(kc, carry):
            sel = (key_scr[rows(kc), :] > thr) | ((bias_scr[rows(kc), :] > 0.0) & (spos(kc) <= cut))
            bias_scr[rows(kc), :] = jnp.where(sel, 0.0, NEG)
            return carry

        lax.fori_loop(0, nk, bias_body, 0)

    acc_scr[...] = jnp.zeros(acc_scr.shape, F32)

    def att_body(kc, carry):
        ms, ls = carry
        for h in range(N_HEADS):
            hs = slice(h * HEAD_DIM, (h + 1) * HEAD_DIM)
            s_scr[h] = lax.dot_general(kb_ref[rows(kc), hs], q_ref[:, hs], NT_DIMS, preferred_element_type=F32)
        bias = bias_scr[rows(kc), :]
        score_chunk(kc, qin_ref, wis_next, None)
        new_ms, new_ls = [], []
        for h in range(N_HEADS):
            hs = slice(h * HEAD_DIM, (h + 1) * HEAD_DIM)
            s = s_scr[h] + bias
            m_new = jnp.maximum(ms[h], jnp.max(s, axis=0, keepdims=True))
            alpha = jnp.exp2(ms[h] - m_new)
            p = jnp.exp2(s - m_new)
            new_ms.append(m_new)
            new_ls.append(alpha * ls[h] + jnp.sum(p, axis=0, keepdims=True))
            pv = jnp.dot(vt_ref[hs, rows(kc)], p.astype(BF16), preferred_element_type=F32)
            acc_scr[hs, :] = alpha * acc_scr[hs, :] + pv
        return tuple(new_ms), tuple(new_ls)

    init = ((jnp.full((1, TQ), NEG, F32),) * N_HEADS, (jnp.zeros((1, TQ), F32),) * N_HEADS)
    _, ls = lax.fori_loop(0, nk, att_body, init)

    @pl.when(i < pl.num_programs(1) - 1)
    def _():
        score_chunk(nk, qin_ref, wis_next, (i + 1) * TQ + lane_q)

    for h in range(N_HEADS):
        hs = slice(h * HEAD_DIM, (h + 1) * HEAD_DIM)
        o_scr[:, hs] = (acc_scr[hs, :] / ls[h]).T.astype(BF16)
    out_ref[...] = x_ref[...] + jnp.dot(o_scr[...], wo_ref[...], preferred_element_type=F32)


def _dsa_prompt(q, kb, vt, qi, kiw, wit, x, wo, bsz, seq):
    n = x.shape[0]
    nq = seq // TQ
    qrow = lambda w: pl.BlockSpec((TQ, w), lambda b, i: (b * nq + i, 0))
    brow = lambda w: pl.BlockSpec((seq, w), lambda b, i: (b, 0))
    nxt = lambda b, i: b * nq + jnp.minimum(i + 1, nq - 1)
    return pl.pallas_call(
        _dsa_prompt_kernel,
        grid=(bsz, nq),
        in_specs=[qrow(D_MODEL), brow(D_MODEL), pl.BlockSpec((None, D_MODEL, seq), lambda b, i: (b, 0, 0)),
                  qrow(IDX_HEADS * LANES), pl.BlockSpec((TQ, IDX_HEADS * LANES), lambda b, i: (nxt(b, i), 0)),
                  brow(LANES), pl.BlockSpec((IDX_HEADS, TQ), lambda b, i: (0, b * nq + i)),
                  pl.BlockSpec((IDX_HEADS, TQ), lambda b, i: (0, nxt(b, i))),
                  qrow(D_MODEL), _const_spec(wo.shape)],
        out_specs=qrow(D_MODEL),
        out_shape=jax.ShapeDtypeStruct((n, D_MODEL), F32),
        scratch_shapes=[pltpu.VMEM((seq, TQ), I32), pltpu.VMEM((seq, TQ), F32), pltpu.VMEM((seq, LANES), BF16),
                        pltpu.VMEM((TQ, D_MODEL), BF16), pltpu.VMEM((D_MODEL, TQ), F32),
                        pltpu.VMEM((N_HEADS, TK, TQ), F32), pltpu.VMEM((seq, TQ), I16),
                        pltpu.VMEM((seq, TQ), I16)],
        compiler_params=_cparams("arbitrary", "arbitrary"),
        name="dsa_prompt",
    )(q, kb, vt, qi, qi, kiw, wit, wit, x, wo)


def _glu_hidden(a_scr, wup_ref, wc_ref, bc_ref, h_scr, gate_taps):
    for lo in range(0, D_FF, FF_CHUNK):
        cs = slice(lo, min(lo + FF_CHUNK, D_FF))
        gate = jnp.dot(a_scr[...], wup_ref[:, cs], preferred_element_type=F32)
        val = jnp.dot(a_scr[...], wup_ref[:, D_FF + cs.start:D_FF + cs.stop], preferred_element_type=F32)
        g2, g1 = gate_taps(cs, gate)
        gc = bc_ref[:, cs] + wc_ref[0:1, cs] * g2 + wc_ref[1:2, cs] * g1 + wc_ref[2:3, cs] * gate
        h_scr[:, cs] = (jax.nn.gelu(gc) * val).astype(BF16)


def _ffn_prompt_kernel(x_ref, g_ref, wup_ref, wc_ref, bc_ref, wdn_ref, gf_ref, out_ref, tail_ref,
                       a_scr, h_scr, carry_scr, *, final_norm):
    j = pl.program_id(1)
    rows = x_ref.shape[0]
    a_scr[...] = _rms(x_ref[...], g_ref[...]).astype(BF16)

    @pl.when(j == 0)
    def _():
        carry_scr[...] = jnp.zeros(carry_scr.shape, F32)

    def gate_taps(cs, gate):
        p0 = carry_scr[SUBLANES - 2:SUBLANES - 1, cs]
        p1 = carry_scr[SUBLANES - 1:SUBLANES, cs]
        g1 = _shift_rows(gate, 1, [p1])
        g2 = _shift_rows(gate, 2, [p0, p1])
        carry_scr[:, cs] = gate[rows - SUBLANES:, :]
        tail_ref[0, :, cs] = gate[rows - SUBLANES:, :]
        return g2, g1

    _glu_hidden(a_scr, wup_ref, wc_ref, bc_ref, h_scr, gate_taps)
    y = x_ref[...] + jnp.dot(h_scr[...], wdn_ref[...], preferred_element_type=F32)
    out_ref[...] = _rms(y, gf_ref[...]) if final_norm else y


def _ffn_prompt(x, g, wup, wc, bc, wdn, gf, bsz, seq, final_norm):
    n = x.shape[0]
    assert seq % FFN_ROWS == 0
    nb = seq // FFN_ROWS
    row = pl.BlockSpec((FFN_ROWS, D_MODEL), lambda b, j: (b * nb + j, 0))
    return pl.pallas_call(
        functools.partial(_ffn_prompt_kernel, final_norm=final_norm),
        grid=(bsz, nb),
        in_specs=[row, _const_spec((1, D_MODEL)), _const_spec(wup.shape), _const_spec(wc.shape),
                  _const_spec(bc.shape), _const_spec(wdn.shape), _const_spec((1, D_MODEL))],
        out_specs=[row, pl.BlockSpec((1, SUBLANES, D_FF), lambda b, j: (b, 0, 0))],
        out_shape=[jax.ShapeDtypeStruct((n, D_MODEL), F32), jax.ShapeDtypeStruct((bsz, SUBLANES, D_FF), F32)],
        scratch_shapes=[pltpu.VMEM((FFN_ROWS, D_MODEL), BF16), pltpu.VMEM((FFN_ROWS, D_FF), BF16),
                        pltpu.VMEM((SUBLANES, D_FF), F32)],
        compiler_params=_cparams("arbitrary", "arbitrary"),
        name="ffn_prompt_final" if final_norm else "ffn_prompt",
    )(x, g, wup, wc, bc, wdn, gf)


def _sigmoid(x):
    return 0.5 * jnp.tanh(0.5 * x) + 0.5


def _lru_coeffs(xc, gg, bga, bgx, neg_c_softplus):
    ig = _sigmoid(gg[:, LANES:] + bgx)
    half = 0.5 * neg_c_softplus
    log_a = jnp.tanh(0.5 * (gg[:, :LANES] + bga)) * half + half
    a = jnp.exp(log_a)
    z = -jnp.tanh(log_a) * (a * a + 1.0)
    root = jnp.where(z > 0.0, z * lax.rsqrt(z), 0.0)
    return a, root * ig * xc


def _group_scan(a, b):
    shape = a.shape
    grouped = (shape[0] // SUBLANES, SUBLANES, shape[1])
    a, b = a.reshape(grouped), b.reshape(grouped)
    row = lax.broadcasted_iota(I32, grouped, 1)
    for d in (1, 2, 4):
        inside = row >= d
        a_prev = jnp.where(inside, pltpu.roll(a, d, 1), 1.0)
        b_prev = jnp.where(inside, pltpu.roll(b, d, 1), 0.0)
        b = a * b_prev + b
        a = a * a_prev
    return a.reshape(shape), b.reshape(shape)


def _rec_prompt_kernel(x_ref, g_ref, win_ref, wc_ref, bc_ref, wg_ref, bga_ref, bgx_ref, lam_ref, wo_ref,
                       out_ref, hlast_ref, ctail_ref, a_scr, at_scr, bt_scr, u_scr, hcar_scr, ccar_scr, xb_scr,
                       gelu_scr):
    j = pl.program_id(1)
    rows = x_ref.shape[0]
    a_scr[...] = _rms(x_ref[...], g_ref[...]).astype(BF16)

    @pl.when(j == 0)
    def _():
        ccar_scr[...] = jnp.zeros(ccar_scr.shape, F32)
        hcar_scr[...] = jnp.zeros(hcar_scr.shape, F32)

    nblk = MXU_N // LANES
    ncs = -RG_C * jax.nn.softplus(-lam_ref[...])
    for c in range(D_MODEL // MXU_N):
        cs = slice(c * MXU_N, (c + 1) * MXU_N)
        xb_scr[:, cs] = jnp.dot(a_scr[...], win_ref[:, D_MODEL + c * MXU_N:D_MODEL + (c + 1) * MXU_N],
                                preferred_element_type=F32)
    for c in range(D_MODEL // MXU_N):
        cs = slice(c * MXU_N, (c + 1) * MXU_N)
        gelu_scr[:, cs] = jax.nn.gelu(jnp.dot(a_scr[...], win_ref[:, cs], preferred_element_type=F32))
    for c in range(D_MODEL // MXU_N):
        cs = slice(c * MXU_N, (c + 1) * MXU_N)
        xb = xb_scr[:, cs]
        prev = [ccar_scr[SUBLANES - 3 + k:SUBLANES - 2 + k, cs] for k in range(3)]
        xc = (bc_ref[:, cs] + wc_ref[0:1, cs] * _shift_rows(xb, 3, prev) + wc_ref[1:2, cs] * _shift_rows(xb, 2, prev[1:])
              + wc_ref[2:3, cs] * _shift_rows(xb, 1, prev[2:]) + wc_ref[3:4, cs] * xb)
        ccar_scr[:, cs] = xb[rows - SUBLANES:, :]
        ctail_ref[0, :, cs] = xb[rows - SUBLANES:, :]
        for k in range(nblk):
            n = c * nblk + k
            ls = slice(n * LANES, (n + 1) * LANES)
            xcn = xc[:, k * LANES:(k + 1) * LANES]
            gg = jnp.dot(xcn.astype(BF16), wg_ref[n], preferred_element_type=F32)
            a_t, b_t = _group_scan(*_lru_coeffs(xcn, gg, bga_ref[:, ls], bgx_ref[:, ls], ncs[:, ls]))
            at_scr[:, ls] = a_t
            bt_scr[:, ls] = b_t

    def group_step(g, h):
        r = pl.ds(pl.multiple_of(g * SUBLANES, SUBLANES), SUBLANES)
        hg = at_scr[r, :] * h + bt_scr[r, :]
        bt_scr[r, :] = hg
        return hg[SUBLANES - 1:, :]

    h_last = lax.fori_loop(0, rows // SUBLANES, group_step, hcar_scr[0:1, :], unroll=4)
    hcar_scr[...] = jnp.broadcast_to(h_last, hcar_scr.shape)
    hlast_ref[0] = jnp.broadcast_to(h_last, hcar_scr.shape)
    u_scr[...] = (gelu_scr[...] * bt_scr[...]).astype(BF16)
    out_ref[...] = x_ref[...] + jnp.dot(u_scr[...], wo_ref[...], preferred_element_type=F32)


def _rec_prompt(x, g, win, wc, bc, wg, bga, bgx, lam, wo, bsz, seq):
    n = x.shape[0]
    nb = seq // ROW_BLOCK
    row = pl.BlockSpec((ROW_BLOCK, D_MODEL), lambda b, j: (b * nb + j, 0))
    tail = pl.BlockSpec((1, SUBLANES, D_MODEL), lambda b, j: (b, 0, 0))
    vec = _const_spec((1, D_MODEL))
    return pl.pallas_call(
        _rec_prompt_kernel,
        grid=(bsz, nb),
        in_specs=[row, vec, _const_spec(win.shape), _const_spec(wc.shape), vec, _const_spec(wg.shape), vec, vec,
                  vec, _const_spec(wo.shape)],
        out_specs=[row, tail, tail],
        out_shape=[jax.ShapeDtypeStruct((n, D_MODEL), F32), jax.ShapeDtypeStruct((bsz, SUBLANES, D_MODEL), F32),
                   jax.ShapeDtypeStruct((bsz, SUBLANES, D_MODEL), F32)],
        scratch_shapes=[pltpu.VMEM((ROW_BLOCK, D_MODEL), BF16), pltpu.VMEM((ROW_BLOCK, D_MODEL), F32),
                        pltpu.VMEM((ROW_BLOCK, D_MODEL), F32), pltpu.VMEM((ROW_BLOCK, D_MODEL), BF16),
                        pltpu.VMEM((SUBLANES, D_MODEL), F32), pltpu.VMEM((SUBLANES, D_MODEL), F32),
                        pltpu.VMEM((ROW_BLOCK, D_MODEL), F32), pltpu.VMEM((ROW_BLOCK, D_MODEL), F32)],
        compiler_params=_cparams("arbitrary", "arbitrary"),
        name="rec_prompt",
    )(x, g, win, wc, bc, wg, bga, bgx, lam, wo)


def _idx_sample_kernel(pt_ref, qi_ref, wi_ref, kin_ref, *refs, n_pages):
    pages, out_ref = refs[:-1], refs[-1]
    lane = lax.broadcasted_iota(I32, (1, LANES), 1)
    for g in range(IDX_SEQS):
        qi = qi_ref[g]
        wi = wi_ref[g]
        for p in range(n_pages):
            page_t = pages[g * n_pages + p][...].astype(BF16)
            s = jnp.dot(qi[:, :IDX_DIM], page_t, preferred_element_type=F32)
            out_ref[g, :, p * PAGE:(p + 1) * PAGE] = (
                jnp.sum(wi * jnp.maximum(s, 0.0), axis=0, keepdims=True) * IDX_SCALE)
        s_new = jnp.sum(qi.astype(F32) * kin_ref[g].astype(BF16).astype(F32), axis=-1, keepdims=True)
        sc_new = jnp.sum(wi * jnp.maximum(s_new, 0.0), axis=0, keepdims=True) * IDX_SCALE
        out_ref[g, :, n_pages * PAGE:] = jnp.where(lane == 0, sc_new, -jnp.inf)


def _idx_sample(pt, qi3, wi3, kin3, pool_ki_t, n_pages):
    db = qi3.shape[0]
    assert db % IDX_SEQS == 0
    per_step = lambda shape: pl.BlockSpec((IDX_SEQS,) + shape, lambda d, pt: (d, 0, 0))
    page_spec = lambda g, p: pl.BlockSpec((None, IDX_DIM, PAGE),
                                          lambda d, pt: (pt[(d * IDX_SEQS + g) * n_pages + p], 0, 0))
    width = n_pages * PAGE + LANES
    return pl.pallas_call(
        functools.partial(_idx_sample_kernel, n_pages=n_pages),
        grid_spec=pltpu.PrefetchScalarGridSpec(
            num_scalar_prefetch=1, grid=(db // IDX_SEQS,),
            in_specs=[per_step((IDX_HEADS, LANES)), per_step((IDX_HEADS, 1)), per_step((1, LANES))]
            + [page_spec(g, p) for g in range(IDX_SEQS) for p in range(n_pages)],
            out_specs=per_step((1, width))),
        out_shape=jax.ShapeDtypeStruct((db, 1, width), F32),
        compiler_params=_cparams("arbitrary"),
        name="idx_sample",
    )(pt, qi3, wi3, kin3, *([pool_ki_t] * (IDX_SEQS * n_pages)))


def _select_sample_kernel(sc_ref, rows_ref, ids_ref, bias_ref, key_scr, tri_scr, rank_scr, ids_scr, *, n_keys):
    lane = lax.broadcasted_iota(I32, (1, sc_ref.shape[1]), 1)
    key_scr[...] = jnp.where(lane < n_keys, _sortable_key(sc_ref[...]), jnp.int32(INT_MIN))

    def count_ge(cand):
        return jnp.sum(jnp.where(key_scr[...] >= cand, 1.0, 0.0), axis=-1, keepdims=True)

    thr, _ = _kth_largest_key(count_ge, float(sc_ref.shape[1]))
    need = float(TOPK) - count_ge(thr + 1)

    def count_tie_le(cm):
        return jnp.sum(jnp.where((key_scr[...] == thr) & (lane <= cm), 1.0, 0.0), axis=-1, keepdims=True)

    cut = _tie_cutoff(count_tie_le, need, 12)
    cut = jnp.where(thr == INT_MIN, jnp.int32(-1), cut)
    key = key_scr[...]
    sel = (key > thr) | ((key == thr) & (lane <= cut))

    past = n_keys - 1
    n_seq = sc_ref.shape[0]
    taken = jnp.where(sel, 1.0, 0.0)[:, :past]
    for r in range(past // TK):
        upper = (lax.broadcasted_iota(I32, (TK, past), 0) + r * TK) <= lax.broadcasted_iota(I32, (TK, past), 1)
        tri_scr[r * TK:(r + 1) * TK, :] = jnp.where(upper, 1.0, 0.0).astype(BF16)
    rank = jnp.dot(taken.astype(BF16), tri_scr[...], preferred_element_type=F32)
    rank_scr[...] = rank * taken
    n_taken = rank[:, past - 1:past]
    slot = (lax.broadcasted_iota(I32, (TOPK, 1), 0) + 1).astype(F32)
    seq_lane = lax.broadcasted_iota(I32, (1, n_seq), 1)
    ids_scr[...] = jnp.zeros(ids_scr.shape, F32)

    def seq_body(d, carry):
        hit = rank_scr[pl.ds(d, 1), :] == slot
        row = jnp.sum(jnp.where(hit, rows_ref[pl.ds(d, 1), :], 0.0), axis=-1, keepdims=True)
        ids_scr[...] += jnp.where(seq_lane == d, row, 0.0)
        return carry

    lax.fori_loop(0, n_seq, seq_body, 0)
    ids_ref[...] = ids_scr[...].T.astype(I32)
    blane = lax.broadcasted_iota(I32, bias_ref.shape, 1)
    new_taken = jnp.sum(jnp.where(sel & (lane == past), 1.0, 0.0), axis=-1, keepdims=True)
    live = (blane.astype(F32) < n_taken) | ((blane == TOPK) & (new_taken > 0.0))
    bias_ref[...] = jnp.where(live, 0.0, NEG)


def _select_sample(scores, cache_rows, n_keys):
    n_seq = scores.shape[0]
    past = n_keys - 1
    return pl.pallas_call(
        functools.partial(_select_sample_kernel, n_keys=n_keys),
        out_shape=[jax.ShapeDtypeStruct((n_seq, TOPK), I32), jax.ShapeDtypeStruct((n_seq, TOPK + LANES), F32)],
        scratch_shapes=[pltpu.VMEM(scores.shape, I32), pltpu.VMEM((past, past), BF16),
                        pltpu.VMEM((n_seq, past), F32), pltpu.VMEM((TOPK, n_seq), F32)],
        compiler_params=_cparams(),
        name="select_sample",
    )(scores, cache_rows)


def _gather_kv_rows(pool_k, pool_v, ids):
    n = ids.shape[0]
    mesh = plsc.VectorSubcoreMesh(core_axis_name="core", subcore_axis_name="subcore")
    n_workers = mesh.num_cores * mesh.num_subcores
    per_worker = n // n_workers
    assert n % (n_workers * SC_GATHER_ROWS) == 0
    row = pool_k.shape[1:]
    out = jax.ShapeDtypeStruct((n,) + row, pool_k.dtype)

    @functools.partial(
        pl.kernel, mesh=mesh, out_type=[out, out],
        scratch_types=[pltpu.VMEM((SC_GATHER_ROWS,), I32), pltpu.VMEM((SC_GATHER_ROWS,) + row, pool_k.dtype),
                       pltpu.VMEM((SC_GATHER_ROWS,) + row, pool_v.dtype), pltpu.SemaphoreType.DMA,
                       pltpu.SemaphoreType.DMA],
        compiler_params=pltpu.CompilerParams(use_tc_tiling_on_sc=True),
        name="gather_kv_rows",
    )
    def gather(k_hbm, v_hbm, ids_hbm, ko_hbm, vo_hbm, ids_v, k_v, v_v, ksem, vsem):
        worker = lax.axis_index("subcore") * mesh.num_cores + lax.axis_index("core")

        @pl.loop(0, per_worker // SC_GATHER_ROWS)
        def _(j):
            off = pl.multiple_of(worker * per_worker + j * SC_GATHER_ROWS, SC_GATHER_ROWS)
            pltpu.sync_copy(ids_hbm.at[pl.ds(off, SC_GATHER_ROWS)], ids_v)
            kcopy = pltpu.async_copy(k_hbm.at[ids_v], k_v, ksem)
            vcopy = pltpu.async_copy(v_hbm.at[ids_v], v_v, vsem)
            kcopy.wait()
            pltpu.sync_copy(k_v, ko_hbm.at[pl.ds(off, SC_GATHER_ROWS)])
            vcopy.wait()
            pltpu.sync_copy(v_v, vo_hbm.at[pl.ds(off, SC_GATHER_ROWS)])

    return gather(pool_k, pool_v, ids)


def _attn_sample_kernel(q_ref, kn_ref, vn_ref, bias_ref, kg_ref, vg_ref, o_ref, kall, vall):
    past_rows = kg_ref.shape[1]
    zeros = jnp.zeros((LANES - N_HEADS, HEAD_DIM), F32)
    for g in range(ATT_SEQS):
        kall[:past_rows, :] = kg_ref[g].astype(BF16)
        vall[:past_rows, :] = vg_ref[g].astype(BF16)
        kall[past_rows:, :] = jnp.concatenate([kn_ref[g], zeros], axis=0).astype(BF16)
        vall[past_rows:, :] = jnp.concatenate([vn_ref[g], zeros], axis=0).astype(BF16)
        s = lax.dot_general(q_ref[g], kall[...], NT_DIMS, preferred_element_type=F32)
        own = (lax.broadcasted_iota(I32, s.shape, 1) & (N_HEADS - 1)) == lax.broadcasted_iota(I32, s.shape, 0)
        s = jnp.where(own, s + bias_ref[g], NEG)
        m = jnp.max(s, axis=-1, keepdims=True)
        p = jnp.exp2(s - m)
        l = jnp.sum(p, axis=-1, keepdims=True)
        o_ref[g] = jnp.dot(p.astype(BF16), vall[...], preferred_element_type=F32) / l


def _attn_sample(q3, kn3, vn3, bias3, kg, vg):
    db, past_rows, _ = kg.shape
    width = past_rows + LANES
    assert db % ATT_SEQS == 0 and bias3.shape == (db, 1, width)
    per_step = lambda r, w: pl.BlockSpec((ATT_SEQS, r, w), lambda d: (d, 0, 0))
    return pl.pallas_call(
        _attn_sample_kernel,
        grid=(db // ATT_SEQS,),
        in_specs=[per_step(N_HEADS, HEAD_DIM)] * 3 + [per_step(1, width)] + [per_step(past_rows, HEAD_DIM)] * 2,
        out_specs=per_step(N_HEADS, HEAD_DIM),
        out_shape=jax.ShapeDtypeStruct((db, N_HEADS, HEAD_DIM), F32),
        scratch_shapes=[pltpu.VMEM((width, HEAD_DIM), BF16), pltpu.VMEM((width, HEAD_DIM), BF16)],
        compiler_params=_cparams("arbitrary"),
        name="attn_sample",
    )(q3, kn3, vn3, bias3, kg, vg)


def _ffn_sample_kernel(x_ref, g_ref, wup_ref, wc_ref, bc_ref, wdn_ref, st0_ref, st1_ref, gf_ref,
                       out_ref, gate_ref, wupb_ref, wdnb_ref, a_scr, acc_scr, *, final_norm):
    t = pl.program_id(0)

    @pl.when(t == 0)
    def _():
        a_scr[...] = _rms(x_ref[...], g_ref[...]).astype(BF16)
        acc_scr[...] = jnp.zeros(acc_scr.shape, F32)

    wup_b = wup_ref[...].astype(BF16)
    wupb_ref[...] = wup_b
    r = jnp.dot(a_scr[...], wup_b, preferred_element_type=F32)

    @pl.when(t < N_FFS)
    def _():
        gate_ref[:, pl.ds(pl.multiple_of(t * FFS_COLS, LANES), FFS_COLS)] = r

    @pl.when(t >= N_FFS)
    def _():
        cs = pl.ds(pl.multiple_of((t - N_FFS) * FFS_COLS, LANES), FFS_COLS)
        gc = (bc_ref[:, cs] + wc_ref[0:1, cs] * st0_ref[:, cs] + wc_ref[1:2, cs] * st1_ref[:, cs]
              + wc_ref[2:3, cs] * gate_ref[:, cs])
        wdn_b = wdn_ref[...].astype(BF16)
        wdnb_ref[...] = wdn_b
        acc_scr[...] += jnp.dot((jax.nn.gelu(gc) * r).astype(BF16), wdn_b, preferred_element_type=F32)

    @pl.when(t == 2 * N_FFS - 1)
    def _():
        y = x_ref[...] + acc_scr[...]
        out_ref[...] = _rms(y, gf_ref[...]) if final_norm else y


def _ffn_sample(x, g, wup, wc, bc, wdn, st0, st1, gf, layer, final_norm):
    db = x.shape[0]
    full = lambda a: pl.BlockSpec(a.shape, lambda t: (0,) * a.ndim)
    dn_blk = lambda t: jnp.maximum(t - N_FFS, 0)
    return pl.pallas_call(
        functools.partial(_ffn_sample_kernel, final_norm=final_norm),
        grid=(2 * N_FFS,),
        in_specs=[full(x), full(g), pl.BlockSpec((None, D_MODEL, FFS_COLS), lambda t: (layer, 0, t)), full(wc),
                  full(bc), pl.BlockSpec((None, FFS_COLS, D_MODEL), lambda t: (layer, dn_blk(t), 0)), full(st0),
                  full(st1), full(gf)],
        out_specs=[pl.BlockSpec((db, D_MODEL), lambda t: (0, 0)), pl.BlockSpec((db, D_FF), lambda t: (0, 0)),
                   pl.BlockSpec((D_MODEL, FFS_COLS), lambda t: (0, t)),
                   pl.BlockSpec((FFS_COLS, D_MODEL), lambda t: (dn_blk(t), 0))],
        out_shape=[jax.ShapeDtypeStruct((db, D_MODEL), F32), jax.ShapeDtypeStruct((db, D_FF), F32),
                   jax.ShapeDtypeStruct((D_MODEL, 2 * D_FF), BF16), jax.ShapeDtypeStruct((D_FF, D_MODEL), BF16)],
        scratch_shapes=[pltpu.VMEM((db, D_MODEL), BF16), pltpu.VMEM((db, D_MODEL), F32)],
        compiler_params=_cparams("arbitrary"),
        name="ffn_sample_final" if final_norm else "ffn_sample",
    )(x, g, wup, wc, bc, wdn, st0, st1, gf)


def _attn_out_sample_kernel(x_ref, o_ref, wo_ref, out_ref):
    out_ref[...] = x_ref[...] + jnp.dot(o_ref[...].astype(BF16), wo_ref[...], preferred_element_type=F32)


def _rec_sample_kernel(x_ref, gm_ref, win_ref, wcr_ref, bcr_ref, wg_ref, bga_ref, bgx_ref, lam_ref, wor_ref,
                       cs0_ref, cs1_ref, cs2_ref, h0_ref, out_ref, hnew_ref, xb_ref, a_scr, u_scr):
    x = x_ref[...]
    a_scr[...] = _rms(x, gm_ref[...]).astype(BF16)
    nblk = MXU_N // LANES
    ncs = -RG_C * jax.nn.softplus(-lam_ref[...])
    for c in range(D_MODEL // MXU_N):
        cs = slice(c * MXU_N, (c + 1) * MXU_N)
        xb = jnp.dot(a_scr[...], win_ref[:, D_MODEL + c * MXU_N:D_MODEL + (c + 1) * MXU_N],
                     preferred_element_type=F32)
        gate = jnp.dot(a_scr[...], win_ref[:, cs], preferred_element_type=F32)
        xb_ref[:, cs] = xb
        xc = (bcr_ref[:, cs] + wcr_ref[0:1, cs] * cs0_ref[:, cs] + wcr_ref[1:2, cs] * cs1_ref[:, cs]
              + wcr_ref[2:3, cs] * cs2_ref[:, cs] + wcr_ref[3:4, cs] * xb)
        for k in range(nblk):
            n = c * nblk + k
            ls = slice(n * LANES, (n + 1) * LANES)
            xcn = xc[:, k * LANES:(k + 1) * LANES]
            gg = jnp.dot(xcn.astype(BF16), wg_ref[n], preferred_element_type=F32)
            a_t, b_t = _lru_coeffs(xcn, gg, bga_ref[:, ls], bgx_ref[:, ls], ncs[:, ls])
            h = a_t * h0_ref[:, ls] + b_t
            hnew_ref[:, ls] = h
            u_scr[:, ls] = (jax.nn.gelu(gate[:, k * LANES:(k + 1) * LANES]) * h).astype(BF16)
    out_ref[...] = x + jnp.dot(u_scr[...], wor_ref[...], preferred_element_type=F32)


def _sample_call(kernel, name, args, out_widths, n_scratch):
    db = args[0].shape[0]
    return pl.pallas_call(
        kernel,
        grid=(1,),
        in_specs=[_const_spec(a.shape) for a in args],
        out_specs=[pl.BlockSpec((db, w), lambda i: (0, 0)) for w in out_widths],
        out_shape=[jax.ShapeDtypeStruct((db, w), F32) for w in out_widths],
        scratch_shapes=[pltpu.VMEM((db, D_MODEL), BF16)] * n_scratch,
        compiler_params=_cparams("arbitrary"),
        name=name,
    )(*args)


def _rope_tables(pos):
    posf = pos.astype(F32)[:, None]

    def cs(d):
        half = d // 2
        inv = ROPE_THETA ** (-jnp.arange(half, dtype=F32) * 2.0 / d)
        ang = posf * inv[None, :]
        return jnp.cos(ang), jnp.sin(ang)

    c, s = cs(HEAD_DIM)
    cos = jnp.concatenate([c, c], axis=-1)
    sin = jnp.concatenate([-s, s], axis=-1)
    c, s = cs(IDX_DIM)
    one, zero = jnp.ones_like(c), jnp.zeros_like(c)
    tile = lambda parts: jnp.concatenate(parts, axis=-1)
    icos = tile([c, c, one, one])
    isa = tile([-s, zero, zero, zero])
    isb = tile([zero, s, zero, zero])
    return cos, sin, icos, isa, isb


def _split_attn_in(w):
    wt = jnp.swapaxes(w, 0, 1).astype(BF16)
    qkv = N_HEADS * HEAD_DIM
    wqk, wv, wi = wt[:2 * qkv], wt[2 * qkv:3 * qkv], wt[3 * qkv:]
    pad = lambda a: jnp.pad(a, ((0, LANES - a.shape[0]), (0, 0)))
    groups = [pad(wi[h * IDX_DIM:(h + 1) * IDX_DIM]) for h in range(IDX_HEADS)]
    groups.append(pad(wi[IDX_HEADS * IDX_DIM:]))
    return wqk, wv, jnp.concatenate(groups, axis=0)


def kernel(x_prompt, x_sample, cache_k, cache_v, cache_kidx, state_lru_h, state_lru_conv, state_ffn_conv,
           page_table, norm_mix, norm_ffn, norm_final, w_attn_in, w_attn_out, w_rec_in, w_rec_conv, b_rec_conv,
           w_gate_a, b_gate_a, w_gate_x, b_gate_x, lru_lambda, w_rec_out, w_ffn_up, w_ffn_conv, b_ffn_conv,
           w_ffn_down):
    bsz, seq, d = x_prompt.shape
    db = x_sample.shape[0]
    n_pages = page_table.shape[1]
    past = n_pages * PAGE
    assert d == D_MODEL and x_sample.shape[1] == 1 and seq % ROW_BLOCK == 0 and seq % TQ == 0
    assert min(TOPK, seq // 4) == TOPK and min(TOPK, (past + 1) // 4) == TOPK

    vec = lambda a: a.reshape(1, -1)
    wqk, wv, widx = _split_attn_in(w_attn_in[0])
    wo_attn = w_attn_out[0].astype(BF16)
    w_rin = w_rec_in[0].astype(BF16)
    w_gates = jnp.concatenate([w_gate_a[0], w_gate_x[0]], axis=-1).astype(BF16)
    wo_rec = w_rec_out[0].astype(BF16)
    rec_vecs = (w_rec_conv[0], vec(b_rec_conv[0]), w_gates, vec(b_gate_a[0]), vec(b_gate_x[0]),
                vec(lru_lambda[0]), wo_rec)
    gfin = vec(norm_final)
    st = state_ffn_conv

    def ffn_sample(x, i, final_norm):
        return _ffn_sample(x, vec(norm_ffn[i]), w_ffn_up, w_ffn_conv[i], vec(b_ffn_conv[i]), w_ffn_down,
                           st[i, :, 0], st[i, :, 1], gfin, i, final_norm)

    def ffn_prompt(x, i, wup_b, wdn_b, final_norm):
        return _ffn_prompt(x, vec(norm_ffn[i]), wup_b, w_ffn_conv[i], vec(b_ffn_conv[i]), wdn_b, gfin, bsz, seq,
                           final_norm)

    xp = x_prompt.reshape(bsz * seq, d)
    q, kf, kb, vf, vt, qi, kiw = _attn_in(xp, vec(norm_mix[0]), wqk, wv, widx, _rope_tables(jnp.arange(seq)),
                                          ROW_BLOCK, seq // ROW_BLOCK)
    wit = kiw[:, IDX_DIM:IDX_DIM + IDX_HEADS].T
    hp_attn = _dsa_prompt(q, kb, vt, qi, kiw, wit, xp, wo_attn, bsz, seq)

    xs = x_sample.reshape(db, d)
    tabs = tuple(jnp.broadcast_to(t, (db, t.shape[1])) for t in _rope_tables(jnp.full((1,), past)))
    qs, kfs, _, vfs, _, qis, kiws = _attn_in(xs, vec(norm_mix[0]), wqk, wv, widx, tabs, db, 1)
    pt = page_table.reshape(-1)
    scores = _idx_sample(pt, qis.reshape(db, IDX_HEADS, LANES),
                         kiws[:, IDX_DIM:IDX_DIM + IDX_HEADS].reshape(db, IDX_HEADS, 1),
                         kiws.reshape(db, 1, LANES), jnp.swapaxes(cache_kidx[0], 1, 2), n_pages)
    cache_rows = (page_table[:, :, None] * PAGE + jnp.arange(PAGE, dtype=I32)).reshape(db, past).astype(F32)
    ids, bias = _select_sample(scores.reshape(db, -1), cache_rows, past + 1)
    heads = (N_HEADS, HEAD_DIM)
    key_rows = lambda pool: pool[0].reshape(-1, *heads)
    ids, _ = lax.optimization_barrier((ids, kb))
    kg, vg = _gather_kv_rows(key_rows(cache_k), key_rows(cache_v), ids.reshape(-1))
    bias_rows = jnp.pad(jnp.repeat(bias[:, :TOPK + 1], N_HEADS, axis=1), ((0, 0), (0, LANES - N_HEADS)),
                        constant_values=NEG)
    bias_rows, _ = lax.optimization_barrier((bias_rows, hp_attn))
    seq_rows = lambda g: g.reshape(db, TOPK * N_HEADS, HEAD_DIM)
    o_s = _attn_sample(qs.reshape(db, *heads), kfs.reshape(db, *heads), vfs.reshape(db, *heads),
                       bias_rows.reshape(db, 1, -1), seq_rows(kg), seq_rows(vg))
    (hs,) = _sample_call(_attn_out_sample_kernel, "attn_out_sample", (xs, o_s.reshape(db, d), wo_attn),
                         (D_MODEL,), 0)
    hs, gate0, wup_b, wdn_b = ffn_sample(hs, 0, False)
    hp, ftail0 = ffn_prompt(hp_attn, 0, wup_b, wdn_b, False)

    hp, hlast, ctail = _rec_prompt(hp, vec(norm_mix[1]), w_rin, *rec_vecs, bsz, seq)
    cst = state_lru_conv[0]
    hs, hnew, xbs = _sample_call(
        _rec_sample_kernel, "rec_sample",
        (hs, vec(norm_mix[1]), w_rin, *rec_vecs, cst[:, 0], cst[:, 1], cst[:, 2], state_lru_h[0]),
        (D_MODEL, D_MODEL, D_MODEL), 2)
    ys, gate1, wup_b, wdn_b = ffn_sample(hs, 1, True)
    yp, ftail1 = ffn_prompt(hp, 1, wup_b, wdn_b, True)

    return (
        yp.reshape(bsz, seq, d),
        ys.reshape(db, 1, d),
        kf.reshape(1, bsz, seq, *heads),
        vf.reshape(1, bsz, seq, *heads),
        kiw[:, :IDX_DIM].reshape(1, bsz, seq, IDX_DIM),
        kfs.reshape(1, db, 1, *heads),
        vfs.reshape(1, db, 1, *heads),
        kiws[:, :IDX_DIM].reshape(1, db, 1, IDX_DIM),
        hlast[None, :, 0, :],
        ctail[None, :, SUBLANES - 3:, :],
        hnew[None],
        jnp.stack([cst[:, 1], cst[:, 2], xbs], axis=1)[None],
        jnp.stack([ftail0[:, SUBLANES - 2:], ftail1[:, SUBLANES - 2:]]),
        jnp.stack([jnp.stack([st[0, :, 1], gate0], axis=1), jnp.stack([st[1, :, 1], gate1], axis=1)]),
    )
```

```python
import functools

import jax
import jax.numpy as jnp
from jax import lax
from jax.experimental import pallas as pl
from jax.experimental.pallas import tpu as pltpu
from jax.experimental.pallas import tpu_sc as plsc

F32 = jnp.float32
BF16 = jnp.bfloat16
I32 = jnp.int32
I16 = jnp.int16

D_MODEL = 1024
N_HEADS = 8
HEAD_DIM = 128
IDX_HEADS = 4
IDX_DIM = 64
TOPK = 256
PAGE = 128
ROPE_THETA = 10000.0
IDX_SCALE = (IDX_DIM * IDX_HEADS) ** -0.5
Q_SCALE = HEAD_DIM ** -0.5 * 1.4426950408889634
RG_C = 8.0
D_FF = 2816
RMS_EPS = 1e-6

LANES = 128
SUBLANES = 8
MXU_N = 256
IDX_W = (IDX_HEADS + 1) * LANES
INT_MIN = -(2 ** 31)
NEG = -1e30
VMEM_LIMIT = 52 * 1024 * 1024

ROW_BLOCK = 512
FFN_ROWS = 1024
TQ = 256
TK = 256
COUNT_ROWS = 32
COUNT_ROWS_16 = 64
IDX_SEQS = 8
ATT_SEQS = 8
SC_GATHER_ROWS = 32
FF_CHUNK = 768
FFS_COLS = 1408
N_FFS = D_FF // FFS_COLS
NT_DIMS = (((1,), (1,)), ((), ()))


def _cparams(*sem):
    return pltpu.CompilerParams(dimension_semantics=sem if sem else None, vmem_limit_bytes=VMEM_LIMIT)


def _const_spec(shape):
    nd = len(shape)
    return pl.BlockSpec(shape, lambda *_: (0,) * nd, pipeline_mode=pl.Buffered(1))


def _rms(x, g):
    return x * lax.rsqrt(jnp.mean(x * x, axis=-1, keepdims=True) + RMS_EPS) * g


def _shift_rows(x, s, prev_rows):
    r = pltpu.roll(x, s, 0)
    top = r[:SUBLANES]
    row = lax.broadcasted_iota(I32, top.shape, 0)
    for k in range(s):
        top = jnp.where(row == k, prev_rows[k], top)
    return jnp.concatenate([top, r[SUBLANES:]], axis=0)


def _sortable_key(score):
    bits = pltpu.bitcast(score + 0.0, I32)
    return jnp.where(bits < 0, bits ^ jnp.int32(0x7FFFFFFF), bits)


def _attn_in_kernel(x_ref, g_ref, wqk_ref, wv_ref, widx_ref, cos_ref, sin_ref, icos_ref, isa_ref, isb_ref,
                    q_ref, kf_ref, kb_ref, vf_ref, vt_ref, qi_ref, kiw_ref, a_scr):
    a_scr[...] = _rms(x_ref[...], g_ref[...]).astype(BF16)
    cos = cos_ref[...]
    sin = sin_ref[...]
    for c in range(2 * D_MODEL // MXU_N):
        r = lax.dot_general(a_scr[...], wqk_ref[c * MXU_N:(c + 1) * MXU_N, :], NT_DIMS, preferred_element_type=F32)
        for hh in range(MXU_N // HEAD_DIM):
            xh = r[:, hh * HEAD_DIM:(hh + 1) * HEAD_DIM]
            y = xh * cos + pltpu.roll(xh, HEAD_DIM // 2, 1) * sin
            col = (c * MXU_N) % D_MODEL + hh * HEAD_DIM
            if c < D_MODEL // MXU_N:
                q_ref[:, col:col + HEAD_DIM] = (y * Q_SCALE).astype(BF16)
            else:
                kf_ref[:, col:col + HEAD_DIM] = y
                kb_ref[:, col:col + HEAD_DIM] = y.astype(BF16)
    for c in range(D_MODEL // MXU_N):
        r = lax.dot_general(a_scr[...], wv_ref[c * MXU_N:(c + 1) * MXU_N, :], NT_DIMS, preferred_element_type=F32)
        vf_ref[:, c * MXU_N:(c + 1) * MXU_N] = r
        vt_ref[c * MXU_N:(c + 1) * MXU_N, :] = r.T.astype(BF16)
    ri = lax.dot_general(a_scr[...], widx_ref[...], NT_DIMS, preferred_element_type=F32)
    groups = lambda t: jnp.concatenate([t] * (IDX_W // LANES), axis=1)
    yi = (ri * groups(icos_ref[...]) + pltpu.roll(ri, IDX_W - IDX_DIM // 2, 1) * groups(isa_ref[...])
          + pltpu.roll(ri, IDX_DIM // 2, 1) * groups(isb_ref[...]))
    qi_ref[...] = yi[:, :IDX_HEADS * LANES].astype(BF16)
    kiw_ref[...] = yi[:, IDX_HEADS * LANES:]


def _attn_in(x, g, wqk, wv, widx, tabs, rows, n_pos_blocks):
    n = x.shape[0]
    cos, sin, icos, isa, isb = tabs
    row_spec = lambda w: pl.BlockSpec((rows, w), lambda i: (i, 0))
    tab_spec = lambda w: pl.BlockSpec((rows, w), lambda i: (i % n_pos_blocks, 0))
    vt_spec = pl.BlockSpec((None, D_MODEL, rows), lambda i: (i // n_pos_blocks, 0, i % n_pos_blocks))
    return pl.pallas_call(
        _attn_in_kernel,
        grid=(n // rows,),
        in_specs=[row_spec(D_MODEL), _const_spec((1, D_MODEL)), _const_spec(wqk.shape), _const_spec(wv.shape),
                  _const_spec(widx.shape), tab_spec(HEAD_DIM), tab_spec(HEAD_DIM), tab_spec(LANES),
                  tab_spec(LANES), tab_spec(LANES)],
        out_specs=[row_spec(D_MODEL)] * 4 + [vt_spec, row_spec(IDX_HEADS * LANES), row_spec(LANES)],
        out_shape=[jax.ShapeDtypeStruct((n, D_MODEL), BF16), jax.ShapeDtypeStruct((n, D_MODEL), F32),
                   jax.ShapeDtypeStruct((n, D_MODEL), BF16), jax.ShapeDtypeStruct((n, D_MODEL), F32),
                   jax.ShapeDtypeStruct((n // (rows * n_pos_blocks), D_MODEL, rows * n_pos_blocks), BF16),
                   jax.ShapeDtypeStruct((n, IDX_HEADS * LANES), BF16),
                   jax.ShapeDtypeStruct((n, LANES), F32)],
        scratch_shapes=[pltpu.VMEM((rows, D_MODEL), BF16)],
        compiler_params=_cparams("arbitrary"),
        name="attn_in",
    )(x, g, wqk, wv, widx, cos, sin, icos, isa, isb)


def _kth_largest_key(count_ge, n_total):
    kf = float(TOPK)
    c0 = count_ge(0)
    t = jnp.where(c0 >= kf, jnp.int32(0), jnp.int32(INT_MIN))
    ct = jnp.where(c0 >= kf, c0, n_total)

    def bit_body(it, carry):
        t, ct = carry
        cand = t + (jnp.int32(1) << (30 - it))
        cnt = count_ge(cand)
        return jnp.where(cnt >= kf, cand, t), jnp.where(cnt >= kf, cnt, ct)

    return lax.fori_loop(0, 31, bit_body, (t, ct))


def _kth_largest_half(count_ge, above, count_all):
    kf = float(TOPK)
    c0 = above + count_ge(0)
    t = jnp.where(c0 >= kf, jnp.int32(0), jnp.int32(-(2 ** 15)))
    ct = jnp.where(c0 >= kf, c0, count_all)

    def bit_body(it, carry):
        t, ct = carry
        cand = t + (jnp.int32(1) << (14 - it))
        cnt = above + count_ge(cand)
        return jnp.where(cnt >= kf, cand, t), jnp.where(cnt >= kf, cnt, ct)

    return lax.fori_loop(0, 15, bit_body, (t, ct))


def _tie_cutoff(count_tie_le, need, n_bits):
    def bit_body(it, c):
        cand = c + (jnp.int32(1) << (n_bits - 1 - it))
        return jnp.where(count_tie_le(cand - 1) < need, cand, c)

    return lax.fori_loop(0, n_bits, bit_body, jnp.zeros_like(need, dtype=I32))


def _dsa_prompt_kernel(q_ref, kb_ref, vt_ref, qi_ref, qin_ref, kiwk_ref, wit_ref, witn_ref, x_ref, wo_ref, out_ref,
                       key_scr, bias_scr, kib_scr, o_scr, acc_scr, s_scr, hi_scr, lo_scr):
    i = pl.program_id(1)
    nk = i + 1
    lane_q = lax.broadcasted_iota(I32, (1, TQ), 1)

    def rows(kc):
        return pl.ds(kc * TK if isinstance(kc, int) else pl.multiple_of(kc * TK, TK), TK)

    def spos(kc):
        return kc * TK + lax.broadcasted_iota(I32, (TK, 1), 0)

    def colsum(a):
        return jnp.sum(a.reshape(TK // COUNT_ROWS, COUNT_ROWS, TQ), axis=0)

    def score_chunk(kc, qidx_ref, wis, qpos):
        kic = kib_scr[rows(kc), :]
        sc = jnp.zeros((TK, TQ), F32)
        for h in range(IDX_HEADS):
            s = lax.dot_general(kic, qidx_ref[:, h * LANES:(h + 1) * LANES], NT_DIMS, preferred_element_type=F32)
            sc = sc + wis[h:h + 1, :] * jnp.maximum(s, 0.0)
        key = _sortable_key(sc)
        if qpos is not None:
            key = jnp.where(spos(kc) <= qpos, key, jnp.int32(INT_MIN))
        key_scr[rows(kc), :] = key
        hi_scr[rows(kc), :] = (key >> 16).astype(I16)
        lo_scr[rows(kc), :] = ((key & 0xFFFF) - 2 ** 15).astype(I16)

    @pl.when(i == 0)
    def _():
        kib_scr[...] = kiwk_ref[...].astype(BF16)
        score_chunk(0, qi_ref, wit_ref[...] * IDX_SCALE, lane_q)

    wis_next = witn_ref[...] * IDX_SCALE

    def count_ge(cand):
        def body(kc, acc):
            return acc + colsum(jnp.where(key_scr[rows(kc), :] >= cand, 1.0, 0.0))
        acc = lax.fori_loop(0, nk, body, jnp.zeros((COUNT_ROWS, TQ), F32))
        return jnp.sum(acc, axis=0, keepdims=True)

    def count_ge_half(half_scr):
        def count(cand):
            c16 = jnp.asarray(cand, I32).astype(I16)

            def body(kc, acc):
                ind = jnp.where(half_scr[rows(kc), :] >= c16, jnp.int16(1), jnp.int16(0))
                for r in range(TK // COUNT_ROWS_16):
                    acc = acc + ind[r * COUNT_ROWS_16:(r + 1) * COUNT_ROWS_16]
                return acc

            acc = lax.fori_loop(0, nk, body, jnp.zeros((COUNT_ROWS_16, TQ), I16))
            return jnp.sum(acc.astype(F32), axis=0, keepdims=True)
        return count

    count_hi, count_lo = count_ge_half(hi_scr), count_ge_half(lo_scr)
    thi, cnt_hi = _kth_largest_half(count_hi, 0.0, (nk * TK).astype(F32))
    above = jnp.where(thi == 2 ** 15 - 1, 0.0, count_hi(jnp.minimum(thi + 1, 2 ** 15 - 1)))
    thi16 = thi.astype(I16)

    def lo_body(kc, carry):
        lo_scr[rows(kc), :] = jnp.where(hi_scr[rows(kc), :] == thi16, lo_scr[rows(kc), :], jnp.int16(-(2 ** 15)))
        return carry

    lax.fori_loop(0, nk, lo_body, 0)
    tlo, cnt_thr = _kth_largest_half(count_lo, above, cnt_hi)
    thr = (thi << 16) | (tlo + 2 ** 15)
    short = thr == INT_MIN
    has_ties = jnp.max(jnp.where(short, 0.0, cnt_thr)) > float(TOPK)

    @pl.when(jnp.logical_not(has_ties))
    def _():
        thr_vis = jnp.maximum(thr, jnp.int32(INT_MIN + 1))

        def bias_body(kc, carry):
            bias_scr[rows(kc), :] = jnp.where(key_scr[rows(kc), :] >= thr_vis, 0.0, NEG)
            return carry

        lax.fori_loop(0, nk, bias_body, 0)

    @pl.when(has_ties)
    def _():
        need = float(TOPK) - count_ge(thr + 1)

        def eq_body(kc, carry):
            bias_scr[rows(kc), :] = jnp.where(key_scr[rows(kc), :] == thr, 1.0, 0.0)
            return carry

        lax.fori_loop(0, nk, eq_body, 0)

        def count_tie_le(cm):
            def body(kc, acc):
                return acc + colsum(jnp.where(spos(kc) <= cm, bias_scr[rows(kc), :], 0.0))
            acc = lax.fori_loop(0, nk, body, jnp.zeros((COUNT_ROWS, TQ), F32))
            return jnp.sum(acc, axis=0, keepdims=True)

        cut = jnp.where(short, jnp.int32(-1), _tie_cutoff(count_tie_le, need, 11))

        def bias_body(kc, carry):
            sel = (key_scr[rows(kc), :] > thr) | ((bias_scr[rows(kc), :] > 0.0) & (spos(kc) <= cut))
            bias_scr[rows(kc), :] = jnp.where(sel, 0.0, NEG)
            return carry

        lax.fori_loop(0, nk, bias_body, 0)

    acc_scr[...] = jnp.zeros(acc_scr.shape, F32)

    def att_body(kc, carry):
        ms, ls = carry
        for h in range(N_HEADS):
            hs = slice(h * HEAD_DIM, (h + 1) * HEAD_DIM)
            s_scr[h] = lax.dot_general(kb_ref[rows(kc), hs], q_ref[:, hs], NT_DIMS, preferred_element_type=F32)
        bias = bias_scr[rows(kc), :]
        score_chunk(kc, qin_ref, wis_next, None)
        new_ms, new_ls = [], []
        for h in range(N_HEADS):
            hs = slice(h * HEAD_DIM, (h + 1) * HEAD_DIM)
            s = s_scr[h] + bias
            m_new = jnp.maximum(ms[h], jnp.max(s, axis=0, keepdims=True))
            alpha = jnp.exp2(ms[h] - m_new)
            p = jnp.exp2(s - m_new)
            new_ms.append(m_new)
            new_ls.append(alpha * ls[h] + jnp.sum(p, axis=0, keepdims=True))
            pv = jnp.dot(vt_ref[hs, rows(kc)], p.astype(BF16), preferred_element_type=F32)
            acc_scr[hs, :] = alpha * acc_scr[hs, :] + pv
        return tuple(new_ms), tuple(new_ls)

    init = ((jnp.full((1, TQ), NEG, F32),) * N_HEADS, (jnp.zeros((1, TQ), F32),) * N_HEADS)
    _, ls = lax.fori_loop(0, nk, att_body, init)

    @pl.when(i < pl.num_programs(1) - 1)
    def _():
        score_chunk(nk, qin_ref, wis_next, (i + 1) * TQ + lane_q)

    for h in range(N_HEADS):
        hs = slice(h * HEAD_DIM, (h + 1) * HEAD_DIM)
        o_scr[:, hs] = (acc_scr[hs, :] / ls[h]).T.astype(BF16)
    out_ref[...] = x_ref[...] + jnp.dot(o_scr[...], wo_ref[...], preferred_element_type=F32)


def _dsa_prompt(q, kb, vt, qi, kiw, wit, x, wo, bsz, seq):
    n = x.shape[0]
    nq = seq // TQ
    qrow = lambda w: pl.BlockSpec((TQ, w), lambda b, i: (b * nq + i, 0))
    brow = lambda w: pl.BlockSpec((seq, w), lambda b, i: (b, 0))
    nxt = lambda b, i: b * nq + jnp.minimum(i + 1, nq - 1)
    return pl.pallas_call(
        _dsa_prompt_kernel,
        grid=(bsz, nq),
        in_specs=[qrow(D_MODEL), brow(D_MODEL), pl.BlockSpec((None, D_MODEL, seq), lambda b, i: (b, 0, 0)),
                  qrow(IDX_HEADS * LANES), pl.BlockSpec((TQ, IDX_HEADS * LANES), lambda b, i: (nxt(b, i), 0)),
                  brow(LANES), pl.BlockSpec((IDX_HEADS, TQ), lambda b, i: (0, b * nq + i)),
                  pl.BlockSpec((IDX_HEADS, TQ), lambda b, i: (0, nxt(b, i))),
                  qrow(D_MODEL), _const_spec(wo.shape)],
        out_specs=qrow(D_MODEL),
        out_shape=jax.ShapeDtypeStruct((n, D_MODEL), F32),
        scratch_shapes=[pltpu.VMEM((seq, TQ), I32), pltpu.VMEM((seq, TQ), F32), pltpu.VMEM((seq, LANES), BF16),
                        pltpu.VMEM((TQ, D_MODEL), BF16), pltpu.VMEM((D_MODEL, TQ), F32),
                        pltpu.VMEM((N_HEADS, TK, TQ), F32), pltpu.VMEM((seq, TQ), I16),
                        pltpu.VMEM((seq, TQ), I16)],
        compiler_params=_cparams("arbitrary", "arbitrary"),
        name="dsa_prompt",
    )(q, kb, vt, qi, qi, kiw, wit, wit, x, wo)


def _glu_hidden(a_scr, wup_ref, wc_ref, bc_ref, h_scr, gate_taps):
    for lo in range(0, D_FF, FF_CHUNK):
        cs = slice(lo, min(lo + FF_CHUNK, D_FF))
        gate = jnp.dot(a_scr[...], wup_ref[:, cs], preferred_element_type=F32)
        val = jnp.dot(a_scr[...], wup_ref[:, D_FF + cs.start:D_FF + cs.stop], preferred_element_type=F32)
        g2, g1 = gate_taps(cs, gate)
        gc = bc_ref[:, cs] + wc_ref[0:1, cs] * g2 + wc_ref[1:2, cs] * g1 + wc_ref[2:3, cs] * gate
        h_scr[:, cs] = (jax.nn.gelu(gc) * val).astype(BF16)


def _ffn_prompt_kernel(x_ref, g_ref, wup_ref, wc_ref, bc_ref, wdn_ref, gf_ref, out_ref, tail_ref,
                       a_scr, h_scr, carry_scr, *, final_norm):
    j = pl.program_id(1)
    rows = x_ref.shape[0]
    a_scr[...] = _rms(x_ref[...], g_ref[...]).astype(BF16)

    @pl.when(j == 0)
    def _():
        carry_scr[...] = jnp.zeros(carry_scr.shape, F32)

    def gate_taps(cs, gate):
        p0 = carry_scr[SUBLANES - 2:SUBLANES - 1, cs]
        p1 = carry_scr[SUBLANES - 1:SUBLANES, cs]
        g1 = _shift_rows(gate, 1, [p1])
        g2 = _shift_rows(gate, 2, [p0, p1])
        carry_scr[:, cs] = gate[rows - SUBLANES:, :]
        tail_ref[0, :, cs] = gate[rows - SUBLANES:, :]
        return g2, g1

    _glu_hidden(a_scr, wup_ref, wc_ref, bc_ref, h_scr, gate_taps)
    y = x_ref[...] + jnp.dot(h_scr[...], wdn_ref[...], preferred_element_type=F32)
    out_ref[...] = _rms(y, gf_ref[...]) if final_norm else y


def _ffn_prompt(x, g, wup, wc, bc, wdn, gf, bsz, seq, final_norm):
    n = x.shape[0]
    assert seq % FFN_ROWS == 0
    nb = seq // FFN_ROWS
    row = pl.BlockSpec((FFN_ROWS, D_MODEL), lambda b, j: (b * nb + j, 0))
    return pl.pallas_call(
        functools.partial(_ffn_prompt_kernel, final_norm=final_norm),
        grid=(bsz, nb),
        in_specs=[row, _const_spec((1, D_MODEL)), _const_spec(wup.shape), _const_spec(wc.shape),
                  _const_spec(bc.shape), _const_spec(wdn.shape), _const_spec((1, D_MODEL))],
        out_specs=[row, pl.BlockSpec((1, SUBLANES, D_FF), lambda b, j: (b, 0, 0))],
        out_shape=[jax.ShapeDtypeStruct((n, D_MODEL), F32), jax.ShapeDtypeStruct((bsz, SUBLANES, D_FF), F32)],
        scratch_shapes=[pltpu.VMEM((FFN_ROWS, D_MODEL), BF16), pltpu.VMEM((FFN_ROWS, D_FF), BF16),
                        pltpu.VMEM((SUBLANES, D_FF), F32)],
        compiler_params=_cparams("arbitrary", "arbitrary"),
        name="ffn_prompt_final" if final_norm else "ffn_prompt",
    )(x, g, wup, wc, bc, wdn, gf)


def _sigmoid(x):
    return 0.5 * jnp.tanh(0.5 * x) + 0.5


def _lru_coeffs(xc, gg, bga, bgx, neg_c_softplus):
    ig = _sigmoid(gg[:, LANES:] + bgx)
    half = 0.5 * neg_c_softplus
    log_a = jnp.tanh(0.5 * (gg[:, :LANES] + bga)) * half + half
    a = jnp.exp(log_a)
    z = -jnp.tanh(log_a) * (a * a + 1.0)
    root = jnp.where(z > 0.0, z * lax.rsqrt(z), 0.0)
    return a, root * ig * xc


def _group_scan(a, b):
    shape = a.shape
    grouped = (shape[0] // SUBLANES, SUBLANES, shape[1])
    a, b = a.reshape(grouped), b.reshape(grouped)
    row = lax.broadcasted_iota(I32, grouped, 1)
    for d in (1, 2, 4):
        inside = row >= d
        a_prev = jnp.where(inside, pltpu.roll(a, d, 1), 1.0)
        b_prev = jnp.where(inside, pltpu.roll(b, d, 1), 0.0)
        b = a * b_prev + b
        a = a * a_prev
    return a.reshape(shape), b.reshape(shape)


def _rec_prompt_kernel(x_ref, g_ref, win_ref, wc_ref, bc_ref, wg_ref, bga_ref, bgx_ref, lam_ref, wo_ref,
                       out_ref, hlast_ref, ctail_ref, a_scr, at_scr, bt_scr, u_scr, hcar_scr, ccar_scr, xb_scr,
                       gelu_scr):
    j = pl.program_id(1)
    rows = x_ref.shape[0]
    a_scr[...] = _rms(x_ref[...], g_ref[...]).astype(BF16)

    @pl.when(j == 0)
    def _():
        ccar_scr[...] = jnp.zeros(ccar_scr.shape, F32)
        hcar_scr[...] = jnp.zeros(hcar_scr.shape, F32)

    nblk = MXU_N // LANES
    ncs = -RG_C * jax.nn.softplus(-lam_ref[...])
    for c in range(D_MODEL // MXU_N):
        cs = slice(c * MXU_N, (c + 1) * MXU_N)
        xb_scr[:, cs] = jnp.dot(a_scr[...], win_ref[:, D_MODEL + c * MXU_N:D_MODEL + (c + 1) * MXU_N],
                                preferred_element_type=F32)
    for c in range(D_MODEL // MXU_N):
        cs = slice(c * MXU_N, (c + 1) * MXU_N)
        gelu_scr[:, cs] = jax.nn.gelu(jnp.dot(a_scr[...], win_ref[:, cs], preferred_element_type=F32))
    for c in range(D_MODEL // MXU_N):
        cs = slice(c * MXU_N, (c + 1) * MXU_N)
        xb = xb_scr[:, cs]
        prev = [ccar_scr[SUBLANES - 3 + k:SUBLANES - 2 + k, cs] for k in range(3)]
        xc = (bc_ref[:, cs] + wc_ref[0:1, cs] * _shift_rows(xb, 3, prev) + wc_ref[1:2, cs] * _shift_rows(xb, 2, prev[1:])
              + wc_ref[2:3, cs] * _shift_rows(xb, 1, prev[2:]) + wc_ref[3:4, cs] * xb)
        ccar_scr[:, cs] = xb[rows - SUBLANES:, :]
        ctail_ref[0, :, cs] = xb[rows - SUBLANES:, :]
        for k in range(nblk):
            n = c * nblk + k
            ls = slice(n * LANES, (n + 1) * LANES)
            xcn = xc[:, k * LANES:(k + 1) * LANES]
            gg = jnp.dot(xcn.astype(BF16), wg_ref[n], preferred_element_type=F32)
            a_t, b_t = _group_scan(*_lru_coeffs(xcn, gg, bga_ref[:, ls], bgx_ref[:, ls], ncs[:, ls]))
            at_scr[:, ls] = a_t
            bt_scr[:, ls] = b_t

    def group_step(g, h):
        r = pl.ds(pl.multiple_of(g * SUBLANES, SUBLANES), SUBLANES)
        hg = at_scr[r, :] * h + bt_scr[r, :]
        bt_scr[r, :] = hg
        return hg[SUBLANES - 1:, :]

    h_last = lax.fori_loop(0, rows // SUBLANES, group_step, hcar_scr[0:1, :], unroll=4)
    hcar_scr[...] = jnp.broadcast_to(h_last, hcar_scr.shape)
    hlast_ref[0] = jnp.broadcast_to(h_last, hcar_scr.shape)
    u_scr[...] = (gelu_scr[...] * bt_scr[...]).astype(BF16)
    out_ref[...] = x_ref[...] + jnp.dot(u_scr[...], wo_ref[...], preferred_element_type=F32)


def _rec_prompt(x, g, win, wc, bc, wg, bga, bgx, lam, wo, bsz, seq):
    n = x.shape[0]
    nb = seq // ROW_BLOCK
    row = pl.BlockSpec((ROW_BLOCK, D_MODEL), lambda b, j: (b * nb + j, 0))
    tail = pl.BlockSpec((1, SUBLANES, D_MODEL), lambda b, j: (b, 0, 0))
    vec = _const_spec((1, D_MODEL))
    return pl.pallas_call(
        _rec_prompt_kernel,
        grid=(bsz, nb),
        in_specs=[row, vec, _const_spec(win.shape), _const_spec(wc.shape), vec, _const_spec(wg.shape), vec, vec,
                  vec, _const_spec(wo.shape)],
        out_specs=[row, tail, tail],
        out_shape=[jax.ShapeDtypeStruct((n, D_MODEL), F32), jax.ShapeDtypeStruct((bsz, SUBLANES, D_MODEL), F32),
                   jax.ShapeDtypeStruct((bsz, SUBLANES, D_MODEL), F32)],
        scratch_shapes=[pltpu.VMEM((ROW_BLOCK, D_MODEL), BF16), pltpu.VMEM((ROW_BLOCK, D_MODEL), F32),
                        pltpu.VMEM((ROW_BLOCK, D_MODEL), F32), pltpu.VMEM((ROW_BLOCK, D_MODEL), BF16),
                        pltpu.VMEM((SUBLANES, D_MODEL), F32), pltpu.VMEM((SUBLANES, D_MODEL), F32),
                        pltpu.VMEM((ROW_BLOCK, D_MODEL), F32), pltpu.VMEM((ROW_BLOCK, D_MODEL), F32)],
        compiler_params=_cparams("arbitrary", "arbitrary"),
        name="rec_prompt",
    )(x, g, win, wc, bc, wg, bga, bgx, lam, wo)


def _idx_sample_kernel(pt_ref, qi_ref, wi_ref, kin_ref, *refs, n_pages):
    pages, out_ref = refs[:-1], refs[-1]
    lane = lax.broadcasted_iota(I32, (1, LANES), 1)
    for g in range(IDX_SEQS):
        qi = qi_ref[g]
        wi = wi_ref[g]
        for p in range(n_pages):
            page_t = pages[g * n_pages + p][...].astype(BF16)
            s = jnp.dot(qi[:, :IDX_DIM], page_t, preferred_element_type=F32)
            out_ref[g, :, p * PAGE:(p + 1) * PAGE] = (
                jnp.sum(wi * jnp.maximum(s, 0.0), axis=0, keepdims=True) * IDX_SCALE)
        s_new = jnp.sum(qi.astype(F32) * kin_ref[g].astype(BF16).astype(F32), axis=-1, keepdims=True)
        sc_new = jnp.sum(wi * jnp.maximum(s_new, 0.0), axis=0, keepdims=True) * IDX_SCALE
        out_ref[g, :, n_pages * PAGE:] = jnp.where(lane == 0, sc_new, -jnp.inf)


def _idx_sample(pt, qi3, wi3, kin3, pool_ki_t, n_pages):
    db = qi3.shape[0]
    assert db % IDX_SEQS == 0
    per_step = lambda shape: pl.BlockSpec((IDX_SEQS,) + shape, lambda d, pt: (d, 0, 0))
    page_spec = lambda g, p: pl.BlockSpec((None, IDX_DIM, PAGE),
                                          lambda d, pt: (pt[(d * IDX_SEQS + g) * n_pages + p], 0, 0))
    width = n_pages * PAGE + LANES
    return pl.pallas_call(
        functools.partial(_idx_sample_kernel, n_pages=n_pages),
        grid_spec=pltpu.PrefetchScalarGridSpec(
            num_scalar_prefetch=1, grid=(db // IDX_SEQS,),
            in_specs=[per_step((IDX_HEADS, LANES)), per_step((IDX_HEADS, 1)), per_step((1, LANES))]
            + [page_spec(g, p) for g in range(IDX_SEQS) for p in range(n_pages)],
            out_specs=per_step((1, width))),
        out_shape=jax.ShapeDtypeStruct((db, 1, width), F32),
        compiler_params=_cparams("arbitrary"),
        name="idx_sample",
    )(pt, qi3, wi3, kin3, *([pool_ki_t] * (IDX_SEQS * n_pages)))


def _select_sample_kernel(sc_ref, rows_ref, ids_ref, bias_ref, key_scr, tri_scr, rank_scr, ids_scr, *, n_keys):
    lane = lax.broadcasted_iota(I32, (1, sc_ref.shape[1]), 1)
    key_scr[...] = jnp.where(lane < n_keys, _sortable_key(sc_ref[...]), jnp.int32(INT_MIN))

    def count_ge(cand):
        return jnp.sum(jnp.where(key_scr[...] >= cand, 1.0, 0.0), axis=-1, keepdims=True)

    thr, _ = _kth_largest_key(count_ge, float(sc_ref.shape[1]))
    need = float(TOPK) - count_ge(thr + 1)

    def count_tie_le(cm):
        return jnp.sum(jnp.where((key_scr[...] == thr) & (lane <= cm), 1.0, 0.0), axis=-1, keepdims=True)

    cut = _tie_cutoff(count_tie_le, need, 12)
    cut = jnp.where(thr == INT_MIN, jnp.int32(-1), cut)
    key = key_scr[...]
    sel = (key > thr) | ((key == thr) & (lane <= cut))

    past = n_keys - 1
    n_seq = sc_ref.shape[0]
    taken = jnp.where(sel, 1.0, 0.0)[:, :past]
    for r in range(past // TK):
        upper = (lax.broadcasted_iota(I32, (TK, past), 0) + r * TK) <= lax.broadcasted_iota(I32, (TK, past), 1)
        tri_scr[r * TK:(r + 1) * TK, :] = jnp.where(upper, 1.0, 0.0).astype(BF16)
    rank = jnp.dot(taken.astype(BF16), tri_scr[...], preferred_element_type=F32)
    rank_scr[...] = rank * taken
    n_taken = rank[:, past - 1:past]
    slot = (lax.broadcasted_iota(I32, (TOPK, 1), 0) + 1).astype(F32)
    seq_lane = lax.broadcasted_iota(I32, (1, n_seq), 1)
    ids_scr[...] = jnp.zeros(ids_scr.shape, F32)

    def seq_body(d, carry):
        hit = rank_scr[pl.ds(d, 1), :] == slot
        row = jnp.sum(jnp.where(hit, rows_ref[pl.ds(d, 1), :], 0.0), axis=-1, keepdims=True)
        ids_scr[...] += jnp.where(seq_lane == d, row, 0.0)
        return carry

    lax.fori_loop(0, n_seq, seq_body, 0)
    ids_ref[...] = ids_scr[...].T.astype(I32)
    blane = lax.broadcasted_iota(I32, bias_ref.shape, 1)
    new_taken = jnp.sum(jnp.where(sel & (lane == past), 1.0, 0.0), axis=-1, keepdims=True)
    live = (blane.astype(F32) < n_taken) | ((blane == TOPK) & (new_taken > 0.0))
    bias_ref[...] = jnp.where(live, 0.0, NEG)


def _select_sample(scores, cache_rows, n_keys):
    n_seq = scores.shape[0]
    past = n_keys - 1
    return pl.pallas_call(
        functools.partial(_select_sample_kernel, n_keys=n_keys),
        out_shape=[jax.ShapeDtypeStruct((n_seq, TOPK), I32), jax.ShapeDtypeStruct((n_seq, TOPK + LANES), F32)],
        scratch_shapes=[pltpu.VMEM(scores.shape, I32), pltpu.VMEM((past, past), BF16),
                        pltpu.VMEM((n_seq, past), F32), pltpu.VMEM((TOPK, n_seq), F32)],
        compiler_params=_cparams(),
        name="select_sample",
    )(scores, cache_rows)


def _gather_kv_rows(pool_k, pool_v, ids):
    n = ids.shape[0]
    mesh = plsc.VectorSubcoreMesh(core_axis_name="core", subcore_axis_name="subcore")
    n_workers = mesh.num_cores * mesh.num_subcores
    per_worker = n // n_workers
    assert n % (n_workers * SC_GATHER_ROWS) == 0
    row = pool_k.shape[1:]
    out = jax.ShapeDtypeStruct((n,) + row, pool_k.dtype)

    @functools.partial(
        pl.kernel, mesh=mesh, out_type=[out, out],
        scratch_types=[pltpu.VMEM((SC_GATHER_ROWS,), I32), pltpu.VMEM((SC_GATHER_ROWS,) + row, pool_k.dtype),
                       pltpu.VMEM((SC_GATHER_ROWS,) + row, pool_v.dtype), pltpu.SemaphoreType.DMA,
                       pltpu.SemaphoreType.DMA],
        compiler_params=pltpu.CompilerParams(use_tc_tiling_on_sc=True),
        name="gather_kv_rows",
    )
    def gather(k_hbm, v_hbm, ids_hbm, ko_hbm, vo_hbm, ids_v, k_v, v_v, ksem, vsem):
        worker = lax.axis_index("subcore") * mesh.num_cores + lax.axis_index("core")

        @pl.loop(0, per_worker // SC_GATHER_ROWS)
        def _(j):
            off = pl.multiple_of(worker * per_worker + j * SC_GATHER_ROWS, SC_GATHER_ROWS)
            pltpu.sync_copy(ids_hbm.at[pl.ds(off, SC_GATHER_ROWS)], ids_v)
            kcopy = pltpu.async_copy(k_hbm.at[ids_v], k_v, ksem)
            vcopy = pltpu.async_copy(v_hbm.at[ids_v], v_v, vsem)
            kcopy.wait()
            pltpu.sync_copy(k_v, ko_hbm.at[pl.ds(off, SC_GATHER_ROWS)])
            vcopy.wait()
            pltpu.sync_copy(v_v, vo_hbm.at[pl.ds(off, SC_GATHER_ROWS)])

    return gather(pool_k, pool_v, ids)


def _attn_sample_kernel(q_ref, kn_ref, vn_ref, bias_ref, kg_ref, vg_ref, o_ref, kall, vall):
    past_rows = kg_ref.shape[1]
    zeros = jnp.zeros((LANES - N_HEADS, HEAD_DIM), F32)
    for g in range(ATT_SEQS):
        kall[:past_rows, :] = kg_ref[g].astype(BF16)
        vall[:past_rows, :] = vg_ref[g].astype(BF16)
        kall[past_rows:, :] = jnp.concatenate([kn_ref[g], zeros], axis=0).astype(BF16)
        vall[past_rows:, :] = jnp.concatenate([vn_ref[g], zeros], axis=0).astype(BF16)
        s = lax.dot_general(q_ref[g], kall[...], NT_DIMS, preferred_element_type=F32)
        own = (lax.broadcasted_iota(I32, s.shape, 1) & (N_HEADS - 1)) == lax.broadcasted_iota(I32, s.shape, 0)
        s = jnp.where(own, s + bias_ref[g], NEG)
        m = jnp.max(s, axis=-1, keepdims=True)
        p = jnp.exp2(s - m)
        l = jnp.sum(p, axis=-1, keepdims=True)
        o_ref[g] = jnp.dot(p.astype(BF16), vall[...], preferred_element_type=F32) / l


def _attn_sample(q3, kn3, vn3, bias3, kg, vg):
    db, past_rows, _ = kg.shape
    width = past_rows + LANES
    assert db % ATT_SEQS == 0 and bias3.shape == (db, 1, width)
    per_step = lambda r, w: pl.BlockSpec((ATT_SEQS, r, w), lambda d: (d, 0, 0))
    return pl.pallas_call(
        _attn_sample_kernel,
        grid=(db // ATT_SEQS,),
        in_specs=[per_step(N_HEADS, HEAD_DIM)] * 3 + [per_step(1, width)] + [per_step(past_rows, HEAD_DIM)] * 2,
        out_specs=per_step(N_HEADS, HEAD_DIM),
        out_shape=jax.ShapeDtypeStruct((db, N_HEADS, HEAD_DIM), F32),
        scratch_shapes=[pltpu.VMEM((width, HEAD_DIM), BF16), pltpu.VMEM((width, HEAD_DIM), BF16)],
        compiler_params=_cparams("arbitrary"),
        name="attn_sample",
    )(q3, kn3, vn3, bias3, kg, vg)


def _ffn_sample_kernel(x_ref, g_ref, wup_ref, wc_ref, bc_ref, wdn_ref, st0_ref, st1_ref, gf_ref,
                       out_ref, gate_ref, wupb_ref, wdnb_ref, a_scr, acc_scr, *, final_norm):
    t = pl.program_id(0)

    @pl.when(t == 0)
    def _():
        a_scr[...] = _rms(x_ref[...], g_ref[...]).astype(BF16)
        acc_scr[...] = jnp.zeros(acc_scr.shape, F32)

    wup_b = wup_ref[...].astype(BF16)
    wupb_ref[...] = wup_b
    r = jnp.dot(a_scr[...], wup_b, preferred_element_type=F32)

    @pl.when(t < N_FFS)
    def _():
        gate_ref[:, pl.ds(pl.multiple_of(t * FFS_COLS, LANES), FFS_COLS)] = r

    @pl.when(t >= N_FFS)
    def _():
        cs = pl.ds(pl.multiple_of((t - N_FFS) * FFS_COLS, LANES), FFS_COLS)
        gc = (bc_ref[:, cs] + wc_ref[0:1, cs] * st0_ref[:, cs] + wc_ref[1:2, cs] * st1_ref[:, cs]
              + wc_ref[2:3, cs] * gate_ref[:, cs])
        wdn_b = wdn_ref[...].astype(BF16)
        wdnb_ref[...] = wdn_b
        acc_scr[...] += jnp.dot((jax.nn.gelu(gc) * r).astype(BF16), wdn_b, preferred_element_type=F32)

    @pl.when(t == 2 * N_FFS - 1)
    def _():
        y = x_ref[...] + acc_scr[...]
        out_ref[...] = _rms(y, gf_ref[...]) if final_norm else y


def _ffn_sample(x, g, wup, wc, bc, wdn, st0, st1, gf, layer, final_norm):
    db = x.shape[0]
    full = lambda a: pl.BlockSpec(a.shape, lambda t: (0,) * a.ndim)
    dn_blk = lambda t: jnp.maximum(t - N_FFS, 0)
    return pl.pallas_call(
        functools.partial(_ffn_sample_kernel, final_norm=final_norm),
        grid=(2 * N_FFS,),
        in_specs=[full(x), full(g), pl.BlockSpec((None, D_MODEL, FFS_COLS), lambda t: (layer, 0, t)), full(wc),
                  full(bc), pl.BlockSpec((None, FFS_COLS, D_MODEL), lambda t: (layer, dn_blk(t), 0)), full(st0),
                  full(st1), full(gf)],
        out_specs=[pl.BlockSpec((db, D_MODEL), lambda t: (0, 0)), pl.BlockSpec((db, D_FF), lambda t: (0, 0)),
                   pl.BlockSpec((D_MODEL, FFS_COLS), lambda t: (0, t)),
                   pl.BlockSpec((FFS_COLS, D_MODEL), lambda t: (dn_blk(t), 0))],
        out_shape=[jax.ShapeDtypeStruct((db, D_MODEL), F32), jax.ShapeDtypeStruct((db, D_FF), F32),
                   jax.ShapeDtypeStruct((D_MODEL, 2 * D_FF), BF16), jax.ShapeDtypeStruct((D_FF, D_MODEL), BF16)],
        scratch_shapes=[pltpu.VMEM((db, D_MODEL), BF16), pltpu.VMEM((db, D_MODEL), F32)],
        compiler_params=_cparams("arbitrary"),
        name="ffn_sample_final" if final_norm else "ffn_sample",
    )(x, g, wup, wc, bc, wdn, st0, st1, gf)


def _attn_out_sample_kernel(x_ref, o_ref, wo_ref, out_ref):
    out_ref[...] = x_ref[...] + jnp.dot(o_ref[...].astype(BF16), wo_ref[...], preferred_element_type=F32)


def _rec_sample_kernel(x_ref, gm_ref, win_ref, wcr_ref, bcr_ref, wg_ref, bga_ref, bgx_ref, lam_ref, wor_ref,
                       cs0_ref, cs1_ref, cs2_ref, h0_ref, out_ref, hnew_ref, xb_ref, a_scr, u_scr):
    x = x_ref[...]
    a_scr[...] = _rms(x, gm_ref[...]).astype(BF16)
    nblk = MXU_N // LANES
    ncs = -RG_C * jax.nn.softplus(-lam_ref[...])
    for c in range(D_MODEL // MXU_N):
        cs = slice(c * MXU_N, (c + 1) * MXU_N)
        xb = jnp.dot(a_scr[...], win_ref[:, D_MODEL + c * MXU_N:D_MODEL + (c + 1) * MXU_N],
                     preferred_element_type=F32)
        gate = jnp.dot(a_scr[...], win_ref[:, cs], preferred_element_type=F32)
        xb_ref[:, cs] = xb
        xc = (bcr_ref[:, cs] + wcr_ref[0:1, cs] * cs0_ref[:, cs] + wcr_ref[1:2, cs] * cs1_ref[:, cs]
              + wcr_ref[2:3, cs] * cs2_ref[:, cs] + wcr_ref[3:4, cs] * xb)
        for k in range(nblk):
            n = c * nblk + k
            ls = slice(n * LANES, (n + 1) * LANES)
            xcn = xc[:, k * LANES:(k + 1) * LANES]
            gg = jnp.dot(xcn.astype(BF16), wg_ref[n], preferred_element_type=F32)
            a_t, b_t = _lru_coeffs(xcn, gg, bga_ref[:, ls], bgx_ref[:, ls], ncs[:, ls])
            h = a_t * h0_ref[:, ls] + b_t
            hnew_ref[:, ls] = h
            u_scr[:, ls] = (jax.nn.gelu(gate[:, k * LANES:(k + 1) * LANES]) * h).astype(BF16)
    out_ref[...] = x + jnp.dot(u_scr[...], wor_ref[...], preferred_element_type=F32)


def _sample_call(kernel, name, args, out_widths, n_scratch):
    db = args[0].shape[0]
    return pl.pallas_call(
        kernel,
        grid=(1,),
        in_specs=[_const_spec(a.shape) for a in args],
        out_specs=[pl.BlockSpec((db, w), lambda i: (0, 0)) for w in out_widths],
        out_shape=[jax.ShapeDtypeStruct((db, w), F32) for w in out_widths],
        scratch_shapes=[pltpu.VMEM((db, D_MODEL), BF16)] * n_scratch,
        compiler_params=_cparams("arbitrary"),
        name=name,
    )(*args)


def _rope_tables(pos):
    posf = pos.astype(F32)[:, None]

    def cs(d):
        half = d // 2
        inv = ROPE_THETA ** (-jnp.arange(half, dtype=F32) * 2.0 / d)
        ang = posf * inv[None, :]
        return jnp.cos(ang), jnp.sin(ang)

    c, s = cs(HEAD_DIM)
    cos = jnp.concatenate([c, c], axis=-1)
    sin = jnp.concatenate([-s, s], axis=-1)
    c, s = cs(IDX_DIM)
    one, zero = jnp.ones_like(c), jnp.zeros_like(c)
    tile = lambda parts: jnp.concatenate(parts, axis=-1)
    icos = tile([c, c, one, one])
    isa = tile([-s, zero, zero, zero])
    isb = tile([zero, s, zero, zero])
    return cos, sin, icos, isa, isb


def _split_attn_in(w):
    wt = jnp.swapaxes(w, 0, 1).astype(BF16)
    qkv = N_HEADS * HEAD_DIM
    wqk, wv, wi = wt[:2 * qkv], wt[2 * qkv:3 * qkv], wt[3 * qkv:]
    pad = lambda a: jnp.pad(a, ((0, LANES - a.shape[0]), (0, 0)))
    groups = [pad(wi[h * IDX_DIM:(h + 1) * IDX_DIM]) for h in range(IDX_HEADS)]
    groups.append(pad(wi[IDX_HEADS * IDX_DIM:]))
    return wqk, wv, jnp.concatenate(groups, axis=0)


def kernel(x_prompt, x_sample, cache_k, cache_v, cache_kidx, state_lru_h, state_lru_conv, state_ffn_conv,
           page_table, norm_mix, norm_ffn, norm_final, w_attn_in, w_attn_out, w_rec_in, w_rec_conv, b_rec_conv,
           w_gate_a, b_gate_a, w_gate_x, b_gate_x, lru_lambda, w_rec_out, w_ffn_up, w_ffn_conv, b_ffn_conv,
           w_ffn_down):
    bsz, seq, d = x_prompt.shape
    db = x_sample.shape[0]
    n_pages = page_table.shape[1]
    past = n_pages * PAGE
    assert d == D_MODEL and x_sample.shape[1] == 1 and seq % ROW_BLOCK == 0 and seq % TQ == 0
    assert min(TOPK, seq // 4) == TOPK and min(TOPK, (past + 1) // 4) == TOPK

    vec = lambda a: a.reshape(1, -1)
    wqk, wv, widx = _split_attn_in(w_attn_in[0])
    wo_attn = w_attn_out[0].astype(BF16)
    w_rin = w_rec_in[0].astype(BF16)
    w_gates = jnp.concatenate([w_gate_a[0], w_gate_x[0]], axis=-1).astype(BF16)
    wo_rec = w_rec_out[0].astype(BF16)
    rec_vecs = (w_rec_conv[0], vec(b_rec_conv[0]), w_gates, vec(b_gate_a[0]), vec(b_gate_x[0]),
                vec(lru_lambda[0]), wo_rec)
    gfin = vec(norm_final)
    st = state_ffn_conv

    def ffn_sample(x, i, final_norm):
        return _ffn_sample(x, vec(norm_ffn[i]), w_ffn_up, w_ffn_conv[i], vec(b_ffn_conv[i]), w_ffn_down,
                           st[i, :, 0], st[i, :, 1], gfin, i, final_norm)

    def ffn_prompt(x, i, wup_b, wdn_b, final_norm):
        return _ffn_prompt(x, vec(norm_ffn[i]), wup_b, w_ffn_conv[i], vec(b_ffn_conv[i]), wdn_b, gfin, bsz, seq,
                           final_norm)

    xp = x_prompt.reshape(bsz * seq, d)
    q, kf, kb, vf, vt, qi, kiw = _attn_in(xp, vec(norm_mix[0]), wqk, wv, widx, _rope_tables(jnp.arange(seq)),
                                          ROW_BLOCK, seq // ROW_BLOCK)
    wit = kiw[:, IDX_DIM:IDX_DIM + IDX_HEADS].T
    hp_attn = _dsa_prompt(q, kb, vt, qi, kiw, wit, xp, wo_attn, bsz, seq)

    xs = x_sample.reshape(db, d)
    tabs = tuple(jnp.broadcast_to(t, (db, t.shape[1])) for t in _rope_tables(jnp.full((1,), past)))
    qs, kfs, _, vfs, _, qis, kiws = _attn_in(xs, vec(norm_mix[0]), wqk, wv, widx, tabs, db, 1)
    pt = page_table.reshape(-1)
    scores = _idx_sample(pt, qis.reshape(db, IDX_HEADS, LANES),
                         kiws[:, IDX_DIM:IDX_DIM + IDX_HEADS].reshape(db, IDX_HEADS, 1),
                         kiws.reshape(db, 1, LANES), jnp.swapaxes(cache_kidx[0], 1, 2), n_pages)
    cache_rows = (page_table[:, :, None] * PAGE + jnp.arange(PAGE, dtype=I32)).reshape(db, past).astype(F32)
    ids, bias = _select_sample(scores.reshape(db, -1), cache_rows, past + 1)
    heads = (N_HEADS, HEAD_DIM)
    key_rows = lambda pool: pool[0].reshape(-1, *heads)
    ids, _ = lax.optimization_barrier((ids, kb))
    kg, vg = _gather_kv_rows(key_rows(cache_k), key_rows(cache_v), ids.reshape(-1))
    bias_rows = jnp.pad(jnp.repeat(bias[:, :TOPK + 1], N_HEADS, axis=1), ((0, 0), (0, LANES - N_HEADS)),
                        constant_values=NEG)
    bias_rows, _ = lax.optimization_barrier((bias_rows, hp_attn))
    seq_rows = lambda g: g.reshape(db, TOPK * N_HEADS, HEAD_DIM)
    o_s = _attn_sample(qs.reshape(db, *heads), kfs.reshape(db, *heads), vfs.reshape(db, *heads),
                       bias_rows.reshape(db, 1, -1), seq_rows(kg), seq_rows(vg))
    (hs,) = _sample_call(_attn_out_sample_kernel, "attn_out_sample", (xs, o_s.reshape(db, d), wo_attn),
                         (D_MODEL,), 0)
    hs, gate0, wup_b, wdn_b = ffn_sample(hs, 0, False)
    hp, ftail0 = ffn_prompt(hp_attn, 0, wup_b, wdn_b, False)

    hp, hlast, ctail = _rec_prompt(hp, vec(norm_mix[1]), w_rin, *rec_vecs, bsz, seq)
    cst = state_lru_conv[0]
    hs, hnew, xbs = _sample_call(
        _rec_sample_kernel, "rec_sample",
        (hs, vec(norm_mix[1]), w_rin, *rec_vecs, cst[:, 0], cst[:, 1], cst[:, 2], state_lru_h[0]),
        (D_MODEL, D_MODEL, D_MODEL), 2)
    ys, gate1, wup_b, wdn_b = ffn_sample(hs, 1, True)
    yp, ftail1 = ffn_prompt(hp, 1, wup_b, wdn_b, True)

    return (
        yp.reshape(bsz, seq, d),
        ys.reshape(db, 1, d),
        kf.reshape(1, bsz, seq, *heads),
        vf.reshape(1, bsz, seq, *heads),
        kiw[:, :IDX_DIM].reshape(1, bsz, seq, IDX_DIM),
        kfs.reshape(1, db, 1, *heads),
        vfs.reshape(1, db, 1, *heads),
        kiws[:, :IDX_DIM].reshape(1, db, 1, IDX_DIM),
        hlast[None, :, 0, :],
        ctail[None, :, SUBLANES - 3:, :],
        hnew[None],
        jnp.stack([cst[:, 1], cst[:, 2], xbs], axis=1)[None],
        jnp.stack([ftail0[:, SUBLANES - 2:], ftail1[:, SUBLANES - 2:]]),
        jnp.stack([jnp.stack([st[0, :, 1], gate0], axis=1), jnp.stack([st[1, :, 1], gate1], axis=1)]),
    )
```
